```python
import math
import jax
import jax.numpy as jnp
from jax import lax
import numpy as np

D_MODEL = 1024
BATCH = 4
SEQ = 8192
DEPTH = 2

CTX_LEN = 256
GRID_W = 64
N_MOD = 9
D_FF = 2816
RMS_EPS = 1e-6

HG_HEADS = 4
HG_DK = 128
HG_DV = 128
HG_K = HG_HEADS * HG_DK
HG_V = HG_HEADS * HG_DV
HG_CHUNK = 64

HY_W = 512
HY_ORDER = 2
HY_EMB = 33
HY_BANDS = (HY_EMB - 1) // 2
HY_FH = 64
HY_TARGET = 1e-2
HY_MIN_DECAY = math.log(HY_TARGET) / 1.5
HY_MAX_DECAY = math.log(HY_TARGET) / 0.3
HY_SHIFT = 0.05

MLA_HEADS = 4
MLA_Q_LORA = 256
MLA_KV_LORA = 128
MLA_NOPE = 128
MLA_ROPE = 64
MLA_V = 128
MLA_QK = MLA_NOPE + MLA_ROPE
MLA_OUT = MLA_HEADS * MLA_V
MLA_SCALE = MLA_QK ** -0.5
ROPE_THETA = 10000.0
Q_BLOCK = 128

IN_SPLITS = (HG_K, HG_K, HG_K, HG_V, HG_V, 3 * HY_W, MLA_Q_LORA, MLA_KV_LORA, MLA_ROPE, D_MODEL, D_MODEL, D_MODEL)
IN_COLS = sum(IN_SPLITS)

kernel_name = 'hybrid_hgrn2_hyena_mla_dit_block'


def rmsnorm(x, g):
    xf = x.astype(jnp.float32)
    y = xf * lax.rsqrt(jnp.mean(xf * xf, axis=-1, keepdims=True) + RMS_EPS)
    return (y * g.astype(jnp.float32)).astype(x.dtype)


def adaln(x, g, shift, scale):
    return rmsnorm(x, g) * (1 + scale) + shift


def swiglu(h, w13, w2):
    a, b = jnp.split(h @ w13, 2, axis=-1)
    return (jax.nn.silu(a) * b) @ w2


def short_conv3(u, w, b):
    up = jnp.pad(u, ((0, 0), (1, 1), (0, 0)))
    return up[:, :-2] * w[0] + up[:, 1:-1] * w[1] + up[:, 2:] * w[2] + b


def gla_scan(q, k, v, logf, s0):
    bsz, t_len, n_h, dk = q.shape
    nc = t_len // HG_CHUNK

    def to_chunks(a):
        a = a.astype(jnp.float32).reshape(bsz, nc, HG_CHUNK, n_h, a.shape[-1])
        return jnp.moveaxis(a, (1, 3), (0, 2))

    tri = jnp.tril(jnp.ones((HG_CHUNK, HG_CHUNK), bool))[:, :, None]

    def step(state, inp):
        qc, kc, vc, gc = inp
        b = jnp.cumsum(gc, axis=2)
        b_last = b[:, :, -1:, :]
        o_inter = jnp.einsum('bhtk,bhkv->bhtv', qc * jnp.exp(b), state)
        diff = b[:, :, :, None, :] - b[:, :, None, :, :]
        decay = jnp.where(tri, jnp.exp(jnp.where(tri, diff, 0.0)), 0.0)
        attn = jnp.einsum('bhtsk,bhsk->bhts', decay * qc[:, :, :, None, :], kc)
        o = o_inter + jnp.einsum('bhts,bhsv->bhtv', attn, vc)
        new_state = jnp.exp(b_last[:, :, 0, :])[..., None] * state + jnp.einsum(
            'bhsk,bhsv->bhkv', kc * jnp.exp(b_last - b), vc)
        return new_state, o

    s_final, o = lax.scan(step, s0.astype(jnp.float32), (to_chunks(q), to_chunks(k), to_chunks(v), to_chunks(logf)))
    o = jnp.moveaxis(o, (0, 2), (1, 3)).reshape(bsz, t_len, n_h, v.shape[-1])
    return o, s_final


def split_heads(a, d):
    return a.reshape(a.shape[0], a.shape[1], -1, d)


def hgrn2_gates(z, lb):
    f = lb + (1.0 - lb) * jax.nn.sigmoid(z.astype(jnp.float32))
    return 1.0 - f, jnp.log(f)


def hgrn2_seq(q, i, zf, zb, lb_f, lb_b, s0f, s0b):
    qh = split_heads(jax.nn.silu(q), HG_DK) * HG_DK ** -0.5
    ih = split_heads(i, HG_DV)
    kf, lf = hgrn2_gates(split_heads(zf, HG_DK), lb_f.reshape(HG_HEADS, HG_DK))
    kb, lbk = hgrn2_gates(split_heads(zb, HG_DK), lb_b.reshape(HG_HEADS, HG_DK))
    o_f, s_f = gla_scan(qh, kf, ih, lf, s0f)
    flip = lambda a: jnp.flip(a, axis=1)
    o_b, s_b = gla_scan(flip(qh), flip(kb), flip(ih), flip(lbk), s0b)
    return o_f + flip(o_b), s_f, s_b


def hgrn2_readout(o, g, gain):
    o = rmsnorm(o, gain)
    return o.reshape(o.shape[0], o.shape[1], HG_V).astype(g.dtype) * jax.nn.silu(g)


def hyena_filters(L, p):
    f32 = jnp.float32
    t = jnp.linspace(0.0, 1.0, L, dtype=f32)[:, None]
    w = (2.0 * math.pi / L) * jnp.arange(L, dtype=f32)[:, None]
    bands = jnp.linspace(1e-4, HY_BANDS - 1, HY_BANDS, dtype=f32)[None, :]
    emb = jnp.concatenate([t, jnp.cos(bands * w), -jnp.sin(bands * w)], axis=-1)
    freq = p['hy_freq'].astype(f32)
    hid = jnp.sin(freq * (emb @ p['hy_w1'].astype(f32) + p['hy_b1'].astype(f32)))
    hid = jnp.sin(freq * (hid @ p['hy_w2'].astype(f32) + p['hy_b2'].astype(f32)))
    hid = jnp.sin(freq * (hid @ p['hy_w3'].astype(f32) + p['hy_b3'].astype(f32)))
    h = (hid @ p['hy_w4'].astype(f32)).reshape(L, 2, HY_ORDER, HY_W)
    deltas = jnp.abs(jnp.linspace(HY_MIN_DECAY, HY_MAX_DECAY, HY_W, dtype=f32))
    h = h * (jnp.exp(-t[:, :, None, None] * deltas) + HY_SHIFT)
    h_fwd = h[:, 0]
    h_bwd = h[1:, 1]
    norm = jnp.sum(jnp.abs(h_fwd), axis=0) + jnp.sum(jnp.abs(h_bwd), axis=0)
    k_circ = jnp.concatenate([h_fwd, jnp.zeros((1, HY_ORDER, HY_W), f32), h_bwd[::-1]], axis=0) / norm
    return jnp.fft.rfft(k_circ, axis=0)


def fft_conv(u, kf):
    L = u.shape[1]
    U = jnp.fft.rfft(u.astype(jnp.float32), n=2 * L, axis=1)
    return jnp.fft.irfft(U * kf[None], n=2 * L, axis=1)[:, :L].astype(u.dtype)


def hyena(u3, p, kf):
    uc = short_conv3(u3, p['hy_conv_w'], p['hy_conv_b'])
    x1, x2, z = jnp.split(uc, 3, axis=-1)
    for o, xg in enumerate((x1, x2)):
        z = xg * (fft_conv(z, kf[:, o]) + z * p['hy_skip'][o])
    return z


def axial_angles(L):
    rows = L // GRID_W
    row = jnp.repeat(jnp.arange(rows, dtype=jnp.float32), GRID_W)
    col = jnp.tile(jnp.arange(GRID_W, dtype=jnp.float32), rows)
    half = MLA_ROPE // 2
    inv = ROPE_THETA ** (-jnp.arange(0, half, 2, dtype=jnp.float32) / half)
    return row[:, None] * inv, col[:, None] * inv


def rotate(x, ang):
    x1, x2 = jnp.split(x, 2, axis=-1)
    cos = jnp.cos(ang).astype(x.dtype)
    sin = jnp.sin(ang).astype(x.dtype)
    return jnp.concatenate([x1 * cos - x2 * sin, x2 * cos + x1 * sin], axis=-1)


def axial_rope(x, ang_r, ang_c):
    xr, xc = jnp.split(x, 2, axis=-1)
    return jnp.concatenate([rotate(xr, ang_r), rotate(xc, ang_c)], axis=-1)


def mla_queries(qa, p):
    bsz, L, _ = qa.shape
    q = (rmsnorm(qa, p['q_a_norm']) @ p['w_uq']).reshape(bsz, L, MLA_HEADS, MLA_QK)
    return rmsnorm(q[..., :MLA_NOPE], p['q_nope_norm']), rmsnorm(q[..., MLA_NOPE:], p['q_rope_norm'])


def mla_keys(kva, kr, p):
    bsz, L, _ = kva.shape
    kv = (rmsnorm(kva, p['kv_a_norm']) @ p['w_ukv']).reshape(bsz, L, MLA_HEADS, MLA_NOPE + MLA_V)
    return rmsnorm(kv[..., :MLA_NOPE], p['k_nope_norm']), rmsnorm(kr, p['k_rope_norm']), kv[..., MLA_NOPE:]


def mla_attend(qn, qr, key_sets):
    s = jnp.concatenate([jnp.einsum('bqhd,bkhd->bhqk', qn, kn) + jnp.einsum('bqhr,bkr->bhqk', qr, kr)
                         for kn, kr, _ in key_sets], axis=-1)
    pr = jax.nn.softmax(s.astype(jnp.float32) * MLA_SCALE, axis=-1)
    out = 0.0
    start = 0
    for _, _, v in key_sets:
        n = v.shape[1]
        out = out + jnp.einsum('bhqk,bkhd->bqhd', pr[..., start:start + n].astype(v.dtype), v)
        start += n
    return out


def latent_attention(qn, qr, kn, kr, v, kn_c, kr_c, v_c):
    bsz, L = qn.shape[0], qn.shape[1]
    nb = L // Q_BLOCK
    blocks = lambda a: jnp.moveaxis(a.reshape(bsz, nb, Q_BLOCK, *a.shape[2:]), 1, 0)
    out = lax.map(lambda qb: mla_attend(qb[0], qb[1], ((kn, kr, v), (kn_c, kr_c, v_c))), (blocks(qn), blocks(qr)))
    return jnp.moveaxis(out, 0, 1).reshape(bsz, L, MLA_OUT)


def merge(y_a, y_b, y_c, g_a, g_b, g_c, p):
    m = (jax.nn.sigmoid(g_a) * (y_a @ p['w_br_a']) + jax.nn.sigmoid(g_b) * (y_b @ p['w_br_b'])
         + jax.nn.sigmoid(g_c) * (y_c @ p['w_br_c']))
    return m @ p['w_out']


def token_mixer(h, hc, p, lb_f, lb_b, need_ctx):
    offs = np.cumsum(IN_SPLITS)[:-1].tolist()
    (q, zf, zb, iv, g, hy, qa, kva, kr, ga, gb, gc) = jnp.split(h @ p['w_in'], offs, axis=-1)
    (q_c, zf_c, zb_c, iv_c, g_c, hy_c, qa_c, kva_c, kr_c, ga_c, gb_c, gc_c) = jnp.split(hc @ p['w_in'], offs, axis=-1)
    bsz, L = h.shape[0], h.shape[1]
    Lc = hc.shape[1]

    zero = jnp.zeros((bsz, HG_HEADS, HG_DK, HG_DV), jnp.float32)
    o_ctx, s_f, s_b = hgrn2_seq(q_c, iv_c, zf_c, zb_c, lb_f, lb_b, zero, zero)
    o_lat, _, _ = hgrn2_seq(q, iv, zf, zb, lb_f, lb_b, s_f, s_b)
    y_a = hgrn2_readout(o_lat, g, p['hg_out_norm'])

    y_b = hyena(hy, p, hyena_filters(L, p))

    ang_r, ang_c = axial_angles(L)
    qn, qr = mla_queries(qa, p)
    qr = axial_rope(qr, ang_r[:, None], ang_c[:, None])
    kn, krr, v = mla_keys(kva, kr, p)
    krr = axial_rope(krr, ang_r, ang_c)
    kn_c, krr_c, v_c = mla_keys(kva_c, kr_c, p)
    y_c = latent_attention(qn, qr, kn, krr, v, kn_c, krr_c, v_c)

    out = merge(y_a, y_b, y_c, ga, gb, gc, p)
    if not need_ctx:
        return out, None
    y_a_c = hgrn2_readout(o_ctx, g_c, p['hg_out_norm'])
    y_b_c = hyena(hy_c, p, hyena_filters(Lc, p))
    qn_c, qr_c = mla_queries(qa_c, p)
    y_c_c = mla_attend(qn_c, qr_c, ((kn_c, krr_c, v_c),)).reshape(bsz, Lc, MLA_OUT)
    out_c = merge(y_a_c, y_b_c, y_c_c, ga_c, gb_c, gc_c, p)
    return out, out_c


def half_ffn(y, m, idx, norm, w13, w2):
    return y + 0.5 * m[idx + 2] * swiglu(adaln(y, norm, m[idx], m[idx + 1]), w13, w2)


def layer(x, xc, c, c_ctx, p, lb_f, lb_b, need_ctx):
    mod = jnp.split((jax.nn.silu(c) @ p['ada_w'] + p['ada_b'])[:, None, :], N_MOD, axis=-1)
    mod_c = jnp.split((jax.nn.silu(c_ctx) @ p['ada_w'] + p['ada_b'])[None, None, :], N_MOD, axis=-1)
    x = half_ffn(x, mod, 0, p['ffn1_norm'], p['ffn1_w13'], p['ffn1_w2'])
    xc = half_ffn(xc, mod_c, 0, p['ffn1_norm'], p['ffn1_w13'], p['ffn1_w2'])
    out, out_c = token_mixer(adaln(x, p['mix_norm'], mod[3], mod[4]),
                             adaln(xc, p['mix_norm'], mod_c[3], mod_c[4]), p, lb_f, lb_b, need_ctx)
    x = x + mod[5] * out
    x = half_ffn(x, mod, 6, p['ffn2_norm'], p['ffn2_w13'], p['ffn2_w2'])
    if need_ctx:
        xc = xc + mod_c[5] * out_c
        xc = half_ffn(xc, mod_c, 6, p['ffn2_norm'], p['ffn2_w13'], p['ffn2_w2'])
    return x, xc


def setup_inputs(seed: int = 0) -> dict:
    key = jax.random.key(seed)
    keys = iter(jax.random.split(key, 64))

    def normal(shape, scale=1.0):
        return scale * jax.random.normal(next(keys), shape, jnp.float32)

    def gain(shape):
        return 1.0 + normal(shape, 0.02)

    L, D = DEPTH, D_MODEL
    return {
        'x': normal((BATCH, SEQ, D)),
        'c': normal((BATCH, D)),
        'ctx': normal((BATCH, CTX_LEN, D)),
        'c_ctx': normal((D,)),
        'ada_w': normal((L, D, N_MOD * D), 0.5 * D ** -0.5),
        'ada_b': normal((L, N_MOD * D), 0.02),
        'ffn1_norm': gain((L, D)),
        'ffn1_w13': normal((L, D, 2 * D_FF), D ** -0.5),
        'ffn1_w2': normal((L, D_FF, D), D_FF ** -0.5),
        'mix_norm': gain((L, D)),
        'w_in': normal((L, D, IN_COLS), D ** -0.5),
        'hg_lb_logits': normal((2, L, HG_K), 0.5),
        'hg_out_norm': gain((L, HG_DV)),
        'hy_conv_w': normal((L, 3, 3 * HY_W), 3 ** -0.5),
        'hy_conv_b': normal((L, 3 * HY_W), 0.02),
        'hy_w1': normal((L, HY_EMB, HY_FH), HY_EMB ** -0.5),
        'hy_b1': normal((L, HY_FH), 0.1),
        'hy_w2': normal((L, HY_FH, HY_FH), HY_FH ** -0.5),
        'hy_b2': normal((L, HY_FH), 0.1),
        'hy_w3': normal((L, HY_FH, HY_FH), HY_FH ** -0.5),
        'hy_b3': normal((L, HY_FH), 0.1),
        'hy_w4': normal((L, HY_FH, 2 * HY_ORDER * HY_W), HY_FH ** -0.5),
        'hy_freq': gain((L, HY_FH)),
        'hy_skip': normal((L, HY_ORDER, HY_W), 0.5),
        'q_a_norm': gain((L, MLA_Q_LORA)),
        'w_uq': normal((L, MLA_Q_LORA, MLA_HEADS * MLA_QK), MLA_Q_LORA ** -0.5),
        'kv_a_norm': gain((L, MLA_KV_LORA)),
        'w_ukv': normal((L, MLA_KV_LORA, MLA_HEADS * (MLA_NOPE + MLA_V)), MLA_KV_LORA ** -0.5),
        'q_nope_norm': gain((L, MLA_NOPE)),
        'q_rope_norm': gain((L, MLA_ROPE)),
        'k_nope_norm': gain((L, MLA_NOPE)),
        'k_rope_norm': gain((L, MLA_ROPE)),
        'w_br_a': normal((L, HG_V, D), HG_V ** -0.5),
        'w_br_b': normal((L, HY_W, D), HY_W ** -0.5),
        'w_br_c': normal((L, MLA_OUT, D), MLA_OUT ** -0.5),
        'w_out': normal((L, D, D), D ** -0.5),
        'ffn2_norm': gain((L, D)),
        'ffn2_w13': normal((L, D, 2 * D_FF), D ** -0.5),
        'ffn2_w2': normal((L, D_FF, D), D_FF ** -0.5),
    }


def reference(x, c, ctx, c_ctx, ada_w, ada_b, ffn1_norm, ffn1_w13, ffn1_w2, mix_norm, w_in,
              hg_lb_logits, hg_out_norm, hy_conv_w, hy_conv_b, hy_w1, hy_b1, hy_w2, hy_b2, hy_w3, hy_b3,
              hy_w4, hy_freq, hy_skip, q_a_norm, w_uq, kv_a_norm, w_ukv, q_nope_norm, q_rope_norm,
              k_nope_norm, k_rope_norm, w_br_a, w_br_b, w_br_c, w_out, ffn2_norm, ffn2_w13, ffn2_w2):
    stacked = {
        'ada_w': ada_w, 'ada_b': ada_b, 'ffn1_norm': ffn1_norm, 'ffn1_w13': ffn1_w13, 'ffn1_w2': ffn1_w2,
        'mix_norm': mix_norm, 'w_in': w_in, 'hg_out_norm': hg_out_norm,
        'hy_conv_w': hy_conv_w, 'hy_conv_b': hy_conv_b, 'hy_w1': hy_w1, 'hy_b1': hy_b1, 'hy_w2': hy_w2,
        'hy_b2': hy_b2, 'hy_w3': hy_w3, 'hy_b3': hy_b3, 'hy_w4': hy_w4, 'hy_freq': hy_freq, 'hy_skip': hy_skip,
        'q_a_norm': q_a_norm, 'w_uq': w_uq, 'kv_a_norm': kv_a_norm, 'w_ukv': w_ukv,
        'q_nope_norm': q_nope_norm, 'q_rope_norm': q_rope_norm, 'k_nope_norm': k_nope_norm,
        'k_rope_norm': k_rope_norm, 'w_br_a': w_br_a, 'w_br_b': w_br_b, 'w_br_c': w_br_c, 'w_out': w_out,
        'ffn2_norm': ffn2_norm, 'ffn2_w13': ffn2_w13, 'ffn2_w2': ffn2_w2,
    }
    lbs = jax.nn.softmax(hg_lb_logits.astype(jnp.float32), axis=1)
    lbs = jnp.cumsum(lbs, axis=1) - lbs[:, :1]
    xc = ctx
    for l in range(DEPTH):
        p = {name: val[l] for name, val in stacked.items()}
        x, xc = layer(x, xc, c, c_ctx, p, lbs[0, l], lbs[1, l], l < DEPTH - 1)
    return x
```

```python
import functools
import math

import numpy as np
import jax
import jax.numpy as jnp
from jax import lax
from jax.experimental import pallas as pl
from jax.experimental.pallas import tpu as pltpu

F32 = jnp.float32
BF16 = jnp.bfloat16

RMS_EPS = 1e-6
N_MOD = 9
GRID_W = 64
ROPE_THETA = 10000.0
HG_HEADS = 4
HG_D = 128
HG_CHUNK = 128
HY_W = 512
HY_TARGET = 1e-2
HY_MIN_DECAY = math.log(HY_TARGET) / 1.5
HY_MAX_DECAY = math.log(HY_TARGET) / 0.3
HY_SHIFT = 0.05
MLA_HEADS = 4
MLA_NOPE = 128
MLA_ROPE = 64
MLA_V = 128
MLA_QK = MLA_NOPE + MLA_ROPE
MLA_SCALE = MLA_QK ** -0.5
MLA_HEAD_PAD = 256
FFT_N1 = 128
LANE = 128

VMEM_LIMIT = 52 * 1024 * 1024

COL_GA, COL_GB, COL_GC = 0, 1024, 2048
COL_Q, COL_ZF, COL_ZB, COL_IV, COL_G = 3072, 3584, 4096, 4608, 5120
COL_HY = 5632
COL_QA, COL_KVA, COL_KR = 7168, 7424, 7552
IN_PACKED = 7680


def _cparams(n_axes):
    return pltpu.CompilerParams(dimension_semantics=("arbitrary",) * n_axes,
                                vmem_limit_bytes=VMEM_LIMIT)


def _dot(a, b):
    return jnp.dot(a, b, preferred_element_type=F32)


def _dot_nt(a, b):
    return lax.dot_general(a, b, (((1,), (1,)), ((), ())), preferred_element_type=F32)


def _dot_tn(a, b):
    return lax.dot_general(a, b, (((0,), (0,)), ((), ())), preferred_element_type=F32)


def _sigmoid(x):
    return 1.0 / (1.0 + jnp.exp(-x))


def _silu(x):
    return x * _sigmoid(x)


def _rms(x, gain):
    return x * lax.rsqrt(jnp.mean(x * x, axis=-1, keepdims=True) + RMS_EPS) * gain


def _mod_kernel(c_ref, w_ref, b_ref, o_ref):
    o_ref[...] = _dot(_silu(c_ref[...]).astype(BF16), w_ref[...]) + b_ref[...]


def _modulation(cs, ada_w, ada_b):
    g, d = cs.shape
    n = ada_w.shape[1]
    tn = n // 4
    out = pl.pallas_call(
        _mod_kernel,
        grid=(n // tn,),
        in_specs=[pl.BlockSpec((g, d), lambda j: (0, 0)),
                  pl.BlockSpec((d, tn), lambda j: (0, j)),
                  pl.BlockSpec((1, tn), lambda j: (0, j))],
        out_specs=pl.BlockSpec((g, tn), lambda j: (0, j)),
        out_shape=jax.ShapeDtypeStruct((g, n), F32),
        compiler_params=_cparams(1),
        name="modulation",
    )(cs, ada_w.astype(BF16), ada_b.reshape(1, n))
    return out.reshape(g, N_MOD, d)


def _ffn_kernel(x_ref, mod_ref, g_ref, w13_ref, w2_ref, o_ref, *, idx, ff, n_chunks):
    x = x_ref[...]
    m = mod_ref[0]
    h = (_rms(x, g_ref[...]) * (1.0 + m[idx + 1:idx + 2]) + m[idx:idx + 1]).astype(BF16)
    ck = ff // n_chunks
    acc = jnp.zeros(x.shape, F32)
    for j in range(n_chunks):
        a = _dot(h, w13_ref[:, j * ck:(j + 1) * ck])
        b = _dot(h, w13_ref[:, ff + j * ck:ff + (j + 1) * ck])
        acc = acc + _dot((_silu(a) * b).astype(BF16), w2_ref[j * ck:(j + 1) * ck, :])
    o_ref[...] = x + (0.5 * m[idx + 2:idx + 3]) * acc


def _group_map(tiles_per_batch, n_batch):
    return lambda i: (jnp.minimum(i // tiles_per_batch, n_batch), 0, 0)


def _half_ffn(xa, mod, gain, w13, w2, idx, n_rows, seq, n_batch, tm=512):
    d = xa.shape[1]
    ff = w2.shape[0]
    n_chunks = ff // 256
    const = dict(pipeline_mode=pl.Buffered(1))
    return pl.pallas_call(
        functools.partial(_ffn_kernel, idx=idx, ff=ff, n_chunks=n_chunks),
        grid=(n_rows // tm,),
        in_specs=[pl.BlockSpec((tm, d), lambda i: (i, 0)),
                  pl.BlockSpec((1, N_MOD, d), _group_map(seq // tm, n_batch)),
                  pl.BlockSpec((1, d), lambda i: (0, 0)),
                  pl.BlockSpec((d, 2 * ff), lambda i: (0, 0), **const),
                  pl.BlockSpec((ff, d), lambda i: (0, 0), **const)],
        out_specs=pl.BlockSpec((tm, d), lambda i: (i, 0)),
        out_shape=jax.ShapeDtypeStruct((n_rows, d), F32),
        compiler_params=_cparams(1),
        name="half_ffn",
    )(xa, mod, gain.reshape(1, d), w13.astype(BF16), w2.astype(BF16))


def _inproj_kernel(x_ref, mod_ref, g_ref, w_ref, o_ref):
    m = mod_ref[0]
    h = (_rms(x_ref[...], g_ref[...]) * (1.0 + m[4:5]) + m[3:4]).astype(BF16)
    o_ref[...] = _dot(h, w_ref[...])


def _pack_w_in(w_in):
    d = w_in.shape[0]
    hk = HG_HEADS * HG_D
    sizes = (hk, hk, hk, hk, hk, 3 * HY_W, 256, 128, MLA_ROPE, d, d, d)
    offs = np.cumsum((0,) + sizes)
    q, zf, zb, iv, g, hy, qa, kva, kr, ga, gb, gc = (w_in[:, offs[i]:offs[i + 1]] for i in range(12))
    return jnp.concatenate([ga, gb, gc, q, zf, zb, iv, g, hy, qa, kva, kr, _rope_swap(kr)], axis=1)


def _rope_swap(a):
    q = MLA_ROPE // 4
    return jnp.concatenate([a[..., q:2 * q], a[..., :q], a[..., 3 * q:], a[..., 2 * q:3 * q]], axis=-1)


def _in_projection(xa, mod, gain, w_packed, seq, n_batch, tm=512, tn=2560):
    n_rows, d = xa.shape
    n = w_packed.shape[1]
    return pl.pallas_call(
        _inproj_kernel,
        grid=(n // tn, n_rows // tm),
        in_specs=[pl.BlockSpec((tm, d), lambda j, i: (i, 0)),
                  pl.BlockSpec((1, N_MOD, d), lambda j, i: (jnp.minimum(i // (seq // tm), n_batch), 0, 0)),
                  pl.BlockSpec((1, d), lambda j, i: (0, 0)),
                  pl.BlockSpec((d, tn), lambda j, i: (0, j))],
        out_specs=pl.BlockSpec((tm, tn), lambda j, i: (i, j)),
        out_shape=jax.ShapeDtypeStruct((n_rows, n), F32),
        compiler_params=_cparams(2),
        name="in_projection",
    )(xa, mod, gain.reshape(1, d), w_packed)


def _hgrn2_tables(c, rev):
    t = np.arange(c)[:, None]
    u = np.arange(c)[None, :]
    mats = [(u <= t), (u > t)]
    masks = [(t == u)]
    h = c // 2
    while h >= 1:
        mid = (t // (2 * h)) * (2 * h) + h
        mats.append(np.where(t >= mid, (u >= mid) & (u <= t), (u >= t + 1) & (u <= mid - 1)))
        mid_s = (u // (2 * h)) * (2 * h) + h
        masks.append((t // (2 * h) == u // (2 * h)) & (u < mid_s) & (t >= mid))
        h //= 2
    mats = np.stack([m.astype(np.float32) for m in mats])
    masks = np.stack([m.astype(np.float32) for m in masks])
    if rev:
        mats = mats[:, ::-1, ::-1]
        masks = masks[:, ::-1, ::-1]
    return (jnp.asarray(np.ascontiguousarray(mats).reshape(-1, c), BF16),
            jnp.asarray(np.ascontiguousarray(masks), F32))


def _hgrn2_kernel(q_ref, z_ref, v_ref, lg_ref, dst_ref, msk_ref, s0_ref, o_ref, sf_ref, st_ref,
                  *, layer, rev, n_levels):
    j = pl.program_id(1)
    c = q_ref.shape[0]
    hk = HG_HEADS * HG_D

    @pl.when(j == 0)
    def _():
        st_ref[...] = s0_ref[0]

    lg = lg_ref[...]
    e = jnp.exp(lg - jnp.max(lg, axis=0, keepdims=True))
    sm = e / jnp.sum(e, axis=0, keepdims=True)
    lb = jnp.zeros((1, hk), F32)
    for i in range(1, layer + 1):
        lb = lb + sm[i:i + 1]

    f = lb + (1.0 - lb) * _sigmoid(z_ref[...])
    kk = 1.0 - f
    g = jnp.log(f)
    g_hi = g.astype(BF16)
    g_lo = (g - g_hi.astype(F32)).astype(BF16)
    dg = _dot(dst_ref[...], jnp.concatenate([g_hi, g_lo], axis=1))
    dg = dg[:, :hk] + dg[:, hk:]
    qs = _silu(q_ref[...]) * HG_D ** -0.5
    vv = v_ref[...]
    last = 0 if rev else c - 1

    for h in range(HG_HEADS):
        hs = slice(h * HG_D, (h + 1) * HG_D)
        qh, kh, vh = qs[:, hs], kk[:, hs], vv[:, hs].astype(BF16)
        b_in = dg[0:c, hs]
        e_out = jnp.exp(dg[c:2 * c, hs])
        a = _dot_nt(qh.astype(BF16), kh.astype(BF16)) * msk_ref[0]
        for l in range(1, n_levels + 1):
            el = jnp.exp(dg[(1 + l) * c:(2 + l) * c, hs])
            a = a + _dot_nt((qh * el).astype(BF16), (kh * el).astype(BF16)) * msk_ref[l]
        st = st_ref[h]
        o = _dot_nt((qh * jnp.exp(b_in)).astype(BF16), st.astype(BF16)) + _dot(a.astype(BF16), vh)
        o_ref[:, hs] = o
        st_ref[h] = jnp.exp(b_in[last:last + 1, :]) * st + _dot_tn(vh, (kh * e_out).astype(BF16))

    @pl.when(j == pl.num_programs(1) - 1)
    def _():
        sf_ref[0] = st_ref[...]


def _hgrn2_scan(p, logits, s0, layer, rev, row0, seq, n_batch):
    c = HG_CHUNK
    hk = HG_HEADS * HG_D
    nc = seq // c
    base = row0 // c
    n_levels = int(math.log2(c))
    dst, msk = _hgrn2_tables(c, rev)
    col_z = COL_ZB if rev else COL_ZF

    def rows(b, j):
        return b * nc + (nc - 1 - j if rev else j)

    def prow(w):
        return pl.BlockSpec((c, hk), lambda b, j: (base + rows(b, j), w // hk))

    return pl.pallas_call(
        functools.partial(_hgrn2_kernel, layer=layer, rev=rev, n_levels=n_levels),
        grid=(n_batch, nc),
        in_specs=[prow(COL_Q), prow(col_z), prow(COL_IV),
                  pl.BlockSpec(logits.shape, lambda b, j: (0, 0)),
                  pl.BlockSpec(dst.shape, lambda b, j: (0, 0)),
                  pl.BlockSpec(msk.shape, lambda b, j: (0, 0, 0)),
                  pl.BlockSpec((1, HG_HEADS, HG_D, HG_D), lambda b, j: (b, 0, 0, 0))],
        out_specs=[pl.BlockSpec((c, hk), lambda b, j: (rows(b, j), 0)),
                   pl.BlockSpec((1, HG_HEADS, HG_D, HG_D), lambda b, j: (b, 0, 0, 0))],
        out_shape=[jax.ShapeDtypeStruct((n_batch * seq, hk), F32),
                   jax.ShapeDtypeStruct((n_batch, HG_HEADS, HG_D, HG_D), F32)],
        scratch_shapes=[pltpu.VMEM((HG_HEADS, HG_D, HG_D), F32)],
        compiler_params=_cparams(2),
        name="hgrn2_scan_bwd" if rev else "hgrn2_scan_fwd",
    )(p, p, p, logits, dst, msk, s0)


def _shortconv_kernel(u_ref, w_ref, b_ref, o_ref):
    u = u_ref[...]
    n = u.shape[0]
    row = lax.broadcasted_iota(jnp.int32, u.shape, 0)
    prev = jnp.where(row == 0, 0.0, pltpu.roll(u, 1, 0))
    nxt = jnp.where(row == n - 1, 0.0, pltpu.roll(u, n - 1, 0))
    w = w_ref[...]
    o_ref[0, 0] = prev * w[0:1] + u * w[1:2] + nxt * w[2:3] + b_ref[...]


def _short_conv(p, w, b, row0, seq, n_batch):
    nb = 3 * HY_W // LANE
    per = HY_W // LANE
    return pl.pallas_call(
        _shortconv_kernel,
        grid=(n_batch, nb),
        in_specs=[pl.BlockSpec((seq, LANE), lambda bi, j: (row0 // seq + bi, COL_HY // LANE + j)),
                  pl.BlockSpec((3, LANE), lambda bi, j: (0, j)),
                  pl.BlockSpec((1, LANE), lambda bi, j: (0, j))],
        out_specs=pl.BlockSpec((1, 1, seq, LANE), lambda bi, j: (j // per, bi, 0, j % per)),
        out_shape=jax.ShapeDtypeStruct((3, n_batch, seq, HY_W), F32),
        compiler_params=_cparams(2),
        name="hyena_short_conv",
    )(p, w, b.reshape(1, -1))


def _hy_filter_kernel(emb_ref, w1_ref, b1_ref, w2_ref, b2_ref, w3_ref, b3_ref, w4_ref, fr_ref, dl_ref,
                      o_ref, nrm_ref, *, seq):
    i = pl.program_id(0)
    hp = lax.Precision.HIGHEST
    fr = fr_ref[...]
    hid = jnp.sin(fr * (jnp.dot(emb_ref[...], w1_ref[...], precision=hp) + b1_ref[...]))
    hid = jnp.sin(fr * (jnp.dot(hid, w2_ref[...], precision=hp) + b2_ref[...]))
    hid = jnp.sin(fr * (jnp.dot(hid, w3_ref[...], precision=hp) + b3_ref[...]))
    h = jnp.dot(hid, w4_ref[...], precision=hp)
    tl, n = h.shape
    pos = lax.broadcasted_iota(jnp.int32, h.shape, 0) + i * tl
    col = lax.broadcasted_iota(jnp.int32, h.shape, 1)
    t = pos.astype(F32) * (1.0 / (seq - 1))
    h = h * (jnp.exp(-t * dl_ref[...]) + HY_SHIFT)
    h = jnp.where((pos == 0) & (col >= n // 2), 0.0, h)
    o_ref[...] = h

    @pl.when(i == 0)
    def _():
        nrm_ref[...] = jnp.zeros_like(nrm_ref)

    nrm_ref[...] += jnp.sum(jnp.abs(h), axis=0, keepdims=True)


def _hyena_filter_taps(seq, w1, b1, w2, b2, w3, b3, w4, freq):
    fh = w1.shape[1]
    n_emb = w1.shape[0]
    bands_n = (n_emb - 1) // 2
    tt = np.linspace(0.0, 1.0, seq, dtype=np.float32)[:, None].astype(np.float64)
    ww = (2.0 * math.pi / seq) * np.arange(seq, dtype=np.float64)[:, None]
    bands = np.linspace(1e-4, bands_n - 1, bands_n, dtype=np.float32)[None, :].astype(np.float64)
    emb = np.concatenate([tt, np.cos(bands * ww), -np.sin(bands * ww)], axis=-1)
    emb = np.pad(emb, ((0, 0), (0, LANE - n_emb))).astype(np.float32)
    w1p = jnp.pad(w1, ((0, LANE - n_emb), (0, 0)))
    deltas = np.abs(np.linspace(HY_MIN_DECAY, HY_MAX_DECAY, HY_W, dtype=np.float32))
    deltas = np.tile(deltas, 4)[None, :]
    n = w4.shape[1]
    tl = min(seq, 512)
    full = lambda a: pl.BlockSpec(a.shape, lambda i: (0,) * a.ndim)
    args = (w1p, b1.reshape(1, fh), w2, b2.reshape(1, fh), w3, b3.reshape(1, fh), w4, freq.reshape(1, fh),
            jnp.asarray(deltas))
    return pl.pallas_call(
        functools.partial(_hy_filter_kernel, seq=seq),
        grid=(seq // tl,),
        in_specs=[pl.BlockSpec((tl, LANE), lambda i: (i, 0))] + [full(a) for a in args],
        out_specs=[pl.BlockSpec((tl, n), lambda i: (i, 0)), pl.BlockSpec((1, n), lambda i: (0, 0))],
        out_shape=[jax.ShapeDtypeStruct((seq, n), F32), jax.ShapeDtypeStruct((1, n), F32)],
        compiler_params=_cparams(1),
        name="hyena_filter_taps",
    )(jnp.asarray(emb), *args)


def _fft_split(seq):
    n = 2 * seq
    n1 = FFT_N1 if n > 1024 else 1
    return n, n1, n // n1


def _dft_tables(seq):
    n, n1, n2 = _fft_split(seq)
    f2 = np.arange(n2)[:, None]
    t2 = np.arange(n2 // 2)[None, :]
    ph = 2.0 * np.pi * ((f2 * t2) % n2) / n2
    w_fwd = np.concatenate([np.cos(ph), -np.sin(ph)], axis=0)
    w_inv = np.concatenate([np.cos(ph.T), -np.sin(ph.T)], axis=1) / n
    out = dict(w_fwd=jnp.asarray(w_fwd, BF16), w_inv=jnp.asarray(w_inv, BF16))
    if n1 > 1:
        f1 = np.arange(n1)[None, :, None]
        t1 = np.arange(n1)[None, None, :]
        ff2 = np.arange(n2)[:, None, None]
        ph = 2.0 * np.pi * ((t1 * f1 * n2 + t1 * ff2) % n) / n
        gr, gi = np.cos(ph), -np.sin(ph)
        m_fwd = np.concatenate([np.concatenate([gr, -gi], axis=2), np.concatenate([gi, gr], axis=2)], axis=1)
        out["m_fwd"] = jnp.asarray(m_fwd, BF16)
        out["m_inv"] = jnp.asarray(np.swapaxes(m_fwd, 1, 2), BF16)
    return out


def _dft_rows_kernel(w_ref, x_ref, o_ref):
    o_ref[0] = _dot(w_ref[...], x_ref[0].astype(BF16)).astype(o_ref.dtype)


def _dft_rows(w, x, tn):
    nb, k, cols = x.shape
    m = w.shape[0]
    return pl.pallas_call(
        _dft_rows_kernel,
        grid=(nb, cols // tn),
        in_specs=[pl.BlockSpec((m, k), lambda b, j: (0, 0)),
                  pl.BlockSpec((1, k, tn), lambda b, j: (b, 0, j))],
        out_specs=pl.BlockSpec((1, m, tn), lambda b, j: (b, 0, j)),
        out_shape=jax.ShapeDtypeStruct((nb, m, cols), BF16),
        compiler_params=_cparams(2),
        name="hyena_dft_rows",
    )(w, x)


def _idft_gate_kernel(w_ref, b_ref, xg_ref, z_ref, sk_ref, o_ref):
    y = _dot(w_ref[...], b_ref[0])
    z = z_ref[0]
    o_ref[0] = xg_ref[0] * (y + z * sk_ref[...])


def _idft_gate(w, bc, xg, z, skip_t, tn):
    nb, k, cols = bc.shape
    m = w.shape[0]
    return pl.pallas_call(
        _idft_gate_kernel,
        grid=(nb, cols // tn),
        in_specs=[pl.BlockSpec((m, k), lambda b, j: (0, 0)),
                  pl.BlockSpec((1, k, tn), lambda b, j: (b, 0, j)),
                  pl.BlockSpec((1, m, tn), lambda b, j: (b, 0, j)),
                  pl.BlockSpec((1, m, tn), lambda b, j: (b, 0, j)),
                  pl.BlockSpec((1, tn), lambda b, j: (0, j))],
        out_specs=pl.BlockSpec((1, m, tn), lambda b, j: (b, 0, j)),
        out_shape=jax.ShapeDtypeStruct((nb, m, cols), F32),
        compiler_params=_cparams(2),
        name="hyena_idft_gate",
    )(w, bc, xg, z, skip_t)


def _fft_mid_filter_kernel(*refs, two_stage):
    if two_stage:
        af_ref, ab_ref, nf_ref, nb_ref, mf_ref, o_ref = refs
    else:
        af_ref, ab_ref, nf_ref, nb_ref, o_ref = refs
    inv = 1.0 / (nf_ref[...] + nb_ref[...])
    n1 = af_ref.shape[3]
    for i in range(af_ref.shape[2]):
        if two_stage:
            rhs = jnp.concatenate([af_ref[0, 0, i], ab_ref[0, 0, i]], axis=1)
            rhs_i = jnp.concatenate([af_ref[0, 1, i], ab_ref[0, 1, i]], axis=1)
            x = _dot(mf_ref[i], jnp.concatenate([rhs, rhs_i], axis=0))
            c = x.shape[1] // 2
            fr, fi, br, bi = x[:n1, :c], x[n1:, :c], x[:n1, c:], x[n1:, c:]
        else:
            fr, fi = af_ref[0, 0, i].astype(F32), af_ref[0, 1, i].astype(F32)
            br, bi = ab_ref[0, 0, i].astype(F32), ab_ref[0, 1, i].astype(F32)
        o_ref[0, i] = (fr + br) * inv
        o_ref[1, i] = (fi - bi) * inv


def _fft_mid_conv_kernel(*refs, two_stage):
    if two_stage:
        a_ref, k_ref, mf_ref, mi_ref, o_ref = refs
    else:
        a_ref, k_ref, o_ref = refs
    n1 = a_ref.shape[3]
    for i in range(a_ref.shape[2]):
        if two_stage:
            x = _dot(mf_ref[i], jnp.concatenate([a_ref[0, 0, i], a_ref[0, 1, i]], axis=0))
            xr, xi = x[:n1], x[n1:]
        else:
            xr, xi = a_ref[0, 0, i].astype(F32), a_ref[0, 1, i].astype(F32)
        kr, ki = k_ref[0, i], k_ref[1, i]
        yr = xr * kr - xi * ki
        yi = xr * ki + xi * kr
        if two_stage:
            y = _dot(mi_ref[i], jnp.concatenate([yr, yi], axis=0).astype(BF16))
            yr, yi = y[:n1], y[n1:]
        o_ref[0, 0, i] = yr.astype(o_ref.dtype)
        o_ref[0, 1, i] = yi.astype(o_ref.dtype)


def _mid_layout(seq):
    n, n1, n2 = _fft_split(seq)
    if n1 > 1:
        return n1, n2, 8
    return 8, n2 // 8, 8


def _hyena_filter_spectrum(seq, taps, nrm, tabs):
    n, n1, n2 = _fft_split(seq)
    r1, g2, fb = _mid_layout(seq)
    cf = taps.shape[1]
    a = _dft_rows(tabs["w_fwd"], taps.reshape(1, n2 // 2, n1 * cf), tn=min(n1 * cf, 4096))
    a = a.reshape(1, 2, g2, r1, cf)
    cb = 256
    nblk = cf // 2 // cb
    two_stage = n1 > 1
    in_specs = [pl.BlockSpec((1, 2, fb, r1, cb), lambda i, j: (0, 0, i, 0, j)),
                pl.BlockSpec((1, 2, fb, r1, cb), lambda i, j: (0, 0, i, 0, j + nblk)),
                pl.BlockSpec((1, cb), lambda i, j: (0, j)),
                pl.BlockSpec((1, cb), lambda i, j: (0, j + nblk))]
    args = [a, a, nrm, nrm]
    if two_stage:
        in_specs.append(pl.BlockSpec((fb, 2 * n1, 2 * n1), lambda i, j: (i, 0, 0)))
        args.append(tabs["m_fwd"])
    return pl.pallas_call(
        functools.partial(_fft_mid_filter_kernel, two_stage=two_stage),
        grid=(g2 // fb, nblk),
        in_specs=in_specs,
        out_specs=pl.BlockSpec((2, fb, r1, cb), lambda i, j: (0, i, 0, j)),
        out_shape=jax.ShapeDtypeStruct((2, g2, r1, cf // 2), F32),
        compiler_params=_cparams(2),
        name="hyena_filter_spectrum",
    )(*args)


def _hyena_conv_gate(seq, z, xg, spec, order, skip, tabs):
    n, n1, n2 = _fft_split(seq)
    r1, g2, fb = _mid_layout(seq)
    nb, _, c = z.shape
    cols = n1 * c
    tn = min(cols, 4096)
    a = _dft_rows(tabs["w_fwd"], z.reshape(nb, n2 // 2, cols), tn=tn).reshape(nb, 2, g2, r1, c)
    two_stage = n1 > 1
    in_specs = [pl.BlockSpec((1, 2, fb, r1, c), lambda b, i: (b, 0, i, 0, 0)),
                pl.BlockSpec((2, fb, r1, c), lambda b, i: (0, i, 0, order))]
    args = [a, spec]
    if two_stage:
        in_specs += [pl.BlockSpec((fb, 2 * n1, 2 * n1), lambda b, i: (i, 0, 0))] * 2
        args += [tabs["m_fwd"], tabs["m_inv"]]
    bc = pl.pallas_call(
        functools.partial(_fft_mid_conv_kernel, two_stage=two_stage),
        grid=(nb, g2 // fb),
        in_specs=in_specs,
        out_specs=pl.BlockSpec((1, 2, fb, r1, c), lambda b, i: (b, 0, i, 0, 0)),
        out_shape=jax.ShapeDtypeStruct((nb, 2, g2, r1, c), BF16),
        compiler_params=_cparams(2),
        name="hyena_fft_mid",
    )(*args)
    skip_t = jnp.tile(skip[order].reshape(1, c), (1, n1))
    out = _idft_gate(tabs["w_inv"], bc.reshape(nb, 2 * n2, cols), xg.reshape(nb, n2 // 2, cols),
                     z.reshape(nb, n2 // 2, cols), skip_t, tn=tn)
    return out.reshape(nb, seq, c)


def _hyena(p, lw, row0, seq, n_batch):
    tabs = _dft_tables(seq)
    taps, nrm = _hyena_filter_taps(seq, lw["hy_w1"], lw["hy_b1"], lw["hy_w2"], lw["hy_b2"], lw["hy_w3"],
                                   lw["hy_b3"], lw["hy_w4"], lw["hy_freq"])
    spec = _hyena_filter_spectrum(seq, taps, nrm, tabs)
    uc = _short_conv(p, lw["hy_conv_w"], lw["hy_conv_b"], row0, seq, n_batch)
    z = uc[2]
    for o in range(2):
        z = _hyena_conv_gate(seq, z, uc[o], spec, o, lw["hy_skip"], tabs)
    return z.reshape(n_batch * seq, HY_W)


def _mla_qkv_kernel(qa_ref, kva_ref, kr_ref, cs_ref, wq_ref, wkv_ref, gqa_ref, gkva_ref, gqn_ref, gqr_ref,
                    gkn_ref, gkr_ref, q_ref, k_ref, v_ref):
    cs = cs_ref[...]
    lane = lax.broadcasted_iota(jnp.int32, cs.shape, 1)
    low = lane < MLA_ROPE

    def rope(pair, gain2):
        ms = jnp.sum(jnp.where(low, pair * pair, 0.0), axis=-1, keepdims=True) * (1.0 / MLA_ROPE)
        t = pair * lax.rsqrt(ms + RMS_EPS) * gain2 * cs
        return jnp.where(low, t + pltpu.roll(t, MLA_ROPE, 1), 0.0)

    qq = _dot(_rms(qa_ref[...], gqa_ref[...]).astype(BF16), wq_ref[...])
    kv = _dot(_rms(kva_ref[...], gkva_ref[...]).astype(BF16), wkv_ref[...])
    kr = rope(kr_ref[...], gkr_ref[...])
    hp = MLA_HEAD_PAD
    for h in range(MLA_HEADS):
        qn = _rms(qq[:, h * hp:h * hp + MLA_NOPE], gqn_ref[...])
        qr = rope(qq[:, h * hp + MLA_NOPE:(h + 1) * hp], gqr_ref[...])
        q_ref[h, :, 0:LANE] = (qn * MLA_SCALE).astype(BF16)
        q_ref[h, :, LANE:2 * LANE] = (qr * MLA_SCALE).astype(BF16)
        kn = _rms(kv[:, h * hp:h * hp + MLA_NOPE], gkn_ref[...])
        k_ref[h, :, 0:LANE] = kn.astype(BF16)
        k_ref[h, :, LANE:2 * LANE] = kr.astype(BF16)
        v_ref[h] = kv[:, h * hp + MLA_NOPE:(h + 1) * hp].astype(BF16)


def _rope_table(seq, n_batch, ctx_rows):
    rows = seq // GRID_W
    row = np.repeat(np.arange(rows, dtype=np.float32), GRID_W)
    col = np.tile(np.arange(GRID_W, dtype=np.float32), rows)
    half = MLA_ROPE // 2
    inv = (ROPE_THETA ** (-np.arange(0, half, 2, dtype=np.float32) / half)).astype(np.float32)
    ar = (row[:, None] * inv).astype(np.float64)
    ac = (col[:, None] * inv).astype(np.float64)
    cos = np.concatenate([np.cos(ar), np.cos(ar), np.cos(ac), np.cos(ac)], axis=1)
    sin = np.concatenate([-np.sin(ar), np.sin(ar), -np.sin(ac), np.sin(ac)], axis=1)
    lat = np.tile(np.concatenate([cos, sin], axis=1), (n_batch, 1))
    ctx = np.concatenate([np.ones((ctx_rows, MLA_ROPE)), np.zeros((ctx_rows, MLA_ROPE))], axis=1)
    return jnp.asarray(np.concatenate([lat, ctx], axis=0), F32)


def _pair_gain(g):
    return jnp.concatenate([g, _rope_swap(g)]).reshape(1, 2 * MLA_ROPE)


def _mla_qkv(p, cs, lw, tm=512):
    n_rows = p.shape[0]
    hd, hp = MLA_HEADS, MLA_HEAD_PAD
    w_uq = lw["w_uq"].reshape(-1, hd, MLA_QK)
    wq = jnp.concatenate([w_uq, _rope_swap(w_uq[..., MLA_NOPE:])], axis=-1).reshape(-1, hd * hp).astype(BF16)
    wkv = lw["w_ukv"].astype(BF16)
    ql, kvl = wq.shape[0], wkv.shape[0]
    vec = lambda a: a.reshape(1, -1)
    full = lambda a: pl.BlockSpec(a.shape, lambda i: (0,) * a.ndim)
    args = (wq, wkv, vec(lw["q_a_norm"]), vec(lw["kv_a_norm"]), vec(lw["q_nope_norm"]),
            _pair_gain(lw["q_rope_norm"]), vec(lw["k_nope_norm"]), _pair_gain(lw["k_rope_norm"]))
    return pl.pallas_call(
        _mla_qkv_kernel,
        grid=(n_rows // tm,),
        in_specs=[pl.BlockSpec((tm, ql), lambda i: (i, COL_QA // ql)),
                  pl.BlockSpec((tm, kvl), lambda i: (i, COL_KVA // kvl)),
                  pl.BlockSpec((tm, LANE), lambda i: (i, COL_KR // LANE)),
                  pl.BlockSpec((tm, LANE), lambda i: (i, 0))] + [full(a) for a in args],
        out_specs=[pl.BlockSpec((hd, tm, hp), lambda i: (0, i, 0)),
                   pl.BlockSpec((hd, tm, hp), lambda i: (0, i, 0)),
                   pl.BlockSpec((hd, tm, MLA_V), lambda i: (0, i, 0))],
        out_shape=[jax.ShapeDtypeStruct((hd, n_rows, hp), BF16),
                   jax.ShapeDtypeStruct((hd, n_rows, hp), BF16),
                   jax.ShapeDtypeStruct((hd, n_rows, MLA_V), BF16)],
        compiler_params=_cparams(1),
        name="mla_qkv",
    )(p, p, p, cs, *args)


def _attn_kernel(*refs, with_latent):
    if with_latent:
        q_ref, k_ref, v_ref, kc_ref, vc_ref, o_ref = refs
    else:
        q_ref, kc_ref, vc_ref, o_ref = refs
    q = q_ref[0]
    sc = _dot_nt(q, kc_ref[0])
    m = jnp.max(sc, axis=-1, keepdims=True)
    if with_latent:
        sl = _dot_nt(q, k_ref[0])
        m = jnp.maximum(m, jnp.max(sl, axis=-1, keepdims=True))
    pc = jnp.exp(sc - m)
    den = jnp.sum(pc, axis=-1, keepdims=True)
    acc = _dot(pc.astype(BF16), vc_ref[0])
    if with_latent:
        pl_ = jnp.exp(sl - m)
        den = den + jnp.sum(pl_, axis=-1, keepdims=True)
        acc = acc + _dot(pl_.astype(BF16), v_ref[0])
    o_ref[...] = acc / den


def _attention(q, k, v, seq, ctx_len, n_batch, latent, tq=256):
    hd, n_rows, hp = q.shape
    lat_rows = n_batch * seq
    kc_spec = pl.BlockSpec((1, ctx_len, hp), lambda b, h, i: (h, lat_rows // ctx_len + b, 0))
    vc_spec = pl.BlockSpec((1, ctx_len, MLA_V), lambda b, h, i: (h, lat_rows // ctx_len + b, 0))
    if latent:
        nq = seq // tq
        in_specs = [pl.BlockSpec((1, tq, hp), lambda b, h, i: (h, b * nq + i, 0)),
                    pl.BlockSpec((1, seq, hp), lambda b, h, i: (h, b, 0)),
                    pl.BlockSpec((1, seq, MLA_V), lambda b, h, i: (h, b, 0)), kc_spec, vc_spec]
        args = (q, k, v, k, v)
        out_spec = pl.BlockSpec((tq, MLA_V), lambda b, h, i: (b * nq + i, h))
        out_rows = lat_rows
    else:
        nq = 1
        tq = ctx_len
        in_specs = [pl.BlockSpec((1, tq, hp), lambda b, h, i: (h, lat_rows // ctx_len + b, 0)), kc_spec, vc_spec]
        args = (q, k, v)
        out_spec = pl.BlockSpec((tq, MLA_V), lambda b, h, i: (b, h))
        out_rows = n_batch * ctx_len
    return pl.pallas_call(
        functools.partial(_attn_kernel, with_latent=latent),
        grid=(n_batch, hd, nq),
        in_specs=in_specs,
        out_specs=out_spec,
        out_shape=jax.ShapeDtypeStruct((out_rows, hd * MLA_V), F32),
        compiler_params=_cparams(3),
        name="mla_attention" if latent else "mla_attention_ctx",
    )(*args)


def _merge_kernel(x_ref, mod_ref, of_ref, ob_ref, g_ref, yb_ref, yc_ref, ga_ref, gb_ref, gc_ref,
                  gain_ref, wa_ref, wb_ref, wc_ref, wo_ref, o_ref):
    m = mod_ref[0]
    o = of_ref[...] + ob_ref[...]
    gain = gain_ref[...]
    ya = jnp.concatenate([_rms(o[:, h * HG_D:(h + 1) * HG_D], gain) for h in range(HG_HEADS)], axis=1)
    ya = (ya * _silu(g_ref[...])).astype(BF16)
    mix = (_sigmoid(ga_ref[...]) * _dot(ya, wa_ref[...])
           + _sigmoid(gb_ref[...]) * _dot(yb_ref[...].astype(BF16), wb_ref[...])
           + _sigmoid(gc_ref[...]) * _dot(yc_ref[...].astype(BF16), wc_ref[...]))
    o_ref[...] = x_ref[...] + m[5:6] * _dot(mix.astype(BF16), wo_ref[...])


def _merge(xa, mod, o_f, o_b, p, y_b, y_c, lw, n_rows, seq, n_batch, tm=512):
    d = xa.shape[1]
    hk = HG_HEADS * HG_D
    row = lambda w: pl.BlockSpec((tm, w), lambda i: (i, 0))
    pcol = lambda w, c: pl.BlockSpec((tm, w), lambda i: (i, c // w))
    full = lambda a: pl.BlockSpec(a.shape, lambda i: (0,) * a.ndim)
    ws = (lw["hg_out_norm"].reshape(1, HG_D), lw["w_br_a"].astype(BF16), lw["w_br_b"].astype(BF16),
          lw["w_br_c"].astype(BF16), lw["w_out"].astype(BF16))
    return pl.pallas_call(
        _merge_kernel,
        grid=(n_rows // tm,),
        in_specs=[row(d), pl.BlockSpec((1, N_MOD, d), _group_map(seq // tm, n_batch)),
                  row(hk), row(hk), pcol(hk, COL_G), row(HY_W), row(MLA_HEADS * MLA_V),
                  pcol(d, COL_GA), pcol(d, COL_GB), pcol(d, COL_GC)] + [full(a) for a in ws],
        out_specs=row(d),
        out_shape=jax.ShapeDtypeStruct((n_rows, d), F32),
        compiler_params=_cparams(1),
        name="merge",
    )(xa, mod, o_f, o_b, p, y_b, y_c, p, p, p, *ws)


def kernel(x, c, ctx, c_ctx, ada_w, ada_b, ffn1_norm, ffn1_w13, ffn1_w2, mix_norm, w_in, hg_lb_logits, hg_out_norm, hy_conv_w, hy_conv_b, hy_w1, hy_b1, hy_w2, hy_b2, hy_w3, hy_b3, hy_w4, hy_freq, hy_skip, q_a_norm, w_uq, kv_a_norm, w_ukv, q_nope_norm, q_rope_norm, k_nope_norm, k_rope_norm, w_br_a, w_br_b, w_br_c, w_out, ffn2_norm, ffn2_w13, ffn2_w2):
    stacked = dict(
        ada_w=ada_w, ada_b=ada_b, ffn1_norm=ffn1_norm, ffn1_w13=ffn1_w13, ffn1_w2=ffn1_w2, mix_norm=mix_norm,
        w_in=w_in, hg_out_norm=hg_out_norm, hy_conv_w=hy_conv_w, hy_conv_b=hy_conv_b, hy_w1=hy_w1, hy_b1=hy_b1,
        hy_w2=hy_w2, hy_b2=hy_b2, hy_w3=hy_w3, hy_b3=hy_b3, hy_w4=hy_w4, hy_freq=hy_freq, hy_skip=hy_skip,
        q_a_norm=q_a_norm, w_uq=w_uq, kv_a_norm=kv_a_norm, w_ukv=w_ukv, q_nope_norm=q_nope_norm,
        q_rope_norm=q_rope_norm, k_nope_norm=k_nope_norm, k_rope_norm=k_rope_norm, w_br_a=w_br_a,
        w_br_b=w_br_b, w_br_c=w_br_c, w_out=w_out, ffn2_norm=ffn2_norm, ffn2_w13=ffn2_w13, ffn2_w2=ffn2_w2)
    n_batch, seq, d = x.shape
    ctx_len = ctx.shape[1]
    depth = ada_w.shape[0]
    lat_rows, ctx_rows = n_batch * seq, n_batch * ctx_len
    all_rows = lat_rows + ctx_rows
    assert seq % 512 == 0 and ctx_rows % 512 == 0 and seq % ctx_len == 0 and seq % GRID_W == 0
    assert ctx_len % HG_CHUNK == 0 and n_batch < 8

    xa = jnp.concatenate([x.reshape(lat_rows, d), ctx.reshape(ctx_rows, d)], axis=0)
    cs = jnp.concatenate([c, c_ctx.reshape(1, d), jnp.zeros((7 - n_batch, d), F32)], axis=0)
    rope_cs = _rope_table(seq, n_batch, ctx_rows)
    zero_state = jnp.zeros((n_batch, HG_HEADS, HG_D, HG_D), F32)

    for l in range(depth):
        lw = {name: val[l] for name, val in stacked.items()}
        need_ctx = l < depth - 1
        mod = _modulation(cs, lw["ada_w"], lw["ada_b"])
        xa = _half_ffn(xa, mod, lw["ffn1_norm"], lw["ffn1_w13"], lw["ffn1_w2"], 0, all_rows, seq, n_batch)
        p = _in_projection(xa, mod, lw["mix_norm"], _pack_w_in(lw["w_in"]).astype(BF16), seq, n_batch)

        mix_rows = all_rows if need_ctx else lat_rows
        o_dir = []
        for rev in (False, True):
            lg = hg_lb_logits[1 if rev else 0]
            o_c, s_c = _hgrn2_scan(p, lg, zero_state, l, rev, lat_rows, ctx_len, n_batch)
            o_l, _ = _hgrn2_scan(p, lg, s_c, l, rev, 0, seq, n_batch)
            o_dir.append(jnp.concatenate([o_l, o_c], axis=0) if need_ctx else o_l)

        y_b = _hyena(p, lw, 0, seq, n_batch)
        if need_ctx:
            y_b = jnp.concatenate([y_b, _hyena(p, lw, lat_rows, ctx_len, n_batch)], axis=0)

        q, k, v = _mla_qkv(p, rope_cs, lw)
        y_c = _attention(q, k, v, seq, ctx_len, n_batch, latent=True)
        if need_ctx:
            y_c = jnp.concatenate([y_c, _attention(q, k, v, seq, ctx_len, n_batch, latent=False)], axis=0)

        xa = _merge(xa, mod, o_dir[0], o_dir[1], p, y_b, y_c, lw, mix_rows, seq, n_batch)
        xa = _half_ffn(xa, mod, lw["ffn2_norm"], lw["ffn2_w13"], lw["ffn2_w2"], 6, mix_rows, seq, n_batch)
    return xa[:lat_rows].reshape(n_batch, seq, d)
```

```python
import functools
import math

import numpy as np
import jax
import jax.numpy as jnp
from jax import lax
from jax.experimental import pallas as pl
from jax.experimental.pallas import tpu as pltpu

F32 = jnp.float32
BF16 = jnp.bfloat16

RMS_EPS = 1e-6
N_MOD = 9
GRID_W = 64
ROPE_THETA = 10000.0
HG_HEADS = 4
HG_D = 128
HG_CHUNK = 128
HY_W = 512
HY_TARGET = 1e-2
HY_MIN_DECAY = math.log(HY_TARGET) / 1.5
HY_MAX_DECAY = math.log(HY_TARGET) / 0.3
HY_SHIFT = 0.05
MLA_HEADS = 4
MLA_NOPE = 128
MLA_ROPE = 64
MLA_V = 128
MLA_QK = MLA_NOPE + MLA_ROPE
MLA_SCALE = MLA_QK ** -0.5
MLA_HEAD_PAD = 256
FFT_N1 = 128
LANE = 128

VMEM_LIMIT = 52 * 1024 * 1024

COL_GA, COL_GB, COL_GC = 0, 1024, 2048
COL_Q, COL_ZF, COL_ZB, COL_IV, COL_G = 3072, 3584, 4096, 4608, 5120
COL_HY = 5632
COL_QA, COL_KVA, COL_KR = 7168, 7424, 7552
IN_PACKED = 7680


def _cparams(n_axes):
    return pltpu.CompilerParams(dimension_semantics=("arbitrary",) * n_axes,
                                vmem_limit_bytes=VMEM_LIMIT)


def _dot(a, b):
    return jnp.dot(a, b, preferred_element_type=F32)


def _dot_nt(a, b):
    return lax.dot_general(a, b, (((1,), (1,)), ((), ())), preferred_element_type=F32)


def _dot_tn(a, b):
    return lax.dot_general(a, b, (((0,), (0,)), ((), ())), preferred_element_type=F32)


def _sigmoid(x):
    return 1.0 / (1.0 + jnp.exp(-x))


def _silu(x):
    return x * _sigmoid(x)


def _rms(x, gain):
    return x * lax.rsqrt(jnp.mean(x * x, axis=-1, keepdims=True) + RMS_EPS) * gain


def _mod_kernel(c_ref, w_ref, b_ref, o_ref):
    o_ref[...] = _dot(_silu(c_ref[...]).astype(BF16), w_ref[...]) + b_ref[...]


def _modulation(cs, ada_w, ada_b):
    g, d = cs.shape
    n = ada_w.shape[1]
    tn = n // 4
    out = pl.pallas_call(
        _mod_kernel,
        grid=(n // tn,),
        in_specs=[pl.BlockSpec((g, d), lambda j: (0, 0)),
                  pl.BlockSpec((d, tn), lambda j: (0, j)),
                  pl.BlockSpec((1, tn), lambda j: (0, j))],
        out_specs=pl.BlockSpec((g, tn), lambda j: (0, j)),
        out_shape=jax.ShapeDtypeStruct((g, n), F32),
        compiler_params=_cparams(1),
        name="modulation",
    )(cs, ada_w.astype(BF16), ada_b.reshape(1, n))
    return out.reshape(g, N_MOD, d)


def _ffn_kernel(x_ref, mod_ref, g_ref, w13_ref, w2_ref, o_ref, *, idx, ff, n_chunks):
    x = x_ref[...]
    m = mod_ref[0]
    h = (_rms(x, g_ref[...]) * (1.0 + m[idx + 1:idx + 2]) + m[idx:idx + 1]).astype(BF16)
    ck = ff // n_chunks
    acc = jnp.zeros(x.shape, F32)
    for j in range(n_chunks):
        a = _dot(h, w13_ref[:, j * ck:(j + 1) * ck])
        b = _dot(h, w13_ref[:, ff + j * ck:ff + (j + 1) * ck])
        acc = acc + _dot((_silu(a) * b).astype(BF16), w2_ref[j * ck:(j + 1) * ck, :])
    o_ref[...] = x + (0.5 * m[idx + 2:idx + 3]) * acc


def _group_map(tiles_per_batch, n_batch):
    return lambda i: (jnp.minimum(i // tiles_per_batch, n_batch), 0, 0)


def _half_ffn(xa, mod, gain, w13, w2, idx, n_rows, seq, n_batch, tm=512):
    d = xa.shape[1]
    ff = w2.shape[0]
    n_chunks = ff // 256
    const = dict(pipeline_mode=pl.Buffered(1))
    return pl.pallas_call(
        functools.partial(_ffn_kernel, idx=idx, ff=ff, n_chunks=n_chunks),
        grid=(n_rows // tm,),
        in_specs=[pl.BlockSpec((tm, d), lambda i: (i, 0)),
                  pl.BlockSpec((1, N_MOD, d), _group_map(seq // tm, n_batch)),
                  pl.BlockSpec((1, d), lambda i: (0, 0)),
                  pl.BlockSpec((d, 2 * ff), lambda i: (0, 0), **const),
                  pl.BlockSpec((ff, d), lambda i: (0, 0), **const)],
        out_specs=pl.BlockSpec((tm, d), lambda i: (i, 0)),
        out_shape=jax.ShapeDtypeStruct((n_rows, d), F32),
        compiler_params=_cparams(1),
        name="half_ffn",
    )(xa, mod, gain.reshape(1, d), w13.astype(BF16), w2.astype(BF16))


def _inproj_kernel(x_ref, mod_ref, g_ref, w_ref, o_ref):
    m = mod_ref[0]
    h = (_rms(x_ref[...], g_ref[...]) * (1.0 + m[4:5]) + m[3:4]).astype(BF16)
    o_ref[...] = _dot(h, w_ref[...])


def _pack_w_in(w_in):
    d = w_in.shape[0]
    hk = HG_HEADS * HG_D
    sizes = (hk, hk, hk, hk, hk, 3 * HY_W, 256, 128, MLA_ROPE, d, d, d)
    offs = np.cumsum((0,) + sizes)
    q, zf, zb, iv, g, hy, qa, kva, kr, ga, gb, gc = (w_in[:, offs[i]:offs[i + 1]] for i in range(12))
    return jnp.concatenate([ga, gb, gc, q, zf, zb, iv, g, hy, qa, kva, kr, _rope_swap(kr)], axis=1)


def _rope_swap(a):
    q = MLA_ROPE // 4
    return jnp.concatenate([a[..., q:2 * q], a[..., :q], a[..., 3 * q:], a[..., 2 * q:3 * q]], axis=-1)


def _in_projection(xa, mod, gain, w_packed, seq, n_batch, tm=512, tn=2560):
    n_rows, d = xa.shape
    n = w_packed.shape[1]
    return pl.pallas_call(
        _inproj_kernel,
        grid=(n // tn, n_rows // tm),
        in_specs=[pl.BlockSpec((tm, d), lambda j, i: (i, 0)),
                  pl.BlockSpec((1, N_MOD, d), lambda j, i: (jnp.minimum(i // (seq // tm), n_batch), 0, 0)),
                  pl.BlockSpec((1, d), lambda j, i: (0, 0)),
                  pl.BlockSpec((d, tn), lambda j, i: (0, j))],
        out_specs=pl.BlockSpec((tm, tn), lambda j, i: (i, j)),
        out_shape=jax.ShapeDtypeStruct((n_rows, n), F32),
        compiler_params=_cparams(2),
        name="in_projection",
    )(xa, mod, gain.reshape(1, d), w_packed)


def _hgrn2_tables(c, rev):
    t = np.arange(c)[:, None]
    u = np.arange(c)[None, :]
    mats = [(u <= t), (u > t)]
    masks = [(t == u)]
    h = c // 2
    while h >= 1:
        mid = (t // (2 * h)) * (2 * h) + h
        mats.append(np.where(t >= mid, (u >= mid) & (u <= t), (u >= t + 1) & (u <= mid - 1)))
        mid_s = (u // (2 * h)) * (2 * h) + h
        masks.append((t // (2 * h) == u // (2 * h)) & (u < mid_s) & (t >= mid))
        h //= 2
    mats = np.stack([m.astype(np.float32) for m in mats])
    masks = np.stack([m.astype(np.float32) for m in masks])
    if rev:
        mats = mats[:, ::-1, ::-1]
        masks = masks[:, ::-1, ::-1]
    return (jnp.asarray(np.ascontiguousarray(mats).reshape(-1, c), BF16),
            jnp.asarray(np.ascontiguousarray(masks), F32))


def _hgrn2_kernel(q_ref, z_ref, v_ref, lg_ref, dst_ref, msk_ref, s0_ref, o_ref, sf_ref, st_ref,
                  *, layer, rev, n_levels):
    j = pl.program_id(1)
    c = q_ref.shape[0]
    hk = HG_HEADS * HG_D

    @pl.when(j == 0)
    def _():
        st_ref[...] = s0_ref[0]

    lg = lg_ref[...]
    e = jnp.exp(lg - jnp.max(lg, axis=0, keepdims=True))
    sm = e / jnp.sum(e, axis=0, keepdims=True)
    lb = jnp.zeros((1, hk), F32)
    for i in range(1, layer + 1):
        lb = lb + sm[i:i + 1]

    f = lb + (1.0 - lb) * _sigmoid(z_ref[...])
    kk = 1.0 - f
    g = jnp.log(f)
    g_hi = g.astype(BF16)
    g_lo = (g - g_hi.astype(F32)).astype(BF16)
    dg = _dot(dst_ref[...], jnp.concatenate([g_hi, g_lo], axis=1))
    dg = dg[:, :hk] + dg[:, hk:]
    qs = _silu(q_ref[...]) * HG_D ** -0.5
    vv = v_ref[...]
    last = 0 if rev else c - 1

    for h in range(HG_HEADS):
        hs = slice(h * HG_D, (h + 1) * HG_D)
        qh, kh, vh = qs[:, hs], kk[:, hs], vv[:, hs].astype(BF16)
        b_in = dg[0:c, hs]
        e_out = jnp.exp(dg[c:2 * c, hs])
        a = _dot_nt(qh.astype(BF16), kh.astype(BF16)) * msk_ref[0]
        for l in range(1, n_levels + 1):
            el = jnp.exp(dg[(1 + l) * c:(2 + l) * c, hs])
            a = a + _dot_nt((qh * el).astype(BF16), (kh * el).astype(BF16)) * msk_ref[l]
        st = st_ref[h]
        o = _dot_nt((qh * jnp.exp(b_in)).astype(BF16), st.astype(BF16)) + _dot(a.astype(BF16), vh)
        o_ref[:, hs] = o
        st_ref[h] = jnp.exp(b_in[last:last + 1, :]) * st + _dot_tn(vh, (kh * e_out).astype(BF16))

    @pl.when(j == pl.num_programs(1) - 1)
    def _():
        sf_ref[0] = st_ref[...]


def _hgrn2_scan(p, logits, s0, layer, rev, row0, seq, n_batch):
    c = HG_CHUNK
    hk = HG_HEADS * HG_D
    nc = seq // c
    base = row0 // c
    n_levels = int(math.log2(c))
    dst, msk = _hgrn2_tables(c, rev)
    col_z = COL_ZB if rev else COL_ZF

    def rows(b, j):
        return b * nc + (nc - 1 - j if rev else j)

    def prow(w):
        return pl.BlockSpec((c, hk), lambda b, j: (base + rows(b, j), w // hk))

    return pl.pallas_call(
        functools.partial(_hgrn2_kernel, layer=layer, rev=rev, n_levels=n_levels),
        grid=(n_batch, nc),
        in_specs=[prow(COL_Q), prow(col_z), prow(COL_IV),
                  pl.BlockSpec(logits.shape, lambda b, j: (0, 0)),
                  pl.BlockSpec(dst.shape, lambda b, j: (0, 0)),
                  pl.BlockSpec(msk.shape, lambda b, j: (0, 0, 0)),
                  pl.BlockSpec((1, HG_HEADS, HG_D, HG_D), lambda b, j: (b, 0, 0, 0))],
        out_specs=[pl.BlockSpec((c, hk), lambda b, j: (rows(b, j), 0)),
                   pl.BlockSpec((1, HG_HEADS, HG_D, HG_D), lambda b, j: (b, 0, 0, 0))],
        out_shape=[jax.ShapeDtypeStruct((n_batch * seq, hk), F32),
                   jax.ShapeDtypeStruct((n_batch, HG_HEADS, HG_D, HG_D), F32)],
        scratch_shapes=[pltpu.VMEM((HG_HEADS, HG_D, HG_D), F32)],
        compiler_params=_cparams(2),
        name="hgrn2_scan_bwd" if rev else "hgrn2_scan_fwd",
    )(p, p, p, logits, dst, msk, s0)


def _shortconv_kernel(u_ref, w_ref, b_ref, o_ref):
    u = u_ref[...]
    n = u.shape[0]
    row = lax.broadcasted_iota(jnp.int32, u.shape, 0)
    prev = jnp.where(row == 0, 0.0, pltpu.roll(u, 1, 0))
    nxt = jnp.where(row == n - 1, 0.0, pltpu.roll(u, n - 1, 0))
    w = w_ref[...]
    o_ref[0, 0] = prev * w[0:1] + u * w[1:2] + nxt * w[2:3] + b_ref[...]


def _short_conv(p, w, b, row0, seq, n_batch):
    nb = 3 * HY_W // LANE
    per = HY_W // LANE
    return pl.pallas_call(
        _shortconv_kernel,
        grid=(n_batch, nb),
        in_specs=[pl.BlockSpec((seq, LANE), lambda bi, j: (row0 // seq + bi, COL_HY // LANE + j)),
                  pl.BlockSpec((3, LANE), lambda bi, j: (0, j)),
                  pl.BlockSpec((1, LANE), lambda bi, j: (0, j))],
        out_specs=pl.BlockSpec((1, 1, seq, LANE), lambda bi, j: (j // per, bi, 0, j % per)),
        out_shape=jax.ShapeDtypeStruct((3, n_batch, seq, HY_W), F32),
        compiler_params=_cparams(2),
        name="hyena_short_conv",
    )(p, w, b.reshape(1, -1))


def _hy_filter_kernel(emb_ref, w1_ref, b1_ref, w2_ref, b2_ref, w3_ref, b3_ref, w4_ref, fr_ref, dl_ref,
                      o_ref, nrm_ref, *, seq):
    i = pl.program_id(0)
    hp = lax.Precision.HIGHEST
    fr = fr_ref[...]
    hid = jnp.sin(fr * (jnp.dot(emb_ref[...], w1_ref[...], precision=hp) + b1_ref[...]))
    hid = jnp.sin(fr * (jnp.dot(hid, w2_ref[...], precision=hp) + b2_ref[...]))
    hid = jnp.sin(fr * (jnp.dot(hid, w3_ref[...], precision=hp) + b3_ref[...]))
    h = jnp.dot(hid, w4_ref[...], precision=hp)
    tl, n = h.shape
    pos = lax.broadcasted_iota(jnp.int32, h.shape, 0) + i * tl
    col = lax.broadcasted_iota(jnp.int32, h.shape, 1)
    t = pos.astype(F32) * (1.0 / (seq - 1))
    h = h * (jnp.exp(-t * dl_ref[...]) + HY_SHIFT)
    h = jnp.where((pos == 0) & (col >= n // 2), 0.0, h)
    o_ref[...] = h

    @pl.when(i == 0)
    def _():
        nrm_ref[...] = jnp.zeros_like(nrm_ref)

    nrm_ref[...] += jnp.sum(jnp.abs(h), axis=0, keepdims=True)


def _hyena_filter_taps(seq, w1, b1, w2, b2, w3, b3, w4, freq):
    fh = w1.shape[1]
    n_emb = w1.shape[0]
    bands_n = (n_emb - 1) // 2
    tt = np.linspace(0.0, 1.0, seq, dtype=np.float32)[:, None].astype(np.float64)
    ww = (2.0 * math.pi / seq) * np.arange(seq, dtype=np.float64)[:, None]
    bands = np.linspace(1e-4, bands_n - 1, bands_n, dtype=np.float32)[None, :].astype(np.float64)
    emb = np.concatenate([tt, np.cos(bands * ww), -np.sin(bands * ww)], axis=-1)
    emb = np.pad(emb, ((0, 0), (0, LANE - n_emb))).astype(np.float32)
    w1p = jnp.pad(w1, ((0, LANE - n_emb), (0, 0)))
    deltas = np.abs(np.linspace(HY_MIN_DECAY, HY_MAX_DECAY, HY_W, dtype=np.float32))
    deltas = np.tile(deltas, 4)[None, :]
    n = w4.shape[1]
    tl = min(seq, 512)
    full = lambda a: pl.BlockSpec(a.shape, lambda i: (0,) * a.ndim)
    args = (w1p, b1.reshape(1, fh), w2, b2.reshape(1, fh), w3, b3.reshape(1, fh), w4, freq.reshape(1, fh),
            jnp.asarray(deltas))
    return pl.pallas_call(
        functools.partial(_hy_filter_kernel, seq=seq),
        grid=(seq // tl,),
        in_specs=[pl.BlockSpec((tl, LANE), lambda i: (i, 0))] + [full(a) for a in args],
        out_specs=[pl.BlockSpec((tl, n), lambda i: (i, 0)), pl.BlockSpec((1, n), lambda i: (0, 0))],
        out_shape=[jax.ShapeDtypeStruct((seq, n), F32), jax.ShapeDtypeStruct((1, n), F32)],
        compiler_params=_cparams(1),
        name="hyena_filter_taps",
    )(jnp.asarray(emb), *args)


def _fft_split(seq):
    n = 2 * seq
    n1 = FFT_N1 if n > 1024 else 1
    return n, n1, n // n1


def _cis(idx, n):
    ph = 2.0 * np.pi * (idx % n) / n
    return np.cos(ph), -np.sin(ph)


def _dft_tables_short(seq):
    n = 2 * seq
    cr, ci = _cis(np.arange(n)[:, None] * np.arange(seq)[None, :], n)
    w_fwd = np.concatenate([cr, ci], axis=0)
    w_inv = np.concatenate([cr.T, ci.T], axis=1) / n
    return dict(w_fwd=jnp.asarray(w_fwd, BF16), w_inv=jnp.asarray(w_inv, BF16))


def _dft_tables_long(seq):
    n, n1, n2 = _fft_split(seq)
    t1 = np.arange(n1)[:, None, None]
    f2 = np.arange(n2)[None, :, None]
    t2 = np.arange(n2 // 2)[None, None, :]
    cr, ci = _cis(f2 * (t1 + n1 * t2), n)
    w1 = np.concatenate([cr, ci], axis=1)
    w4 = np.concatenate([np.swapaxes(cr, 1, 2), np.swapaxes(ci, 1, 2)], axis=2) / n
    j = (n1 - t1) % n1 + n1 * t2
    br, bi = _cis(f2 * (n - j), n)
    w1f = np.concatenate([np.concatenate([cr, br], axis=2), np.concatenate([ci, bi], axis=2)], axis=1)
    gr, gi = _cis(np.arange(n1)[:, None] * np.arange(n1)[None, :], n1)
    wd = np.concatenate([np.concatenate([gr, -gi], axis=1), np.concatenate([gi, gr], axis=1)], axis=0)
    return dict(w1=jnp.asarray(w1, BF16), w4=jnp.asarray(w4, BF16), w1f=jnp.asarray(w1f, BF16),
                wd=jnp.asarray(wd, BF16), wdi=jnp.asarray(wd.T, BF16))


def _dft_rows_kernel(w_ref, x_ref, o_ref):
    o_ref[0] = _dot(w_ref[...], x_ref[0].astype(BF16)).astype(o_ref.dtype)


def _dft_rows(w, x, tn):
    nb, k, cols = x.shape
    m = w.shape[0]
    return pl.pallas_call(
        _dft_rows_kernel,
        grid=(nb, cols // tn),
        in_specs=[pl.BlockSpec((m, k), lambda b, j: (0, 0)),
                  pl.BlockSpec((1, k, tn), lambda b, j: (b, 0, j))],
        out_specs=pl.BlockSpec((1, m, tn), lambda b, j: (b, 0, j)),
        out_shape=jax.ShapeDtypeStruct((nb, m, cols), BF16),
        compiler_params=_cparams(2),
        name="hyena_dft_rows",
    )(w, x)


def _idft_gate_kernel(w_ref, b_ref, xg_ref, z_ref, sk_ref, o_ref):
    y = _dot(w_ref[...], b_ref[0])
    z = z_ref[0]
    o_ref[0] = xg_ref[0] * (y + z * sk_ref[...])


def _idft_gate(w, bc, xg, z, skip_t, tn):
    nb, k, cols = bc.shape
    m = w.shape[0]
    return pl.pallas_call(
        _idft_gate_kernel,
        grid=(nb, cols // tn),
        in_specs=[pl.BlockSpec((m, k), lambda b, j: (0, 0)),
                  pl.BlockSpec((1, k, tn), lambda b, j: (b, 0, j)),
                  pl.BlockSpec((1, m, tn), lambda b, j: (b, 0, j)),
                  pl.BlockSpec((1, m, tn), lambda b, j: (b, 0, j)),
                  pl.BlockSpec((1, tn), lambda b, j: (0, j))],
        out_specs=pl.BlockSpec((1, m, tn), lambda b, j: (b, 0, j)),
        out_shape=jax.ShapeDtypeStruct((nb, m, cols), F32),
        compiler_params=_cparams(2),
        name="hyena_idft_gate",
    )(w, bc, xg, z, skip_t)


def _spec_combine_kernel(af_ref, ab_ref, nf_ref, nb_ref, o_ref):
    inv = 1.0 / (nf_ref[...] + nb_ref[...])
    n = o_ref.shape[1]
    o_ref[0] = (af_ref[0, :n].astype(F32) + ab_ref[0, :n].astype(F32)) * inv
    o_ref[1] = (af_ref[0, n:].astype(F32) - ab_ref[0, n:].astype(F32)) * inv


def _spec_mul_kernel(a_ref, k_ref, o_ref):
    n = k_ref.shape[1]
    xr, xi = a_ref[0, :n].astype(F32), a_ref[0, n:].astype(F32)
    kr, ki = k_ref[0], k_ref[1]
    o_ref[0, :n] = (xr * kr - xi * ki).astype(o_ref.dtype)
    o_ref[0, n:] = (xr * ki + xi * kr).astype(o_ref.dtype)


def _hyena_short(seq, taps, nrm, uc, skip):
    n = 2 * seq
    tabs = _dft_tables_short(seq)
    cf = taps.shape[1]
    c = HY_W
    a = _dft_rows(tabs["w_fwd"], taps.reshape(1, seq, cf), tn=cf)
    nblk = cf // 2 // c
    spec = pl.pallas_call(
        _spec_combine_kernel,
        grid=(nblk,),
        in_specs=[pl.BlockSpec((1, 2 * n, c), lambda j: (0, 0, j)),
                  pl.BlockSpec((1, 2 * n, c), lambda j: (0, 0, j + nblk)),
                  pl.BlockSpec((1, c), lambda j: (0, j)),
                  pl.BlockSpec((1, c), lambda j: (0, j + nblk))],
        out_specs=pl.BlockSpec((2, n, c), lambda j: (0, 0, j)),
        out_shape=jax.ShapeDtypeStruct((2, n, cf // 2), F32),
        compiler_params=_cparams(1),
        name="hyena_spec_combine",
    )(a, a, nrm, nrm)
    nb = uc.shape[1]
    z = uc[2]
    for order in range(2):
        a = _dft_rows(tabs["w_fwd"], z, tn=c)
        bc = pl.pallas_call(
            _spec_mul_kernel,
            grid=(nb,),
            in_specs=[pl.BlockSpec((1, 2 * n, c), lambda b: (b, 0, 0)),
                      pl.BlockSpec((2, n, c), lambda b: (0, 0, order))],
            out_specs=pl.BlockSpec((1, 2 * n, c), lambda b: (b, 0, 0)),
            out_shape=jax.ShapeDtypeStruct((nb, 2 * n, c), BF16),
            compiler_params=_cparams(1),
            name="hyena_spec_mul",
        )(a, spec)
        z = _idft_gate(tabs["w_inv"], bc, uc[order], z, skip[order].reshape(1, c), tn=c)
    return z


def _slab_pitch(n2):
    return 2 * n2 + 8


def _stage2(a_scr, wd_ref, f2, n1, n2, pitch):
    ar = a_scr[pl.ds(f2, n1, stride=pitch), :]
    ai = a_scr[pl.ds(n2 + f2, n1, stride=pitch), :]
    return _dot(wd_ref[...], jnp.concatenate([ar, ai], axis=0).astype(BF16))


def _hy_spectrum_kernel(hf_ref, hb_ref, nf_ref, nb_ref, w1f_ref, wd_ref, o_ref, a_scr, *, n1, n2, fb):
    i = pl.program_id(1)
    pitch = _slab_pitch(n2)
    half = n2 // 2

    @pl.when(i == 0)
    def _():
        def body(t1, carry):
            past = hf_ref[pl.ds(t1, half, stride=n1), :]
            fut = hb_ref[pl.ds(jnp.where(t1 == 0, 0, n1 - t1), half, stride=n1), :]
            xs = jnp.concatenate([past, fut], axis=0).astype(BF16)
            a_scr[pl.ds(pl.multiple_of(t1 * pitch, 8), 2 * n2), :] = _dot(w1f_ref[t1], xs)
            return carry
        lax.fori_loop(0, n1, body, 0)

    inv = 1.0 / (nf_ref[...] + nb_ref[...])
    for jj in range(fb):
        x = _stage2(a_scr, wd_ref, i * fb + jj, n1, n2, pitch)
        o_ref[0, jj] = x[:n1] * inv
        o_ref[1, jj] = x[n1:] * inv


def _hy_fftconv_kernel(z_ref, xg_ref, k_ref, sk_ref, w1_ref, wd_ref, wdi_ref, w4_ref, o_ref, a_scr,
                       *, n1, n2, fb):
    i = pl.program_id(2)
    pitch = _slab_pitch(n2)
    half = n2 // 2

    @pl.when(i == 0)
    def _():
        def body(t1, carry):
            xs = z_ref[0, pl.ds(t1, half, stride=n1), :].astype(BF16)
            a_scr[pl.ds(pl.multiple_of(t1 * pitch, 8), 2 * n2), :] = _dot(w1_ref[t1], xs)
            return carry
        lax.fori_loop(0, n1, body, 0)

    for jj in range(fb):
        f2 = i * fb + jj
        x = _stage2(a_scr, wd_ref, f2, n1, n2, pitch)
        xr, xi = x[:n1], x[n1:]
        kr, ki = k_ref[0, jj], k_ref[1, jj]
        y = jnp.concatenate([xr * kr - xi * ki, xr * ki + xi * kr], axis=0).astype(BF16)
        bv = _dot(wdi_ref[...], y)
        a_scr[pl.ds(f2, n1, stride=pitch), :] = bv[:n1]
        a_scr[pl.ds(n2 + f2, n1, stride=pitch), :] = bv[n1:]

    @pl.when(i == pl.num_programs(2) - 1)
    def _():
        sk = sk_ref[...]

        def body(t1, carry):
            slab = a_scr[pl.ds(pl.multiple_of(t1 * pitch, 8), 2 * n2), :].astype(BF16)
            y = _dot(w4_ref[t1], slab)
            rows = pl.ds(t1, half, stride=n1)
            o_ref[0, rows, :] = xg_ref[0, rows, :] * (y + z_ref[0, rows, :] * sk)
            return carry
        lax.fori_loop(0, n1, body, 0)


def _hyena_long(seq, taps, nrm, uc, skip, fb=8):
    n, n1, n2 = _fft_split(seq)
    tabs = _dft_tables_long(seq)
    pitch = _slab_pitch(n2)
    c = HY_W
    nblk = c // LANE
    cf = taps.shape[1]
    nfil = cf // 2 // LANE
    const = dict(pipeline_mode=pl.Buffered(1))
    scratch = [pltpu.VMEM((n1 * pitch, LANE), F32)]
    spec = pl.pallas_call(
        functools.partial(_hy_spectrum_kernel, n1=n1, n2=n2, fb=fb),
        grid=(nfil, n2 // fb),
        in_specs=[pl.BlockSpec((seq, LANE), lambda j, i: (0, j), **const),
                  pl.BlockSpec((seq, LANE), lambda j, i: (0, j + nfil), **const),
                  pl.BlockSpec((1, LANE), lambda j, i: (0, j)),
                  pl.BlockSpec((1, LANE), lambda j, i: (0, j + nfil)),
                  pl.BlockSpec(tabs["w1f"].shape, lambda j, i: (0, 0, 0), **const),
                  pl.BlockSpec(tabs["wd"].shape, lambda j, i: (0, 0), **const)],
        out_specs=pl.BlockSpec((2, fb, n1, LANE), lambda j, i: (0, i, 0, j)),
        out_shape=jax.ShapeDtypeStruct((2, n2, n1, cf // 2), F32),
        scratch_shapes=scratch,
        compiler_params=_cparams(2),
        name="hyena_filter_spectrum",
    )(taps, taps, nrm, nrm, tabs["w1f"], tabs["wd"])
    nb = uc.shape[1]
    z = uc[2]
    for order in range(2):
        blk = lambda b, j, i: (b, 0, j)
        z = pl.pallas_call(
            functools.partial(_hy_fftconv_kernel, n1=n1, n2=n2, fb=fb),
            grid=(nb, nblk, n2 // fb),
            in_specs=[pl.BlockSpec((1, seq, LANE), blk, **const),
                      pl.BlockSpec((1, seq, LANE), blk, **const),
                      pl.BlockSpec((2, fb, n1, LANE), lambda b, j, i: (0, i, 0, order * nblk + j)),
                      pl.BlockSpec((1, LANE), lambda b, j, i: (0, j)),
                      pl.BlockSpec(tabs["w1"].shape, lambda b, j, i: (0, 0, 0), **const),
                      pl.BlockSpec(tabs["wd"].shape, lambda b, j, i: (0, 0), **const),
                      pl.BlockSpec(tabs["wdi"].shape, lambda b, j, i: (0, 0), **const),
                      pl.BlockSpec(tabs["w4"].shape, lambda b, j, i: (0, 0, 0), **const)],
            out_specs=pl.BlockSpec((1, seq, LANE), blk),
            out_shape=jax.ShapeDtypeStruct((nb, seq, c), F32),
            scratch_shapes=scratch,
            compiler_params=_cparams(3),
            name="hyena_fft_conv",
        )(z, uc[order], spec, skip[order].reshape(1, c), tabs["w1"], tabs["wd"], tabs["wdi"], tabs["w4"])
    return z


def _hyena(p, lw, row0, seq, n_batch):
    taps, nrm = _hyena_filter_taps(seq, lw["hy_w1"], lw["hy_b1"], lw["hy_w2"], lw["hy_b2"], lw["hy_w3"],
                                   lw["hy_b3"], lw["hy_w4"], lw["hy_freq"])
    uc = _short_conv(p, lw["hy_conv_w"], lw["hy_conv_b"], row0, seq, n_batch)
    run = _hyena_long if _fft_split(seq)[1] > 1 else _hyena_short
    return run(seq, taps, nrm, uc, lw["hy_skip"]).reshape(n_batch * seq, HY_W)


def _mla_qkv_kernel(qa_ref, kva_ref, kr_ref, cs_ref, wq_ref, wkv_ref, gqa_ref, gkva_ref, gqn_ref, gqr_ref,
                    gkn_ref, gkr_ref, q_ref, k_ref, v_ref):
    cs = cs_ref[...]
    lane = lax.broadcasted_iota(jnp.int32, cs.shape, 1)
    low = lane < MLA_ROPE

    def rope(pair, gain2):
        ms = jnp.sum(jnp.where(low, pair * pair, 0.0), axis=-1, keepdims=True) * (1.0 / MLA_ROPE)
        t = pair * lax.rsqrt(ms + RMS_EPS) * gain2 * cs
        return jnp.where(low, t + pltpu.roll(t, MLA_ROPE, 1), 0.0)

    qq = _dot(_rms(qa_ref[...], gqa_ref[...]).astype(BF16), wq_ref[...])
    kv = _dot(_rms(kva_ref[...], gkva_ref[...]).astype(BF16), wkv_ref[...])
    kr = rope(kr_ref[...], gkr_ref[...])
    hp = MLA_HEAD_PAD
    for h in range(MLA_HEADS):
        qn = _rms(qq[:, h * hp:h * hp + MLA_NOPE], gqn_ref[...])
        qr = rope(qq[:, h * hp + MLA_NOPE:(h + 1) * hp], gqr_ref[...])
        q_ref[h, :, 0:LANE] = (qn * MLA_SCALE).astype(BF16)
        q_ref[h, :, LANE:2 * LANE] = (qr * MLA_SCALE).astype(BF16)
        kn = _rms(kv[:, h * hp:h * hp + MLA_NOPE], gkn_ref[...])
        k_ref[h, :, 0:LANE] = kn.astype(BF16)
        k_ref[h, :, LANE:2 * LANE] = kr.astype(BF16)
        v_ref[h] = kv[:, h * hp + MLA_NOPE:(h + 1) * hp].astype(BF16)


def _rope_table(seq, n_batch, ctx_rows):
    rows = seq // GRID_W
    row = np.repeat(np.arange(rows, dtype=np.float32), GRID_W)
    col = np.tile(np.arange(GRID_W, dtype=np.float32), rows)
    half = MLA_ROPE // 2
    inv = (ROPE_THETA ** (-np.arange(0, half, 2, dtype=np.float32) / half)).astype(np.float32)
    ar = (row[:, None] * inv).astype(np.float64)
    ac = (col[:, None] * inv).astype(np.float64)
    cos = np.concatenate([np.cos(ar), np.cos(ar), np.cos(ac), np.cos(ac)], axis=1)
    sin = np.concatenate([-np.sin(ar), np.sin(ar), -np.sin(ac), np.sin(ac)], axis=1)
    lat = np.tile(np.concatenate([cos, sin], axis=1), (n_batch, 1))
    ctx = np.concatenate([np.ones((ctx_rows, MLA_ROPE)), np.zeros((ctx_rows, MLA_ROPE))], axis=1)
    return jnp.asarray(np.concatenate([lat, ctx], axis=0), F32)


def _pair_gain(g):
    return jnp.concatenate([g, _rope_swap(g)]).reshape(1, 2 * MLA_ROPE)


def _mla_qkv(p, cs, lw, tm=512):
    n_rows = p.shape[0]
    hd, hp = MLA_HEADS, MLA_HEAD_PAD
    w_uq = lw["w_uq"].reshape(-1, hd, MLA_QK)
    wq = jnp.concatenate([w_uq, _rope_swap(w_uq[..., MLA_NOPE:])], axis=-1).reshape(-1, hd * hp).astype(BF16)
    wkv = lw["w_ukv"].astype(BF16)
    ql, kvl = wq.shape[0], wkv.shape[0]
    vec = lambda a: a.reshape(1, -1)
    full = lambda a: pl.BlockSpec(a.shape, lambda i: (0,) * a.ndim)
    args = (wq, wkv, vec(lw["q_a_norm"]), vec(lw["kv_a_norm"]), vec(lw["q_nope_norm"]),
            _pair_gain(lw["q_rope_norm"]), vec(lw["k_nope_norm"]), _pair_gain(lw["k_rope_norm"]))
    return pl.pallas_call(
        _mla_qkv_kernel,
        grid=(n_rows // tm,),
        in_specs=[pl.BlockSpec((tm, ql), lambda i: (i, COL_QA // ql)),
                  pl.BlockSpec((tm, kvl), lambda i: (i, COL_KVA // kvl)),
                  pl.BlockSpec((tm, LANE), lambda i: (i, COL_KR // LANE)),
                  pl.BlockSpec((tm, LANE), lambda i: (i, 0))] + [full(a) for a in args],
        out_specs=[pl.BlockSpec((hd, tm, hp), lambda i: (0, i, 0)),
                   pl.BlockSpec((hd, tm, hp), lambda i: (0, i, 0)),
                   pl.BlockSpec((hd, tm, MLA_V), lambda i: (0, i, 0))],
        out_shape=[jax.ShapeDtypeStruct((hd, n_rows, hp), BF16),
                   jax.ShapeDtypeStruct((hd, n_rows, hp), BF16),
                   jax.ShapeDtypeStruct((hd, n_rows, MLA_V), BF16)],
        compiler_params=_cparams(1),
        name="mla_qkv",
    )(p, p, p, cs, *args)


def _attn_kernel(*refs, with_latent):
    if with_latent:
        q_ref, k_ref, v_ref, kc_ref, vc_ref, o_ref = refs
    else:
        q_ref, kc_ref, vc_ref, o_ref = refs
    q = q_ref[0]
    sc = _dot_nt(q, kc_ref[0])
    m = jnp.max(sc, axis=-1, keepdims=True)
    if with_latent:
        sl = _dot_nt(q, k_ref[0])
        m = jnp.maximum(m, jnp.max(sl, axis=-1, keepdims=True))
    pc = jnp.exp(sc - m)
    den = jnp.sum(pc, axis=-1, keepdims=True)
    acc = _dot(pc.astype(BF16), vc_ref[0])
    if with_latent:
        pl_ = jnp.exp(sl - m)
        den = den + jnp.sum(pl_, axis=-1, keepdims=True)
        acc = acc + _dot(pl_.astype(BF16), v_ref[0])
    o_ref[...] = acc / den


def _attention(q, k, v, seq, ctx_len, n_batch, latent, tq=256):
    hd, n_rows, hp = q.shape
    lat_rows = n_batch * seq
    kc_spec = pl.BlockSpec((1, ctx_len, hp), lambda b, h, i: (h, lat_rows // ctx_len + b, 0))
    vc_spec = pl.BlockSpec((1, ctx_len, MLA_V), lambda b, h, i: (h, lat_rows // ctx_len + b, 0))
    if latent:
        nq = seq // tq
        in_specs = [pl.BlockSpec((1, tq, hp), lambda b, h, i: (h, b * nq + i, 0)),
                    pl.BlockSpec((1, seq, hp), lambda b, h, i: (h, b, 0)),
                    pl.BlockSpec((1, seq, MLA_V), lambda b, h, i: (h, b, 0)), kc_spec, vc_spec]
        args = (q, k, v, k, v)
        out_spec = pl.BlockSpec((tq, MLA_V), lambda b, h, i: (b * nq + i, h))
        out_rows = lat_rows
    else:
        nq = 1
        tq = ctx_len
        in_specs = [pl.BlockSpec((1, tq, hp), lambda b, h, i: (h, lat_rows // ctx_len + b, 0)), kc_spec, vc_spec]
        args = (q, k, v)
        out_spec = pl.BlockSpec((tq, MLA_V), lambda b, h, i: (b, h))
        out_rows = n_batch * ctx_len
    return pl.pallas_call(
        functools.partial(_attn_kernel, with_latent=latent),
        grid=(n_batch, hd, nq),
        in_specs=in_specs,
        out_specs=out_spec,
        out_shape=jax.ShapeDtypeStruct((out_rows, hd * MLA_V), F32),
        compiler_params=_cparams(3),
        name="mla_attention" if latent else "mla_attention_ctx",
    )(*args)


def _merge_kernel(x_ref, mod_ref, of_ref, ob_ref, g_ref, yb_ref, yc_ref, ga_ref, gb_ref, gc_ref,
                  gain_ref, wa_ref, wb_ref, wc_ref, wo_ref, o_ref):
    m = mod_ref[0]
    o = of_ref[...] + ob_ref[...]
    gain = gain_ref[...]
    ya = jnp.concatenate([_rms(o[:, h * HG_D:(h + 1) * HG_D], gain) for h in range(HG_HEADS)], axis=1)
    ya = (ya * _silu(g_ref[...])).astype(BF16)
    mix = (_sigmoid(ga_ref[...]) * _dot(ya, wa_ref[...])
           + _sigmoid(gb_ref[...]) * _dot(yb_ref[...].astype(BF16), wb_ref[...])
           + _sigmoid(gc_ref[...]) * _dot(yc_ref[...].astype(BF16), wc_ref[...]))
    o_ref[...] = x_ref[...] + m[5:6] * _dot(mix.astype(BF16), wo_ref[...])


def _merge(xa, mod, o_f, o_b, p, y_b, y_c, lw, n_rows, seq, n_batch, tm=512):
    d = xa.shape[1]
    hk = HG_HEADS * HG_D
    row = lambda w: pl.BlockSpec((tm, w), lambda i: (i, 0))
    pcol = lambda w, c: pl.BlockSpec((tm, w), lambda i: (i, c // w))
    full = lambda a: pl.BlockSpec(a.shape, lambda i: (0,) * a.ndim)
    ws = (lw["hg_out_norm"].reshape(1, HG_D), lw["w_br_a"].astype(BF16), lw["w_br_b"].astype(BF16),
          lw["w_br_c"].astype(BF16), lw["w_out"].astype(BF16))
    return pl.pallas_call(
        _merge_kernel,
        grid=(n_rows // tm,),
        in_specs=[row(d), pl.BlockSpec((1, N_MOD, d), _group_map(seq // tm, n_batch)),
                  row(hk), row(hk), pcol(hk, COL_G), row(HY_W), row(MLA_HEADS * MLA_V),
                  pcol(d, COL_GA), pcol(d, COL_GB), pcol(d, COL_GC)] + [full(a) for a in ws],
        out_specs=row(d),
        out_shape=jax.ShapeDtypeStruct((n_rows, d), F32),
        compiler_params=_cparams(1),
        name="merge",
    )(xa, mod, o_f, o_b, p, y_b, y_c, p, p, p, *ws)


def kernel(x, c, ctx, c_ctx, ada_w, ada_b, ffn1_norm, ffn1_w13, ffn1_w2, mix_norm, w_in, hg_lb_logits, hg_out_norm, hy_conv_w, hy_conv_b, hy_w1, hy_b1, hy_w2, hy_b2, hy_w3, hy_b3, hy_w4, hy_freq, hy_skip, q_a_norm, w_uq, kv_a_norm, w_ukv, q_nope_norm, q_rope_norm, k_nope_norm, k_rope_norm, w_br_a, w_br_b, w_br_c, w_out, ffn2_norm, ffn2_w13, ffn2_w2):
    stacked = dict(
        ada_w=ada_w, ada_b=ada_b, ffn1_norm=ffn1_norm, ffn1_w13=ffn1_w13, ffn1_w2=ffn1_w2, mix_norm=mix_norm,
        w_in=w_in, hg_out_norm=hg_out_norm, hy_conv_w=hy_conv_w, hy_conv_b=hy_conv_b, hy_w1=hy_w1, hy_b1=hy_b1,
        hy_w2=hy_w2, hy_b2=hy_b2, hy_w3=hy_w3, hy_b3=hy_b3, hy_w4=hy_w4, hy_freq=hy_freq, hy_skip=hy_skip,
        q_a_norm=q_a_norm, w_uq=w_uq, kv_a_norm=kv_a_norm, w_ukv=w_ukv, q_nope_norm=q_nope_norm,
        q_rope_norm=q_rope_norm, k_nope_norm=k_nope_norm, k_rope_norm=k_rope_norm, w_br_a=w_br_a,
        w_br_b=w_br_b, w_br_c=w_br_c, w_out=w_out, ffn2_norm=ffn2_norm, ffn2_w13=ffn2_w13, ffn2_w2=ffn2_w2)
    n_batch, seq, d = x.shape
    ctx_len = ctx.shape[1]
    depth = ada_w.shape[0]
    lat_rows, ctx_rows = n_batch * seq, n_batch * ctx_len
    all_rows = lat_rows + ctx_rows
    assert seq % 512 == 0 and ctx_rows % 512 == 0 and seq % ctx_len == 0 and seq % GRID_W == 0
    assert ctx_len % HG_CHUNK == 0 and n_batch < 8

    xa = jnp.concatenate([x.reshape(lat_rows, d), ctx.reshape(ctx_rows, d)], axis=0)
    cs = jnp.concatenate([c, c_ctx.reshape(1, d), jnp.zeros((7 - n_batch, d), F32)], axis=0)
    rope_cs = _rope_table(seq, n_batch, ctx_rows)
    zero_state = jnp.zeros((n_batch, HG_HEADS, HG_D, HG_D), F32)

    for l in range(depth):
        lw = {name: val[l] for name, val in stacked.items()}
        need_ctx = l < depth - 1
        mod = _modulation(cs, lw["ada_w"], lw["ada_b"])
        xa = _half_ffn(xa, mod, lw["ffn1_norm"], lw["ffn1_w13"], lw["ffn1_w2"], 0, all_rows, seq, n_batch)
        p = _in_projection(xa, mod, lw["mix_norm"], _pack_w_in(lw["w_in"]).astype(BF16), seq, n_batch)

        mix_rows = all_rows if need_ctx else lat_rows
        o_dir = []
        for rev in (False, True):
            lg = hg_lb_logits[1 if rev else 0]
            o_c, s_c = _hgrn2_scan(p, lg, zero_state, l, rev, lat_rows, ctx_len, n_batch)
            o_l, _ = _hgrn2_scan(p, lg, s_c, l, rev, 0, seq, n_batch)
            o_dir.append(jnp.concatenate([o_l, o_c], axis=0) if need_ctx else o_l)

        y_b = _hyena(p, lw, 0, seq, n_batch)
        if need_ctx:
            y_b = jnp.concatenate([y_b, _hyena(p, lw, lat_rows, ctx_len, n_batch)], axis=0)

        q, k, v = _mla_qkv(p, rope_cs, lw)
        y_c = _attention(q, k, v, seq, ctx_len, n_batch, latent=True)
        if need_ctx:
            y_c = jnp.concatenate([y_c, _attention(q, k, v, seq, ctx_len, n_batch, latent=False)], axis=0)

        xa = _merge(xa, mod, o_dir[0], o_dir[1], p, y_b, y_c, lw, mix_rows, seq, n_batch)
        xa = _half_ffn(xa, mod, lw["ffn2_norm"], lw["ffn2_w13"], lw["ffn2_w2"], 6, mix_rows, seq, n_batch)
    return xa[:lat_rows].reshape(n_batch, seq, d)
```

```python
import functools
import math

import numpy as np
import jax
import jax.numpy as jnp
from jax import lax
from jax.experimental import pallas as pl
from jax.experimental.pallas import tpu as pltpu

F32 = jnp.float32
BF16 = jnp.bfloat16

RMS_EPS = 1e-6
N_MOD = 9
GRID_W = 64
ROPE_THETA = 10000.0
HG_HEADS = 4
HG_D = 128
HG_CHUNK = 128
HY_W = 512
HY_TARGET = 1e-2
HY_MIN_DECAY = math.log(HY_TARGET) / 1.5
HY_MAX_DECAY = math.log(HY_TARGET) / 0.3
HY_SHIFT = 0.05
MLA_HEADS = 4
MLA_NOPE = 128
MLA_ROPE = 64
MLA_V = 128
MLA_QK = MLA_NOPE + MLA_ROPE
MLA_SCALE = MLA_QK ** -0.5
MLA_HEAD_PAD = 256
V_ROWS = MLA_V + 16
LOG2_E = math.log2(math.e)
FFT_N1 = 128
LANE = 128

VMEM_LIMIT = 52 * 1024 * 1024

COL_GA, COL_GB, COL_GC = 0, 1024, 2048
COL_Q, COL_ZF, COL_ZB, COL_IV, COL_G = 3072, 3584, 4096, 4608, 5120
COL_HY = 5632
COL_QA, COL_KVA, COL_KR = 7168, 7424, 7552
IN_PACKED = 7680


def _cparams(n_axes):
    return pltpu.CompilerParams(dimension_semantics=("arbitrary",) * n_axes,
                                vmem_limit_bytes=VMEM_LIMIT)


def _dot(a, b):
    return jnp.dot(a, b, preferred_element_type=F32)


def _dot_nt(a, b):
    return lax.dot_general(a, b, (((1,), (1,)), ((), ())), preferred_element_type=F32)


def _dot_tn(a, b):
    return lax.dot_general(a, b, (((0,), (0,)), ((), ())), preferred_element_type=F32)


def _sigmoid(x):
    return 1.0 / (1.0 + jnp.exp(-x))


def _silu(x):
    return x * _sigmoid(x)


def _rms(x, gain):
    return x * lax.rsqrt(jnp.mean(x * x, axis=-1, keepdims=True) + RMS_EPS) * gain


def _mod_kernel(c_ref, w_ref, b_ref, o_ref):
    o_ref[...] = _dot(_silu(c_ref[...]).astype(BF16), w_ref[...]) + b_ref[...]


def _modulation(cs, ada_w, ada_b):
    g, d = cs.shape
    n = ada_w.shape[1]
    tn = n // 4
    out = pl.pallas_call(
        _mod_kernel,
        grid=(n // tn,),
        in_specs=[pl.BlockSpec((g, d), lambda j: (0, 0)),
                  pl.BlockSpec((d, tn), lambda j: (0, j)),
                  pl.BlockSpec((1, tn), lambda j: (0, j))],
        out_specs=pl.BlockSpec((g, tn), lambda j: (0, j)),
        out_shape=jax.ShapeDtypeStruct((g, n), F32),
        compiler_params=_cparams(1),
        name="modulation",
    )(cs, ada_w.astype(BF16), ada_b.reshape(1, n))
    return out.reshape(g, N_MOD, d)


def _ffn_kernel(x_ref, mod_ref, g_ref, w13_ref, w2_ref, o_ref, *, idx, ff, n_chunks):
    x = x_ref[...]
    m = mod_ref[0]
    h = (_rms(x, g_ref[...]) * (1.0 + m[idx + 1:idx + 2]) + m[idx:idx + 1]).astype(BF16)
    ck = ff // n_chunks
    acc = jnp.zeros(x.shape, F32)
    for j in range(n_chunks):
        a = _dot(h, w13_ref[:, j * ck:(j + 1) * ck])
        b = _dot(h, w13_ref[:, ff + j * ck:ff + (j + 1) * ck])
        acc = acc + _dot((_silu(a) * b).astype(BF16), w2_ref[j * ck:(j + 1) * ck, :])
    o_ref[...] = x + (0.5 * m[idx + 2:idx + 3]) * acc


def _group_map(tiles_per_batch, n_batch):
    return lambda i: (jnp.minimum(i // tiles_per_batch, n_batch), 0, 0)


def _half_ffn(xa, mod, gain, w13, w2, idx, n_rows, seq, n_batch, tm=512):
    d = xa.shape[1]
    ff = w2.shape[0]
    n_chunks = ff // 256
    const = dict(pipeline_mode=pl.Buffered(1))
    return pl.pallas_call(
        functools.partial(_ffn_kernel, idx=idx, ff=ff, n_chunks=n_chunks),
        grid=(n_rows // tm,),
        in_specs=[pl.BlockSpec((tm, d), lambda i: (i, 0)),
                  pl.BlockSpec((1, N_MOD, d), _group_map(seq // tm, n_batch)),
                  pl.BlockSpec((1, d), lambda i: (0, 0)),
                  pl.BlockSpec((d, 2 * ff), lambda i: (0, 0), **const),
                  pl.BlockSpec((ff, d), lambda i: (0, 0), **const)],
        out_specs=pl.BlockSpec((tm, d), lambda i: (i, 0)),
        out_shape=jax.ShapeDtypeStruct((n_rows, d), F32),
        compiler_params=_cparams(1),
        name="half_ffn",
    )(xa, mod, gain.reshape(1, d), w13.astype(BF16), w2.astype(BF16))


def _inproj_kernel(x_ref, mod_ref, g_ref, w_ref, o_ref):
    m = mod_ref[0]
    h = (_rms(x_ref[...], g_ref[...]) * (1.0 + m[4:5]) + m[3:4]).astype(BF16)
    o_ref[...] = _dot(h, w_ref[...])


def _pack_w_in(w_in):
    d = w_in.shape[0]
    hk = HG_HEADS * HG_D
    sizes = (hk, hk, hk, hk, hk, 3 * HY_W, 256, 128, MLA_ROPE, d, d, d)
    offs = np.cumsum((0,) + sizes)
    q, zf, zb, iv, g, hy, qa, kva, kr, ga, gb, gc = (w_in[:, offs[i]:offs[i + 1]] for i in range(12))
    return jnp.concatenate([ga, gb, gc, q, zf, zb, iv, g, hy, qa, kva, kr, _rope_swap(kr)], axis=1)


def _rope_swap(a):
    q = MLA_ROPE // 4
    return jnp.concatenate([a[..., q:2 * q], a[..., :q], a[..., 3 * q:], a[..., 2 * q:3 * q]], axis=-1)


def _in_projection(xa, mod, gain, w_packed, seq, n_batch, tm=512, tn=2560):
    n_rows, d = xa.shape
    n = w_packed.shape[1]
    return pl.pallas_call(
        _inproj_kernel,
        grid=(n // tn, n_rows // tm),
        in_specs=[pl.BlockSpec((tm, d), lambda j, i: (i, 0)),
                  pl.BlockSpec((1, N_MOD, d), lambda j, i: (jnp.minimum(i // (seq // tm), n_batch), 0, 0)),
                  pl.BlockSpec((1, d), lambda j, i: (0, 0)),
                  pl.BlockSpec((d, tn), lambda j, i: (0, j))],
        out_specs=pl.BlockSpec((tm, tn), lambda j, i: (i, j)),
        out_shape=jax.ShapeDtypeStruct((n_rows, n), F32),
        compiler_params=_cparams(2),
        name="in_projection",
    )(xa, mod, gain.reshape(1, d), w_packed)


def _hgrn2_tables(c, rev):
    t = np.arange(c)[:, None]
    u = np.arange(c)[None, :]
    mats = [(u <= t), (u > t)]
    masks = [(t == u)]
    h = c // 2
    while h >= 1:
        mid = (t // (2 * h)) * (2 * h) + h
        mats.append(np.where(t >= mid, (u >= mid) & (u <= t), (u >= t + 1) & (u <= mid - 1)))
        mid_s = (u // (2 * h)) * (2 * h) + h
        masks.append((t // (2 * h) == u // (2 * h)) & (u < mid_s) & (t >= mid))
        h //= 2
    mats = np.stack([m.astype(np.float32) for m in mats])
    masks = np.stack([m.astype(np.float32) for m in masks])
    if rev:
        mats = mats[:, ::-1, ::-1]
        masks = masks[:, ::-1, ::-1]
    return (jnp.asarray(np.ascontiguousarray(mats).reshape(-1, c), BF16),
            jnp.asarray(np.ascontiguousarray(masks), F32))


def _hgrn2_kernel(q_ref, z_ref, v_ref, lg_ref, dst_ref, msk_ref, s0_ref, o_ref, sf_ref, st_ref,
                  *, layer, rev, n_levels):
    j = pl.program_id(1)
    c = q_ref.shape[0]
    hk = HG_HEADS * HG_D

    @pl.when(j == 0)
    def _():
        st_ref[...] = s0_ref[0]

    lg = lg_ref[...]
    e = jnp.exp(lg - jnp.max(lg, axis=0, keepdims=True))
    sm = e / jnp.sum(e, axis=0, keepdims=True)
    lb = jnp.zeros((1, hk), F32)
    for i in range(1, layer + 1):
        lb = lb + sm[i:i + 1]

    f = lb + (1.0 - lb) * _sigmoid(z_ref[...])
    kk = 1.0 - f
    g = jnp.log(f)
    g_hi = g.astype(BF16)
    g_lo = (g - g_hi.astype(F32)).astype(BF16)
    dg = _dot(dst_ref[...], jnp.concatenate([g_hi, g_lo], axis=1))
    dg = dg[:, :hk] + dg[:, hk:]
    qs = _silu(q_ref[...]) * HG_D ** -0.5
    vv = v_ref[...]
    last = 0 if rev else c - 1

    for h in range(HG_HEADS):
        hs = slice(h * HG_D, (h + 1) * HG_D)
        qh, kh, vh = qs[:, hs], kk[:, hs], vv[:, hs].astype(BF16)
        b_in = dg[0:c, hs]
        e_out = jnp.exp(dg[c:2 * c, hs])
        a = _dot_nt(qh.astype(BF16), kh.astype(BF16)) * msk_ref[0]
        for l in range(1, n_levels + 1):
            el = jnp.exp(dg[(1 + l) * c:(2 + l) * c, hs])
            a = a + _dot_nt((qh * el).astype(BF16), (kh * el).astype(BF16)) * msk_ref[l]
        st = st_ref[h]
        o = _dot_nt((qh * jnp.exp(b_in)).astype(BF16), st.astype(BF16)) + _dot(a.astype(BF16), vh)
        o_ref[:, hs] = o
        st_ref[h] = jnp.exp(b_in[last:last + 1, :]) * st + _dot_tn(vh, (kh * e_out).astype(BF16))

    @pl.when(j == pl.num_programs(1) - 1)
    def _():
        sf_ref[0] = st_ref[...]


def _hgrn2_scan(p, logits, s0, layer, rev, row0, seq, n_batch):
    c = HG_CHUNK
    hk = HG_HEADS * HG_D
    nc = seq // c
    base = row0 // c
    n_levels = int(math.log2(c))
    dst, msk = _hgrn2_tables(c, rev)
    col_z = COL_ZB if rev else COL_ZF

    def rows(b, j):
        return b * nc + (nc - 1 - j if rev else j)

    def prow(w):
        return pl.BlockSpec((c, hk), lambda b, j: (base + rows(b, j), w // hk))

    return pl.pallas_call(
        functools.partial(_hgrn2_kernel, layer=layer, rev=rev, n_levels=n_levels),
        grid=(n_batch, nc),
        in_specs=[prow(COL_Q), prow(col_z), prow(COL_IV),
                  pl.BlockSpec(logits.shape, lambda b, j: (0, 0)),
                  pl.BlockSpec(dst.shape, lambda b, j: (0, 0)),
                  pl.BlockSpec(msk.shape, lambda b, j: (0, 0, 0)),
                  pl.BlockSpec((1, HG_HEADS, HG_D, HG_D), lambda b, j: (b, 0, 0, 0))],
        out_specs=[pl.BlockSpec((c, hk), lambda b, j: (rows(b, j), 0)),
                   pl.BlockSpec((1, HG_HEADS, HG_D, HG_D), lambda b, j: (b, 0, 0, 0))],
        out_shape=[jax.ShapeDtypeStruct((n_batch * seq, hk), F32),
                   jax.ShapeDtypeStruct((n_batch, HG_HEADS, HG_D, HG_D), F32)],
        scratch_shapes=[pltpu.VMEM((HG_HEADS, HG_D, HG_D), F32)],
        compiler_params=_cparams(2),
        name="hgrn2_scan_bwd" if rev else "hgrn2_scan_fwd",
    )(p, p, p, logits, dst, msk, s0)


def _shortconv_kernel(u_ref, w_ref, b_ref, o_ref):
    u = u_ref[...]
    n = u.shape[0]
    row = lax.broadcasted_iota(jnp.int32, u.shape, 0)
    prev = jnp.where(row == 0, 0.0, pltpu.roll(u, 1, 0))
    nxt = jnp.where(row == n - 1, 0.0, pltpu.roll(u, n - 1, 0))
    w = w_ref[...]
    o_ref[0, 0] = prev * w[0:1] + u * w[1:2] + nxt * w[2:3] + b_ref[...]


def _short_conv(p, w, b, row0, seq, n_batch):
    nb = 3 * HY_W // LANE
    per = HY_W // LANE
    return pl.pallas_call(
        _shortconv_kernel,
        grid=(n_batch, nb),
        in_specs=[pl.BlockSpec((seq, LANE), lambda bi, j: (row0 // seq + bi, COL_HY // LANE + j)),
                  pl.BlockSpec((3, LANE), lambda bi, j: (0, j)),
                  pl.BlockSpec((1, LANE), lambda bi, j: (0, j))],
        out_specs=pl.BlockSpec((1, 1, seq, LANE), lambda bi, j: (j // per, bi, 0, j % per)),
        out_shape=jax.ShapeDtypeStruct((3, n_batch, seq, HY_W), F32),
        compiler_params=_cparams(2),
        name="hyena_short_conv",
    )(p, w, b.reshape(1, -1))


def _hy_filter_kernel(emb_ref, w1_ref, b1_ref, w2_ref, b2_ref, w3_ref, b3_ref, w4_ref, fr_ref, dl_ref,
                      o_ref, nrm_ref, *, seq):
    i = pl.program_id(0)
    hp = lax.Precision.HIGHEST
    fr = fr_ref[...]
    hid = jnp.sin(fr * (jnp.dot(emb_ref[...], w1_ref[...], precision=hp) + b1_ref[...]))
    hid = jnp.sin(fr * (jnp.dot(hid, w2_ref[...], precision=hp) + b2_ref[...]))
    hid = jnp.sin(fr * (jnp.dot(hid, w3_ref[...], precision=hp) + b3_ref[...]))
    h = jnp.dot(hid, w4_ref[...], precision=hp)
    tl, n = h.shape
    pos = lax.broadcasted_iota(jnp.int32, h.shape, 0) + i * tl
    col = lax.broadcasted_iota(jnp.int32, h.shape, 1)
    t = pos.astype(F32) * (1.0 / (seq - 1))
    h = h * (jnp.exp(-t * dl_ref[...]) + HY_SHIFT)
    h = jnp.where((pos == 0) & (col >= n // 2), 0.0, h)
    o_ref[...] = h

    @pl.when(i == 0)
    def _():
        nrm_ref[...] = jnp.zeros_like(nrm_ref)

    nrm_ref[...] += jnp.sum(jnp.abs(h), axis=0, keepdims=True)


def _hyena_filter_taps(seq, w1, b1, w2, b2, w3, b3, w4, freq):
    fh = w1.shape[1]
    n_emb = w1.shape[0]
    bands_n = (n_emb - 1) // 2
    tt = np.linspace(0.0, 1.0, seq, dtype=np.float32)[:, None].astype(np.float64)
    ww = (2.0 * math.pi / seq) * np.arange(seq, dtype=np.float64)[:, None]
    bands = np.linspace(1e-4, bands_n - 1, bands_n, dtype=np.float32)[None, :].astype(np.float64)
    emb = np.concatenate([tt, np.cos(bands * ww), -np.sin(bands * ww)], axis=-1)
    emb = np.pad(emb, ((0, 0), (0, LANE - n_emb))).astype(np.float32)
    w1p = jnp.pad(w1, ((0, LANE - n_emb), (0, 0)))
    deltas = np.abs(np.linspace(HY_MIN_DECAY, HY_MAX_DECAY, HY_W, dtype=np.float32))
    deltas = np.tile(deltas, 4)[None, :]
    n = w4.shape[1]
    tl = min(seq, 512)
    full = lambda a: pl.BlockSpec(a.shape, lambda i: (0,) * a.ndim)
    args = (w1p, b1.reshape(1, fh), w2, b2.reshape(1, fh), w3, b3.reshape(1, fh), w4, freq.reshape(1, fh),
            jnp.asarray(deltas))
    return pl.pallas_call(
        functools.partial(_hy_filter_kernel, seq=seq),
        grid=(seq // tl,),
        in_specs=[pl.BlockSpec((tl, LANE), lambda i: (i, 0))] + [full(a) for a in args],
        out_specs=[pl.BlockSpec((tl, n), lambda i: (i, 0)), pl.BlockSpec((1, n), lambda i: (0, 0))],
        out_shape=[jax.ShapeDtypeStruct((seq, n), F32), jax.ShapeDtypeStruct((1, n), F32)],
        compiler_params=_cparams(1),
        name="hyena_filter_taps",
    )(jnp.asarray(emb), *args)


def _fft_split(seq):
    n = 2 * seq
    n1 = FFT_N1 if n > 1024 else 1
    return n, n1, n // n1


def _cis(idx, n):
    ph = 2.0 * np.pi * (idx % n) / n
    return np.cos(ph), -np.sin(ph)


def _dft_tables_short(seq):
    n = 2 * seq
    cr, ci = _cis(np.arange(n)[:, None] * np.arange(seq)[None, :], n)
    w_fwd = np.concatenate([cr, ci], axis=0)
    w_inv = np.concatenate([cr.T, ci.T], axis=1) / n
    return dict(w_fwd=jnp.asarray(w_fwd, BF16), w_inv=jnp.asarray(w_inv, BF16))


def _dft_tables_long(seq):
    n, n1, n2 = _fft_split(seq)
    t1 = np.arange(n1)[:, None, None]
    f2 = np.arange(n2)[None, :, None]
    t2 = np.arange(n2 // 2)[None, None, :]
    cr, ci = _cis(f2 * (t1 + n1 * t2), n)
    w1 = np.concatenate([cr, ci], axis=1)
    w4 = np.concatenate([np.swapaxes(cr, 1, 2), np.swapaxes(ci, 1, 2)], axis=2) / n
    j = (n1 - t1) % n1 + n1 * t2
    br, bi = _cis(f2 * (n - j), n)
    w1f = np.concatenate([np.concatenate([cr, br], axis=2), np.concatenate([ci, bi], axis=2)], axis=1)
    gr, gi = _cis(np.arange(n1)[:, None] * np.arange(n1)[None, :], n1)
    wd = np.concatenate([np.concatenate([gr, -gi], axis=1), np.concatenate([gi, gr], axis=1)], axis=0)
    return dict(w1=jnp.asarray(w1, BF16), w4=jnp.asarray(w4, BF16), w1f=jnp.asarray(w1f, BF16),
                wd=jnp.asarray(wd, BF16), wdi=jnp.asarray(wd.T, BF16))


def _dft_rows_kernel(w_ref, x_ref, o_ref):
    o_ref[0] = _dot(w_ref[...], x_ref[0].astype(BF16)).astype(o_ref.dtype)


def _dft_rows(w, x, tn):
    nb, k, cols = x.shape
    m = w.shape[0]
    return pl.pallas_call(
        _dft_rows_kernel,
        grid=(nb, cols // tn),
        in_specs=[pl.BlockSpec((m, k), lambda b, j: (0, 0)),
                  pl.BlockSpec((1, k, tn), lambda b, j: (b, 0, j))],
        out_specs=pl.BlockSpec((1, m, tn), lambda b, j: (b, 0, j)),
        out_shape=jax.ShapeDtypeStruct((nb, m, cols), BF16),
        compiler_params=_cparams(2),
        name="hyena_dft_rows",
    )(w, x)


def _idft_gate_kernel(w_ref, b_ref, xg_ref, z_ref, sk_ref, o_ref):
    y = _dot(w_ref[...], b_ref[0])
    z = z_ref[0]
    o_ref[0] = xg_ref[0] * (y + z * sk_ref[...])


def _idft_gate(w, bc, xg, z, skip_t, tn):
    nb, k, cols = bc.shape
    m = w.shape[0]
    return pl.pallas_call(
        _idft_gate_kernel,
        grid=(nb, cols // tn),
        in_specs=[pl.BlockSpec((m, k), lambda b, j: (0, 0)),
                  pl.BlockSpec((1, k, tn), lambda b, j: (b, 0, j)),
                  pl.BlockSpec((1, m, tn), lambda b, j: (b, 0, j)),
                  pl.BlockSpec((1, m, tn), lambda b, j: (b, 0, j)),
                  pl.BlockSpec((1, tn), lambda b, j: (0, j))],
        out_specs=pl.BlockSpec((1, m, tn), lambda b, j: (b, 0, j)),
        out_shape=jax.ShapeDtypeStruct((nb, m, cols), F32),
        compiler_params=_cparams(2),
        name="hyena_idft_gate",
    )(w, bc, xg, z, skip_t)


def _spec_combine_kernel(af_ref, ab_ref, nf_ref, nb_ref, o_ref):
    inv = 1.0 / (nf_ref[...] + nb_ref[...])
    n = o_ref.shape[1]
    o_ref[0] = (af_ref[0, :n].astype(F32) + ab_ref[0, :n].astype(F32)) * inv
    o_ref[1] = (af_ref[0, n:].astype(F32) - ab_ref[0, n:].astype(F32)) * inv


def _spec_mul_kernel(a_ref, k_ref, o_ref):
    n = k_ref.shape[1]
    xr, xi = a_ref[0, :n].astype(F32), a_ref[0, n:].astype(F32)
    kr, ki = k_ref[0], k_ref[1]
    o_ref[0, :n] = (xr * kr - xi * ki).astype(o_ref.dtype)
    o_ref[0, n:] = (xr * ki + xi * kr).astype(o_ref.dtype)


def _hyena_short(seq, taps, nrm, uc, skip):
    n = 2 * seq
    tabs = _dft_tables_short(seq)
    cf = taps.shape[1]
    c = HY_W
    a = _dft_rows(tabs["w_fwd"], taps.reshape(1, seq, cf), tn=cf)
    nblk = cf // 2 // c
    spec = pl.pallas_call(
        _spec_combine_kernel,
        grid=(nblk,),
        in_specs=[pl.BlockSpec((1, 2 * n, c), lambda j: (0, 0, j)),
                  pl.BlockSpec((1, 2 * n, c), lambda j: (0, 0, j + nblk)),
                  pl.BlockSpec((1, c), lambda j: (0, j)),
                  pl.BlockSpec((1, c), lambda j: (0, j + nblk))],
        out_specs=pl.BlockSpec((2, n, c), lambda j: (0, 0, j)),
        out_shape=jax.ShapeDtypeStruct((2, n, cf // 2), F32),
        compiler_params=_cparams(1),
        name="hyena_spec_combine",
    )(a, a, nrm, nrm)
    nb = uc.shape[1]
    z = uc[2]
    for order in range(2):
        a = _dft_rows(tabs["w_fwd"], z, tn=c)
        bc = pl.pallas_call(
            _spec_mul_kernel,
            grid=(nb,),
            in_specs=[pl.BlockSpec((1, 2 * n, c), lambda b: (b, 0, 0)),
                      pl.BlockSpec((2, n, c), lambda b: (0, 0, order))],
            out_specs=pl.BlockSpec((1, 2 * n, c), lambda b: (b, 0, 0)),
            out_shape=jax.ShapeDtypeStruct((nb, 2 * n, c), BF16),
            compiler_params=_cparams(1),
            name="hyena_spec_mul",
        )(a, spec)
        z = _idft_gate(tabs["w_inv"], bc, uc[order], z, skip[order].reshape(1, c), tn=c)
    return z


def _slab_pitch(n2):
    return 2 * n2 + 8


def _stage2(a_scr, wd_ref, f2, n1, n2, pitch):
    ar = a_scr[pl.ds(f2, n1, stride=pitch), :]
    ai = a_scr[pl.ds(n2 + f2, n1, stride=pitch), :]
    return _dot(wd_ref[...], jnp.concatenate([ar, ai], axis=0).astype(BF16))


def _hy_spectrum_kernel(hf_ref, hb_ref, nf_ref, nb_ref, w1f_ref, wd_ref, o_ref, a_scr, *, n1, n2, fb):
    i = pl.program_id(1)
    pitch = _slab_pitch(n2)
    half = n2 // 2

    @pl.when(i == 0)
    def _():
        def body(t1, carry):
            past = hf_ref[pl.ds(t1, half, stride=n1), :]
            fut = hb_ref[pl.ds(jnp.where(t1 == 0, 0, n1 - t1), half, stride=n1), :]
            xs = jnp.concatenate([past, fut], axis=0).astype(BF16)
            a_scr[pl.ds(pl.multiple_of(t1 * pitch, 8), 2 * n2), :] = _dot(w1f_ref[t1], xs)
            return carry
        lax.fori_loop(0, n1, body, 0, unroll=8)

    inv = 1.0 / (nf_ref[...] + nb_ref[...])
    for jj in range(fb):
        x = _stage2(a_scr, wd_ref, i * fb + jj, n1, n2, pitch)
        o_ref[0, jj] = x[:n1] * inv
        o_ref[1, jj] = x[n1:] * inv


def _hy_fftconv_kernel(z_ref, xg_ref, k_ref, sk_ref, w1_ref, wd_ref, wdi_ref, w4_ref, o_ref, a_scr,
                       *, n1, n2, fb):
    i = pl.program_id(2)
    pitch = _slab_pitch(n2)
    half = n2 // 2

    @pl.when(i == 0)
    def _():
        def body(t1, carry):
            xs = z_ref[0, pl.ds(t1, half, stride=n1), :].astype(BF16)
            a_scr[pl.ds(pl.multiple_of(t1 * pitch, 8), 2 * n2), :] = _dot(w1_ref[t1], xs)
            return carry
        lax.fori_loop(0, n1, body, 0, unroll=8)

    for jj in range(fb):
        f2 = i * fb + jj
        x = _stage2(a_scr, wd_ref, f2, n1, n2, pitch)
        xr, xi = x[:n1], x[n1:]
        kr, ki = k_ref[0, jj], k_ref[1, jj]
        y = jnp.concatenate([xr * kr - xi * ki, xr * ki + xi * kr], axis=0).astype(BF16)
        bv = _dot(wdi_ref[...], y)
        a_scr[pl.ds(f2, n1, stride=pitch), :] = bv[:n1]
        a_scr[pl.ds(n2 + f2, n1, stride=pitch), :] = bv[n1:]

    @pl.when(i == pl.num_programs(2) - 1)
    def _():
        sk = sk_ref[...]

        def body(t1, carry):
            slab = a_scr[pl.ds(pl.multiple_of(t1 * pitch, 8), 2 * n2), :].astype(BF16)
            y = _dot(w4_ref[t1], slab)
            rows = pl.ds(t1, half, stride=n1)
            o_ref[0, rows, :] = xg_ref[0, rows, :] * (y + z_ref[0, rows, :] * sk)
            return carry
        lax.fori_loop(0, n1, body, 0, unroll=8)


def _hyena_long(seq, taps, nrm, uc, skip, fb=8):
    n, n1, n2 = _fft_split(seq)
    tabs = _dft_tables_long(seq)
    pitch = _slab_pitch(n2)
    c = HY_W
    nblk = c // LANE
    cf = taps.shape[1]
    nfil = cf // 2 // LANE
    const = dict(pipeline_mode=pl.Buffered(1))
    scratch = [pltpu.VMEM((n1 * pitch, LANE), F32)]
    spec = pl.pallas_call(
        functools.partial(_hy_spectrum_kernel, n1=n1, n2=n2, fb=fb),
        grid=(nfil, n2 // fb),
        in_specs=[pl.BlockSpec((seq, LANE), lambda j, i: (0, j), **const),
                  pl.BlockSpec((seq, LANE), lambda j, i: (0, j + nfil), **const),
                  pl.BlockSpec((1, LANE), lambda j, i: (0, j)),
                  pl.BlockSpec((1, LANE), lambda j, i: (0, j + nfil)),
                  pl.BlockSpec(tabs["w1f"].shape, lambda j, i: (0, 0, 0), **const),
                  pl.BlockSpec(tabs["wd"].shape, lambda j, i: (0, 0), **const)],
        out_specs=pl.BlockSpec((2, fb, n1, LANE), lambda j, i: (0, i, 0, j)),
        out_shape=jax.ShapeDtypeStruct((2, n2, n1, cf // 2), F32),
        scratch_shapes=scratch,
        compiler_params=_cparams(2),
        name="hyena_filter_spectrum",
    )(taps, taps, nrm, nrm, tabs["w1f"], tabs["wd"])
    nb = uc.shape[1]
    z = uc[2]
    for order in range(2):
        blk = lambda b, j, i: (b, 0, j)
        z = pl.pallas_call(
            functools.partial(_hy_fftconv_kernel, n1=n1, n2=n2, fb=fb),
            grid=(nb, nblk, n2 // fb),
            in_specs=[pl.BlockSpec((1, seq, LANE), blk, **const),
                      pl.BlockSpec((1, seq, LANE), blk, **const),
                      pl.BlockSpec((2, fb, n1, LANE), lambda b, j, i: (0, i, 0, order * nblk + j)),
                      pl.BlockSpec((1, LANE), lambda b, j, i: (0, j)),
                      pl.BlockSpec(tabs["w1"].shape, lambda b, j, i: (0, 0, 0), **const),
                      pl.BlockSpec(tabs["wd"].shape, lambda b, j, i: (0, 0), **const),
                      pl.BlockSpec(tabs["wdi"].shape, lambda b, j, i: (0, 0), **const),
                      pl.BlockSpec(tabs["w4"].shape, lambda b, j, i: (0, 0, 0), **const)],
            out_specs=pl.BlockSpec((1, seq, LANE), blk),
            out_shape=jax.ShapeDtypeStruct((nb, seq, c), F32),
            scratch_shapes=scratch,
            compiler_params=_cparams(3),
            name="hyena_fft_conv",
        )(z, uc[order], spec, skip[order].reshape(1, c), tabs["w1"], tabs["wd"], tabs["wdi"], tabs["w4"])
    return z


def _hyena(p, lw, row0, seq, n_batch):
    taps, nrm = _hyena_filter_taps(seq, lw["hy_w1"], lw["hy_b1"], lw["hy_w2"], lw["hy_b2"], lw["hy_w3"],
                                   lw["hy_b3"], lw["hy_w4"], lw["hy_freq"])
    uc = _short_conv(p, lw["hy_conv_w"], lw["hy_conv_b"], row0, seq, n_batch)
    run = _hyena_long if _fft_split(seq)[1] > 1 else _hyena_short
    return run(seq, taps, nrm, uc, lw["hy_skip"]).reshape(n_batch * seq, HY_W)


def _mla_qkv_kernel(qa_ref, kva_ref, kr_ref, cs_ref, wq_ref, wkv_ref, gqa_ref, gkva_ref, gqn_ref, gqr_ref,
                    gkn_ref, gkr_ref, q_ref, k_ref, v_ref):
    cs = cs_ref[...]
    lane = lax.broadcasted_iota(jnp.int32, cs.shape, 1)
    low = lane < MLA_ROPE

    def rope(pair, gain2):
        ms = jnp.sum(jnp.where(low, pair * pair, 0.0), axis=-1, keepdims=True) * (1.0 / MLA_ROPE)
        t = pair * lax.rsqrt(ms + RMS_EPS) * gain2 * cs
        return jnp.where(low, t + pltpu.roll(t, MLA_ROPE, 1), 0.0)

    qq = _dot(_rms(qa_ref[...], gqa_ref[...]).astype(BF16), wq_ref[...])
    kv = _dot(_rms(kva_ref[...], gkva_ref[...]).astype(BF16), wkv_ref[...])
    kr = rope(kr_ref[...], gkr_ref[...])
    hp = MLA_HEAD_PAD
    tm = kr.shape[0]
    qscale = MLA_SCALE * LOG2_E
    ones_row = (lax.broadcasted_iota(jnp.int32, (V_ROWS - MLA_V, tm), 0) == 0).astype(BF16)
    for h in range(MLA_HEADS):
        qn = _rms(qq[:, h * hp:h * hp + MLA_NOPE], gqn_ref[...])
        qr = rope(qq[:, h * hp + MLA_NOPE:(h + 1) * hp], gqr_ref[...])
        q_ref[h, 0:LANE, :] = (qn * qscale).T.astype(BF16)
        q_ref[h, LANE:2 * LANE, :] = (qr * qscale).T.astype(BF16)
        kn = _rms(kv[:, h * hp:h * hp + MLA_NOPE], gkn_ref[...])
        k_ref[h, :, 0:LANE] = kn.astype(BF16)
        k_ref[h, :, LANE:2 * LANE] = kr.astype(BF16)
        v_ref[h, 0:MLA_V, :] = kv[:, h * hp + MLA_NOPE:(h + 1) * hp].T.astype(BF16)
        v_ref[h, MLA_V:V_ROWS, :] = ones_row


def _rope_table(seq, n_batch, ctx_rows):
    rows = seq // GRID_W
    row = np.repeat(np.arange(rows, dtype=np.float32), GRID_W)
    col = np.tile(np.arange(GRID_W, dtype=np.float32), rows)
    half = MLA_ROPE // 2
    inv = (ROPE_THETA ** (-np.arange(0, half, 2, dtype=np.float32) / half)).astype(np.float32)
    ar = (row[:, None] * inv).astype(np.float64)
    ac = (col[:, None] * inv).astype(np.float64)
    cos = np.concatenate([np.cos(ar), np.cos(ar), np.cos(ac), np.cos(ac)], axis=1)
    sin = np.concatenate([-np.sin(ar), np.sin(ar), -np.sin(ac), np.sin(ac)], axis=1)
    lat = np.tile(np.concatenate([cos, sin], axis=1), (n_batch, 1))
    ctx = np.concatenate([np.ones((ctx_rows, MLA_ROPE)), np.zeros((ctx_rows, MLA_ROPE))], axis=1)
    return jnp.asarray(np.concatenate([lat, ctx], axis=0), F32)


def _pair_gain(g):
    return jnp.concatenate([g, _rope_swap(g)]).reshape(1, 2 * MLA_ROPE)


def _mla_qkv(p, cs, lw, tm=512):
    n_rows = p.shape[0]
    hd, hp = MLA_HEADS, MLA_HEAD_PAD
    w_uq = lw["w_uq"].reshape(-1, hd, MLA_QK)
    wq = jnp.concatenate([w_uq, _rope_swap(w_uq[..., MLA_NOPE:])], axis=-1).reshape(-1, hd * hp).astype(BF16)
    wkv = lw["w_ukv"].astype(BF16)
    ql, kvl = wq.shape[0], wkv.shape[0]
    vec = lambda a: a.reshape(1, -1)
    full = lambda a: pl.BlockSpec(a.shape, lambda i: (0,) * a.ndim)
    args = (wq, wkv, vec(lw["q_a_norm"]), vec(lw["kv_a_norm"]), vec(lw["q_nope_norm"]),
            _pair_gain(lw["q_rope_norm"]), vec(lw["k_nope_norm"]), _pair_gain(lw["k_rope_norm"]))
    return pl.pallas_call(
        _mla_qkv_kernel,
        grid=(n_rows // tm,),
        in_specs=[pl.BlockSpec((tm, ql), lambda i: (i, COL_QA // ql)),
                  pl.BlockSpec((tm, kvl), lambda i: (i, COL_KVA // kvl)),
                  pl.BlockSpec((tm, LANE), lambda i: (i, COL_KR // LANE)),
                  pl.BlockSpec((tm, LANE), lambda i: (i, 0))] + [full(a) for a in args],
        out_specs=[pl.BlockSpec((hd, hp, tm), lambda i: (0, 0, i)),
                   pl.BlockSpec((hd, tm, hp), lambda i: (0, i, 0)),
                   pl.BlockSpec((hd, V_ROWS, tm), lambda i: (0, 0, i))],
        out_shape=[jax.ShapeDtypeStruct((hd, hp, n_rows), BF16),
                   jax.ShapeDtypeStruct((hd, n_rows, hp), BF16),
                   jax.ShapeDtypeStruct((hd, V_ROWS, n_rows), BF16)],
        compiler_params=_cparams(1),
        name="mla_qkv",
    )(p, p, p, cs, *args)


def _attn_kernel(*refs, with_latent, tk):
    if with_latent:
        q_ref, k_ref, v_ref, kc_ref, vc_ref, o_ref, s_scr = refs
    else:
        q_ref, kc_ref, vc_ref, o_ref, s_scr = refs
    qt = q_ref[0]
    n_ctx = kc_ref.shape[1]
    chunks = [(kc_ref, vc_ref, 0, n_ctx, 0)]
    if with_latent:
        chunks += [(k_ref, v_ref, j * tk, tk, n_ctx + j * tk) for j in range(k_ref.shape[1] // tk)]
    m = None
    for kr, _, r0, rn, s0 in chunks:
        s = _dot(kr[0, r0:r0 + rn, :], qt)
        s_scr[s0:s0 + rn, :] = s
        mj = jnp.max(s, axis=0, keepdims=True)
        m = mj if m is None else jnp.maximum(m, mj)
    acc = None
    for _, vr, r0, rn, s0 in chunks:
        p = jnp.exp2((s_scr[s0:s0 + rn, :] - m).astype(BF16))
        part = _dot(vr[0, :, r0:r0 + rn], p)
        acc = part if acc is None else acc + part
    o_ref[...] = (acc[:MLA_V] / acc[MLA_V:MLA_V + 1]).astype(o_ref.dtype)


def _attention(qt, k, vt, seq, ctx_len, n_batch, latent, tq=256, tk=1024):
    hd, hp, n_rows = qt.shape
    lat_rows = n_batch * seq
    cblk = lat_rows // ctx_len
    kc_spec = pl.BlockSpec((1, ctx_len, hp), lambda b, h, i: (h, cblk + b, 0))
    vc_spec = pl.BlockSpec((1, V_ROWS, ctx_len), lambda b, h, i: (h, 0, cblk + b))
    if latent:
        nq = seq // tq
        in_specs = [pl.BlockSpec((1, hp, tq), lambda b, h, i: (h, 0, b * nq + i)),
                    pl.BlockSpec((1, seq, hp), lambda b, h, i: (h, b, 0)),
                    pl.BlockSpec((1, V_ROWS, seq), lambda b, h, i: (h, 0, b)), kc_spec, vc_spec]
        args = (qt, k, vt, k, vt)
        out_spec = pl.BlockSpec((MLA_V, tq), lambda b, h, i: (h, b * nq + i))
        out_cols, n_keys = lat_rows, seq + ctx_len
    else:
        nq = 1
        tq = ctx_len
        in_specs = [pl.BlockSpec((1, hp, tq), lambda b, h, i: (h, 0, cblk + b)), kc_spec, vc_spec]
        args = (qt, k, vt)
        out_spec = pl.BlockSpec((MLA_V, tq), lambda b, h, i: (h, b))
        out_cols, n_keys = n_batch * ctx_len, ctx_len
    return pl.pallas_call(
        functools.partial(_attn_kernel, with_latent=latent, tk=tk),
        grid=(n_batch, hd, nq),
        in_specs=in_specs,
        out_specs=out_spec,
        out_shape=jax.ShapeDtypeStruct((hd * MLA_V, out_cols), BF16),
        scratch_shapes=[pltpu.VMEM((n_keys, tq), F32)],
        compiler_params=_cparams(3),
        name="mla_attention" if latent else "mla_attention_ctx",
    )(*args)


def _merge_kernel(x_ref, mod_ref, of_ref, ob_ref, g_ref, yb_ref, yc_ref, ga_ref, gb_ref, gc_ref,
                  gain_ref, wa_ref, wb_ref, wc_ref, wo_ref, o_ref):
    m = mod_ref[0]
    o = of_ref[...] + ob_ref[...]
    gain = gain_ref[...]
    ya = jnp.concatenate([_rms(o[:, h * HG_D:(h + 1) * HG_D], gain) for h in range(HG_HEADS)], axis=1)
    ya = (ya * _silu(g_ref[...])).astype(BF16)
    mix = (_sigmoid(ga_ref[...]) * _dot(ya, wa_ref[...])
           + _sigmoid(gb_ref[...]) * _dot(yb_ref[...].astype(BF16), wb_ref[...])
           + _sigmoid(gc_ref[...]) * _dot_tn(yc_ref[...], wc_ref[...]))
    o_ref[...] = x_ref[...] + m[5:6] * _dot(mix.astype(BF16), wo_ref[...])


def _merge(xa, mod, o_f, o_b, p, y_b, y_c, lw, n_rows, seq, n_batch, tm=512):
    d = xa.shape[1]
    hk = HG_HEADS * HG_D
    row = lambda w: pl.BlockSpec((tm, w), lambda i: (i, 0))
    pcol = lambda w, c: pl.BlockSpec((tm, w), lambda i: (i, c // w))
    full = lambda a: pl.BlockSpec(a.shape, lambda i: (0,) * a.ndim)
    ws = (lw["hg_out_norm"].reshape(1, HG_D), lw["w_br_a"].astype(BF16), lw["w_br_b"].astype(BF16),
          lw["w_br_c"].astype(BF16), lw["w_out"].astype(BF16))
    return pl.pallas_call(
        _merge_kernel,
        grid=(n_rows // tm,),
        in_specs=[row(d), pl.BlockSpec((1, N_MOD, d), _group_map(seq // tm, n_batch)),
                  row(hk), row(hk), pcol(hk, COL_G), row(HY_W),
                  pl.BlockSpec((MLA_HEADS * MLA_V, tm), lambda i: (0, i)),
                  pcol(d, COL_GA), pcol(d, COL_GB), pcol(d, COL_GC)] + [full(a) for a in ws],
        out_specs=row(d),
        out_shape=jax.ShapeDtypeStruct((n_rows, d), F32),
        compiler_params=_cparams(1),
        name="merge",
    )(xa, mod, o_f, o_b, p, y_b, y_c, p, p, p, *ws)


def kernel(x, c, ctx, c_ctx, ada_w, ada_b, ffn1_norm, ffn1_w13, ffn1_w2, mix_norm, w_in, hg_lb_logits, hg_out_norm, hy_conv_w, hy_conv_b, hy_w1, hy_b1, hy_w2, hy_b2, hy_w3, hy_b3, hy_w4, hy_freq, hy_skip, q_a_norm, w_uq, kv_a_norm, w_ukv, q_nope_norm, q_rope_norm, k_nope_norm, k_rope_norm, w_br_a, w_br_b, w_br_c, w_out, ffn2_norm, ffn2_w13, ffn2_w2):
    stacked = dict(
        ada_w=ada_w, ada_b=ada_b, ffn1_norm=ffn1_norm, ffn1_w13=ffn1_w13, ffn1_w2=ffn1_w2, mix_norm=mix_norm,
        w_in=w_in, hg_out_norm=hg_out_norm, hy_conv_w=hy_conv_w, hy_conv_b=hy_conv_b, hy_w1=hy_w1, hy_b1=hy_b1,
        hy_w2=hy_w2, hy_b2=hy_b2, hy_w3=hy_w3, hy_b3=hy_b3, hy_w4=hy_w4, hy_freq=hy_freq, hy_skip=hy_skip,
        q_a_norm=q_a_norm, w_uq=w_uq, kv_a_norm=kv_a_norm, w_ukv=w_ukv, q_nope_norm=q_nope_norm,
        q_rope_norm=q_rope_norm, k_nope_norm=k_nope_norm, k_rope_norm=k_rope_norm, w_br_a=w_br_a,
        w_br_b=w_br_b, w_br_c=w_br_c, w_out=w_out, ffn2_norm=ffn2_norm, ffn2_w13=ffn2_w13, ffn2_w2=ffn2_w2)
    n_batch, seq, d = x.shape
    ctx_len = ctx.shape[1]
    depth = ada_w.shape[0]
    lat_rows, ctx_rows = n_batch * seq, n_batch * ctx_len
    all_rows = lat_rows + ctx_rows
    assert seq % 512 == 0 and ctx_rows % 512 == 0 and seq % ctx_len == 0 and seq % GRID_W == 0
    assert ctx_len % HG_CHUNK == 0 and n_batch < 8

    xa = jnp.concatenate([x.reshape(lat_rows, d), ctx.reshape(ctx_rows, d)], axis=0)
    cs = jnp.concatenate([c, c_ctx.reshape(1, d), jnp.zeros((7 - n_batch, d), F32)], axis=0)
    rope_cs = _rope_table(seq, n_batch, ctx_rows)
    zero_state = jnp.zeros((n_batch, HG_HEADS, HG_D, HG_D), F32)

    for l in range(depth):
        lw = {name: val[l] for name, val in stacked.items()}
        need_ctx = l < depth - 1
        mod = _modulation(cs, lw["ada_w"], lw["ada_b"])
        xa = _half_ffn(xa, mod, lw["ffn1_norm"], lw["ffn1_w13"], lw["ffn1_w2"], 0, all_rows, seq, n_batch)
        p = _in_projection(xa, mod, lw["mix_norm"], _pack_w_in(lw["w_in"]).astype(BF16), seq, n_batch)

        mix_rows = all_rows if need_ctx else lat_rows
        o_dir = []
        for rev in (False, True):
            lg = hg_lb_logits[1 if rev else 0]
            o_c, s_c = _hgrn2_scan(p, lg, zero_state, l, rev, lat_rows, ctx_len, n_batch)
            o_l, _ = _hgrn2_scan(p, lg, s_c, l, rev, 0, seq, n_batch)
            o_dir.append(jnp.concatenate([o_l, o_c], axis=0) if need_ctx else o_l)

        y_b = _hyena(p, lw, 0, seq, n_batch)
        if need_ctx:
            y_b = jnp.concatenate([y_b, _hyena(p, lw, lat_rows, ctx_len, n_batch)], axis=0)

        qt, k, vt = _mla_qkv(p, rope_cs, lw)
        y_c = _attention(qt, k, vt, seq, ctx_len, n_batch, latent=True)
        if need_ctx:
            y_c = jnp.concatenate([y_c, _attention(qt, k, vt, seq, ctx_len, n_batch, latent=False)], axis=1)

        xa = _merge(xa, mod, o_dir[0], o_dir[1], p, y_b, y_c, lw, mix_rows, seq, n_batch)
        xa = _half_ffn(xa, mod, lw["ffn2_norm"], lw["ffn2_w13"], lw["ffn2_w2"], 6, mix_rows, seq, n_batch)
    return xa[:lat_rows].reshape(n_batch, seq, d)
```

```python
import functools
import math

import numpy as np
import jax
import jax.numpy as jnp
from jax import lax
from jax.experimental import pallas as pl
from jax.experimental.pallas import tpu as pltpu

F32 = jnp.float32
BF16 = jnp.bfloat16

RMS_EPS = 1e-6
N_MOD = 9
GRID_W = 64
ROPE_THETA = 10000.0
HG_HEADS = 4
HG_D = 128
HG_CHUNK = 128
HY_W = 512
HY_TARGET = 1e-2
HY_MIN_DECAY = math.log(HY_TARGET) / 1.5
HY_MAX_DECAY = math.log(HY_TARGET) / 0.3
HY_SHIFT = 0.05
MLA_HEADS = 4
MLA_NOPE = 128
MLA_ROPE = 64
MLA_V = 128
MLA_QK = MLA_NOPE + MLA_ROPE
MLA_SCALE = MLA_QK ** -0.5
MLA_HEAD_PAD = 256
V_ROWS = MLA_V + 16
LOG2_E = math.log2(math.e)
FFT_N1 = 128
LANE = 128

VMEM_LIMIT = 52 * 1024 * 1024

COL_GA, COL_GB, COL_GC = 0, 1024, 2048
COL_Q, COL_ZF, COL_ZB, COL_IV, COL_G = 3072, 3584, 4096, 4608, 5120
COL_HY = 5632
COL_QA, COL_KVA, COL_KR = 7168, 7424, 7552
IN_PACKED = 7680


def _cparams(n_axes):
    return pltpu.CompilerParams(dimension_semantics=("arbitrary",) * n_axes,
                                vmem_limit_bytes=VMEM_LIMIT)


def _dot(a, b):
    return jnp.dot(a, b, preferred_element_type=F32)


def _dot_nt(a, b):
    return lax.dot_general(a, b, (((1,), (1,)), ((), ())), preferred_element_type=F32)


def _dot_tn(a, b):
    return lax.dot_general(a, b, (((0,), (0,)), ((), ())), preferred_element_type=F32)


def _sigmoid(x):
    return 1.0 / (1.0 + jnp.exp(-x))


def _silu(x):
    return x * _sigmoid(x)


def _rms(x, gain):
    return x * lax.rsqrt(jnp.mean(x * x, axis=-1, keepdims=True) + RMS_EPS) * gain


def _mod_kernel(c_ref, w_ref, b_ref, o_ref):
    o_ref[...] = _dot(_silu(c_ref[...]).astype(BF16), w_ref[...]) + b_ref[...]


def _modulation(cs, ada_w, ada_b):
    g, d = cs.shape
    n = ada_w.shape[1]
    tn = n // 4
    out = pl.pallas_call(
        _mod_kernel,
        grid=(n // tn,),
        in_specs=[pl.BlockSpec((g, d), lambda j: (0, 0)),
                  pl.BlockSpec((d, tn), lambda j: (0, j)),
                  pl.BlockSpec((1, tn), lambda j: (0, j))],
        out_specs=pl.BlockSpec((g, tn), lambda j: (0, j)),
        out_shape=jax.ShapeDtypeStruct((g, n), F32),
        compiler_params=_cparams(1),
        name="modulation",
    )(cs, ada_w.astype(BF16), ada_b.reshape(1, n))
    return out.reshape(g, N_MOD, d)


def _ffn_kernel(x_ref, mod_ref, g_ref, w13_ref, w2_ref, o_ref, *, idx, ff, n_chunks):
    x = x_ref[...]
    m = mod_ref[0]
    h = (_rms(x, g_ref[...]) * (1.0 + m[idx + 1:idx + 2]) + m[idx:idx + 1]).astype(BF16)
    ck = ff // n_chunks
    acc = jnp.zeros(x.shape, F32)
    for j in range(n_chunks):
        a = _dot(h, w13_ref[:, j * ck:(j + 1) * ck])
        b = _dot(h, w13_ref[:, ff + j * ck:ff + (j + 1) * ck])
        acc = acc + _dot((_silu(a) * b).astype(BF16), w2_ref[j * ck:(j + 1) * ck, :])
    o_ref[...] = x + (0.5 * m[idx + 2:idx + 3]) * acc


def _group_map(tiles_per_batch, n_batch):
    return lambda i: (jnp.minimum(i // tiles_per_batch, n_batch), 0, 0)


def _half_ffn(xa, mod, gain, w13, w2, idx, n_rows, seq, n_batch, tm=512):
    d = xa.shape[1]
    ff = w2.shape[0]
    n_chunks = ff // 256
    const = dict(pipeline_mode=pl.Buffered(1))
    return pl.pallas_call(
        functools.partial(_ffn_kernel, idx=idx, ff=ff, n_chunks=n_chunks),
        grid=(n_rows // tm,),
        in_specs=[pl.BlockSpec((tm, d), lambda i: (i, 0)),
                  pl.BlockSpec((1, N_MOD, d), _group_map(seq // tm, n_batch)),
                  pl.BlockSpec((1, d), lambda i: (0, 0)),
                  pl.BlockSpec((d, 2 * ff), lambda i: (0, 0), **const),
                  pl.BlockSpec((ff, d), lambda i: (0, 0), **const)],
        out_specs=pl.BlockSpec((tm, d), lambda i: (i, 0)),
        out_shape=jax.ShapeDtypeStruct((n_rows, d), F32),
        compiler_params=_cparams(1),
        name="half_ffn",
    )(xa, mod, gain.reshape(1, d), w13.astype(BF16), w2.astype(BF16))


def _inproj_kernel(x_ref, mod_ref, g_ref, w_ref, o_ref):
    m = mod_ref[0]
    h = (_rms(x_ref[...], g_ref[...]) * (1.0 + m[4:5]) + m[3:4]).astype(BF16)
    o_ref[...] = _dot(h, w_ref[...])


def _pack_w_in(w_in):
    d = w_in.shape[0]
    hk = HG_HEADS * HG_D
    sizes = (hk, hk, hk, hk, hk, 3 * HY_W, 256, 128, MLA_ROPE, d, d, d)
    offs = np.cumsum((0,) + sizes)
    q, zf, zb, iv, g, hy, qa, kva, kr, ga, gb, gc = (w_in[:, offs[i]:offs[i + 1]] for i in range(12))
    return jnp.concatenate([ga, gb, gc, q, zf, zb, iv, g, hy, qa, kva, kr, _rope_swap(kr)], axis=1)


def _rope_swap(a):
    q = MLA_ROPE // 4
    return jnp.concatenate([a[..., q:2 * q], a[..., :q], a[..., 3 * q:], a[..., 2 * q:3 * q]], axis=-1)


def _in_projection(xa, mod, gain, w_packed, seq, n_batch, tm=512, tn=2560):
    n_rows, d = xa.shape
    n = w_packed.shape[1]
    return pl.pallas_call(
        _inproj_kernel,
        grid=(n // tn, n_rows // tm),
        in_specs=[pl.BlockSpec((tm, d), lambda j, i: (i, 0)),
                  pl.BlockSpec((1, N_MOD, d), lambda j, i: (jnp.minimum(i // (seq // tm), n_batch), 0, 0)),
                  pl.BlockSpec((1, d), lambda j, i: (0, 0)),
                  pl.BlockSpec((d, tn), lambda j, i: (0, j))],
        out_specs=pl.BlockSpec((tm, tn), lambda j, i: (i, j)),
        out_shape=jax.ShapeDtypeStruct((n_rows, n), F32),
        compiler_params=_cparams(2),
        name="in_projection",
    )(xa, mod, gain.reshape(1, d), w_packed)


def _hgrn2_tables(c, rev):
    t = np.arange(c)[:, None]
    u = np.arange(c)[None, :]
    mats = [(u <= t), (u > t)]
    masks = [(t == u)]
    h = c // 2
    while h >= 1:
        mid = (t // (2 * h)) * (2 * h) + h
        mats.append(np.where(t >= mid, (u >= mid) & (u <= t), (u >= t + 1) & (u <= mid - 1)))
        mid_s = (u // (2 * h)) * (2 * h) + h
        masks.append((t // (2 * h) == u // (2 * h)) & (u < mid_s) & (t >= mid))
        h //= 2
    mats = np.stack([m.astype(np.float32) for m in mats])
    masks = np.stack([m.astype(np.float32) for m in masks])
    if rev:
        mats = mats[:, ::-1, ::-1]
        masks = masks[:, ::-1, ::-1]
    return (jnp.asarray(np.ascontiguousarray(mats).reshape(-1, c), BF16),
            jnp.asarray(np.ascontiguousarray(masks), F32))


def _hgrn2_kernel(q_ref, z_ref, v_ref, lg_ref, dst_ref, msk_ref, s0_ref, o_ref, sf_ref, st_ref,
                  *, layer, rev, n_levels):
    j = pl.program_id(1)
    c = q_ref.shape[0]
    hk = HG_HEADS * HG_D

    @pl.when(j == 0)
    def _():
        st_ref[...] = s0_ref[0]

    lg = lg_ref[...]
    e = jnp.exp(lg - jnp.max(lg, axis=0, keepdims=True))
    sm = e / jnp.sum(e, axis=0, keepdims=True)
    lb = jnp.zeros((1, hk), F32)
    for i in range(1, layer + 1):
        lb = lb + sm[i:i + 1]

    f = lb + (1.0 - lb) * _sigmoid(z_ref[...])
    kk = 1.0 - f
    g = jnp.log(f)
    g_hi = g.astype(BF16)
    g_lo = (g - g_hi.astype(F32)).astype(BF16)
    dg = _dot(dst_ref[...], jnp.concatenate([g_hi, g_lo], axis=1))
    dg = dg[:, :hk] + dg[:, hk:]
    qs = _silu(q_ref[...]) * HG_D ** -0.5
    vv = v_ref[...]
    last = 0 if rev else c - 1

    for h in range(HG_HEADS):
        hs = slice(h * HG_D, (h + 1) * HG_D)
        qh, kh, vh = qs[:, hs], kk[:, hs], vv[:, hs].astype(BF16)
        b_in = dg[0:c, hs]
        e_out = jnp.exp(dg[c:2 * c, hs])
        a = _dot_nt(qh.astype(BF16), kh.astype(BF16)) * msk_ref[0]
        for l in range(1, n_levels + 1):
            el = jnp.exp(dg[(1 + l) * c:(2 + l) * c, hs])
            a = a + _dot_nt((qh * el).astype(BF16), (kh * el).astype(BF16)) * msk_ref[l]
        st = st_ref[h]
        o = _dot_nt((qh * jnp.exp(b_in)).astype(BF16), st.astype(BF16)) + _dot(a.astype(BF16), vh)
        o_ref[:, hs] = o
        st_ref[h] = jnp.exp(b_in[last:last + 1, :]) * st + _dot_tn(vh, (kh * e_out).astype(BF16))

    @pl.when(j == pl.num_programs(1) - 1)
    def _():
        sf_ref[0] = st_ref[...]


def _hgrn2_scan(p, logits, s0, layer, rev, row0, seq, n_batch):
    c = HG_CHUNK
    hk = HG_HEADS * HG_D
    nc = seq // c
    base = row0 // c
    n_levels = int(math.log2(c))
    dst, msk = _hgrn2_tables(c, rev)
    col_z = COL_ZB if rev else COL_ZF

    def rows(b, j):
        return b * nc + (nc - 1 - j if rev else j)

    def prow(w):
        return pl.BlockSpec((c, hk), lambda b, j: (base + rows(b, j), w // hk))

    return pl.pallas_call(
        functools.partial(_hgrn2_kernel, layer=layer, rev=rev, n_levels=n_levels),
        grid=(n_batch, nc),
        in_specs=[prow(COL_Q), prow(col_z), prow(COL_IV),
                  pl.BlockSpec(logits.shape, lambda b, j: (0, 0)),
                  pl.BlockSpec(dst.shape, lambda b, j: (0, 0)),
                  pl.BlockSpec(msk.shape, lambda b, j: (0, 0, 0)),
                  pl.BlockSpec((1, HG_HEADS, HG_D, HG_D), lambda b, j: (b, 0, 0, 0))],
        out_specs=[pl.BlockSpec((c, hk), lambda b, j: (rows(b, j), 0)),
                   pl.BlockSpec((1, HG_HEADS, HG_D, HG_D), lambda b, j: (b, 0, 0, 0))],
        out_shape=[jax.ShapeDtypeStruct((n_batch * seq, hk), F32),
                   jax.ShapeDtypeStruct((n_batch, HG_HEADS, HG_D, HG_D), F32)],
        scratch_shapes=[pltpu.VMEM((HG_HEADS, HG_D, HG_D), F32)],
        compiler_params=_cparams(2),
        name="hgrn2_scan_bwd" if rev else "hgrn2_scan_fwd",
    )(p, p, p, logits, dst, msk, s0)


def _shortconv_kernel(u_ref, w_ref, b_ref, o_ref):
    u = u_ref[...]
    n = u.shape[0]
    row = lax.broadcasted_iota(jnp.int32, u.shape, 0)
    prev = jnp.where(row == 0, 0.0, pltpu.roll(u, 1, 0))
    nxt = jnp.where(row == n - 1, 0.0, pltpu.roll(u, n - 1, 0))
    w = w_ref[...]
    o_ref[0, 0, 0] = prev * w[0:1] + u * w[1:2] + nxt * w[2:3] + b_ref[...]


def _shortconv_t1major_kernel(u_ref, w_ref, b_ref, o_ref, *, n1):
    half = u_ref.shape[0] // n1
    w = w_ref[...]
    b = b_ref[...]
    row = lax.broadcasted_iota(jnp.int32, (half, LANE), 0)
    col = lambda t1: u_ref[pl.ds(t1, half, stride=n1), :]
    before = jnp.where(row == 0, 0.0, pltpu.roll(col(n1 - 1), 1, 0))
    after = jnp.where(row == half - 1, 0.0, pltpu.roll(col(0), half - 1, 0))

    def body(t1, carry):
        prev, cur = carry
        nxt = jnp.where(t1 == n1 - 1, after, col(jnp.minimum(t1 + 1, n1 - 1)))
        o_ref[0, 0, 0, pl.ds(pl.multiple_of(t1 * half, 8), half), :] = prev * w[0:1] + cur * w[1:2] + nxt * w[2:3] + b
        return cur, nxt

    lax.fori_loop(0, n1, body, (before, col(0)), unroll=4)


def _short_conv(p, w, b, row0, seq, n_batch, n1):
    nb = 3 * HY_W // LANE
    per = HY_W // LANE
    body = _shortconv_kernel if n1 == 1 else functools.partial(_shortconv_t1major_kernel, n1=n1)
    return pl.pallas_call(
        body,
        grid=(n_batch, nb),
        in_specs=[pl.BlockSpec((seq, LANE), lambda bi, j: (row0 // seq + bi, COL_HY // LANE + j)),
                  pl.BlockSpec((3, LANE), lambda bi, j: (0, j)),
                  pl.BlockSpec((1, LANE), lambda bi, j: (0, j))],
        out_specs=pl.BlockSpec((1, 1, 1, seq, LANE), lambda bi, j: (j // per, bi, j % per, 0, 0)),
        out_shape=jax.ShapeDtypeStruct((3, n_batch, per, seq, LANE), F32),
        compiler_params=_cparams(2),
        name="hyena_short_conv",
    )(p, w, b.reshape(1, -1))


def _hy_filter_kernel(emb_ref, embr_ref, w1_ref, b1_ref, w2_ref, b2_ref, w3_ref, b3_ref, w4_ref, fr_ref,
                      dl_ref, o_ref, nrm_ref, *, seq, group):
    i = pl.program_id(0)
    hp = lax.Precision.HIGHEST
    fr = fr_ref[...]

    def mlp(emb):
        hid = jnp.sin(fr * (jnp.dot(emb, w1_ref[...], precision=hp) + b1_ref[...]))
        hid = jnp.sin(fr * (jnp.dot(hid, w2_ref[...], precision=hp) + b2_ref[...]))
        return jnp.sin(fr * (jnp.dot(hid, w3_ref[...], precision=hp) + b3_ref[...]))

    tl = emb_ref.shape[0]
    n = w4_ref.shape[1]
    half = n // 2
    r = lax.broadcasted_iota(jnp.int32, (tl, half), 0) + i * tl
    per = seq // group
    pos = (r >> (per.bit_length() - 1)) + group * (r & (per - 1))
    posr = (pos & ~(group - 1)) + ((group - (pos & (group - 1))) & (group - 1))
    for side, (emb, pp) in enumerate(((emb_ref[...], pos), (embr_ref[...], posr))):
        h = jnp.dot(mlp(emb), w4_ref[:, side * half:(side + 1) * half], precision=hp)
        t = pp.astype(F32) * (1.0 / (seq - 1))
        h = h * (jnp.exp(-t * dl_ref[...]) + HY_SHIFT)
        if side == 1:
            h = jnp.where(pp == 0, 0.0, h)
        for k in range(half // LANE):
            o_ref[side * (half // LANE) + k] = h[:, k * LANE:(k + 1) * LANE]

        @pl.when(i == 0)
        def _():
            nrm_ref[:, side * half:(side + 1) * half] = jnp.zeros((1, half), F32)

        nrm_ref[:, side * half:(side + 1) * half] += jnp.sum(jnp.abs(h), axis=0, keepdims=True)


def _hyena_filter_taps(seq, w1, b1, w2, b2, w3, b3, w4, freq, group):
    fh = w1.shape[1]
    n_emb = w1.shape[0]
    bands_n = (n_emb - 1) // 2
    tt = np.linspace(0.0, 1.0, seq, dtype=np.float32)[:, None].astype(np.float64)
    ww = (2.0 * math.pi / seq) * np.arange(seq, dtype=np.float64)[:, None]
    bands = np.linspace(1e-4, bands_n - 1, bands_n, dtype=np.float32)[None, :].astype(np.float64)
    emb = np.concatenate([tt, np.cos(bands * ww), -np.sin(bands * ww)], axis=-1)
    emb = np.pad(emb, ((0, 0), (0, LANE - n_emb))).astype(np.float32)
    assert seq % group == 0 and group & (group - 1) == 0 and (seq // group) & (seq // group - 1) == 0
    r = np.arange(seq)
    pos = r // (seq // group) + group * (r % (seq // group))
    embr = emb[(pos // group) * group + (group - pos % group) % group]
    emb = emb[pos]
    w1p = jnp.pad(w1, ((0, LANE - n_emb), (0, 0)))
    deltas = np.abs(np.linspace(HY_MIN_DECAY, HY_MAX_DECAY, HY_W, dtype=np.float32))
    deltas = np.tile(deltas, 2)[None, :]
    n = w4.shape[1]
    tl = min(seq, 512)
    full = lambda a: pl.BlockSpec(a.shape, lambda i: (0,) * a.ndim)
    args = (w1p, b1.reshape(1, fh), w2, b2.reshape(1, fh), w3, b3.reshape(1, fh), w4, freq.reshape(1, fh),
            jnp.asarray(deltas))
    return pl.pallas_call(
        functools.partial(_hy_filter_kernel, seq=seq, group=group),
        grid=(seq // tl,),
        in_specs=[pl.BlockSpec((tl, LANE), lambda i: (i, 0))] * 2 + [full(a) for a in args],
        out_specs=[pl.BlockSpec((n // LANE, tl, LANE), lambda i: (0, i, 0)),
                   pl.BlockSpec((1, n), lambda i: (0, 0))],
        out_shape=[jax.ShapeDtypeStruct((n // LANE, seq, LANE), F32), jax.ShapeDtypeStruct((1, n), F32)],
        compiler_params=_cparams(1),
        name="hyena_filter_taps",
    )(jnp.asarray(emb), jnp.asarray(embr), *args)


def _fft_split(seq):
    n = 2 * seq
    n1 = FFT_N1 if n > 1024 else 1
    return n, n1, n // n1


def _cis(idx, n):
    ph = 2.0 * np.pi * (idx % n) / n
    return np.cos(ph), -np.sin(ph)


def _dft_tables_short(seq):
    n = 2 * seq
    cr, ci = _cis(np.arange(n)[:, None] * np.arange(seq)[None, :], n)
    w_fwd = np.concatenate([cr, ci], axis=0)
    w_inv = np.concatenate([cr.T, ci.T], axis=1) / n
    return dict(w_fwd=jnp.asarray(w_fwd, BF16), w_inv=jnp.asarray(w_inv, BF16))


def _dft_tables_long(seq):
    n, n1, n2 = _fft_split(seq)
    t1 = np.arange(n1)[:, None, None]
    f2 = np.arange(n2)[None, :, None]
    t2 = np.arange(n2 // 2)[None, None, :]
    cr, ci = _cis(f2 * (t1 + n1 * t2), n)
    w1 = np.concatenate([cr, ci], axis=1)
    w4 = np.concatenate([np.swapaxes(cr, 1, 2), np.swapaxes(ci, 1, 2)], axis=2) / n
    j = (n1 - t1) % n1 + n1 * t2
    br, bi = _cis(f2 * (n - j), n)
    w1f = np.concatenate([np.concatenate([cr, br], axis=2), np.concatenate([ci, bi], axis=2)], axis=1)
    gr, gi = _cis(np.arange(n1)[:, None] * np.arange(n1)[None, :], n1)
    wd = np.concatenate([np.concatenate([gr, -gi], axis=1), np.concatenate([gi, gr], axis=1)], axis=0)
    return dict(w1=jnp.asarray(w1, BF16), w4=jnp.asarray(w4, BF16), w1f=jnp.asarray(w1f, BF16),
                wd=jnp.asarray(wd, BF16), wdi=jnp.asarray(wd.T, BF16))


def _dft_rows_kernel(w_ref, x_ref, o_ref):
    o_ref[0] = _dot(w_ref[...], x_ref[0].astype(BF16)).astype(o_ref.dtype)


def _dft_rows(w, x, tn):
    nb, k, cols = x.shape
    m = w.shape[0]
    return pl.pallas_call(
        _dft_rows_kernel,
        grid=(nb, cols // tn),
        in_specs=[pl.BlockSpec((m, k), lambda b, j: (0, 0)),
                  pl.BlockSpec((1, k, tn), lambda b, j: (b, 0, j))],
        out_specs=pl.BlockSpec((1, m, tn), lambda b, j: (b, 0, j)),
        out_shape=jax.ShapeDtypeStruct((nb, m, cols), BF16),
        compiler_params=_cparams(2),
        name="hyena_dft_rows",
    )(w, x)


def _idft_gate_kernel(w_ref, b_ref, xg_ref, z_ref, sk_ref, o_ref):
    y = _dot(w_ref[...], b_ref[0])
    z = z_ref[0]
    o_ref[0] = xg_ref[0] * (y + z * sk_ref[...])


def _idft_gate(w, bc, xg, z, skip_t, tn):
    nb, k, cols = bc.shape
    m = w.shape[0]
    return pl.pallas_call(
        _idft_gate_kernel,
        grid=(nb, cols // tn),
        in_specs=[pl.BlockSpec((m, k), lambda b, j: (0, 0)),
                  pl.BlockSpec((1, k, tn), lambda b, j: (b, 0, j)),
                  pl.BlockSpec((1, m, tn), lambda b, j: (b, 0, j)),
                  pl.BlockSpec((1, m, tn), lambda b, j: (b, 0, j)),
                  pl.BlockSpec((1, tn), lambda b, j: (0, j))],
        out_specs=pl.BlockSpec((1, m, tn), lambda b, j: (b, 0, j)),
        out_shape=jax.ShapeDtypeStruct((nb, m, cols), F32),
        compiler_params=_cparams(2),
        name="hyena_idft_gate",
    )(w, bc, xg, z, skip_t)


def _spec_combine_kernel(af_ref, ab_ref, nf_ref, nb_ref, o_ref):
    inv = 1.0 / (nf_ref[...] + nb_ref[...])
    n = o_ref.shape[1]
    o_ref[0] = (af_ref[0, :n].astype(F32) + ab_ref[0, :n].astype(F32)) * inv
    o_ref[1] = (af_ref[0, n:].astype(F32) - ab_ref[0, n:].astype(F32)) * inv


def _spec_mul_kernel(a_ref, k_ref, o_ref):
    n = k_ref.shape[1]
    xr, xi = a_ref[0, :n].astype(F32), a_ref[0, n:].astype(F32)
    kr, ki = k_ref[0], k_ref[1]
    o_ref[0, :n] = (xr * kr - xi * ki).astype(o_ref.dtype)
    o_ref[0, n:] = (xr * ki + xi * kr).astype(o_ref.dtype)


def _hyena_short(seq, taps, nrm, uc, skip):
    n = 2 * seq
    uc = jnp.swapaxes(uc, 2, 3).reshape(3, uc.shape[1], seq, HY_W)
    tabs = _dft_tables_short(seq)
    cf = taps.shape[1]
    c = HY_W
    a = _dft_rows(tabs["w_fwd"], taps.reshape(1, seq, cf), tn=cf)
    nblk = cf // 2 // c
    spec = pl.pallas_call(
        _spec_combine_kernel,
        grid=(nblk,),
        in_specs=[pl.BlockSpec((1, 2 * n, c), lambda j: (0, 0, j)),
                  pl.BlockSpec((1, 2 * n, c), lambda j: (0, 0, j + nblk)),
                  pl.BlockSpec((1, c), lambda j: (0, j)),
                  pl.BlockSpec((1, c), lambda j: (0, j + nblk))],
        out_specs=pl.BlockSpec((2, n, c), lambda j: (0, 0, j)),
        out_shape=jax.ShapeDtypeStruct((2, n, cf // 2), F32),
        compiler_params=_cparams(1),
        name="hyena_spec_combine",
    )(a, a, nrm, nrm)
    nb = uc.shape[1]
    z = uc[2]
    for order in range(2):
        a = _dft_rows(tabs["w_fwd"], z, tn=c)
        bc = pl.pallas_call(
            _spec_mul_kernel,
            grid=(nb,),
            in_specs=[pl.BlockSpec((1, 2 * n, c), lambda b: (b, 0, 0)),
                      pl.BlockSpec((2, n, c), lambda b: (0, 0, order))],
            out_specs=pl.BlockSpec((1, 2 * n, c), lambda b: (b, 0, 0)),
            out_shape=jax.ShapeDtypeStruct((nb, 2 * n, c), BF16),
            compiler_params=_cparams(1),
            name="hyena_spec_mul",
        )(a, spec)
        z = _idft_gate(tabs["w_inv"], bc, uc[order], z, skip[order].reshape(1, c), tn=c)
    return z


HY_SLABS = 2


def _slab_pitch(n2):
    return n2 + 8


def _pack_c(re, im):
    hi = lax.bitcast_convert_type(re.astype(BF16).astype(F32), jnp.uint32)
    lo = lax.bitcast_convert_type(im.astype(BF16).astype(F32), jnp.uint32)
    return hi | (lo >> 16)


def _unpack_c(w):
    re = lax.bitcast_convert_type(w & jnp.uint32(0xFFFF0000), F32)
    im = lax.bitcast_convert_type(w << 16, F32)
    return re.astype(BF16), im.astype(BF16)


def _store_slab(a_scr, row, words):
    for s in range(HY_SLABS):
        a_scr[s, pl.ds(row, words.shape[0]), :] = words[:, s * LANE:(s + 1) * LANE]


def _stage1(a_scr, w_ref, xs_of, i, tb, n2, pitch):
    for j in range(tb):
        a = _dot(w_ref[j], xs_of(j))
        _store_slab(a_scr, pl.multiple_of((i * tb + j) * pitch, 8), _pack_c(a[:n2], a[n2:]))


def _stage2(a_scr, wd_ref, f2, n1, pitch):
    w = jnp.concatenate([a_scr[s, pl.ds(f2, n1, stride=pitch), :] for s in range(HY_SLABS)], axis=1)
    re, im = _unpack_c(w)
    return _dot(wd_ref[...], jnp.concatenate([re, im], axis=0))


def _time_col(ref, j, half):
    lead = (0,) * (len(ref.shape) - 3)
    return jnp.concatenate([ref[lead + (s, slice(j * half, (j + 1) * half), slice(None))]
                            for s in range(HY_SLABS)], axis=1)


def _hy_spectrum_kernel(hf_ref, hb_ref, nf_ref, nb_ref, w1f_ref, wd_ref, o_ref, a_scr, *, n1, n2, tb, fb):
    i = pl.program_id(1)
    pitch = _slab_pitch(n2)
    half = n2 // 2
    nt = n1 // tb

    @pl.when(i < nt)
    def _():
        def xs_of(j):
            return jnp.concatenate([_time_col(hf_ref, j, half), _time_col(hb_ref, j, half)], axis=0).astype(BF16)
        _stage1(a_scr, w1f_ref, xs_of, i, tb, n2, pitch)

    @pl.when(i >= nt)
    def _():
        inv = 1.0 / (nf_ref[...] + nb_ref[...])
        for jj in range(fb):
            x = _stage2(a_scr, wd_ref, (i - nt) * fb + jj, n1, pitch)
            o_ref[0, jj] = x[:n1] * inv
            o_ref[1, jj] = x[n1:] * inv


def _hy_fftconv_kernel(z_ref, xg_ref, k_ref, sk_ref, w1_ref, wd_ref, wdi_ref, w4_ref, o_ref, a_scr,
                       *, n1, n2, tb, fb):
    i = pl.program_id(2)
    pitch = _slab_pitch(n2)
    half = n2 // 2
    nt, nf = n1 // tb, n2 // fb

    @pl.when(i < nt)
    def _():
        _stage1(a_scr, w1_ref, lambda j: _time_col(z_ref, j, half).astype(BF16), i, tb, n2, pitch)

    @pl.when((i >= nt) & (i < nt + nf))
    def _():
        for jj in range(fb):
            f2 = (i - nt) * fb + jj
            x = _stage2(a_scr, wd_ref, f2, n1, pitch)
            xr, xi = x[:n1], x[n1:]
            kr, ki = k_ref[0, jj], k_ref[1, jj]
            y = jnp.concatenate([xr * kr - xi * ki, xr * ki + xi * kr], axis=0).astype(BF16)
            bv = _dot(wdi_ref[...], y)
            words = _pack_c(bv[:n1], bv[n1:])
            for s in range(HY_SLABS):
                a_scr[s, pl.ds(f2, n1, stride=pitch), :] = words[:, s * LANE:(s + 1) * LANE]

    @pl.when(i >= nt + nf)
    def _():
        sk = sk_ref[...]
        for j in range(tb):
            row = pl.multiple_of(((i - nt - nf) * tb + j) * pitch, 8)
            w = jnp.concatenate([a_scr[s, pl.ds(row, n2), :] for s in range(HY_SLABS)], axis=1)
            re, im = _unpack_c(w)
            y = _dot(w4_ref[j], jnp.concatenate([re, im], axis=0))
            out = _time_col(xg_ref, j, half) * (y + _time_col(z_ref, j, half) * sk)
            for s in range(HY_SLABS):
                o_ref[0, s, j * half:(j + 1) * half, :] = out[:, s * LANE:(s + 1) * LANE]


def _hyena_long(seq, taps, nrm, uc, skip, tb=32, fb=16):
    n, n1, n2 = _fft_split(seq)
    tabs = _dft_tables_long(seq)
    pitch = _slab_pitch(n2)
    half = n2 // 2
    cb = HY_SLABS * LANE
    tb, fb = min(tb, n1), min(fb, n2)
    nt, nf = n1 // tb, n2 // fb
    cf = taps.shape[0] * LANE
    ngrp = cf // 2 // cb
    scratch = [pltpu.VMEM((HY_SLABS, n1 * pitch, LANE), jnp.uint32)]
    tcol = lambda i: jnp.minimum(i, nt - 1)
    spec = pl.pallas_call(
        functools.partial(_hy_spectrum_kernel, n1=n1, n2=n2, tb=tb, fb=fb),
        grid=(ngrp, nt + nf),
        in_specs=[pl.BlockSpec((HY_SLABS, tb * half, LANE), lambda g, i: (g, tcol(i), 0)),
                  pl.BlockSpec((HY_SLABS, tb * half, LANE), lambda g, i: (ngrp + g, tcol(i), 0)),
                  pl.BlockSpec((1, cb), lambda g, i: (0, g)),
                  pl.BlockSpec((1, cb), lambda g, i: (0, ngrp + g)),
                  pl.BlockSpec((tb, 2 * n2, n2), lambda g, i: (tcol(i), 0, 0)),
                  pl.BlockSpec(tabs["wd"].shape, lambda g, i: (0, 0))],
        out_specs=pl.BlockSpec((2, fb, n1, cb), lambda g, i: (0, jnp.maximum(i - nt, 0), 0, g)),
        out_shape=jax.ShapeDtypeStruct((2, n2, n1, cf // 2), F32),
        scratch_shapes=scratch,
        compiler_params=_cparams(2),
        name="hyena_filter_spectrum",
    )(taps, taps, nrm, nrm, tabs["w1f"], tabs["wd"])
    nb, nblk = uc.shape[1], uc.shape[2]
    ngc = nblk // HY_SLABS
    z = uc[2]

    def tblk(b, g, i):
        return (b, g, jnp.where(i < nt, i, jnp.maximum(i - nt - nf, 0)), 0)

    def tblk_late(b, g, i):
        return (b, g, jnp.maximum(i - nt - nf, 0), 0)

    for order in range(2):
        z = pl.pallas_call(
            functools.partial(_hy_fftconv_kernel, n1=n1, n2=n2, tb=tb, fb=fb),
            grid=(nb, ngc, 2 * nt + nf),
            in_specs=[pl.BlockSpec((1, HY_SLABS, tb * half, LANE), tblk),
                      pl.BlockSpec((1, HY_SLABS, tb * half, LANE), tblk_late),
                      pl.BlockSpec((2, fb, n1, cb),
                                   lambda b, g, i: (0, jnp.clip(i - nt, 0, nf - 1), 0, order * ngc + g)),
                      pl.BlockSpec((1, cb), lambda b, g, i: (0, g)),
                      pl.BlockSpec((tb, 2 * n2, half), lambda b, g, i: (tcol(i), 0, 0)),
                      pl.BlockSpec(tabs["wd"].shape, lambda b, g, i: (0, 0)),
                      pl.BlockSpec(tabs["wdi"].shape, lambda b, g, i: (0, 0)),
                      pl.BlockSpec((tb, half, 2 * n2), lambda b, g, i: (jnp.maximum(i - nt - nf, 0), 0, 0))],
            out_specs=pl.BlockSpec((1, HY_SLABS, tb * half, LANE), tblk_late),
            out_shape=jax.ShapeDtypeStruct((nb, nblk, seq, LANE), F32),
            scratch_shapes=scratch,
            compiler_params=_cparams(3),
            name="hyena_fft_conv",
        )(z, uc[order], spec, skip[order].reshape(1, HY_W), tabs["w1"], tabs["wd"], tabs["wdi"], tabs["w4"])
    return z


def _hyena(p, lw, row0, seq, n_batch):
    long = _fft_split(seq)[1] > 1
    n1 = FFT_N1 if long else 1
    taps, nrm = _hyena_filter_taps(seq, lw["hy_w1"], lw["hy_b1"], lw["hy_w2"], lw["hy_b2"], lw["hy_w3"],
                                   lw["hy_b3"], lw["hy_w4"], lw["hy_freq"], n1)
    uc = _short_conv(p, lw["hy_conv_w"], lw["hy_conv_b"], row0, seq, n_batch, n1)
    if long:
        return _hyena_long(seq, taps, nrm, uc, lw["hy_skip"])
    taps = jnp.swapaxes(taps, 0, 1).reshape(seq, -1)
    return _hyena_short(seq, taps, nrm, uc, lw["hy_skip"]).reshape(n_batch * seq, HY_W)


def _mla_qkv_kernel(qa_ref, kva_ref, kr_ref, cs_ref, wq_ref, wkv_ref, gqa_ref, gkva_ref, gqn_ref, gqr_ref,
                    gkn_ref, gkr_ref, q_ref, k_ref, v_ref):
    cs = cs_ref[...]
    lane = lax.broadcasted_iota(jnp.int32, cs.shape, 1)
    low = lane < MLA_ROPE

    def rope(pair, gain2):
        ms = jnp.sum(jnp.where(low, pair * pair, 0.0), axis=-1, keepdims=True) * (1.0 / MLA_ROPE)
        t = pair * lax.rsqrt(ms + RMS_EPS) * gain2 * cs
        return jnp.where(low, t + pltpu.roll(t, MLA_ROPE, 1), 0.0)

    qq = _dot(_rms(qa_ref[...], gqa_ref[...]).astype(BF16), wq_ref[...])
    kv = _dot(_rms(kva_ref[...], gkva_ref[...]).astype(BF16), wkv_ref[...])
    kr = rope(kr_ref[...], gkr_ref[...])
    hp = MLA_HEAD_PAD
    tm = kr.shape[0]
    qscale = MLA_SCALE * LOG2_E
    ones_row = (lax.broadcasted_iota(jnp.int32, (V_ROWS - MLA_V, tm), 0) == 0).astype(BF16)
    for h in range(MLA_HEADS):
        qn = _rms(qq[:, h * hp:h * hp + MLA_NOPE], gqn_ref[...])
        qr = rope(qq[:, h * hp + MLA_NOPE:(h + 1) * hp], gqr_ref[...])
        q_ref[h, 0:LANE, :] = (qn * qscale).T.astype(BF16)
        q_ref[h, LANE:2 * LANE, :] = (qr * qscale).T.astype(BF16)
        kn = _rms(kv[:, h * hp:h * hp + MLA_NOPE], gkn_ref[...])
        k_ref[h, :, 0:LANE] = kn.astype(BF16)
        k_ref[h, :, LANE:2 * LANE] = kr.astype(BF16)
        v_ref[h, 0:MLA_V, :] = kv[:, h * hp + MLA_NOPE:(h + 1) * hp].T.astype(BF16)
        v_ref[h, MLA_V:V_ROWS, :] = ones_row


def _rope_table(seq, n_batch, ctx_rows):
    rows = seq // GRID_W
    row = np.repeat(np.arange(rows, dtype=np.float32), GRID_W)
    col = np.tile(np.arange(GRID_W, dtype=np.float32), rows)
    half = MLA_ROPE // 2
    inv = (ROPE_THETA ** (-np.arange(0, half, 2, dtype=np.float32) / half)).astype(np.float32)
    ar = (row[:, None] * inv).astype(np.float64)
    ac = (col[:, None] * inv).astype(np.float64)
    cos = np.concatenate([np.cos(ar), np.cos(ar), np.cos(ac), np.cos(ac)], axis=1)
    sin = np.concatenate([-np.sin(ar), np.sin(ar), -np.sin(ac), np.sin(ac)], axis=1)
    lat = np.tile(np.concatenate([cos, sin], axis=1), (n_batch, 1))
    ctx = np.concatenate([np.ones((ctx_rows, MLA_ROPE)), np.zeros((ctx_rows, MLA_ROPE))], axis=1)
    return jnp.asarray(np.concatenate([lat, ctx], axis=0), F32)


def _pair_gain(g):
    return jnp.concatenate([g, _rope_swap(g)]).reshape(1, 2 * MLA_ROPE)


def _mla_qkv(p, cs, lw, tm=512):
    n_rows = p.shape[0]
    hd, hp = MLA_HEADS, MLA_HEAD_PAD
    w_uq = lw["w_uq"].reshape(-1, hd, MLA_QK)
    wq = jnp.concatenate([w_uq, _rope_swap(w_uq[..., MLA_NOPE:])], axis=-1).reshape(-1, hd * hp).astype(BF16)
    wkv = lw["w_ukv"].astype(BF16)
    ql, kvl = wq.shape[0], wkv.shape[0]
    vec = lambda a: a.reshape(1, -1)
    full = lambda a: pl.BlockSpec(a.shape, lambda i: (0,) * a.ndim)
    args = (wq, wkv, vec(lw["q_a_norm"]), vec(lw["kv_a_norm"]), vec(lw["q_nope_norm"]),
            _pair_gain(lw["q_rope_norm"]), vec(lw["k_nope_norm"]), _pair_gain(lw["k_rope_norm"]))
    return pl.pallas_call(
        _mla_qkv_kernel,
        grid=(n_rows // tm,),
        in_specs=[pl.BlockSpec((tm, ql), lambda i: (i, COL_QA // ql)),
                  pl.BlockSpec((tm, kvl), lambda i: (i, COL_KVA // kvl)),
                  pl.BlockSpec((tm, LANE), lambda i: (i, COL_KR // LANE)),
                  pl.BlockSpec((tm, LANE), lambda i: (i, 0))] + [full(a) for a in args],
        out_specs=[pl.BlockSpec((hd, hp, tm), lambda i: (0, 0, i)),
                   pl.BlockSpec((hd, tm, hp), lambda i: (0, i, 0)),
                   pl.BlockSpec((hd, V_ROWS, tm), lambda i: (0, 0, i))],
        out_shape=[jax.ShapeDtypeStruct((hd, hp, n_rows), BF16),
                   jax.ShapeDtypeStruct((hd, n_rows, hp), BF16),
                   jax.ShapeDtypeStruct((hd, V_ROWS, n_rows), BF16)],
        compiler_params=_cparams(1),
        name="mla_qkv",
    )(p, p, p, cs, *args)


def _attn_kernel(*refs, with_latent, tk):
    if with_latent:
        q_ref, k_ref, v_ref, kc_ref, vc_ref, o_ref, s_scr = refs
    else:
        q_ref, kc_ref, vc_ref, o_ref, s_scr = refs
    qt = q_ref[0]
    n_ctx = kc_ref.shape[1]
    chunks = [(kc_ref, vc_ref, 0, n_ctx, 0)]
    if with_latent:
        chunks += [(k_ref, v_ref, j * tk, tk, n_ctx + j * tk) for j in range(k_ref.shape[1] // tk)]
    m = None
    for kr, _, r0, rn, s0 in chunks:
        s = _dot(kr[0, r0:r0 + rn, :], qt)
        s_scr[s0:s0 + rn, :] = s
        mj = jnp.max(s, axis=0, keepdims=True)
        m = mj if m is None else jnp.maximum(m, mj)
    acc = None
    for _, vr, r0, rn, s0 in chunks:
        p = jnp.exp2((s_scr[s0:s0 + rn, :] - m).astype(BF16))
        part = _dot(vr[0, :, r0:r0 + rn], p)
        acc = part if acc is None else acc + part
    o_ref[...] = (acc[:MLA_V] / acc[MLA_V:MLA_V + 1]).astype(o_ref.dtype)


def _attention(qt, k, vt, seq, ctx_len, n_batch, latent, tq=256, tk=1024):
    hd, hp, n_rows = qt.shape
    lat_rows = n_batch * seq
    cblk = lat_rows // ctx_len
    kc_spec = pl.BlockSpec((1, ctx_len, hp), lambda b, h, i: (h, cblk + b, 0))
    vc_spec = pl.BlockSpec((1, V_ROWS, ctx_len), lambda b, h, i: (h, 0, cblk + b))
    if latent:
        nq = seq // tq
        in_specs = [pl.BlockSpec((1, hp, tq), lambda b, h, i: (h, 0, b * nq + i)),
                    pl.BlockSpec((1, seq, hp), lambda b, h, i: (h, b, 0)),
                    pl.BlockSpec((1, V_ROWS, seq), lambda b, h, i: (h, 0, b)), kc_spec, vc_spec]
        args = (qt, k, vt, k, vt)
        out_spec = pl.BlockSpec((MLA_V, tq), lambda b, h, i: (h, b * nq + i))
        out_cols, n_keys = lat_rows, seq + ctx_len
    else:
        nq = 1
        tq = ctx_len
        in_specs = [pl.BlockSpec((1, hp, tq), lambda b, h, i: (h, 0, cblk + b)), kc_spec, vc_spec]
        args = (qt, k, vt)
        out_spec = pl.BlockSpec((MLA_V, tq), lambda b, h, i: (h, b))
        out_cols, n_keys = n_batch * ctx_len, ctx_len
    return pl.pallas_call(
        functools.partial(_attn_kernel, with_latent=latent, tk=tk),
        grid=(n_batch, hd, nq),
        in_specs=in_specs,
        out_specs=out_spec,
        out_shape=jax.ShapeDtypeStruct((hd * MLA_V, out_cols), BF16),
        scratch_shapes=[pltpu.VMEM((n_keys, tq), F32)],
        compiler_params=_cparams(3),
        name="mla_attention" if latent else "mla_attention_ctx",
    )(*args)


def _merge_kernel(*refs, lat_tiles):
    (x_ref, mod_ref, of_ref, ob_ref, g_ref, yb_ref, yc_ref, ga_ref, gb_ref, gc_ref,
     gain_ref, wa_ref, wb_ref, wc_ref, wo_ref) = refs[:15]
    o_ref = refs[-1]
    m = mod_ref[0]
    o = of_ref[...] + ob_ref[...]
    gain = gain_ref[...]
    ya = jnp.concatenate([_rms(o[:, h * HG_D:(h + 1) * HG_D], gain) for h in range(HG_HEADS)], axis=1)
    ya = (ya * _silu(g_ref[...])).astype(BF16)
    nslab, tcols = yb_ref.shape[1], yb_ref.shape[3]
    qn = x_ref.shape[0] // yb_ref.shape[2]
    parts = [jnp.concatenate([jnp.concatenate([yb_ref[0, k, :, q, :] for k in range(nslab)], axis=1)
                              for q in range(h * qn, (h + 1) * qn)], axis=0) for h in range(tcols // qn)]
    sub = pl.program_id(0) % len(parts)
    yb = parts[0]
    for h in range(1, len(parts)):
        yb = jnp.where(sub == h, parts[h], yb)
    if len(refs) == 17:
        yb = jnp.where(pl.program_id(0) >= lat_tiles, refs[15][...], yb)
    mix = (_sigmoid(ga_ref[...]) * _dot(ya, wa_ref[...])
           + _sigmoid(gb_ref[...]) * _dot(yb.astype(BF16), wb_ref[...])
           + _sigmoid(gc_ref[...]) * _dot_tn(yc_ref[...], wc_ref[...]))
    o_ref[...] = x_ref[...] + m[5:6] * _dot(mix.astype(BF16), wo_ref[...])


def _merge(xa, mod, o_f, o_b, p, y_b, y_b_ctx, y_c, lw, n_rows, seq, n_batch, tm=512):
    d = xa.shape[1]
    hk = HG_HEADS * HG_D
    tpb = seq // tm
    lat_tiles = n_batch * tpb
    n1 = FFT_N1
    tcols = 8
    assert tm % n1 == 0 and tcols % (tm // n1) == 0 and seq % (tcols * n1) == 0
    per_blk = tcols // (tm // n1)
    y_b = y_b.reshape(n_batch, y_b.shape[1], n1, seq // n1, LANE)
    row = lambda w: pl.BlockSpec((tm, w), lambda i: (i, 0))
    pcol = lambda w, c: pl.BlockSpec((tm, w), lambda i: (i, c // w))
    full = lambda a: pl.BlockSpec(a.shape, lambda i: (0,) * a.ndim)
    ws = (lw["hg_out_norm"].reshape(1, HG_D), lw["w_br_a"].astype(BF16), lw["w_br_b"].astype(BF16),
          lw["w_br_c"].astype(BF16), lw["w_out"].astype(BF16))
    yb_spec = pl.BlockSpec((1, y_b.shape[1], n1, tcols, LANE),
                           lambda i: (jnp.minimum(i // tpb, n_batch - 1), 0, 0, (i % tpb) // per_blk, 0))
    extra_specs, extra = [], []
    if y_b_ctx is not None:
        extra_specs = [pl.BlockSpec((tm, HY_W), lambda i: (jnp.maximum(i - lat_tiles, 0), 0))]
        extra = [y_b_ctx]
    return pl.pallas_call(
        functools.partial(_merge_kernel, lat_tiles=lat_tiles),
        grid=(n_rows // tm,),
        in_specs=[row(d), pl.BlockSpec((1, N_MOD, d), _group_map(tpb, n_batch)),
                  row(hk), row(hk), pcol(hk, COL_G), yb_spec,
                  pl.BlockSpec((MLA_HEADS * MLA_V, tm), lambda i: (0, i)),
                  pcol(d, COL_GA), pcol(d, COL_GB), pcol(d, COL_GC)] + [full(a) for a in ws] + extra_specs,
        out_specs=row(d),
        out_shape=jax.ShapeDtypeStruct((n_rows, d), F32),
        compiler_params=_cparams(1),
        name="merge",
    )(xa, mod, o_f, o_b, p, y_b, y_c, p, p, p, *ws, *extra)


def kernel(x, c, ctx, c_ctx, ada_w, ada_b, ffn1_norm, ffn1_w13, ffn1_w2, mix_norm, w_in, hg_lb_logits, hg_out_norm, hy_conv_w, hy_conv_b, hy_w1, hy_b1, hy_w2, hy_b2, hy_w3, hy_b3, hy_w4, hy_freq, hy_skip, q_a_norm, w_uq, kv_a_norm, w_ukv, q_nope_norm, q_rope_norm, k_nope_norm, k_rope_norm, w_br_a, w_br_b, w_br_c, w_out, ffn2_norm, ffn2_w13, ffn2_w2):
    stacked = dict(
        ada_w=ada_w, ada_b=ada_b, ffn1_norm=ffn1_norm, ffn1_w13=ffn1_w13, ffn1_w2=ffn1_w2, mix_norm=mix_norm,
        w_in=w_in, hg_out_norm=hg_out_norm, hy_conv_w=hy_conv_w, hy_conv_b=hy_conv_b, hy_w1=hy_w1, hy_b1=hy_b1,
        hy_w2=hy_w2, hy_b2=hy_b2, hy_w3=hy_w3, hy_b3=hy_b3, hy_w4=hy_w4, hy_freq=hy_freq, hy_skip=hy_skip,
        q_a_norm=q_a_norm, w_uq=w_uq, kv_a_norm=kv_a_norm, w_ukv=w_ukv, q_nope_norm=q_nope_norm,
        q_rope_norm=q_rope_norm, k_nope_norm=k_nope_norm, k_rope_norm=k_rope_norm, w_br_a=w_br_a,
        w_br_b=w_br_b, w_br_c=w_br_c, w_out=w_out, ffn2_norm=ffn2_norm, ffn2_w13=ffn2_w13, ffn2_w2=ffn2_w2)
    n_batch, seq, d = x.shape
    ctx_len = ctx.shape[1]
    depth = ada_w.shape[0]
    lat_rows, ctx_rows = n_batch * seq, n_batch * ctx_len
    all_rows = lat_rows + ctx_rows
    assert seq % 512 == 0 and ctx_rows % 512 == 0 and seq % ctx_len == 0 and seq % GRID_W == 0
    assert ctx_len % HG_CHUNK == 0 and n_batch < 8

    xa = jnp.concatenate([x.reshape(lat_rows, d), ctx.reshape(ctx_rows, d)], axis=0)
    cs = jnp.concatenate([c, c_ctx.reshape(1, d), jnp.zeros((7 - n_batch, d), F32)], axis=0)
    rope_cs = _rope_table(seq, n_batch, ctx_rows)
    zero_state = jnp.zeros((n_batch, HG_HEADS, HG_D, HG_D), F32)

    for l in range(depth):
        lw = {name: val[l] for name, val in stacked.items()}
        need_ctx = l < depth - 1
        mod = _modulation(cs, lw["ada_w"], lw["ada_b"])
        xa = _half_ffn(xa, mod, lw["ffn1_norm"], lw["ffn1_w13"], lw["ffn1_w2"], 0, all_rows, seq, n_batch)
        p = _in_projection(xa, mod, lw["mix_norm"], _pack_w_in(lw["w_in"]).astype(BF16), seq, n_batch)

        mix_rows = all_rows if need_ctx else lat_rows
        o_dir = []
        for rev in (False, True):
            lg = hg_lb_logits[1 if rev else 0]
            o_c, s_c = _hgrn2_scan(p, lg, zero_state, l, rev, lat_rows, ctx_len, n_batch)
            o_l, _ = _hgrn2_scan(p, lg, s_c, l, rev, 0, seq, n_batch)
            o_dir.append(jnp.concatenate([o_l, o_c], axis=0) if need_ctx else o_l)

        y_b = _hyena(p, lw, 0, seq, n_batch)
        y_b_ctx = _hyena(p, lw, lat_rows, ctx_len, n_batch) if need_ctx else None

        qt, k, vt = _mla_qkv(p, rope_cs, lw)
        y_c = _attention(qt, k, vt, seq, ctx_len, n_batch, latent=True)
        if need_ctx:
            y_c = jnp.concatenate([y_c, _attention(qt, k, vt, seq, ctx_len, n_batch, latent=False)], axis=1)

        xa = _merge(xa, mod, o_dir[0], o_dir[1], p, y_b, y_b_ctx, y_c, lw, mix_rows, seq, n_batch)
        xa = _half_ffn(xa, mod, lw["ffn2_norm"], lw["ffn2_w13"], lw["ffn2_w2"], 6, mix_rows, seq, n_batch)
    return xa[:lat_rows].reshape(n_batch, seq, d)
```

```python
import functools
import math

import numpy as np
import jax
import jax.numpy as jnp
from jax import lax
from jax.experimental import pallas as pl
from jax.experimental.pallas import tpu as pltpu

F32 = jnp.float32
BF16 = jnp.bfloat16

RMS_EPS = 1e-6
N_MOD = 9
GRID_W = 64
ROPE_THETA = 10000.0
HG_HEADS = 4
HG_D = 128
HG_CHUNK = 128
HY_W = 512
HY_TARGET = 1e-2
HY_MIN_DECAY = math.log(HY_TARGET) / 1.5
HY_MAX_DECAY = math.log(HY_TARGET) / 0.3
HY_SHIFT = 0.05
MLA_HEADS = 4
MLA_NOPE = 128
MLA_ROPE = 64
MLA_V = 128
MLA_QK = MLA_NOPE + MLA_ROPE
MLA_SCALE = MLA_QK ** -0.5
MLA_HEAD_PAD = 256
V_ROWS = MLA_V + 16
LOG2_E = math.log2(math.e)
FFT_N1 = 128
LANE = 128

VMEM_LIMIT = 52 * 1024 * 1024

COL_GA, COL_GB, COL_GC = 0, 1024, 2048
COL_Q, COL_ZF, COL_ZB, COL_IV, COL_G = 3072, 3584, 4096, 4608, 5120
COL_HY = 5632
COL_QA, COL_KVA, COL_KR = 7168, 7424, 7552
IN_PACKED = 7680


def _cparams(n_axes):
    return pltpu.CompilerParams(dimension_semantics=("arbitrary",) * n_axes,
                                vmem_limit_bytes=VMEM_LIMIT)


def _dot(a, b):
    return jnp.dot(a, b, preferred_element_type=F32)


def _dot_nt(a, b):
    return lax.dot_general(a, b, (((1,), (1,)), ((), ())), preferred_element_type=F32)


def _dot_tn(a, b):
    return lax.dot_general(a, b, (((0,), (0,)), ((), ())), preferred_element_type=F32)


def _sigmoid(x):
    return 1.0 / (1.0 + jnp.exp(-x))


def _silu(x):
    return x * _sigmoid(x)


def _rms(x, gain):
    return x * lax.rsqrt(jnp.mean(x * x, axis=-1, keepdims=True) + RMS_EPS) * gain


def _mod_kernel(c_ref, w_ref, b_ref, o_ref):
    o_ref[...] = _dot(_silu(c_ref[...]).astype(BF16), w_ref[...]) + b_ref[...]


def _modulation(cs, ada_w, ada_b):
    g, d = cs.shape
    n = ada_w.shape[1]
    tn = n // 4
    out = pl.pallas_call(
        _mod_kernel,
        grid=(n // tn,),
        in_specs=[pl.BlockSpec((g, d), lambda j: (0, 0)),
                  pl.BlockSpec((d, tn), lambda j: (0, j)),
                  pl.BlockSpec((1, tn), lambda j: (0, j))],
        out_specs=pl.BlockSpec((g, tn), lambda j: (0, j)),
        out_shape=jax.ShapeDtypeStruct((g, n), F32),
        compiler_params=_cparams(1),
        name="modulation",
    )(cs, ada_w.astype(BF16), ada_b.reshape(1, n))
    return out.reshape(g, N_MOD, d)


def _ffn_kernel(x_ref, mod_ref, g_ref, w13_ref, w2_ref, o_ref, *, idx, ff, n_chunks):
    x = x_ref[...]
    m = mod_ref[0]
    h = (_rms(x, g_ref[...]) * (1.0 + m[idx + 1:idx + 2]) + m[idx:idx + 1]).astype(BF16)
    ck = ff // n_chunks
    acc = jnp.zeros(x.shape, F32)
    for j in range(n_chunks):
        a = _dot(h, w13_ref[:, j * ck:(j + 1) * ck])
        b = _dot(h, w13_ref[:, ff + j * ck:ff + (j + 1) * ck])
        acc = acc + _dot((_silu(a) * b).astype(BF16), w2_ref[j * ck:(j + 1) * ck, :])
    o_ref[...] = x + (0.5 * m[idx + 2:idx + 3]) * acc


def _group_map(tiles_per_batch, n_batch):
    return lambda i: (jnp.minimum(i // tiles_per_batch, n_batch), 0, 0)


def _half_ffn(xa, mod, gain, w13, w2, idx, n_rows, seq, n_batch, tm=512):
    d = xa.shape[1]
    ff = w2.shape[0]
    n_chunks = ff // 256
    const = dict(pipeline_mode=pl.Buffered(1))
    return pl.pallas_call(
        functools.partial(_ffn_kernel, idx=idx, ff=ff, n_chunks=n_chunks),
        grid=(n_rows // tm,),
        in_specs=[pl.BlockSpec((tm, d), lambda i: (i, 0)),
                  pl.BlockSpec((1, N_MOD, d), _group_map(seq // tm, n_batch)),
                  pl.BlockSpec((1, d), lambda i: (0, 0)),
                  pl.BlockSpec((d, 2 * ff), lambda i: (0, 0), **const),
                  pl.BlockSpec((ff, d), lambda i: (0, 0), **const)],
        out_specs=pl.BlockSpec((tm, d), lambda i: (i, 0)),
        out_shape=jax.ShapeDtypeStruct((n_rows, d), F32),
        compiler_params=_cparams(1),
        name="half_ffn",
    )(xa, mod, gain.reshape(1, d), w13.astype(BF16), w2.astype(BF16))


def _inproj_kernel(x_ref, mod_ref, g_ref, w_ref, o_ref):
    m = mod_ref[0]
    h = (_rms(x_ref[...], g_ref[...]) * (1.0 + m[4:5]) + m[3:4]).astype(BF16)
    o_ref[...] = _dot(h, w_ref[...])


def _pack_w_in(w_in):
    d = w_in.shape[0]
    hk = HG_HEADS * HG_D
    sizes = (hk, hk, hk, hk, hk, 3 * HY_W, 256, 128, MLA_ROPE, d, d, d)
    offs = np.cumsum((0,) + sizes)
    q, zf, zb, iv, g, hy, qa, kva, kr, ga, gb, gc = (w_in[:, offs[i]:offs[i + 1]] for i in range(12))
    return jnp.concatenate([ga, gb, gc, q, zf, zb, iv, g, hy, qa, kva, kr, _rope_swap(kr)], axis=1)


def _rope_swap(a):
    q = MLA_ROPE // 4
    return jnp.concatenate([a[..., q:2 * q], a[..., :q], a[..., 3 * q:], a[..., 2 * q:3 * q]], axis=-1)


def _in_projection(xa, mod, gain, w_packed, seq, n_batch, tm=512, tn=2560):
    n_rows, d = xa.shape
    n = w_packed.shape[1]
    return pl.pallas_call(
        _inproj_kernel,
        grid=(n // tn, n_rows // tm),
        in_specs=[pl.BlockSpec((tm, d), lambda j, i: (i, 0)),
                  pl.BlockSpec((1, N_MOD, d), lambda j, i: (jnp.minimum(i // (seq // tm), n_batch), 0, 0)),
                  pl.BlockSpec((1, d), lambda j, i: (0, 0)),
                  pl.BlockSpec((d, tn), lambda j, i: (0, j))],
        out_specs=pl.BlockSpec((tm, tn), lambda j, i: (i, j)),
        out_shape=jax.ShapeDtypeStruct((n_rows, n), F32),
        compiler_params=_cparams(2),
        name="in_projection",
    )(xa, mod, gain.reshape(1, d), w_packed)


def _hgrn2_tables(c, rev):
    t = np.arange(c)[:, None]
    u = np.arange(c)[None, :]
    mats = [(u <= t), (u > t)]
    masks = [(t == u)]
    h = c // 2
    while h >= 1:
        mid = (t // (2 * h)) * (2 * h) + h
        mats.append(np.where(t >= mid, (u >= mid) & (u <= t), (u >= t + 1) & (u <= mid - 1)))
        mid_s = (u // (2 * h)) * (2 * h) + h
        masks.append((t // (2 * h) == u // (2 * h)) & (u < mid_s) & (t >= mid))
        h //= 2
    mats = np.stack([m.astype(np.float32) for m in mats])
    masks = np.stack([m.astype(np.float32) for m in masks])
    if rev:
        mats = mats[:, ::-1, ::-1]
        masks = masks[:, ::-1, ::-1]
    mats = np.ascontiguousarray(mats).reshape(-1, c)
    return (jnp.asarray(np.concatenate([mats, mats], axis=1), BF16),
            jnp.asarray(np.ascontiguousarray(masks), F32))


def _hgrn2_kernel(q_ref, z_ref, v_ref, lg_ref, dst_ref, msk_ref, s0_ref, o_ref, sf_ref, st_ref,
                  *, layer, rev, n_levels):
    j = pl.program_id(1)
    c = q_ref.shape[0]
    hk = HG_HEADS * HG_D

    @pl.when(j == 0)
    def _():
        st_ref[...] = s0_ref[0]

    lg = lg_ref[...]
    e = jnp.exp(lg - jnp.max(lg, axis=0, keepdims=True))
    sm = e / jnp.sum(e, axis=0, keepdims=True)
    lb = jnp.zeros((1, hk), F32)
    for i in range(1, layer + 1):
        lb = lb + sm[i:i + 1]

    f = lb + (1.0 - lb) * _sigmoid(z_ref[...])
    kk = 1.0 - f
    g = jnp.log(f)
    g_hi = g.astype(BF16)
    g_lo = (g - g_hi.astype(F32)).astype(BF16)
    dg = _dot(dst_ref[...], jnp.concatenate([g_hi, g_lo], axis=0))
    qs = _silu(q_ref[...]) * HG_D ** -0.5
    vv = v_ref[...]
    last = 0 if rev else c - 1

    for h in range(HG_HEADS):
        hs = slice(h * HG_D, (h + 1) * HG_D)
        qh, kh, vh = qs[:, hs], kk[:, hs], vv[:, hs].astype(BF16)
        b_in = dg[0:c, hs]
        e_out = jnp.exp(dg[c:2 * c, hs])
        a = _dot_nt(qh.astype(BF16), kh.astype(BF16)) * msk_ref[0]
        for l in range(1, n_levels + 1):
            el = jnp.exp(dg[(1 + l) * c:(2 + l) * c, hs])
            a = a + _dot_nt((qh * el).astype(BF16), (kh * el).astype(BF16)) * msk_ref[l]
        st = st_ref[h]
        o = _dot_nt((qh * jnp.exp(b_in)).astype(BF16), st.astype(BF16)) + _dot(a.astype(BF16), vh)
        o_ref[:, hs] = o
        st_ref[h] = jnp.exp(b_in[last:last + 1, :]) * st + _dot_tn(vh, (kh * e_out).astype(BF16))

    @pl.when(j == pl.num_programs(1) - 1)
    def _():
        sf_ref[0] = st_ref[...]


def _hgrn2_scan(p, logits, s0, layer, rev, row0, seq, n_batch):
    c = HG_CHUNK
    hk = HG_HEADS * HG_D
    nc = seq // c
    base = row0 // c
    n_levels = int(math.log2(c))
    dst, msk = _hgrn2_tables(c, rev)
    col_z = COL_ZB if rev else COL_ZF

    def rows(b, j):
        return b * nc + (nc - 1 - j if rev else j)

    def prow(w):
        return pl.BlockSpec((c, hk), lambda b, j: (base + rows(b, j), w // hk))

    return pl.pallas_call(
        functools.partial(_hgrn2_kernel, layer=layer, rev=rev, n_levels=n_levels),
        grid=(n_batch, nc),
        in_specs=[prow(COL_Q), prow(col_z), prow(COL_IV),
                  pl.BlockSpec(logits.shape, lambda b, j: (0, 0)),
                  pl.BlockSpec(dst.shape, lambda b, j: (0, 0)),
                  pl.BlockSpec(msk.shape, lambda b, j: (0, 0, 0)),
                  pl.BlockSpec((1, HG_HEADS, HG_D, HG_D), lambda b, j: (b, 0, 0, 0))],
        out_specs=[pl.BlockSpec((c, hk), lambda b, j: (rows(b, j), 0)),
                   pl.BlockSpec((1, HG_HEADS, HG_D, HG_D), lambda b, j: (b, 0, 0, 0))],
        out_shape=[jax.ShapeDtypeStruct((n_batch * seq, hk), F32),
                   jax.ShapeDtypeStruct((n_batch, HG_HEADS, HG_D, HG_D), F32)],
        scratch_shapes=[pltpu.VMEM((HG_HEADS, HG_D, HG_D), F32)],
        compiler_params=_cparams(2),
        name="hgrn2_scan_bwd" if rev else "hgrn2_scan_fwd",
    )(p, p, p, logits, dst, msk, s0)


def _shortconv_kernel(u_ref, w_ref, b_ref, o_ref):
    u = u_ref[...]
    n = u.shape[0]
    row = lax.broadcasted_iota(jnp.int32, u.shape, 0)
    prev = jnp.where(row == 0, 0.0, pltpu.roll(u, 1, 0))
    nxt = jnp.where(row == n - 1, 0.0, pltpu.roll(u, n - 1, 0))
    w = w_ref[...]
    o_ref[0, 0, 0] = prev * w[0:1] + u * w[1:2] + nxt * w[2:3] + b_ref[...]


def _shortconv_t1major_kernel(u_ref, w_ref, b_ref, o_ref, *, n1):
    half = u_ref.shape[0] // n1
    w = w_ref[...]
    b = b_ref[...]
    row = lax.broadcasted_iota(jnp.int32, (half, LANE), 0)
    col = lambda t1: u_ref[pl.ds(t1, half, stride=n1), :]
    before = jnp.where(row == 0, 0.0, pltpu.roll(col(n1 - 1), 1, 0))
    after = jnp.where(row == half - 1, 0.0, pltpu.roll(col(0), half - 1, 0))

    def body(t1, carry):
        prev, cur = carry
        nxt = jnp.where(t1 == n1 - 1, after, col(jnp.minimum(t1 + 1, n1 - 1)))
        o_ref[0, 0, 0, pl.ds(pl.multiple_of(t1 * half, 8), half), :] = prev * w[0:1] + cur * w[1:2] + nxt * w[2:3] + b
        return cur, nxt

    lax.fori_loop(0, n1, body, (before, col(0)), unroll=4)


def _short_conv(p, w, b, row0, seq, n_batch, n1):
    nb = 3 * HY_W // LANE
    per = HY_W // LANE
    body = _shortconv_kernel if n1 == 1 else functools.partial(_shortconv_t1major_kernel, n1=n1)
    return pl.pallas_call(
        body,
        grid=(n_batch, nb),
        in_specs=[pl.BlockSpec((seq, LANE), lambda bi, j: (row0 // seq + bi, COL_HY // LANE + j)),
                  pl.BlockSpec((3, LANE), lambda bi, j: (0, j)),
                  pl.BlockSpec((1, LANE), lambda bi, j: (0, j))],
        out_specs=pl.BlockSpec((1, 1, 1, seq, LANE), lambda bi, j: (j // per, bi, j % per, 0, 0)),
        out_shape=jax.ShapeDtypeStruct((3, n_batch, per, seq, LANE), F32),
        compiler_params=_cparams(2),
        name="hyena_short_conv",
    )(p, w, b.reshape(1, -1))


def _hy_filter_kernel(emb_ref, embr_ref, w1_ref, b1_ref, w2_ref, b2_ref, w3_ref, b3_ref, w4_ref, fr_ref,
                      dl_ref, o_ref, nrm_ref, *, seq, group):
    i = pl.program_id(0)
    hp = lax.Precision.HIGHEST
    fr = fr_ref[...]

    def mlp(emb):
        hid = jnp.sin(fr * (jnp.dot(emb, w1_ref[...], precision=hp) + b1_ref[...]))
        hid = jnp.sin(fr * (jnp.dot(hid, w2_ref[...], precision=hp) + b2_ref[...]))
        return jnp.sin(fr * (jnp.dot(hid, w3_ref[...], precision=hp) + b3_ref[...]))

    tl = emb_ref.shape[0]
    n = w4_ref.shape[1]
    half = n // 2
    r = lax.broadcasted_iota(jnp.int32, (tl, half), 0) + i * tl
    per = seq // group
    pos = (r >> (per.bit_length() - 1)) + group * (r & (per - 1))
    posr = (pos & ~(group - 1)) + ((group - (pos & (group - 1))) & (group - 1))
    for side, (emb, pp) in enumerate(((emb_ref[...], pos), (embr_ref[...], posr))):
        h = jnp.dot(mlp(emb), w4_ref[:, side * half:(side + 1) * half], precision=hp)
        t = pp.astype(F32) * (1.0 / (seq - 1))
        h = h * (jnp.exp(-t * dl_ref[...]) + HY_SHIFT)
        if side == 1:
            h = jnp.where(pp == 0, 0.0, h)
        for k in range(half // LANE):
            o_ref[side * (half // LANE) + k] = h[:, k * LANE:(k + 1) * LANE]

        @pl.when(i == 0)
        def _():
            nrm_ref[:, side * half:(side + 1) * half] = jnp.zeros((1, half), F32)

        nrm_ref[:, side * half:(side + 1) * half] += jnp.sum(jnp.abs(h), axis=0, keepdims=True)


def _hyena_filter_taps(seq, w1, b1, w2, b2, w3, b3, w4, freq, group):
    fh = w1.shape[1]
    n_emb = w1.shape[0]
    bands_n = (n_emb - 1) // 2
    tt = np.linspace(0.0, 1.0, seq, dtype=np.float32)[:, None].astype(np.float64)
    ww = (2.0 * math.pi / seq) * np.arange(seq, dtype=np.float64)[:, None]
    bands = np.linspace(1e-4, bands_n - 1, bands_n, dtype=np.float32)[None, :].astype(np.float64)
    emb = np.concatenate([tt, np.cos(bands * ww), -np.sin(bands * ww)], axis=-1)
    emb = np.pad(emb, ((0, 0), (0, LANE - n_emb))).astype(np.float32)
    assert seq % group == 0 and group & (group - 1) == 0 and (seq // group) & (seq // group - 1) == 0
    r = np.arange(seq)
    pos = r // (seq // group) + group * (r % (seq // group))
    embr = emb[(pos // group) * group + (group - pos % group) % group]
    emb = emb[pos]
    w1p = jnp.pad(w1, ((0, LANE - n_emb), (0, 0)))
    deltas = np.abs(np.linspace(HY_MIN_DECAY, HY_MAX_DECAY, HY_W, dtype=np.float32))
    deltas = np.tile(deltas, 2)[None, :]
    n = w4.shape[1]
    tl = min(seq, 512)
    full = lambda a: pl.BlockSpec(a.shape, lambda i: (0,) * a.ndim)
    args = (w1p, b1.reshape(1, fh), w2, b2.reshape(1, fh), w3, b3.reshape(1, fh), w4, freq.reshape(1, fh),
            jnp.asarray(deltas))
    return pl.pallas_call(
        functools.partial(_hy_filter_kernel, seq=seq, group=group),
        grid=(seq // tl,),
        in_specs=[pl.BlockSpec((tl, LANE), lambda i: (i, 0))] * 2 + [full(a) for a in args],
        out_specs=[pl.BlockSpec((n // LANE, tl, LANE), lambda i: (0, i, 0)),
                   pl.BlockSpec((1, n), lambda i: (0, 0))],
        out_shape=[jax.ShapeDtypeStruct((n // LANE, seq, LANE), F32), jax.ShapeDtypeStruct((1, n), F32)],
        compiler_params=_cparams(1),
        name="hyena_filter_taps",
    )(jnp.asarray(emb), jnp.asarray(embr), *args)


def _fft_split(seq):
    n = 2 * seq
    n1 = FFT_N1 if n > 1024 else 1
    return n, n1, n // n1


def _cis(idx, n):
    ph = 2.0 * np.pi * (idx % n) / n
    return np.cos(ph), -np.sin(ph)


def _dft_tables_short(seq):
    n = 2 * seq
    cr, ci = _cis(np.arange(n)[:, None] * np.arange(seq)[None, :], n)
    w_fwd = np.concatenate([cr, ci], axis=0)
    w_inv = np.concatenate([cr.T, ci.T], axis=1) / n
    return dict(w_fwd=jnp.asarray(w_fwd, BF16), w_inv=jnp.asarray(w_inv, BF16))


def _dft_tables_long(seq):
    n, n1, n2 = _fft_split(seq)
    t1 = np.arange(n1)[:, None, None]
    f2 = np.arange(n2)[None, :, None]
    t2 = np.arange(n2 // 2)[None, None, :]
    cr, ci = _cis(f2 * (t1 + n1 * t2), n)
    w1 = np.concatenate([cr, ci], axis=1)
    w4 = np.concatenate([np.swapaxes(cr, 1, 2), np.swapaxes(ci, 1, 2)], axis=2) / n
    j = (n1 - t1) % n1 + n1 * t2
    br, bi = _cis(f2 * (n - j), n)
    w1f = np.concatenate([np.concatenate([cr, br], axis=2), np.concatenate([ci, bi], axis=2)], axis=1)
    gr, gi = _cis(np.arange(n1)[:, None] * np.arange(n1)[None, :], n1)
    wd = np.concatenate([np.concatenate([gr, -gi], axis=1), np.concatenate([gi, gr], axis=1)], axis=0)
    return dict(w1=jnp.asarray(w1, BF16), w4=jnp.asarray(w4, BF16), w1f=jnp.asarray(w1f, BF16),
                wd=jnp.asarray(wd, BF16), wdi=jnp.asarray(wd.T, BF16))


def _dft_rows_kernel(w_ref, x_ref, o_ref):
    o_ref[0] = _dot(w_ref[...], x_ref[0].astype(BF16)).astype(o_ref.dtype)


def _dft_rows(w, x, tn):
    nb, k, cols = x.shape
    m = w.shape[0]
    return pl.pallas_call(
        _dft_rows_kernel,
        grid=(nb, cols // tn),
        in_specs=[pl.BlockSpec((m, k), lambda b, j: (0, 0)),
                  pl.BlockSpec((1, k, tn), lambda b, j: (b, 0, j))],
        out_specs=pl.BlockSpec((1, m, tn), lambda b, j: (b, 0, j)),
        out_shape=jax.ShapeDtypeStruct((nb, m, cols), BF16),
        compiler_params=_cparams(2),
        name="hyena_dft_rows",
    )(w, x)


def _idft_gate_kernel(w_ref, b_ref, xg_ref, z_ref, sk_ref, o_ref):
    y = _dot(w_ref[...], b_ref[0])
    z = z_ref[0]
    o_ref[0] = xg_ref[0] * (y + z * sk_ref[...])


def _idft_gate(w, bc, xg, z, skip_t, tn):
    nb, k, cols = bc.shape
    m = w.shape[0]
    return pl.pallas_call(
        _idft_gate_kernel,
        grid=(nb, cols // tn),
        in_specs=[pl.BlockSpec((m, k), lambda b, j: (0, 0)),
                  pl.BlockSpec((1, k, tn), lambda b, j: (b, 0, j)),
                  pl.BlockSpec((1, m, tn), lambda b, j: (b, 0, j)),
                  pl.BlockSpec((1, m, tn), lambda b, j: (b, 0, j)),
                  pl.BlockSpec((1, tn), lambda b, j: (0, j))],
        out_specs=pl.BlockSpec((1, m, tn), lambda b, j: (b, 0, j)),
        out_shape=jax.ShapeDtypeStruct((nb, m, cols), F32),
        compiler_params=_cparams(2),
        name="hyena_idft_gate",
    )(w, bc, xg, z, skip_t)


def _spec_combine_kernel(af_ref, ab_ref, nf_ref, nb_ref, o_ref):
    inv = 1.0 / (nf_ref[...] + nb_ref[...])
    n = o_ref.shape[1]
    o_ref[0] = (af_ref[0, :n].astype(F32) + ab_ref[0, :n].astype(F32)) * inv
    o_ref[1] = (af_ref[0, n:].astype(F32) - ab_ref[0, n:].astype(F32)) * inv


def _spec_mul_kernel(a_ref, k_ref, o_ref):
    n = k_ref.shape[1]
    xr, xi = a_ref[0, :n].astype(F32), a_ref[0, n:].astype(F32)
    kr, ki = k_ref[0], k_ref[1]
    o_ref[0, :n] = (xr * kr - xi * ki).astype(o_ref.dtype)
    o_ref[0, n:] = (xr * ki + xi * kr).astype(o_ref.dtype)


def _hyena_short(seq, taps, nrm, uc, skip):
    n = 2 * seq
    uc = jnp.swapaxes(uc, 2, 3).reshape(3, uc.shape[1], seq, HY_W)
    tabs = _dft_tables_short(seq)
    cf = taps.shape[1]
    c = HY_W
    a = _dft_rows(tabs["w_fwd"], taps.reshape(1, seq, cf), tn=cf)
    nblk = cf // 2 // c
    spec = pl.pallas_call(
        _spec_combine_kernel,
        grid=(nblk,),
        in_specs=[pl.BlockSpec((1, 2 * n, c), lambda j: (0, 0, j)),
                  pl.BlockSpec((1, 2 * n, c), lambda j: (0, 0, j + nblk)),
                  pl.BlockSpec((1, c), lambda j: (0, j)),
                  pl.BlockSpec((1, c), lambda j: (0, j + nblk))],
        out_specs=pl.BlockSpec((2, n, c), lambda j: (0, 0, j)),
        out_shape=jax.ShapeDtypeStruct((2, n, cf // 2), F32),
        compiler_params=_cparams(1),
        name="hyena_spec_combine",
    )(a, a, nrm, nrm)
    nb = uc.shape[1]
    z = uc[2]
    for order in range(2):
        a = _dft_rows(tabs["w_fwd"], z, tn=c)
        bc = pl.pallas_call(
            _spec_mul_kernel,
            grid=(nb,),
            in_specs=[pl.BlockSpec((1, 2 * n, c), lambda b: (b, 0, 0)),
                      pl.BlockSpec((2, n, c), lambda b: (0, 0, order))],
            out_specs=pl.BlockSpec((1, 2 * n, c), lambda b: (b, 0, 0)),
            out_shape=jax.ShapeDtypeStruct((nb, 2 * n, c), BF16),
            compiler_params=_cparams(1),
            name="hyena_spec_mul",
        )(a, spec)
        z = _idft_gate(tabs["w_inv"], bc, uc[order], z, skip[order].reshape(1, c), tn=c)
    return z


HY_SLABS = 2


def _slab_pitch(n2):
    return n2 + 8


def _pack_c(re, im):
    hi = lax.bitcast_convert_type(re.astype(BF16).astype(F32), jnp.uint32)
    lo = lax.bitcast_convert_type(im.astype(BF16).astype(F32), jnp.uint32)
    return hi | (lo >> 16)


def _unpack_c(w):
    re = lax.bitcast_convert_type(w & jnp.uint32(0xFFFF0000), F32)
    im = lax.bitcast_convert_type(w << 16, F32)
    return re.astype(BF16), im.astype(BF16)


def _store_slab(a_scr, row, words):
    for s in range(HY_SLABS):
        a_scr[s, pl.ds(row, words.shape[0]), :] = words[:, s * LANE:(s + 1) * LANE]


def _stage1(a_scr, w_ref, xs_of, i, tb, n2, pitch):
    for j in range(tb):
        a = _dot(w_ref[j], xs_of(j))
        _store_slab(a_scr, pl.multiple_of((i * tb + j) * pitch, 8), _pack_c(a[:n2], a[n2:]))


def _stage2(a_scr, wd_ref, f2, n1, pitch):
    w = jnp.concatenate([a_scr[s, pl.ds(f2, n1, stride=pitch), :] for s in range(HY_SLABS)], axis=1)
    re, im = _unpack_c(w)
    return _dot(wd_ref[...], jnp.concatenate([re, im], axis=0))


def _time_col(ref, j, half):
    lead = (0,) * (len(ref.shape) - 3)
    return jnp.concatenate([ref[lead + (s, slice(j * half, (j + 1) * half), slice(None))]
                            for s in range(HY_SLABS)], axis=1)


def _hy_spectrum_kernel(hf_ref, hb_ref, nf_ref, nb_ref, w1f_ref, wd_ref, o_ref, a_scr, *, n1, n2, tb, fb):
    i = pl.program_id(1)
    pitch = _slab_pitch(n2)
    half = n2 // 2
    nt = n1 // tb

    @pl.when(i < nt)
    def _():
        def xs_of(j):
            return jnp.concatenate([_time_col(hf_ref, j, half), _time_col(hb_ref, j, half)], axis=0).astype(BF16)
        _stage1(a_scr, w1f_ref, xs_of, i, tb, n2, pitch)

    @pl.when(i >= nt)
    def _():
        inv = 1.0 / (nf_ref[...] + nb_ref[...])
        for jj in range(fb):
            x = _stage2(a_scr, wd_ref, (i - nt) * fb + jj, n1, pitch)
            o_ref[0, jj] = x[:n1] * inv
            o_ref[1, jj] = x[n1:] * inv


def _hy_fftconv_kernel(z_ref, xg_ref, k_ref, sk_ref, w1_ref, wd_ref, wdi_ref, w4_ref, o_ref, a_scr,
                       *, n1, n2, tb, fb):
    i = pl.program_id(2)
    pitch = _slab_pitch(n2)
    half = n2 // 2
    nt, nf = n1 // tb, n2 // fb

    @pl.when(i < nt)
    def _():
        _stage1(a_scr, w1_ref, lambda j: _time_col(z_ref, j, half).astype(BF16), i, tb, n2, pitch)

    @pl.when((i >= nt) & (i < nt + nf))
    def _():
        for jj in range(fb):
            f2 = (i - nt) * fb + jj
            x = _stage2(a_scr, wd_ref, f2, n1, pitch)
            xr, xi = x[:n1], x[n1:]
            kr, ki = k_ref[0, jj], k_ref[1, jj]
            y = jnp.concatenate([xr * kr - xi * ki, xr * ki + xi * kr], axis=0).astype(BF16)
            bv = _dot(wdi_ref[...], y)
            words = _pack_c(bv[:n1], bv[n1:])
            for s in range(HY_SLABS):
                a_scr[s, pl.ds(f2, n1, stride=pitch), :] = words[:, s * LANE:(s + 1) * LANE]

    @pl.when(i >= nt + nf)
    def _():
        sk = sk_ref[...]
        for j in range(tb):
            row = pl.multiple_of(((i - nt - nf) * tb + j) * pitch, 8)
            w = jnp.concatenate([a_scr[s, pl.ds(row, n2), :] for s in range(HY_SLABS)], axis=1)
            re, im = _unpack_c(w)
            y = _dot(w4_ref[j], jnp.concatenate([re, im], axis=0))
            out = _time_col(xg_ref, j, half) * (y + _time_col(z_ref, j, half) * sk)
            for s in range(HY_SLABS):
                o_ref[0, s, j * half:(j + 1) * half, :] = out[:, s * LANE:(s + 1) * LANE]


def _hyena_long(seq, taps, nrm, uc, skip, tb=32, fb=16):
    n, n1, n2 = _fft_split(seq)
    tabs = _dft_tables_long(seq)
    pitch = _slab_pitch(n2)
    half = n2 // 2
    cb = HY_SLABS * LANE
    tb, fb = min(tb, n1), min(fb, n2)
    nt, nf = n1 // tb, n2 // fb
    cf = taps.shape[0] * LANE
    ngrp = cf // 2 // cb
    scratch = [pltpu.VMEM((HY_SLABS, n1 * pitch, LANE), jnp.uint32)]
    tcol = lambda i: jnp.minimum(i, nt - 1)
    spec = pl.pallas_call(
        functools.partial(_hy_spectrum_kernel, n1=n1, n2=n2, tb=tb, fb=fb),
        grid=(ngrp, nt + nf),
        in_specs=[pl.BlockSpec((HY_SLABS, tb * half, LANE), lambda g, i: (g, tcol(i), 0)),
                  pl.BlockSpec((HY_SLABS, tb * half, LANE), lambda g, i: (ngrp + g, tcol(i), 0)),
                  pl.BlockSpec((1, cb), lambda g, i: (0, g)),
                  pl.BlockSpec((1, cb), lambda g, i: (0, ngrp + g)),
                  pl.BlockSpec((tb, 2 * n2, n2), lambda g, i: (tcol(i), 0, 0)),
                  pl.BlockSpec(tabs["wd"].shape, lambda g, i: (0, 0))],
        out_specs=pl.BlockSpec((2, fb, n1, cb), lambda g, i: (0, jnp.maximum(i - nt, 0), 0, g)),
        out_shape=jax.ShapeDtypeStruct((2, n2, n1, cf // 2), F32),
        scratch_shapes=scratch,
        compiler_params=_cparams(2),
        name="hyena_filter_spectrum",
    )(taps, taps, nrm, nrm, tabs["w1f"], tabs["wd"])
    nb, nblk = uc.shape[1], uc.shape[2]
    ngc = nblk // HY_SLABS
    z = uc[2]

    def tblk(b, g, i):
        return (b, g, jnp.where(i < nt, i, jnp.maximum(i - nt - nf, 0)), 0)

    def tblk_late(b, g, i):
        return (b, g, jnp.maximum(i - nt - nf, 0), 0)

    for order in range(2):
        z = pl.pallas_call(
            functools.partial(_hy_fftconv_kernel, n1=n1, n2=n2, tb=tb, fb=fb),
            grid=(nb, ngc, 2 * nt + nf),
            in_specs=[pl.BlockSpec((1, HY_SLABS, tb * half, LANE), tblk),
                      pl.BlockSpec((1, HY_SLABS, tb * half, LANE), tblk_late),
                      pl.BlockSpec((2, fb, n1, cb),
                                   lambda b, g, i: (0, jnp.clip(i - nt, 0, nf - 1), 0, order * ngc + g)),
                      pl.BlockSpec((1, cb), lambda b, g, i: (0, g)),
                      pl.BlockSpec((tb, 2 * n2, half), lambda b, g, i: (tcol(i), 0, 0)),
                      pl.BlockSpec(tabs["wd"].shape, lambda b, g, i: (0, 0)),
                      pl.BlockSpec(tabs["wdi"].shape, lambda b, g, i: (0, 0)),
                      pl.BlockSpec((tb, half, 2 * n2), lambda b, g, i: (jnp.maximum(i - nt - nf, 0), 0, 0))],
            out_specs=pl.BlockSpec((1, HY_SLABS, tb * half, LANE), tblk_late),
            out_shape=jax.ShapeDtypeStruct((nb, nblk, seq, LANE), F32),
            scratch_shapes=scratch,
            compiler_params=_cparams(3),
            name="hyena_fft_conv",
        )(z, uc[order], spec, skip[order].reshape(1, HY_W), tabs["w1"], tabs["wd"], tabs["wdi"], tabs["w4"])
    return z


def _hyena(p, lw, row0, seq, n_batch):
    long = _fft_split(seq)[1] > 1
    n1 = FFT_N1 if long else 1
    taps, nrm = _hyena_filter_taps(seq, lw["hy_w1"], lw["hy_b1"], lw["hy_w2"], lw["hy_b2"], lw["hy_w3"],
                                   lw["hy_b3"], lw["hy_w4"], lw["hy_freq"], n1)
    uc = _short_conv(p, lw["hy_conv_w"], lw["hy_conv_b"], row0, seq, n_batch, n1)
    if long:
        return _hyena_long(seq, taps, nrm, uc, lw["hy_skip"])
    taps = jnp.swapaxes(taps, 0, 1).reshape(seq, -1)
    return _hyena_short(seq, taps, nrm, uc, lw["hy_skip"]).reshape(n_batch * seq, HY_W)


def _mla_qkv_kernel(qa_ref, kva_ref, kr_ref, cs_ref, wq_ref, wkv_ref, gqa_ref, gkva_ref, gqn_ref, gqr_ref,
                    gkn_ref, gkr_ref, q_ref, k_ref, v_ref):
    cs = cs_ref[...]
    lane = lax.broadcasted_iota(jnp.int32, cs.shape, 1)
    low = lane < MLA_ROPE

    def rope(pair, gain2):
        ms = jnp.sum(jnp.where(low, pair * pair, 0.0), axis=-1, keepdims=True) * (1.0 / MLA_ROPE)
        t = pair * lax.rsqrt(ms + RMS_EPS) * gain2 * cs
        return jnp.where(low, t + pltpu.roll(t, MLA_ROPE, 1), 0.0)

    qq = _dot(_rms(qa_ref[...], gqa_ref[...]).astype(BF16), wq_ref[...])
    kv = _dot(_rms(kva_ref[...], gkva_ref[...]).astype(BF16), wkv_ref[...])
    kr = rope(kr_ref[...], gkr_ref[...])
    hp = MLA_HEAD_PAD
    tm = kr.shape[0]
    qscale = MLA_SCALE * LOG2_E
    ones_row = (lax.broadcasted_iota(jnp.int32, (V_ROWS - MLA_V, tm), 0) == 0).astype(BF16)
    for h in range(MLA_HEADS):
        qn = _rms(qq[:, h * hp:h * hp + MLA_NOPE], gqn_ref[...])
        qr = rope(qq[:, h * hp + MLA_NOPE:(h + 1) * hp], gqr_ref[...])
        q_ref[h, 0:LANE, :] = (qn * qscale).T.astype(BF16)
        q_ref[h, LANE:2 * LANE, :] = (qr * qscale).T.astype(BF16)
        kn = _rms(kv[:, h * hp:h * hp + MLA_NOPE], gkn_ref[...])
        k_ref[h, :, 0:LANE] = kn.astype(BF16)
        k_ref[h, :, LANE:2 * LANE] = kr.astype(BF16)
        v_ref[h, 0:MLA_V, :] = kv[:, h * hp + MLA_NOPE:(h + 1) * hp].T.astype(BF16)
        v_ref[h, MLA_V:V_ROWS, :] = ones_row


def _rope_table(seq, n_batch, ctx_rows):
    rows = seq // GRID_W
    row = np.repeat(np.arange(rows, dtype=np.float32), GRID_W)
    col = np.tile(np.arange(GRID_W, dtype=np.float32), rows)
    half = MLA_ROPE // 2
    inv = (ROPE_THETA ** (-np.arange(0, half, 2, dtype=np.float32) / half)).astype(np.float32)
    ar = (row[:, None] * inv).astype(np.float64)
    ac = (col[:, None] * inv).astype(np.float64)
    cos = np.concatenate([np.cos(ar), np.cos(ar), np.cos(ac), np.cos(ac)], axis=1)
    sin = np.concatenate([-np.sin(ar), np.sin(ar), -np.sin(ac), np.sin(ac)], axis=1)
    lat = np.tile(np.concatenate([cos, sin], axis=1), (n_batch, 1))
    ctx = np.concatenate([np.ones((ctx_rows, MLA_ROPE)), np.zeros((ctx_rows, MLA_ROPE))], axis=1)
    return jnp.asarray(np.concatenate([lat, ctx], axis=0), F32)


def _pair_gain(g):
    return jnp.concatenate([g, _rope_swap(g)]).reshape(1, 2 * MLA_ROPE)


def _mla_qkv(p, cs, lw, tm=512):
    n_rows = p.shape[0]
    hd, hp = MLA_HEADS, MLA_HEAD_PAD
    w_uq = lw["w_uq"].reshape(-1, hd, MLA_QK)
    wq = jnp.concatenate([w_uq, _rope_swap(w_uq[..., MLA_NOPE:])], axis=-1).reshape(-1, hd * hp).astype(BF16)
    wkv = lw["w_ukv"].astype(BF16)
    ql, kvl = wq.shape[0], wkv.shape[0]
    vec = lambda a: a.reshape(1, -1)
    full = lambda a: pl.BlockSpec(a.shape, lambda i: (0,) * a.ndim)
    args = (wq, wkv, vec(lw["q_a_norm"]), vec(lw["kv_a_norm"]), vec(lw["q_nope_norm"]),
            _pair_gain(lw["q_rope_norm"]), vec(lw["k_nope_norm"]), _pair_gain(lw["k_rope_norm"]))
    return pl.pallas_call(
        _mla_qkv_kernel,
        grid=(n_rows // tm,),
        in_specs=[pl.BlockSpec((tm, ql), lambda i: (i, COL_QA // ql)),
                  pl.BlockSpec((tm, kvl), lambda i: (i, COL_KVA // kvl)),
                  pl.BlockSpec((tm, LANE), lambda i: (i, COL_KR // LANE)),
                  pl.BlockSpec((tm, LANE), lambda i: (i, 0))] + [full(a) for a in args],
        out_specs=[pl.BlockSpec((hd, hp, tm), lambda i: (0, 0, i)),
                   pl.BlockSpec((hd, tm, hp), lambda i: (0, i, 0)),
                   pl.BlockSpec((hd, V_ROWS, tm), lambda i: (0, 0, i))],
        out_shape=[jax.ShapeDtypeStruct((hd, hp, n_rows), BF16),
                   jax.ShapeDtypeStruct((hd, n_rows, hp), BF16),
                   jax.ShapeDtypeStruct((hd, V_ROWS, n_rows), BF16)],
        compiler_params=_cparams(1),
        name="mla_qkv",
    )(p, p, p, cs, *args)


def _key_chunks(k_refs, v_refs, tk):
    chunks, s0 = [], 0
    for kr, vr in zip(k_refs, v_refs):
        n = kr.shape[1]
        step = min(tk, n)
        chunks += [(kr, vr, r0, step, s0 + r0) for r0 in range(0, n, step)]
        s0 += n
    return chunks


def _score_pass(qt, chunks, s_scr):
    m = None
    for kr, _, r0, rn, s0 in chunks:
        s = _dot(kr[0, r0:r0 + rn, :], qt)
        s_scr[s0:s0 + rn, :] = s
        mj = jnp.max(s, axis=0, keepdims=True)
        m = mj if m is None else jnp.maximum(m, mj)
    return m


def _value_pass(chunks, s_scr, m, o_ref):
    acc = None
    for _, vr, r0, rn, s0 in chunks:
        p = jnp.exp2((s_scr[s0:s0 + rn, :] - m).astype(BF16))
        part = _dot(vr[0, :, r0:r0 + rn], p)
        acc = part if acc is None else acc + part
    o_ref[...] = (acc[:MLA_V] / acc[MLA_V:MLA_V + 1]).astype(o_ref.dtype)


def _attn_ctx_kernel(q_ref, kc_ref, vc_ref, o_ref, s_scr, *, tk):
    chunks = _key_chunks([kc_ref], [vc_ref], tk)
    _value_pass(chunks, s_scr, _score_pass(q_ref[0], chunks, s_scr), o_ref)


def _attn_kernel(q_ref, k_ref, v_ref, kc_ref, vc_ref, o_ref, s_a, s_b, m_a, m_b, *, tk):
    i = pl.program_id(2)
    chunks = _key_chunks([kc_ref, k_ref], [vc_ref, v_ref], tk)

    @pl.when((pl.program_id(0) == 0) & (pl.program_id(1) == 0) & (i == 0))
    def _():
        s_b[...] = jnp.zeros_like(s_b)
        m_b[...] = jnp.zeros_like(m_b)

    def step(s_cur, m_cur, s_prev, m_prev):
        _value_pass(chunks, s_prev, m_prev[0:1, :], o_ref)
        m_cur[...] = jnp.broadcast_to(_score_pass(q_ref[0], chunks, s_cur), m_cur.shape)

    @pl.when(i % 2 == 0)
    def _():
        step(s_a, m_a, s_b, m_b)

    @pl.when(i % 2 == 1)
    def _():
        step(s_b, m_b, s_a, m_a)


def _attention(qt, k, vt, seq, ctx_len, n_batch, latent, tq=256, tk=1024):
    hd, hp, n_rows = qt.shape
    lat_rows = n_batch * seq
    cblk = lat_rows // ctx_len
    kc_spec = pl.BlockSpec((1, ctx_len, hp), lambda b, h, i: (h, cblk + b, 0))
    vc_spec = pl.BlockSpec((1, V_ROWS, ctx_len), lambda b, h, i: (h, 0, cblk + b))
    if latent:
        nq = seq // tq
        steps = nq + 1
        in_specs = [pl.BlockSpec((1, hp, tq), lambda b, h, i: (h, 0, b * nq + jnp.minimum(i, nq - 1))),
                    pl.BlockSpec((1, seq, hp), lambda b, h, i: (h, b, 0)),
                    pl.BlockSpec((1, V_ROWS, seq), lambda b, h, i: (h, 0, b)), kc_spec, vc_spec]
        args = (qt, k, vt, k, vt)
        out_spec = pl.BlockSpec((MLA_V, tq), lambda b, h, i: (h, b * nq + jnp.maximum(i - 1, 0)))
        out_cols, n_keys = lat_rows, seq + ctx_len
        body = functools.partial(_attn_kernel, tk=tk)
        scratch = [pltpu.VMEM((n_keys, tq), F32)] * 2 + [pltpu.VMEM((8, tq), F32)] * 2
    else:
        steps = 1
        tq = ctx_len
        in_specs = [pl.BlockSpec((1, hp, tq), lambda b, h, i: (h, 0, cblk + b)), kc_spec, vc_spec]
        args = (qt, k, vt)
        out_spec = pl.BlockSpec((MLA_V, tq), lambda b, h, i: (h, b))
        out_cols, n_keys = n_batch * ctx_len, ctx_len
        body = functools.partial(_attn_ctx_kernel, tk=tk)
        scratch = [pltpu.VMEM((n_keys, tq), F32)]
    return pl.pallas_call(
        body,
        grid=(n_batch, hd, steps),
        in_specs=in_specs,
        out_specs=out_spec,
        out_shape=jax.ShapeDtypeStruct((hd * MLA_V, out_cols), BF16),
        scratch_shapes=scratch,
        compiler_params=_cparams(3),
        name="mla_attention" if latent else "mla_attention_ctx",
    )(*args)


def _merge_kernel(*refs, lat_tiles):
    (x_ref, mod_ref, of_ref, ob_ref, g_ref, yb_ref, yc_ref, ga_ref, gb_ref, gc_ref,
     gain_ref, wa_ref, wb_ref, wc_ref, wo_ref) = refs[:15]
    o_ref = refs[-1]
    m = mod_ref[0]
    o = of_ref[...] + ob_ref[...]
    gain = gain_ref[...]
    ya = jnp.concatenate([_rms(o[:, h * HG_D:(h + 1) * HG_D], gain) for h in range(HG_HEADS)], axis=1)
    ya = (ya * _silu(g_ref[...])).astype(BF16)
    nslab, tcols = yb_ref.shape[1], yb_ref.shape[3]
    qn = x_ref.shape[0] // yb_ref.shape[2]
    parts = [jnp.concatenate([jnp.concatenate([yb_ref[0, k, :, q, :] for k in range(nslab)], axis=1)
                              for q in range(h * qn, (h + 1) * qn)], axis=0) for h in range(tcols // qn)]
    sub = pl.program_id(0) % len(parts)
    yb = parts[0]
    for h in range(1, len(parts)):
        yb = jnp.where(sub == h, parts[h], yb)
    if len(refs) == 17:
        yb = jnp.where(pl.program_id(0) >= lat_tiles, refs[15][...], yb)
    mix = (_sigmoid(ga_ref[...]) * _dot(ya, wa_ref[...])
           + _sigmoid(gb_ref[...]) * _dot(yb.astype(BF16), wb_ref[...])
           + _sigmoid(gc_ref[...]) * _dot_tn(yc_ref[...], wc_ref[...]))
    o_ref[...] = x_ref[...] + m[5:6] * _dot(mix.astype(BF16), wo_ref[...])


def _merge(xa, mod, o_f, o_b, p, y_b, y_b_ctx, y_c, lw, n_rows, seq, n_batch, tm=512):
    d = xa.shape[1]
    hk = HG_HEADS * HG_D
    tpb = seq // tm
    lat_tiles = n_batch * tpb
    n1 = FFT_N1
    tcols = 8
    assert tm % n1 == 0 and tcols % (tm // n1) == 0 and seq % (tcols * n1) == 0
    per_blk = tcols // (tm // n1)
    y_b = y_b.reshape(n_batch, y_b.shape[1], n1, seq // n1, LANE)
    row = lambda w: pl.BlockSpec((tm, w), lambda i: (i, 0))
    pcol = lambda w, c: pl.BlockSpec((tm, w), lambda i: (i, c // w))
    full = lambda a: pl.BlockSpec(a.shape, lambda i: (0,) * a.ndim)
    ws = (lw["hg_out_norm"].reshape(1, HG_D), lw["w_br_a"].astype(BF16), lw["w_br_b"].astype(BF16),
          lw["w_br_c"].astype(BF16), lw["w_out"].astype(BF16))
    yb_spec = pl.BlockSpec((1, y_b.shape[1], n1, tcols, LANE),
                           lambda i: (jnp.minimum(i // tpb, n_batch - 1), 0, 0, (i % tpb) // per_blk, 0))
    extra_specs, extra = [], []
    if y_b_ctx is not None:
        extra_specs = [pl.BlockSpec((tm, HY_W), lambda i: (jnp.maximum(i - lat_tiles, 0), 0))]
        extra = [y_b_ctx]
    return pl.pallas_call(
        functools.partial(_merge_kernel, lat_tiles=lat_tiles),
        grid=(n_rows // tm,),
        in_specs=[row(d), pl.BlockSpec((1, N_MOD, d), _group_map(tpb, n_batch)),
                  row(hk), row(hk), pcol(hk, COL_G), yb_spec,
                  pl.BlockSpec((MLA_HEADS * MLA_V, tm), lambda i: (0, i)),
                  pcol(d, COL_GA), pcol(d, COL_GB), pcol(d, COL_GC)] + [full(a) for a in ws] + extra_specs,
        out_specs=row(d),
        out_shape=jax.ShapeDtypeStruct((n_rows, d), F32),
        compiler_params=_cparams(1),
        name="merge",
    )(xa, mod, o_f, o_b, p, y_b, y_c, p, p, p, *ws, *extra)


def kernel(x, c, ctx, c_ctx, ada_w, ada_b, ffn1_norm, ffn1_w13, ffn1_w2, mix_norm, w_in, hg_lb_logits, hg_out_norm, hy_conv_w, hy_conv_b, hy_w1, hy_b1, hy_w2, hy_b2, hy_w3, hy_b3, hy_w4, hy_freq, hy_skip, q_a_norm, w_uq, kv_a_norm, w_ukv, q_nope_norm, q_rope_norm, k_nope_norm, k_rope_norm, w_br_a, w_br_b, w_br_c, w_out, ffn2_norm, ffn2_w13, ffn2_w2):
    stacked = dict(
        ada_w=ada_w, ada_b=ada_b, ffn1_norm=ffn1_norm, ffn1_w13=ffn1_w13, ffn1_w2=ffn1_w2, mix_norm=mix_norm,
        w_in=w_in, hg_out_norm=hg_out_norm, hy_conv_w=hy_conv_w, hy_conv_b=hy_conv_b, hy_w1=hy_w1, hy_b1=hy_b1,
        hy_w2=hy_w2, hy_b2=hy_b2, hy_w3=hy_w3, hy_b3=hy_b3, hy_w4=hy_w4, hy_freq=hy_freq, hy_skip=hy_skip,
        q_a_norm=q_a_norm, w_uq=w_uq, kv_a_norm=kv_a_norm, w_ukv=w_ukv, q_nope_norm=q_nope_norm,
        q_rope_norm=q_rope_norm, k_nope_norm=k_nope_norm, k_rope_norm=k_rope_norm, w_br_a=w_br_a,
        w_br_b=w_br_b, w_br_c=w_br_c, w_out=w_out, ffn2_norm=ffn2_norm, ffn2_w13=ffn2_w13, ffn2_w2=ffn2_w2)
    n_batch, seq, d = x.shape
    ctx_len = ctx.shape[1]
    depth = ada_w.shape[0]
    lat_rows, ctx_rows = n_batch * seq, n_batch * ctx_len
    all_rows = lat_rows + ctx_rows
    assert seq % 512 == 0 and ctx_rows % 512 == 0 and seq % ctx_len == 0 and seq % GRID_W == 0
    assert ctx_len % HG_CHUNK == 0 and n_batch < 8

    xa = jnp.concatenate([x.reshape(lat_rows, d), ctx.reshape(ctx_rows, d)], axis=0)
    cs = jnp.concatenate([c, c_ctx.reshape(1, d), jnp.zeros((7 - n_batch, d), F32)], axis=0)
    rope_cs = _rope_table(seq, n_batch, ctx_rows)
    zero_state = jnp.zeros((n_batch, HG_HEADS, HG_D, HG_D), F32)

    for l in range(depth):
        lw = {name: val[l] for name, val in stacked.items()}
        need_ctx = l < depth - 1
        mod = _modulation(cs, lw["ada_w"], lw["ada_b"])
        xa = _half_ffn(xa, mod, lw["ffn1_norm"], lw["ffn1_w13"], lw["ffn1_w2"], 0, all_rows, seq, n_batch)
        p = _in_projection(xa, mod, lw["mix_norm"], _pack_w_in(lw["w_in"]).astype(BF16), seq, n_batch)

        mix_rows = all_rows if need_ctx else lat_rows
        o_dir = []
        for rev in (False, True):
            lg = hg_lb_logits[1 if rev else 0]
            o_c, s_c = _hgrn2_scan(p, lg, zero_state, l, rev, lat_rows, ctx_len, n_batch)
            o_l, _ = _hgrn2_scan(p, lg, s_c, l, rev, 0, seq, n_batch)
            o_dir.append(jnp.concatenate([o_l, o_c], axis=0) if need_ctx else o_l)

        y_b = _hyena(p, lw, 0, seq, n_batch)
        y_b_ctx = _hyena(p, lw, lat_rows, ctx_len, n_batch) if need_ctx else None

        qt, k, vt = _mla_qkv(p, rope_cs, lw)
        y_c = _attention(qt, k, vt, seq, ctx_len, n_batch, latent=True)
        if need_ctx:
            y_c = jnp.concatenate([y_c, _attention(qt, k, vt, seq, ctx_len, n_batch, latent=False)], axis=1)

        xa = _merge(xa, mod, o_dir[0], o_dir[1], p, y_b, y_b_ctx, y_c, lw, mix_rows, seq, n_batch)
        xa = _half_ffn(xa, mod, lw["ffn2_norm"], lw["ffn2_w13"], lw["ffn2_w2"], 6, mix_rows, seq, n_batch)
    return xa[:lat_rows].reshape(n_batch, seq, d)
```

```python
import functools
import math

import numpy as np
import jax
import jax.numpy as jnp
from jax import lax
from jax.experimental import pallas as pl
from jax.experimental.pallas import tpu as pltpu

F32 = jnp.float32
BF16 = jnp.bfloat16

RMS_EPS = 1e-6
N_MOD = 9
GRID_W = 64
ROPE_THETA = 10000.0
HG_HEADS = 4
HG_D = 128
HG_CHUNK = 128
HY_W = 512
HY_TARGET = 1e-2
HY_MIN_DECAY = math.log(HY_TARGET) / 1.5
HY_MAX_DECAY = math.log(HY_TARGET) / 0.3
HY_SHIFT = 0.05
MLA_HEADS = 4
MLA_NOPE = 128
MLA_ROPE = 64
MLA_V = 128
MLA_QK = MLA_NOPE + MLA_ROPE
MLA_SCALE = MLA_QK ** -0.5
MLA_HEAD_PAD = 256
V_ROWS = MLA_V + 16
LOG2_E = math.log2(math.e)
FFT_N1 = 128
LANE = 128

VMEM_LIMIT = 52 * 1024 * 1024

COL_GA, COL_GB, COL_GC = 0, 1024, 2048
COL_Q, COL_ZF, COL_ZB, COL_IV, COL_G = 3072, 3584, 4096, 4608, 5120
COL_HY = 5632
COL_QA, COL_KVA, COL_KR = 7168, 7424, 7552
IN_PACKED = 7680


def _cparams(n_axes):
    return pltpu.CompilerParams(dimension_semantics=("arbitrary",) * n_axes,
                                vmem_limit_bytes=VMEM_LIMIT)


def _dot(a, b):
    return jnp.dot(a, b, preferred_element_type=F32)


def _dot_nt(a, b):
    return lax.dot_general(a, b, (((1,), (1,)), ((), ())), preferred_element_type=F32)


def _dot_tn(a, b):
    return lax.dot_general(a, b, (((0,), (0,)), ((), ())), preferred_element_type=F32)


def _sigmoid(x):
    return 1.0 / (1.0 + jnp.exp(-x))


def _silu(x):
    return x * _sigmoid(x)


def _rms(x, gain):
    return x * lax.rsqrt(jnp.mean(x * x, axis=-1, keepdims=True) + RMS_EPS) * gain


def _mod_kernel(c_ref, w_ref, b_ref, o_ref):
    o_ref[...] = _dot(_silu(c_ref[...]).astype(BF16), w_ref[...]) + b_ref[...]


def _modulation(cs, ada_w, ada_b):
    g, d = cs.shape
    n = ada_w.shape[1]
    tn = n // 4
    out = pl.pallas_call(
        _mod_kernel,
        grid=(n // tn,),
        in_specs=[pl.BlockSpec((g, d), lambda j: (0, 0)),
                  pl.BlockSpec((d, tn), lambda j: (0, j)),
                  pl.BlockSpec((1, tn), lambda j: (0, j))],
        out_specs=pl.BlockSpec((g, tn), lambda j: (0, j)),
        out_shape=jax.ShapeDtypeStruct((g, n), F32),
        compiler_params=_cparams(1),
        name="modulation",
    )(cs, ada_w.astype(BF16), ada_b.reshape(1, n))
    return out.reshape(g, N_MOD, d)


def _ffn_kernel(*refs, idx, ff, ck, lat_tiles):
    x_ref, mod_ref, g_ref, w13_ref, w2_ref = refs[:5]
    o_ref = refs[-1]
    x = x_ref[...]
    if len(refs) == 7:
        x = jnp.where(pl.program_id(0) >= lat_tiles, refs[5][...], x)
    m = mod_ref[0]
    h = (_rms(x, g_ref[...]) * (1.0 + m[idx + 1:idx + 2]) + m[idx:idx + 1]).astype(BF16)
    acc = jnp.zeros(x.shape, F32)
    for c0 in range(0, ff, ck):
        c1 = min(c0 + ck, ff)
        a = _dot(h, w13_ref[:, c0:c1])
        b = _dot(h, w13_ref[:, ff + c0:ff + c1])
        acc = acc + _dot((_silu(a) * b).astype(BF16), w2_ref[c0:c1, :])
    o_ref[...] = x + (0.5 * m[idx + 2:idx + 3]) * acc


def _group_map(tiles_per_batch, n_batch):
    return lambda i: (jnp.minimum(i // tiles_per_batch, n_batch), 0, 0)


def _half_ffn(xa, mod, gain, w13, w2, idx, n_rows, seq, n_batch, xc=None, tm=512, ck=512):
    d = xa.shape[1]
    ff = w2.shape[0]
    lat_tiles = n_batch * seq // tm
    const = dict(pipeline_mode=pl.Buffered(1))
    x_specs, xs = [pl.BlockSpec((tm, d), lambda i: (jnp.minimum(i, xa.shape[0] // tm - 1), 0))], [xa]
    tail_specs, tail = [], []
    if xc is not None:
        tail_specs, tail = [pl.BlockSpec((tm, d), lambda i: (jnp.maximum(i - lat_tiles, 0), 0))], [xc]
    return pl.pallas_call(
        functools.partial(_ffn_kernel, idx=idx, ff=ff, ck=ck, lat_tiles=lat_tiles),
        grid=(n_rows // tm,),
        in_specs=x_specs + [pl.BlockSpec((1, N_MOD, d), _group_map(seq // tm, n_batch)),
                            pl.BlockSpec((1, d), lambda i: (0, 0)),
                            pl.BlockSpec((d, 2 * ff), lambda i: (0, 0), **const),
                            pl.BlockSpec((ff, d), lambda i: (0, 0), **const)] + tail_specs,
        out_specs=pl.BlockSpec((tm, d), lambda i: (i, 0)),
        out_shape=jax.ShapeDtypeStruct((n_rows, d), F32),
        compiler_params=_cparams(1),
        name="half_ffn",
    )(*xs, mod, gain.reshape(1, d), w13.astype(BF16), w2.astype(BF16), *tail)


def _inproj_kernel(x_ref, mod_ref, g_ref, w_ref, o_ref):
    m = mod_ref[0]
    h = (_rms(x_ref[...], g_ref[...]) * (1.0 + m[4:5]) + m[3:4]).astype(BF16)
    o_ref[...] = _dot(h, w_ref[...])


def _pack_w_in(w_in):
    d = w_in.shape[0]
    hk = HG_HEADS * HG_D
    sizes = (hk, hk, hk, hk, hk, 3 * HY_W, 256, 128, MLA_ROPE, d, d, d)
    offs = np.cumsum((0,) + sizes)
    q, zf, zb, iv, g, hy, qa, kva, kr, ga, gb, gc = (w_in[:, offs[i]:offs[i + 1]] for i in range(12))
    return jnp.concatenate([ga, gb, gc, q, zf, zb, iv, g, hy, qa, kva, kr, _rope_swap(kr)], axis=1)


def _rope_swap(a):
    q = MLA_ROPE // 4
    return jnp.concatenate([a[..., q:2 * q], a[..., :q], a[..., 3 * q:], a[..., 2 * q:3 * q]], axis=-1)


def _in_projection(xa, mod, gain, w_packed, seq, n_batch, tm=512, tn=2560):
    n_rows, d = xa.shape
    n = w_packed.shape[1]
    return pl.pallas_call(
        _inproj_kernel,
        grid=(n // tn, n_rows // tm),
        in_specs=[pl.BlockSpec((tm, d), lambda j, i: (i, 0)),
                  pl.BlockSpec((1, N_MOD, d), lambda j, i: (jnp.minimum(i // (seq // tm), n_batch), 0, 0)),
                  pl.BlockSpec((1, d), lambda j, i: (0, 0)),
                  pl.BlockSpec((d, tn), lambda j, i: (0, j))],
        out_specs=pl.BlockSpec((tm, tn), lambda j, i: (i, j)),
        out_shape=jax.ShapeDtypeStruct((n_rows, n), F32),
        compiler_params=_cparams(2),
        name="in_projection",
    )(xa, mod, gain.reshape(1, d), w_packed)


def _hgrn2_tables(c, rev):
    t = np.arange(c)[:, None]
    u = np.arange(c)[None, :]
    mats = [(u <= t), (u > t)]
    masks = [(t == u)]
    h = c // 2
    while h >= 1:
        mid = (t // (2 * h)) * (2 * h) + h
        mats.append(np.where(t >= mid, (u >= mid) & (u <= t), (u >= t + 1) & (u <= mid - 1)))
        mid_s = (u // (2 * h)) * (2 * h) + h
        masks.append((t // (2 * h) == u // (2 * h)) & (u < mid_s) & (t >= mid))
        h //= 2
    mats = np.stack([m.astype(np.float32) for m in mats])
    masks = np.stack([m.astype(np.float32) for m in masks])
    if rev:
        mats = mats[:, ::-1, ::-1]
        masks = masks[:, ::-1, ::-1]
    mats = np.ascontiguousarray(mats).reshape(-1, c)
    return (jnp.asarray(np.concatenate([mats, mats], axis=1), BF16),
            jnp.asarray(np.ascontiguousarray(masks), F32))


HG_PAIR = 2


def _hgrn2_chunk(q, z, v, lg, dst_ref, msk_ref, st_ref, o_ref, ci, *, layer, rev, n_levels):
    c = q.shape[0]
    hk = HG_HEADS * HG_D
    e = jnp.exp(lg - jnp.max(lg, axis=0, keepdims=True))
    sm = e / jnp.sum(e, axis=0, keepdims=True)
    lb = jnp.zeros((1, hk), F32)
    for i in range(1, layer + 1):
        lb = lb + sm[i:i + 1]

    f = lb + (1.0 - lb) * _sigmoid(z)
    kk = 1.0 - f
    g = jnp.log(f)
    g_hi = g.astype(BF16)
    g_lo = (g - g_hi.astype(F32)).astype(BF16)
    dg = _dot(dst_ref[...], jnp.concatenate([g_hi, g_lo], axis=0))
    qs = _silu(q) * HG_D ** -0.5
    last = 0 if rev else c - 1

    for h in range(HG_HEADS):
        hs = slice(h * HG_D, (h + 1) * HG_D)
        qh, kh, vh = qs[:, hs], kk[:, hs], v[:, hs].astype(BF16)
        b_in = dg[0:c, hs]
        e_out = jnp.exp(dg[c:2 * c, hs])
        a = _dot_nt(qh.astype(BF16), kh.astype(BF16)) * msk_ref[0]
        for l in range(1, n_levels + 1):
            el = jnp.exp(dg[(1 + l) * c:(2 + l) * c, hs])
            a = a + _dot_nt((qh * el).astype(BF16), (kh * el).astype(BF16)) * msk_ref[l]
        st = st_ref[ci, h]
        o = _dot_nt((qh * jnp.exp(b_in)).astype(BF16), st.astype(BF16)) + _dot(a.astype(BF16), vh)
        o_ref[:, hs] = o
        st_ref[ci, h] = jnp.exp(b_in[last:last + 1, :]) * st + _dot_tn(vh, (kh * e_out).astype(BF16))


def _hgrn2_kernel(*refs, layer, n_levels):
    n_in = 6 * HG_PAIR
    ins, (lg_ref, dstf_ref, mskf_ref, dstb_ref, mskb_ref, s0_ref) = refs[:n_in], refs[n_in:n_in + 6]
    of_ref, ob_ref, sf_ref, st_ref = refs[n_in + 6:]
    j = pl.program_id(1)

    @pl.when(j == 0)
    def _():
        st_ref[...] = s0_ref[...].reshape(st_ref.shape)

    for e in range(HG_PAIR):
        qf, zf, vf, qb, zb, vb = (r[...] for r in ins[6 * e:6 * e + 6])
        _hgrn2_chunk(qf, zf, vf, lg_ref[0], dstf_ref, mskf_ref, st_ref, of_ref.at[e], e,
                     layer=layer, rev=False, n_levels=n_levels)
        _hgrn2_chunk(qb, zb, vb, lg_ref[1], dstb_ref, mskb_ref, st_ref, ob_ref.at[e], HG_PAIR + e,
                     layer=layer, rev=True, n_levels=n_levels)

    @pl.when(j == pl.num_programs(1) - 1)
    def _():
        sf_ref[...] = st_ref[...].reshape(sf_ref.shape)


def _hgrn2_scan(p, logits, s0, layer, row0, seq, n_batch):
    c = HG_CHUNK
    hk = HG_HEADS * HG_D
    nc = seq // c
    base = row0 // c
    n_levels = int(math.log2(c))
    tabs = _hgrn2_tables(c, False) + _hgrn2_tables(c, True)
    assert n_batch % HG_PAIR == 0

    def prow(w, e, rev):
        return pl.BlockSpec((c, hk), lambda bp, j: (base + (bp * HG_PAIR + e) * nc + (nc - 1 - j if rev else j),
                                                   w // hk))

    in_specs = []
    for e in range(HG_PAIR):
        in_specs += [prow(COL_Q, e, False), prow(COL_ZF, e, False), prow(COL_IV, e, False),
                     prow(COL_Q, e, True), prow(COL_ZB, e, True), prow(COL_IV, e, True)]
    full = lambda a: pl.BlockSpec(a.shape, lambda bp, j: (0,) * a.ndim)
    st_spec = pl.BlockSpec((2, HG_PAIR, HG_HEADS, HG_D, HG_D), lambda bp, j: (0, bp, 0, 0, 0))
    return pl.pallas_call(
        functools.partial(_hgrn2_kernel, layer=layer, n_levels=n_levels),
        grid=(n_batch // HG_PAIR, nc),
        in_specs=in_specs + [full(logits)] + [full(t) for t in tabs] + [st_spec],
        out_specs=[pl.BlockSpec((HG_PAIR, c, hk), lambda bp, j: (bp, j, 0)),
                   pl.BlockSpec((HG_PAIR, c, hk), lambda bp, j: (bp, nc - 1 - j, 0)),
                   st_spec],
        out_shape=[jax.ShapeDtypeStruct((n_batch, seq, hk), F32),
                   jax.ShapeDtypeStruct((n_batch, seq, hk), F32),
                   jax.ShapeDtypeStruct((2, n_batch, HG_HEADS, HG_D, HG_D), F32)],
        scratch_shapes=[pltpu.VMEM((2 * HG_PAIR, HG_HEADS, HG_D, HG_D), F32)],
        compiler_params=_cparams(2),
        name="hgrn2_scan",
    )(*([p] * (6 * HG_PAIR)), logits, *tabs, s0)


def _shortconv_kernel(u_ref, w_ref, b_ref, o_ref):
    u = u_ref[...]
    n = u.shape[0]
    row = lax.broadcasted_iota(jnp.int32, u.shape, 0)
    prev = jnp.where(row == 0, 0.0, pltpu.roll(u, 1, 0))
    nxt = jnp.where(row == n - 1, 0.0, pltpu.roll(u, n - 1, 0))
    w = w_ref[...]
    o_ref[0, 0, 0] = prev * w[0:1] + u * w[1:2] + nxt * w[2:3] + b_ref[...]


def _shortconv_t1major_kernel(u_ref, w_ref, b_ref, o_ref, *, n1):
    half = u_ref.shape[0] // n1
    w = w_ref[...]
    b = b_ref[...]
    row = lax.broadcasted_iota(jnp.int32, (half, LANE), 0)
    col = lambda t1: u_ref[pl.ds(t1, half, stride=n1), :]
    before = jnp.where(row == 0, 0.0, pltpu.roll(col(n1 - 1), 1, 0))
    after = jnp.where(row == half - 1, 0.0, pltpu.roll(col(0), half - 1, 0))

    def body(t1, carry):
        prev, cur = carry
        nxt = jnp.where(t1 == n1 - 1, after, col(jnp.minimum(t1 + 1, n1 - 1)))
        o_ref[0, 0, 0, pl.ds(pl.multiple_of(t1 * half, 8), half), :] = prev * w[0:1] + cur * w[1:2] + nxt * w[2:3] + b
        return cur, nxt

    lax.fori_loop(0, n1, body, (before, col(0)), unroll=4)


def _short_conv(p, w, b, row0, seq, n_batch, n1):
    nb = 3 * HY_W // LANE
    per = HY_W // LANE
    body = _shortconv_kernel if n1 == 1 else functools.partial(_shortconv_t1major_kernel, n1=n1)
    return pl.pallas_call(
        body,
        grid=(n_batch, nb),
        in_specs=[pl.BlockSpec((seq, LANE), lambda bi, j: (row0 // seq + bi, COL_HY // LANE + j)),
                  pl.BlockSpec((3, LANE), lambda bi, j: (0, j)),
                  pl.BlockSpec((1, LANE), lambda bi, j: (0, j))],
        out_specs=pl.BlockSpec((1, 1, 1, seq, LANE), lambda bi, j: (j // per, bi, j % per, 0, 0)),
        out_shape=jax.ShapeDtypeStruct((3, n_batch, per, seq, LANE), F32),
        compiler_params=_cparams(2),
        name="hyena_short_conv",
    )(p, w, b.reshape(1, -1))


def _hy_filter_kernel(emb_ref, embr_ref, w1_ref, b1_ref, w2_ref, b2_ref, w3_ref, b3_ref, w4_ref, fr_ref,
                      dl_ref, o_ref, nrm_ref, *, seq, group):
    i = pl.program_id(0)
    hp = lax.Precision.HIGHEST
    fr = fr_ref[...]

    def mlp(emb):
        hid = jnp.sin(fr * (jnp.dot(emb, w1_ref[...], precision=hp) + b1_ref[...]))
        hid = jnp.sin(fr * (jnp.dot(hid, w2_ref[...], precision=hp) + b2_ref[...]))
        return jnp.sin(fr * (jnp.dot(hid, w3_ref[...], precision=hp) + b3_ref[...]))

    tl = emb_ref.shape[0]
    n = w4_ref.shape[1]
    half = n // 2
    r = lax.broadcasted_iota(jnp.int32, (tl, half), 0) + i * tl
    per = seq // group
    pos = (r >> (per.bit_length() - 1)) + group * (r & (per - 1))
    posr = (pos & ~(group - 1)) + ((group - (pos & (group - 1))) & (group - 1))
    for side, (emb, pp) in enumerate(((emb_ref[...], pos), (embr_ref[...], posr))):
        h = jnp.dot(mlp(emb), w4_ref[:, side * half:(side + 1) * half], precision=hp)
        t = pp.astype(F32) * (1.0 / (seq - 1))
        h = h * (jnp.exp(-t * dl_ref[...]) + HY_SHIFT)
        if side == 1:
            h = jnp.where(pp == 0, 0.0, h)
        for k in range(half // LANE):
            o_ref[side * (half // LANE) + k] = h[:, k * LANE:(k + 1) * LANE]

        @pl.when(i == 0)
        def _():
            nrm_ref[:, side * half:(side + 1) * half] = jnp.zeros((1, half), F32)

        nrm_ref[:, side * half:(side + 1) * half] += jnp.sum(jnp.abs(h), axis=0, keepdims=True)


def _hyena_filter_taps(seq, w1, b1, w2, b2, w3, b3, w4, freq, group):
    fh = w1.shape[1]
    n_emb = w1.shape[0]
    bands_n = (n_emb - 1) // 2
    tt = np.linspace(0.0, 1.0, seq, dtype=np.float32)[:, None].astype(np.float64)
    ww = (2.0 * math.pi / seq) * np.arange(seq, dtype=np.float64)[:, None]
    bands = np.linspace(1e-4, bands_n - 1, bands_n, dtype=np.float32)[None, :].astype(np.float64)
    emb = np.concatenate([tt, np.cos(bands * ww), -np.sin(bands * ww)], axis=-1)
    emb = np.pad(emb, ((0, 0), (0, LANE - n_emb))).astype(np.float32)
    assert seq % group == 0 and group & (group - 1) == 0 and (seq // group) & (seq // group - 1) == 0
    r = np.arange(seq)
    pos = r // (seq // group) + group * (r % (seq // group))
    embr = emb[(pos // group) * group + (group - pos % group) % group]
    emb = emb[pos]
    w1p = jnp.pad(w1, ((0, LANE - n_emb), (0, 0)))
    deltas = np.abs(np.linspace(HY_MIN_DECAY, HY_MAX_DECAY, HY_W, dtype=np.float32))
    deltas = np.tile(deltas, 2)[None, :]
    n = w4.shape[1]
    tl = min(seq, 512)
    full = lambda a: pl.BlockSpec(a.shape, lambda i: (0,) * a.ndim)
    args = (w1p, b1.reshape(1, fh), w2, b2.reshape(1, fh), w3, b3.reshape(1, fh), w4, freq.reshape(1, fh),
            jnp.asarray(deltas))
    return pl.pallas_call(
        functools.partial(_hy_filter_kernel, seq=seq, group=group),
        grid=(seq // tl,),
        in_specs=[pl.BlockSpec((tl, LANE), lambda i: (i, 0))] * 2 + [full(a) for a in args],
        out_specs=[pl.BlockSpec((n // LANE, tl, LANE), lambda i: (0, i, 0)),
                   pl.BlockSpec((1, n), lambda i: (0, 0))],
        out_shape=[jax.ShapeDtypeStruct((n // LANE, seq, LANE), F32), jax.ShapeDtypeStruct((1, n), F32)],
        compiler_params=_cparams(1),
        name="hyena_filter_taps",
    )(jnp.asarray(emb), jnp.asarray(embr), *args)


def _fft_split(seq):
    n = 2 * seq
    n1 = FFT_N1 if n > 1024 else 1
    return n, n1, n // n1


def _cis(idx, n):
    ph = 2.0 * np.pi * (idx % n) / n
    return np.cos(ph), -np.sin(ph)


def _dft_tables_short(seq):
    n = 2 * seq
    cr, ci = _cis(np.arange(n)[:, None] * np.arange(seq)[None, :], n)
    w_fwd = np.concatenate([cr, ci], axis=0)
    w_inv = np.concatenate([cr.T, ci.T], axis=1) / n
    return dict(w_fwd=jnp.asarray(w_fwd, BF16), w_inv=jnp.asarray(w_inv, BF16))


def _dft_tables_long(seq):
    n, n1, n2 = _fft_split(seq)
    t1 = np.arange(n1)[:, None, None]
    f2 = np.arange(n2)[None, :, None]
    t2 = np.arange(n2 // 2)[None, None, :]
    cr, ci = _cis(f2 * (t1 + n1 * t2), n)
    w1 = np.concatenate([cr, ci], axis=1)
    w4 = np.concatenate([np.swapaxes(cr, 1, 2), np.swapaxes(ci, 1, 2)], axis=2) / n
    j = (n1 - t1) % n1 + n1 * t2
    br, bi = _cis(f2 * (n - j), n)
    w1f = np.concatenate([np.concatenate([cr, br], axis=2), np.concatenate([ci, bi], axis=2)], axis=1)
    gr, gi = _cis(np.arange(n1)[:, None] * np.arange(n1)[None, :], n1)
    wd = np.concatenate([np.concatenate([gr, -gi], axis=1), np.concatenate([gi, gr], axis=1)], axis=0)
    return dict(w1=jnp.asarray(w1, BF16), w4=jnp.asarray(w4, BF16), w1f=jnp.asarray(w1f, BF16),
                wd=jnp.asarray(wd, BF16), wdi=jnp.asarray(wd.T, BF16))


def _dft_rows_kernel(w_ref, x_ref, o_ref):
    o_ref[0] = _dot(w_ref[...], x_ref[0].astype(BF16)).astype(o_ref.dtype)


def _dft_rows(w, x, tn):
    nb, k, cols = x.shape
    m = w.shape[0]
    return pl.pallas_call(
        _dft_rows_kernel,
        grid=(nb, cols // tn),
        in_specs=[pl.BlockSpec((m, k), lambda b, j: (0, 0)),
                  pl.BlockSpec((1, k, tn), lambda b, j: (b, 0, j))],
        out_specs=pl.BlockSpec((1, m, tn), lambda b, j: (b, 0, j)),
        out_shape=jax.ShapeDtypeStruct((nb, m, cols), BF16),
        compiler_params=_cparams(2),
        name="hyena_dft_rows",
    )(w, x)


def _idft_gate_kernel(w_ref, b_ref, xg_ref, z_ref, sk_ref, o_ref):
    y = _dot(w_ref[...], b_ref[0])
    z = z_ref[0]
    o_ref[0] = xg_ref[0] * (y + z * sk_ref[...])


def _idft_gate(w, bc, xg, z, skip_t, tn):
    nb, k, cols = bc.shape
    m = w.shape[0]
    return pl.pallas_call(
        _idft_gate_kernel,
        grid=(nb, cols // tn),
        in_specs=[pl.BlockSpec((m, k), lambda b, j: (0, 0)),
                  pl.BlockSpec((1, k, tn), lambda b, j: (b, 0, j)),
                  pl.BlockSpec((1, m, tn), lambda b, j: (b, 0, j)),
                  pl.BlockSpec((1, m, tn), lambda b, j: (b, 0, j)),
                  pl.BlockSpec((1, tn), lambda b, j: (0, j))],
        out_specs=pl.BlockSpec((1, m, tn), lambda b, j: (b, 0, j)),
        out_shape=jax.ShapeDtypeStruct((nb, m, cols), F32),
        compiler_params=_cparams(2),
        name="hyena_idft_gate",
    )(w, bc, xg, z, skip_t)


def _spec_combine_kernel(af_ref, ab_ref, nf_ref, nb_ref, o_ref):
    inv = 1.0 / (nf_ref[...] + nb_ref[...])
    n = o_ref.shape[1]
    o_ref[0] = (af_ref[0, :n].astype(F32) + ab_ref[0, :n].astype(F32)) * inv
    o_ref[1] = (af_ref[0, n:].astype(F32) - ab_ref[0, n:].astype(F32)) * inv


def _spec_mul_kernel(a_ref, k_ref, o_ref):
    n = k_ref.shape[1]
    xr, xi = a_ref[0, :n].astype(F32), a_ref[0, n:].astype(F32)
    kr, ki = k_ref[0], k_ref[1]
    o_ref[0, :n] = (xr * kr - xi * ki).astype(o_ref.dtype)
    o_ref[0, n:] = (xr * ki + xi * kr).astype(o_ref.dtype)


def _hyena_short(seq, taps, nrm, uc, skip):
    n = 2 * seq
    uc = jnp.swapaxes(uc, 2, 3).reshape(3, uc.shape[1], seq, HY_W)
    tabs = _dft_tables_short(seq)
    cf = taps.shape[1]
    c = HY_W
    a = _dft_rows(tabs["w_fwd"], taps.reshape(1, seq, cf), tn=cf)
    nblk = cf // 2 // c
    spec = pl.pallas_call(
        _spec_combine_kernel,
        grid=(nblk,),
        in_specs=[pl.BlockSpec((1, 2 * n, c), lambda j: (0, 0, j)),
                  pl.BlockSpec((1, 2 * n, c), lambda j: (0, 0, j + nblk)),
                  pl.BlockSpec((1, c), lambda j: (0, j)),
                  pl.BlockSpec((1, c), lambda j: (0, j + nblk))],
        out_specs=pl.BlockSpec((2, n, c), lambda j: (0, 0, j)),
        out_shape=jax.ShapeDtypeStruct((2, n, cf // 2), F32),
        compiler_params=_cparams(1),
        name="hyena_spec_combine",
    )(a, a, nrm, nrm)
    nb = uc.shape[1]
    z = uc[2]
    for order in range(2):
        a = _dft_rows(tabs["w_fwd"], z, tn=c)
        bc = pl.pallas_call(
            _spec_mul_kernel,
            grid=(nb,),
            in_specs=[pl.BlockSpec((1, 2 * n, c), lambda b: (b, 0, 0)),
                      pl.BlockSpec((2, n, c), lambda b: (0, 0, order))],
            out_specs=pl.BlockSpec((1, 2 * n, c), lambda b: (b, 0, 0)),
            out_shape=jax.ShapeDtypeStruct((nb, 2 * n, c), BF16),
            compiler_params=_cparams(1),
            name="hyena_spec_mul",
        )(a, spec)
        z = _idft_gate(tabs["w_inv"], bc, uc[order], z, skip[order].reshape(1, c), tn=c)
    return z


HY_SLABS = 2


def _slab_pitch(n2):
    return n2 + 8


def _pack_c(re, im):
    hi = lax.bitcast_convert_type(re.astype(BF16).astype(F32), jnp.uint32)
    lo = lax.bitcast_convert_type(im.astype(BF16).astype(F32), jnp.uint32)
    return hi | (lo >> 16)


def _unpack_c(w):
    re = lax.bitcast_convert_type(w & jnp.uint32(0xFFFF0000), F32)
    im = lax.bitcast_convert_type(w << 16, F32)
    return re.astype(BF16), im.astype(BF16)


def _store_slab(a_scr, row, words):
    for s in range(HY_SLABS):
        a_scr[s, pl.ds(row, words.shape[0]), :] = words[:, s * LANE:(s + 1) * LANE]


def _stage1(a_scr, w_ref, xs_of, i, tb, n2, pitch):
    for j in range(tb):
        a = _dot(w_ref[j], xs_of(j))
        _store_slab(a_scr, pl.multiple_of((i * tb + j) * pitch, 8), _pack_c(a[:n2], a[n2:]))


def _stage2(a_scr, wd_ref, f2, n1, pitch):
    w = jnp.concatenate([a_scr[s, pl.ds(f2, n1, stride=pitch), :] for s in range(HY_SLABS)], axis=1)
    re, im = _unpack_c(w)
    return _dot(wd_ref[...], jnp.concatenate([re, im], axis=0))


def _time_col(ref, j, half):
    lead = (0,) * (len(ref.shape) - 3)
    return jnp.concatenate([ref[lead + (s, slice(j * half, (j + 1) * half), slice(None))]
                            for s in range(HY_SLABS)], axis=1)


def _hy_spectrum_kernel(hf_ref, hb_ref, nf_ref, nb_ref, w1f_ref, wd_ref, o_ref, a_scr, *, n1, n2, tb, fb):
    i = pl.program_id(1)
    pitch = _slab_pitch(n2)
    half = n2 // 2
    nt = n1 // tb

    @pl.when(i < nt)
    def _():
        def xs_of(j):
            return jnp.concatenate([_time_col(hf_ref, j, half), _time_col(hb_ref, j, half)], axis=0).astype(BF16)
        _stage1(a_scr, w1f_ref, xs_of, i, tb, n2, pitch)

    @pl.when(i >= nt)
    def _():
        inv = 1.0 / (nf_ref[...] + nb_ref[...])
        for jj in range(fb):
            x = _stage2(a_scr, wd_ref, (i - nt) * fb + jj, n1, pitch)
            o_ref[0, jj] = x[:n1] * inv
            o_ref[1, jj] = x[n1:] * inv


def _hy_fftconv_kernel(z_ref, xg_ref, k_ref, sk_ref, w1_ref, wd_ref, wdi_ref, w4_ref, o_ref, a_scr,
                       *, n1, n2, tb, fb):
    i = pl.program_id(2)
    pitch = _slab_pitch(n2)
    half = n2 // 2
    nt, nf = n1 // tb, n2 // fb

    @pl.when(i < nt)
    def _():
        _stage1(a_scr, w1_ref, lambda j: _time_col(z_ref, j, half).astype(BF16), i, tb, n2, pitch)

    @pl.when((i >= nt) & (i < nt + nf))
    def _():
        for jj in range(fb):
            f2 = (i - nt) * fb + jj
            x = _stage2(a_scr, wd_ref, f2, n1, pitch)
            xr, xi = x[:n1], x[n1:]
            kr, ki = k_ref[0, jj], k_ref[1, jj]
            y = jnp.concatenate([xr * kr - xi * ki, xr * ki + xi * kr], axis=0).astype(BF16)
            bv = _dot(wdi_ref[...], y)
            words = _pack_c(bv[:n1], bv[n1:])
            for s in range(HY_SLABS):
                a_scr[s, pl.ds(f2, n1, stride=pitch), :] = words[:, s * LANE:(s + 1) * LANE]

    @pl.when(i >= nt + nf)
    def _():
        sk = sk_ref[...]
        for j in range(tb):
            row = pl.multiple_of(((i - nt - nf) * tb + j) * pitch, 8)
            w = jnp.concatenate([a_scr[s, pl.ds(row, n2), :] for s in range(HY_SLABS)], axis=1)
            re, im = _unpack_c(w)
            y = _dot(w4_ref[j], jnp.concatenate([re, im], axis=0))
            out = _time_col(xg_ref, j, half) * (y + _time_col(z_ref, j, half) * sk)
            for s in range(HY_SLABS):
                o_ref[0, s, j * half:(j + 1) * half, :] = out[:, s * LANE:(s + 1) * LANE]


def _hyena_long(seq, taps, nrm, uc, skip, tb=32, fb=16):
    n, n1, n2 = _fft_split(seq)
    tabs = _dft_tables_long(seq)
    pitch = _slab_pitch(n2)
    half = n2 // 2
    cb = HY_SLABS * LANE
    tb, fb = min(tb, n1), min(fb, n2)
    nt, nf = n1 // tb, n2 // fb
    cf = taps.shape[0] * LANE
    ngrp = cf // 2 // cb
    scratch = [pltpu.VMEM((HY_SLABS, n1 * pitch, LANE), jnp.uint32)]
    tcol = lambda i: jnp.minimum(i, nt - 1)
    spec = pl.pallas_call(
        functools.partial(_hy_spectrum_kernel, n1=n1, n2=n2, tb=tb, fb=fb),
        grid=(ngrp, nt + nf),
        in_specs=[pl.BlockSpec((HY_SLABS, tb * half, LANE), lambda g, i: (g, tcol(i), 0)),
                  pl.BlockSpec((HY_SLABS, tb * half, LANE), lambda g, i: (ngrp + g, tcol(i), 0)),
                  pl.BlockSpec((1, cb), lambda g, i: (0, g)),
                  pl.BlockSpec((1, cb), lambda g, i: (0, ngrp + g)),
                  pl.BlockSpec((tb, 2 * n2, n2), lambda g, i: (tcol(i), 0, 0)),
                  pl.BlockSpec(tabs["wd"].shape, lambda g, i: (0, 0))],
        out_specs=pl.BlockSpec((2, fb, n1, cb), lambda g, i: (0, jnp.maximum(i - nt, 0), 0, g)),
        out_shape=jax.ShapeDtypeStruct((2, n2, n1, cf // 2), F32),
        scratch_shapes=scratch,
        compiler_params=_cparams(2),
        name="hyena_filter_spectrum",
    )(taps, taps, nrm, nrm, tabs["w1f"], tabs["wd"])
    nb, nblk = uc.shape[1], uc.shape[2]
    ngc = nblk // HY_SLABS
    z, zsel = uc, 2

    def tblk(b, g, i):
        return (b, g, jnp.where(i < nt, i, jnp.maximum(i - nt - nf, 0)), 0)

    def tblk_late(b, g, i):
        return (b, g, jnp.maximum(i - nt - nf, 0), 0)

    for order in range(2):
        stacked = lambda sel, f: (lambda b, g, i: (sel,) + f(b, g, i))
        z = pl.pallas_call(
            functools.partial(_hy_fftconv_kernel, n1=n1, n2=n2, tb=tb, fb=fb),
            grid=(nb, ngc, 2 * nt + nf),
            in_specs=[pl.BlockSpec((1, 1, HY_SLABS, tb * half, LANE), stacked(zsel, tblk)),
                      pl.BlockSpec((1, 1, HY_SLABS, tb * half, LANE), stacked(order, tblk_late)),
                      pl.BlockSpec((2, fb, n1, cb),
                                   lambda b, g, i: (0, jnp.clip(i - nt, 0, nf - 1), 0, order * ngc + g)),
                      pl.BlockSpec((1, cb), lambda b, g, i: (0, g)),
                      pl.BlockSpec((tb, 2 * n2, half), lambda b, g, i: (tcol(i), 0, 0)),
                      pl.BlockSpec(tabs["wd"].shape, lambda b, g, i: (0, 0)),
                      pl.BlockSpec(tabs["wdi"].shape, lambda b, g, i: (0, 0)),
                      pl.BlockSpec((tb, half, 2 * n2), lambda b, g, i: (jnp.maximum(i - nt - nf, 0), 0, 0))],
            out_specs=pl.BlockSpec((1, HY_SLABS, tb * half, LANE), tblk_late),
            out_shape=jax.ShapeDtypeStruct((nb, nblk, seq, LANE), F32),
            scratch_shapes=scratch,
            compiler_params=_cparams(3),
            name="hyena_fft_conv",
        )(z, uc, spec, skip[order].reshape(1, HY_W), tabs["w1"], tabs["wd"], tabs["wdi"], tabs["w4"])
        z, zsel = z.reshape((1,) + z.shape), 0
    return z[0]


def _hyena(p, lw, row0, seq, n_batch):
    long = _fft_split(seq)[1] > 1
    n1 = FFT_N1 if long else 1
    taps, nrm = _hyena_filter_taps(seq, lw["hy_w1"], lw["hy_b1"], lw["hy_w2"], lw["hy_b2"], lw["hy_w3"],
                                   lw["hy_b3"], lw["hy_w4"], lw["hy_freq"], n1)
    uc = _short_conv(p, lw["hy_conv_w"], lw["hy_conv_b"], row0, seq, n_batch, n1)
    if long:
        return _hyena_long(seq, taps, nrm, uc, lw["hy_skip"])
    taps = jnp.swapaxes(taps, 0, 1).reshape(seq, -1)
    return _hyena_short(seq, taps, nrm, uc, lw["hy_skip"]).reshape(n_batch * seq, HY_W)


def _mla_qkv_kernel(qa_ref, kva_ref, kr_ref, cs_ref, wq_ref, wkv_ref, gqa_ref, gkva_ref, gqn_ref, gqr_ref,
                    gkn_ref, gkr_ref, q_ref, k_ref, v_ref):
    cs = cs_ref[...]
    lane = lax.broadcasted_iota(jnp.int32, cs.shape, 1)
    low = lane < MLA_ROPE

    def rope(pair, gain2):
        ms = jnp.sum(jnp.where(low, pair * pair, 0.0), axis=-1, keepdims=True) * (1.0 / MLA_ROPE)
        t = pair * lax.rsqrt(ms + RMS_EPS) * gain2 * cs
        return jnp.where(low, t + pltpu.roll(t, MLA_ROPE, 1), 0.0)

    qq = _dot(_rms(qa_ref[...], gqa_ref[...]).astype(BF16), wq_ref[...])
    kv = _dot(_rms(kva_ref[...], gkva_ref[...]).astype(BF16), wkv_ref[...])
    kr = rope(kr_ref[...], gkr_ref[...])
    hp = MLA_HEAD_PAD
    tm = kr.shape[0]
    qscale = MLA_SCALE * LOG2_E
    ones_row = (lax.broadcasted_iota(jnp.int32, (V_ROWS - MLA_V, tm), 0) == 0).astype(BF16)
    for h in range(MLA_HEADS):
        qn = _rms(qq[:, h * hp:h * hp + MLA_NOPE], gqn_ref[...])
        qr = rope(qq[:, h * hp + MLA_NOPE:(h + 1) * hp], gqr_ref[...])
        q_ref[h, 0:LANE, :] = (qn * qscale).T.astype(BF16)
        q_ref[h, LANE:2 * LANE, :] = (qr * qscale).T.astype(BF16)
        kn = _rms(kv[:, h * hp:h * hp + MLA_NOPE], gkn_ref[...])
        k_ref[h, :, 0:LANE] = kn.astype(BF16)
        k_ref[h, :, LANE:2 * LANE] = kr.astype(BF16)
        v_ref[h, 0:MLA_V, :] = kv[:, h * hp + MLA_NOPE:(h + 1) * hp].T.astype(BF16)
        v_ref[h, MLA_V:V_ROWS, :] = ones_row


def _rope_table(seq, n_batch, ctx_rows):
    rows = seq // GRID_W
    row = np.repeat(np.arange(rows, dtype=np.float32), GRID_W)
    col = np.tile(np.arange(GRID_W, dtype=np.float32), rows)
    half = MLA_ROPE // 2
    inv = (ROPE_THETA ** (-np.arange(0, half, 2, dtype=np.float32) / half)).astype(np.float32)
    ar = (row[:, None] * inv).astype(np.float64)
    ac = (col[:, None] * inv).astype(np.float64)
    cos = np.concatenate([np.cos(ar), np.cos(ar), np.cos(ac), np.cos(ac)], axis=1)
    sin = np.concatenate([-np.sin(ar), np.sin(ar), -np.sin(ac), np.sin(ac)], axis=1)
    lat = np.tile(np.concatenate([cos, sin], axis=1), (n_batch, 1))
    ctx = np.concatenate([np.ones((ctx_rows, MLA_ROPE)), np.zeros((ctx_rows, MLA_ROPE))], axis=1)
    return jnp.asarray(np.concatenate([lat, ctx], axis=0), F32)


def _pair_gain(g):
    return jnp.concatenate([g, _rope_swap(g)]).reshape(1, 2 * MLA_ROPE)


def _mla_qkv(p, cs, lw, tm=512):
    n_rows = p.shape[0]
    hd, hp = MLA_HEADS, MLA_HEAD_PAD
    w_uq = lw["w_uq"].reshape(-1, hd, MLA_QK)
    wq = jnp.concatenate([w_uq, _rope_swap(w_uq[..., MLA_NOPE:])], axis=-1).reshape(-1, hd * hp).astype(BF16)
    wkv = lw["w_ukv"].astype(BF16)
    ql, kvl = wq.shape[0], wkv.shape[0]
    vec = lambda a: a.reshape(1, -1)
    full = lambda a: pl.BlockSpec(a.shape, lambda i: (0,) * a.ndim)
    args = (wq, wkv, vec(lw["q_a_norm"]), vec(lw["kv_a_norm"]), vec(lw["q_nope_norm"]),
            _pair_gain(lw["q_rope_norm"]), vec(lw["k_nope_norm"]), _pair_gain(lw["k_rope_norm"]))
    return pl.pallas_call(
        _mla_qkv_kernel,
        grid=(n_rows // tm,),
        in_specs=[pl.BlockSpec((tm, ql), lambda i: (i, COL_QA // ql)),
                  pl.BlockSpec((tm, kvl), lambda i: (i, COL_KVA // kvl)),
                  pl.BlockSpec((tm, LANE), lambda i: (i, COL_KR // LANE)),
                  pl.BlockSpec((tm, LANE), lambda i: (i, 0))] + [full(a) for a in args],
        out_specs=[pl.BlockSpec((hd, hp, tm), lambda i: (0, 0, i)),
                   pl.BlockSpec((hd, tm, hp), lambda i: (0, i, 0)),
                   pl.BlockSpec((hd, V_ROWS, tm), lambda i: (0, 0, i))],
        out_shape=[jax.ShapeDtypeStruct((hd, hp, n_rows), BF16),
                   jax.ShapeDtypeStruct((hd, n_rows, hp), BF16),
                   jax.ShapeDtypeStruct((hd, V_ROWS, n_rows), BF16)],
        compiler_params=_cparams(1),
        name="mla_qkv",
    )(p, p, p, cs, *args)


def _key_chunks(k_refs, v_refs, tk):
    chunks, s0 = [], 0
    for kr, vr in zip(k_refs, v_refs):
        n = kr.shape[1]
        step = min(tk, n)
        chunks += [(kr, vr, r0, step, s0 + r0) for r0 in range(0, n, step)]
        s0 += n
    return chunks


def _score_pass(qt, chunks, s_scr):
    m = None
    for kr, _, r0, rn, s0 in chunks:
        s = _dot(kr[0, r0:r0 + rn, :], qt)
        s_scr[s0:s0 + rn, :] = s
        mj = jnp.max(s, axis=0, keepdims=True)
        m = mj if m is None else jnp.maximum(m, mj)
    return m


def _value_pass(chunks, s_scr, m, o_ref):
    acc = None
    for _, vr, r0, rn, s0 in chunks:
        p = jnp.exp2((s_scr[s0:s0 + rn, :] - m).astype(BF16))
        part = _dot(vr[0, :, r0:r0 + rn], p)
        acc = part if acc is None else acc + part
    o_ref[...] = (acc[:MLA_V] / acc[MLA_V:MLA_V + 1]).astype(o_ref.dtype)


def _attn_ctx_kernel(q_ref, kc_ref, vc_ref, o_ref, s_scr, *, tk):
    chunks = _key_chunks([kc_ref], [vc_ref], tk)
    _value_pass(chunks, s_scr, _score_pass(q_ref[0], chunks, s_scr), o_ref)


def _attn_kernel(q_ref, k_ref, v_ref, kc_ref, vc_ref, o_ref, s_a, s_b, m_a, m_b, *, tk):
    i = pl.program_id(2)
    chunks = _key_chunks([kc_ref, k_ref], [vc_ref, v_ref], tk)

    @pl.when((pl.program_id(0) == 0) & (pl.program_id(1) == 0) & (i == 0))
    def _():
        s_b[...] = jnp.zeros_like(s_b)
        m_b[...] = jnp.zeros_like(m_b)

    def step(s_cur, m_cur, s_prev, m_prev):
        _value_pass(chunks, s_prev, m_prev[0:1, :], o_ref)
        m_cur[...] = jnp.broadcast_to(_score_pass(q_ref[0], chunks, s_cur), m_cur.shape)

    @pl.when(i % 2 == 0)
    def _():
        step(s_a, m_a, s_b, m_b)

    @pl.when(i % 2 == 1)
    def _():
        step(s_b, m_b, s_a, m_a)


def _attention(qt, k, vt, seq, ctx_len, n_batch, latent, tq=256, tk=1024):
    hd, hp, n_rows = qt.shape
    lat_rows = n_batch * seq
    cblk = lat_rows // ctx_len
    kc_spec = pl.BlockSpec((1, ctx_len, hp), lambda b, h, i: (h, cblk + b, 0))
    vc_spec = pl.BlockSpec((1, V_ROWS, ctx_len), lambda b, h, i: (h, 0, cblk + b))
    if latent:
        nq = seq // tq
        steps = nq + 1
        in_specs = [pl.BlockSpec((1, hp, tq), lambda b, h, i: (h, 0, b * nq + jnp.minimum(i, nq - 1))),
                    pl.BlockSpec((1, seq, hp), lambda b, h, i: (h, b, 0)),
                    pl.BlockSpec((1, V_ROWS, seq), lambda b, h, i: (h, 0, b)), kc_spec, vc_spec]
        args = (qt, k, vt, k, vt)
        out_spec = pl.BlockSpec((MLA_V, tq), lambda b, h, i: (h, b * nq + jnp.maximum(i - 1, 0)))
        out_cols, n_keys = lat_rows, seq + ctx_len
        body = functools.partial(_attn_kernel, tk=tk)
        scratch = [pltpu.VMEM((n_keys, tq), F32)] * 2 + [pltpu.VMEM((8, tq), F32)] * 2
    else:
        steps = 1
        tq = ctx_len
        in_specs = [pl.BlockSpec((1, hp, tq), lambda b, h, i: (h, 0, cblk + b)), kc_spec, vc_spec]
        args = (qt, k, vt)
        out_spec = pl.BlockSpec((MLA_V, tq), lambda b, h, i: (h, b))
        out_cols, n_keys = n_batch * ctx_len, ctx_len
        body = functools.partial(_attn_ctx_kernel, tk=tk)
        scratch = [pltpu.VMEM((n_keys, tq), F32)]
    return pl.pallas_call(
        body,
        grid=(n_batch, hd, steps),
        in_specs=in_specs,
        out_specs=out_spec,
        out_shape=jax.ShapeDtypeStruct((hd * MLA_V, out_cols), BF16),
        scratch_shapes=scratch,
        compiler_params=_cparams(3),
        name="mla_attention" if latent else "mla_attention_ctx",
    )(*args)


def _merge_kernel(*refs, lat_tiles):
    (x_ref, mod_ref, of_ref, ob_ref, g_ref, yb_ref, yc_ref, ga_ref, gb_ref, gc_ref,
     gain_ref, wa_ref, wb_ref, wc_ref, wo_ref) = refs[:15]
    ctx_refs = refs[15:-1]
    o_ref = refs[-1]
    is_ctx = pl.program_id(0) >= lat_tiles
    m = mod_ref[0]
    o = of_ref[...] + ob_ref[...]
    if ctx_refs:
        o = jnp.where(is_ctx, ctx_refs[0][...] + ctx_refs[1][...], o)
    gain = gain_ref[...]
    ya = jnp.concatenate([_rms(o[:, h * HG_D:(h + 1) * HG_D], gain) for h in range(HG_HEADS)], axis=1)
    ya = (ya * _silu(g_ref[...])).astype(BF16)
    nslab, tcols = yb_ref.shape[1], yb_ref.shape[3]
    qn = x_ref.shape[0] // yb_ref.shape[2]
    parts = [jnp.concatenate([jnp.concatenate([yb_ref[0, k, :, q, :] for k in range(nslab)], axis=1)
                              for q in range(h * qn, (h + 1) * qn)], axis=0) for h in range(tcols // qn)]
    sub = pl.program_id(0) % len(parts)
    yb = parts[0]
    for h in range(1, len(parts)):
        yb = jnp.where(sub == h, parts[h], yb)
    yc = yc_ref[...]
    if ctx_refs:
        yb = jnp.where(is_ctx, ctx_refs[2][...], yb)
        yc = jnp.where(is_ctx, ctx_refs[3][...], yc)
    mix = (_sigmoid(ga_ref[...]) * _dot(ya, wa_ref[...])
           + _sigmoid(gb_ref[...]) * _dot(yb.astype(BF16), wb_ref[...])
           + _sigmoid(gc_ref[...]) * _dot_tn(yc, wc_ref[...]))
    o_ref[...] = x_ref[...] + m[5:6] * _dot(mix.astype(BF16), wo_ref[...])


def _merge(xa, mod, o_f, o_b, p, y_b, y_c, ctx_parts, lw, n_rows, seq, n_batch, tm=512):
    d = xa.shape[1]
    hk = HG_HEADS * HG_D
    tpb = seq // tm
    lat_tiles = n_batch * tpb
    n1 = FFT_N1
    tcols = 8
    assert tm % n1 == 0 and tcols % (tm // n1) == 0 and seq % (tcols * n1) == 0
    per_blk = tcols // (tm // n1)
    y_b = y_b.reshape(n_batch, y_b.shape[1], n1, seq // n1, LANE)
    row = lambda w: pl.BlockSpec((tm, w), lambda i: (i, 0))
    pcol = lambda w, c: pl.BlockSpec((tm, w), lambda i: (i, c // w))
    full = lambda a: pl.BlockSpec(a.shape, lambda i: (0,) * a.ndim)
    ws = (lw["hg_out_norm"].reshape(1, HG_D), lw["w_br_a"].astype(BF16), lw["w_br_b"].astype(BF16),
          lw["w_br_c"].astype(BF16), lw["w_out"].astype(BF16))
    yb_spec = pl.BlockSpec((1, y_b.shape[1], n1, tcols, LANE),
                           lambda i: (jnp.minimum(i // tpb, n_batch - 1), 0, 0, (i % tpb) // per_blk, 0))
    lat = lambda i: jnp.minimum(i, lat_tiles - 1)
    late = lambda i: jnp.maximum(i - lat_tiles, 0)
    extra_specs, extra = [], []
    if ctx_parts is not None:
        extra_specs = [pl.BlockSpec((tm, hk), lambda i: (late(i), 0)),
                       pl.BlockSpec((tm, hk), lambda i: (late(i), 0)),
                       pl.BlockSpec((tm, HY_W), lambda i: (late(i), 0)),
                       pl.BlockSpec((MLA_HEADS * MLA_V, tm), lambda i: (0, late(i)))]
        extra = list(ctx_parts)
    return pl.pallas_call(
        functools.partial(_merge_kernel, lat_tiles=lat_tiles),
        grid=(n_rows // tm,),
        in_specs=[row(d), pl.BlockSpec((1, N_MOD, d), _group_map(tpb, n_batch)),
                  pl.BlockSpec((tm, hk), lambda i: (lat(i), 0)), pl.BlockSpec((tm, hk), lambda i: (lat(i), 0)),
                  pcol(hk, COL_G), yb_spec,
                  pl.BlockSpec((MLA_HEADS * MLA_V, tm), lambda i: (0, lat(i))),
                  pcol(d, COL_GA), pcol(d, COL_GB), pcol(d, COL_GC)] + [full(a) for a in ws] + extra_specs,
        out_specs=row(d),
        out_shape=jax.ShapeDtypeStruct((n_rows, d), F32),
        compiler_params=_cparams(1),
        name="merge",
    )(xa, mod, o_f, o_b, p, y_b, y_c, p, p, p, *ws, *extra)


def kernel(x, c, ctx, c_ctx, ada_w, ada_b, ffn1_norm, ffn1_w13, ffn1_w2, mix_norm, w_in, hg_lb_logits, hg_out_norm, hy_conv_w, hy_conv_b, hy_w1, hy_b1, hy_w2, hy_b2, hy_w3, hy_b3, hy_w4, hy_freq, hy_skip, q_a_norm, w_uq, kv_a_norm, w_ukv, q_nope_norm, q_rope_norm, k_nope_norm, k_rope_norm, w_br_a, w_br_b, w_br_c, w_out, ffn2_norm, ffn2_w13, ffn2_w2):
    stacked = dict(
        ada_w=ada_w, ada_b=ada_b, ffn1_norm=ffn1_norm, ffn1_w13=ffn1_w13, ffn1_w2=ffn1_w2, mix_norm=mix_norm,
        w_in=w_in, hg_out_norm=hg_out_norm, hy_conv_w=hy_conv_w, hy_conv_b=hy_conv_b, hy_w1=hy_w1, hy_b1=hy_b1,
        hy_w2=hy_w2, hy_b2=hy_b2, hy_w3=hy_w3, hy_b3=hy_b3, hy_w4=hy_w4, hy_freq=hy_freq, hy_skip=hy_skip,
        q_a_norm=q_a_norm, w_uq=w_uq, kv_a_norm=kv_a_norm, w_ukv=w_ukv, q_nope_norm=q_nope_norm,
        q_rope_norm=q_rope_norm, k_nope_norm=k_nope_norm, k_rope_norm=k_rope_norm, w_br_a=w_br_a,
        w_br_b=w_br_b, w_br_c=w_br_c, w_out=w_out, ffn2_norm=ffn2_norm, ffn2_w13=ffn2_w13, ffn2_w2=ffn2_w2)
    n_batch, seq, d = x.shape
    ctx_len = ctx.shape[1]
    depth = ada_w.shape[0]
    lat_rows, ctx_rows = n_batch * seq, n_batch * ctx_len
    all_rows = lat_rows + ctx_rows
    assert seq % 512 == 0 and ctx_rows % 512 == 0 and seq % ctx_len == 0 and seq % GRID_W == 0
    assert ctx_len % HG_CHUNK == 0 and n_batch < 8

    xa, xc = x.reshape(lat_rows, d), ctx.reshape(ctx_rows, d)
    cs = jnp.concatenate([c, c_ctx.reshape(1, d), jnp.zeros((7 - n_batch, d), F32)], axis=0)
    rope_cs = _rope_table(seq, n_batch, ctx_rows)
    zero_state = jnp.zeros((2, n_batch, HG_HEADS, HG_D, HG_D), F32)
    hk = HG_HEADS * HG_D

    for l in range(depth):
        lw = {name: val[l] for name, val in stacked.items()}
        need_ctx = l < depth - 1
        mod = _modulation(cs, lw["ada_w"], lw["ada_b"])
        xa = _half_ffn(xa, mod, lw["ffn1_norm"], lw["ffn1_w13"], lw["ffn1_w2"], 0, all_rows, seq, n_batch, xc=xc)
        xc = None
        p = _in_projection(xa, mod, lw["mix_norm"], _pack_w_in(lw["w_in"]).astype(BF16), seq, n_batch)

        ocf, ocb, s_c = _hgrn2_scan(p, hg_lb_logits, zero_state, l, lat_rows, ctx_len, n_batch)
        olf, olb, _ = _hgrn2_scan(p, hg_lb_logits, s_c, l, 0, seq, n_batch)
        y_b = _hyena(p, lw, 0, seq, n_batch)
        qt, k, vt = _mla_qkv(p, rope_cs, lw)
        y_c = _attention(qt, k, vt, seq, ctx_len, n_batch, latent=True)

        ctx_parts = None
        if need_ctx:
            ctx_parts = (ocf.reshape(ctx_rows, hk), ocb.reshape(ctx_rows, hk),
                         _hyena(p, lw, lat_rows, ctx_len, n_batch),
                         _attention(qt, k, vt, seq, ctx_len, n_batch, latent=False))
        mix_rows = all_rows if need_ctx else lat_rows
        xa = _merge(xa, mod, olf.reshape(lat_rows, hk), olb.reshape(lat_rows, hk), p, y_b, y_c, ctx_parts, lw,
                    mix_rows, seq, n_batch)
        xa = _half_ffn(xa, mod, lw["ffn2_norm"], lw["ffn2_w13"], lw["ffn2_w2"], 6, mix_rows, seq, n_batch)
    return xa[:lat_rows].reshape(n_batch, seq, d)
```

```python
import functools
import math

import numpy as np
import jax
import jax.numpy as jnp
from jax import lax
from jax.experimental import pallas as pl
from jax.experimental.pallas import tpu as pltpu

F32 = jnp.float32
BF16 = jnp.bfloat16

RMS_EPS = 1e-6
N_MOD = 9
GRID_W = 64
ROPE_THETA = 10000.0
HG_HEADS = 4
HG_D = 128
HG_CHUNK = 128
HY_W = 512
HY_TARGET = 1e-2
HY_MIN_DECAY = math.log(HY_TARGET) / 1.5
HY_MAX_DECAY = math.log(HY_TARGET) / 0.3
HY_SHIFT = 0.05
MLA_HEADS = 4
MLA_NOPE = 128
MLA_ROPE = 64
MLA_V = 128
MLA_QK = MLA_NOPE + MLA_ROPE
MLA_SCALE = MLA_QK ** -0.5
MLA_HEAD_PAD = 256
V_ROWS = MLA_V + 16
LOG2_E = math.log2(math.e)
FFT_N1 = 128
LANE = 128

VMEM_LIMIT = 52 * 1024 * 1024

COL_GA, COL_GB, COL_GC = 0, 1024, 2048
COL_Q, COL_ZF, COL_ZB, COL_IV, COL_G = 3072, 3584, 4096, 4608, 5120
COL_HY = 5632
COL_QA, COL_KVA, COL_KR = 7168, 7424, 7552
IN_PACKED = 7680


def _cparams(n_axes):
    return pltpu.CompilerParams(dimension_semantics=("arbitrary",) * n_axes,
                                vmem_limit_bytes=VMEM_LIMIT)


def _dot(a, b):
    return jnp.dot(a, b, preferred_element_type=F32)


def _dot_nt(a, b):
    return lax.dot_general(a, b, (((1,), (1,)), ((), ())), preferred_element_type=F32)


def _dot_tn(a, b):
    return lax.dot_general(a, b, (((0,), (0,)), ((), ())), preferred_element_type=F32)


def _sigmoid(x):
    return 1.0 / (1.0 + jnp.exp(-x))


def _silu(x):
    return x * _sigmoid(x)


def _rms(x, gain):
    return x * lax.rsqrt(jnp.mean(x * x, axis=-1, keepdims=True) + RMS_EPS) * gain


def _mod_kernel(c_ref, w_ref, b_ref, o_ref):
    o_ref[...] = _dot(_silu(c_ref[...]).astype(BF16), w_ref[...]) + b_ref[...]


def _modulation(cs, ada_w, ada_b):
    g, d = cs.shape
    n = ada_w.shape[1]
    tn = n // 4
    out = pl.pallas_call(
        _mod_kernel,
        grid=(n // tn,),
        in_specs=[pl.BlockSpec((g, d), lambda j: (0, 0)),
                  pl.BlockSpec((d, tn), lambda j: (0, j)),
                  pl.BlockSpec((1, tn), lambda j: (0, j))],
        out_specs=pl.BlockSpec((g, tn), lambda j: (0, j)),
        out_shape=jax.ShapeDtypeStruct((g, n), F32),
        compiler_params=_cparams(1),
        name="modulation",
    )(cs, ada_w.astype(BF16), ada_b.reshape(1, n))
    return out.reshape(g, N_MOD, d)


def _ffn_kernel(*refs, idx, ff, ck, lat_tiles):
    x_ref, mod_ref, g_ref, w13_ref, w2_ref = refs[:5]
    o_ref = refs[-1]
    x = x_ref[...]
    if len(refs) == 7:
        x = jnp.where(pl.program_id(0) >= lat_tiles, refs[5][...], x)
    m = mod_ref[0]
    h = (_rms(x, g_ref[...]) * (1.0 + m[idx + 1:idx + 2]) + m[idx:idx + 1]).astype(BF16)
    acc = jnp.zeros(x.shape, F32)
    for c0 in range(0, ff, ck):
        c1 = min(c0 + ck, ff)
        a = _dot(h, w13_ref[:, c0:c1])
        b = _dot(h, w13_ref[:, ff + c0:ff + c1])
        acc = acc + _dot((_silu(a) * b).astype(BF16), w2_ref[c0:c1, :])
    o_ref[...] = x + (0.5 * m[idx + 2:idx + 3]) * acc


def _group_map(tiles_per_batch, n_batch):
    return lambda i: (jnp.minimum(i // tiles_per_batch, n_batch), 0, 0)


def _half_ffn(xa, mod, gain, w13, w2, idx, n_rows, seq, n_batch, xc=None, tm=512, ck=512):
    d = xa.shape[1]
    ff = w2.shape[0]
    lat_tiles = n_batch * seq // tm
    const = dict(pipeline_mode=pl.Buffered(1))
    x_specs, xs = [pl.BlockSpec((tm, d), lambda i: (jnp.minimum(i, xa.shape[0] // tm - 1), 0))], [xa]
    tail_specs, tail = [], []
    if xc is not None:
        tail_specs, tail = [pl.BlockSpec((tm, d), lambda i: (jnp.maximum(i - lat_tiles, 0), 0))], [xc]
    return pl.pallas_call(
        functools.partial(_ffn_kernel, idx=idx, ff=ff, ck=ck, lat_tiles=lat_tiles),
        grid=(n_rows // tm,),
        in_specs=x_specs + [pl.BlockSpec((1, N_MOD, d), _group_map(seq // tm, n_batch)),
                            pl.BlockSpec((1, d), lambda i: (0, 0)),
                            pl.BlockSpec((d, 2 * ff), lambda i: (0, 0), **const),
                            pl.BlockSpec((ff, d), lambda i: (0, 0), **const)] + tail_specs,
        out_specs=pl.BlockSpec((tm, d), lambda i: (i, 0)),
        out_shape=jax.ShapeDtypeStruct((n_rows, d), F32),
        compiler_params=_cparams(1),
        name="half_ffn",
    )(*xs, mod, gain.reshape(1, d), w13.astype(BF16), w2.astype(BF16), *tail)


def _inproj_kernel(x_ref, mod_ref, g_ref, w_ref, o_ref):
    m = mod_ref[0]
    h = (_rms(x_ref[...], g_ref[...]) * (1.0 + m[4:5]) + m[3:4]).astype(BF16)
    o_ref[...] = _dot(h, w_ref[...])


def _pack_w_in(w_in):
    d = w_in.shape[0]
    hk = HG_HEADS * HG_D
    sizes = (hk, hk, hk, hk, hk, 3 * HY_W, 256, 128, MLA_ROPE, d, d, d)
    offs = np.cumsum((0,) + sizes)
    q, zf, zb, iv, g, hy, qa, kva, kr, ga, gb, gc = (w_in[:, offs[i]:offs[i + 1]] for i in range(12))
    return jnp.concatenate([ga, gb, gc, q, zf, zb, iv, g, hy, qa, kva, kr, _rope_swap(kr)], axis=1)


def _rope_swap(a):
    q = MLA_ROPE // 4
    return jnp.concatenate([a[..., q:2 * q], a[..., :q], a[..., 3 * q:], a[..., 2 * q:3 * q]], axis=-1)


def _in_projection(xa, mod, gain, w_packed, seq, n_batch, tm=512, tn=2560):
    n_rows, d = xa.shape
    n = w_packed.shape[1]
    return pl.pallas_call(
        _inproj_kernel,
        grid=(n // tn, n_rows // tm),
        in_specs=[pl.BlockSpec((tm, d), lambda j, i: (i, 0)),
                  pl.BlockSpec((1, N_MOD, d), lambda j, i: (jnp.minimum(i // (seq // tm), n_batch), 0, 0)),
                  pl.BlockSpec((1, d), lambda j, i: (0, 0)),
                  pl.BlockSpec((d, tn), lambda j, i: (0, j))],
        out_specs=pl.BlockSpec((tm, tn), lambda j, i: (i, j)),
        out_shape=jax.ShapeDtypeStruct((n_rows, n), F32),
        compiler_params=_cparams(2),
        name="in_projection",
    )(xa, mod, gain.reshape(1, d), w_packed)


def _hgrn2_tables(c, rev):
    t = np.arange(c)[:, None]
    u = np.arange(c)[None, :]
    mats = [(u <= t), (u > t)]
    masks = [(t == u)]
    h = c // 2
    while h >= 1:
        mid = (t // (2 * h)) * (2 * h) + h
        mats.append(np.where(t >= mid, (u >= mid) & (u <= t), (u >= t + 1) & (u <= mid - 1)))
        mid_s = (u // (2 * h)) * (2 * h) + h
        masks.append((t // (2 * h) == u // (2 * h)) & (u < mid_s) & (t >= mid))
        h //= 2
    mats = np.stack([m.astype(np.float32) for m in mats])
    masks = np.stack([m.astype(np.float32) for m in masks])
    if rev:
        mats = mats[:, ::-1, ::-1]
        masks = masks[:, ::-1, ::-1]
    mats = np.ascontiguousarray(mats).reshape(-1, c)
    return (jnp.asarray(np.concatenate([mats, mats], axis=1), BF16),
            jnp.asarray(np.ascontiguousarray(masks), F32))


HG_PAIR = 2


def _hgrn2_chunk(q, z, v, lg, dst_ref, msk_ref, st_ref, o_ref, ci, *, layer, rev, n_levels):
    c = q.shape[0]
    hk = HG_HEADS * HG_D
    e = jnp.exp(lg - jnp.max(lg, axis=0, keepdims=True))
    sm = e / jnp.sum(e, axis=0, keepdims=True)
    lb = jnp.zeros((1, hk), F32)
    for i in range(1, layer + 1):
        lb = lb + sm[i:i + 1]

    f = lb + (1.0 - lb) * _sigmoid(z)
    kk = 1.0 - f
    g = jnp.log(f)
    g_hi = g.astype(BF16)
    g_lo = (g - g_hi.astype(F32)).astype(BF16)
    dg = _dot(dst_ref[...], jnp.concatenate([g_hi, g_lo], axis=0))
    qs = _silu(q) * HG_D ** -0.5
    last = 0 if rev else c - 1

    for h in range(HG_HEADS):
        hs = slice(h * HG_D, (h + 1) * HG_D)
        qh, kh, vh = qs[:, hs], kk[:, hs], v[:, hs].astype(BF16)
        b_in = dg[0:c, hs]
        e_out = jnp.exp(dg[c:2 * c, hs])
        a = _dot_nt(qh.astype(BF16), kh.astype(BF16)) * msk_ref[0]
        for l in range(1, n_levels + 1):
            el = jnp.exp(dg[(1 + l) * c:(2 + l) * c, hs])
            a = a + _dot_nt((qh * el).astype(BF16), (kh * el).astype(BF16)) * msk_ref[l]
        st = st_ref[ci, h]
        o = _dot_nt((qh * jnp.exp(b_in)).astype(BF16), st.astype(BF16)) + _dot(a.astype(BF16), vh)
        o_ref[:, hs] = o
        st_ref[ci, h] = jnp.exp(b_in[last:last + 1, :]) * st + _dot_tn(vh, (kh * e_out).astype(BF16))


def _hgrn2_kernel(*refs, layer, n_levels):
    n_in = 6 * HG_PAIR
    ins, (lg_ref, dstf_ref, mskf_ref, dstb_ref, mskb_ref, s0_ref) = refs[:n_in], refs[n_in:n_in + 6]
    of_ref, ob_ref, sf_ref, st_ref = refs[n_in + 6:]
    j = pl.program_id(1)

    @pl.when(j == 0)
    def _():
        st_ref[...] = s0_ref[...].reshape(st_ref.shape)

    for e in range(HG_PAIR):
        qf, zf, vf, qb, zb, vb = (r[...] for r in ins[6 * e:6 * e + 6])
        _hgrn2_chunk(qf, zf, vf, lg_ref[0], dstf_ref, mskf_ref, st_ref, of_ref.at[e], e,
                     layer=layer, rev=False, n_levels=n_levels)
        _hgrn2_chunk(qb, zb, vb, lg_ref[1], dstb_ref, mskb_ref, st_ref, ob_ref.at[e], HG_PAIR + e,
                     layer=layer, rev=True, n_levels=n_levels)

    @pl.when(j == pl.num_programs(1) - 1)
    def _():
        sf_ref[...] = st_ref[...].reshape(sf_ref.shape)


def _hgrn2_scan(p, logits, s0, layer, row0, seq, n_batch):
    c = HG_CHUNK
    hk = HG_HEADS * HG_D
    nc = seq // c
    base = row0 // c
    n_levels = int(math.log2(c))
    tabs = _hgrn2_tables(c, False) + _hgrn2_tables(c, True)
    assert n_batch % HG_PAIR == 0

    def prow(w, e, rev):
        return pl.BlockSpec((c, hk), lambda bp, j: (base + (bp * HG_PAIR + e) * nc + (nc - 1 - j if rev else j),
                                                   w // hk))

    in_specs = []
    for e in range(HG_PAIR):
        in_specs += [prow(COL_Q, e, False), prow(COL_ZF, e, False), prow(COL_IV, e, False),
                     prow(COL_Q, e, True), prow(COL_ZB, e, True), prow(COL_IV, e, True)]
    full = lambda a: pl.BlockSpec(a.shape, lambda bp, j: (0,) * a.ndim)
    st_spec = pl.BlockSpec((2, HG_PAIR, HG_HEADS, HG_D, HG_D), lambda bp, j: (0, bp, 0, 0, 0))
    return pl.pallas_call(
        functools.partial(_hgrn2_kernel, layer=layer, n_levels=n_levels),
        grid=(n_batch // HG_PAIR, nc),
        in_specs=in_specs + [full(logits)] + [full(t) for t in tabs] + [st_spec],
        out_specs=[pl.BlockSpec((HG_PAIR, c, hk), lambda bp, j: (bp, j, 0)),
                   pl.BlockSpec((HG_PAIR, c, hk), lambda bp, j: (bp, nc - 1 - j, 0)),
                   st_spec],
        out_shape=[jax.ShapeDtypeStruct((n_batch, seq, hk), F32),
                   jax.ShapeDtypeStruct((n_batch, seq, hk), F32),
                   jax.ShapeDtypeStruct((2, n_batch, HG_HEADS, HG_D, HG_D), F32)],
        scratch_shapes=[pltpu.VMEM((2 * HG_PAIR, HG_HEADS, HG_D, HG_D), F32)],
        compiler_params=_cparams(2),
        name="hgrn2_scan",
    )(*([p] * (6 * HG_PAIR)), logits, *tabs, s0)


def _shortconv_kernel(u_ref, w_ref, b_ref, o_ref):
    u = u_ref[...]
    n = u.shape[0]
    row = lax.broadcasted_iota(jnp.int32, u.shape, 0)
    prev = jnp.where(row == 0, 0.0, pltpu.roll(u, 1, 0))
    nxt = jnp.where(row == n - 1, 0.0, pltpu.roll(u, n - 1, 0))
    w = w_ref[...]
    o_ref[0, 0, 0] = prev * w[0:1] + u * w[1:2] + nxt * w[2:3] + b_ref[...]


def _shortconv_t1major_kernel(u_ref, w_ref, b_ref, o_ref, u_scr, *, n1):
    half = u_ref.shape[0] // n1
    w = w_ref[...]
    b = b_ref[...]
    pitch = n1 + 8

    def copy(g, carry):
        u_scr[pl.ds(pl.multiple_of(g * pitch, 8), n1), :] = u_ref[pl.ds(pl.multiple_of(g * n1, 8), n1), :]
        return carry

    lax.fori_loop(0, half, copy, 0, unroll=8)
    row = lax.broadcasted_iota(jnp.int32, (half, LANE), 0)
    col = lambda t1: u_scr[pl.ds(t1, half, stride=pitch), :]
    before = jnp.where(row == 0, 0.0, pltpu.roll(col(n1 - 1), 1, 0))
    after = jnp.where(row == half - 1, 0.0, pltpu.roll(col(0), half - 1, 0))

    def body(t1, carry):
        prev, cur = carry
        nxt = jnp.where(t1 == n1 - 1, after, col(jnp.minimum(t1 + 1, n1 - 1)))
        o_ref[0, 0, 0, pl.ds(pl.multiple_of(t1 * half, 8), half), :] = prev * w[0:1] + cur * w[1:2] + nxt * w[2:3] + b
        return cur, nxt

    lax.fori_loop(0, n1, body, (before, col(0)), unroll=16)


def _short_conv(p, w, b, row0, seq, n_batch, n1):
    nb = 3 * HY_W // LANE
    per = HY_W // LANE
    body, scratch = _shortconv_kernel, []
    if n1 > 1:
        body = functools.partial(_shortconv_t1major_kernel, n1=n1)
        scratch = [pltpu.VMEM((seq // n1 * (n1 + 8), LANE), F32)]
    return pl.pallas_call(
        body,
        grid=(n_batch, nb),
        in_specs=[pl.BlockSpec((seq, LANE), lambda bi, j: (row0 // seq + bi, COL_HY // LANE + j)),
                  pl.BlockSpec((3, LANE), lambda bi, j: (0, j)),
                  pl.BlockSpec((1, LANE), lambda bi, j: (0, j))],
        out_specs=pl.BlockSpec((1, 1, 1, seq, LANE), lambda bi, j: (j // per, bi, j % per, 0, 0)),
        out_shape=jax.ShapeDtypeStruct((3, n_batch, per, seq, LANE), F32),
        scratch_shapes=scratch,
        compiler_params=_cparams(2),
        name="hyena_short_conv",
    )(p, w, b.reshape(1, -1))


def _hy_filter_kernel(emb_ref, embr_ref, w1_ref, b1_ref, w2_ref, b2_ref, w3_ref, b3_ref, w4_ref, fr_ref,
                      dl_ref, o_ref, nrm_ref, *, seq, group):
    i = pl.program_id(0)
    hp = lax.Precision.HIGHEST
    fr = fr_ref[...]

    def mlp(emb):
        hid = jnp.sin(fr * (jnp.dot(emb, w1_ref[...], precision=hp) + b1_ref[...]))
        hid = jnp.sin(fr * (jnp.dot(hid, w2_ref[...], precision=hp) + b2_ref[...]))
        return jnp.sin(fr * (jnp.dot(hid, w3_ref[...], precision=hp) + b3_ref[...]))

    tl = emb_ref.shape[0]
    n = w4_ref.shape[1]
    half = n // 2
    r = lax.broadcasted_iota(jnp.int32, (tl, half), 0) + i * tl
    per = seq // group
    pos = (r >> (per.bit_length() - 1)) + group * (r & (per - 1))
    posr = (pos & ~(group - 1)) + ((group - (pos & (group - 1))) & (group - 1))
    for side, (emb, pp) in enumerate(((emb_ref[...], pos), (embr_ref[...], posr))):
        h = jnp.dot(mlp(emb), w4_ref[:, side * half:(side + 1) * half], precision=hp)
        t = pp.astype(F32) * (1.0 / (seq - 1))
        h = h * (jnp.exp(-t * dl_ref[...]) + HY_SHIFT)
        if side == 1:
            h = jnp.where(pp == 0, 0.0, h)
        for k in range(half // LANE):
            o_ref[side * (half // LANE) + k] = h[:, k * LANE:(k + 1) * LANE]

        @pl.when(i == 0)
        def _():
            nrm_ref[:, side * half:(side + 1) * half] = jnp.zeros((1, half), F32)

        nrm_ref[:, side * half:(side + 1) * half] += jnp.sum(jnp.abs(h), axis=0, keepdims=True)


def _hyena_filter_taps(seq, w1, b1, w2, b2, w3, b3, w4, freq, group):
    fh = w1.shape[1]
    n_emb = w1.shape[0]
    bands_n = (n_emb - 1) // 2
    tt = np.linspace(0.0, 1.0, seq, dtype=np.float32)[:, None].astype(np.float64)
    ww = (2.0 * math.pi / seq) * np.arange(seq, dtype=np.float64)[:, None]
    bands = np.linspace(1e-4, bands_n - 1, bands_n, dtype=np.float32)[None, :].astype(np.float64)
    emb = np.concatenate([tt, np.cos(bands * ww), -np.sin(bands * ww)], axis=-1)
    emb = np.pad(emb, ((0, 0), (0, LANE - n_emb))).astype(np.float32)
    assert seq % group == 0 and group & (group - 1) == 0 and (seq // group) & (seq // group - 1) == 0
    r = np.arange(seq)
    pos = r // (seq // group) + group * (r % (seq // group))
    embr = emb[(pos // group) * group + (group - pos % group) % group]
    emb = emb[pos]
    w1p = jnp.pad(w1, ((0, LANE - n_emb), (0, 0)))
    deltas = np.abs(np.linspace(HY_MIN_DECAY, HY_MAX_DECAY, HY_W, dtype=np.float32))
    deltas = np.tile(deltas, 2)[None, :]
    n = w4.shape[1]
    tl = min(seq, 512)
    full = lambda a: pl.BlockSpec(a.shape, lambda i: (0,) * a.ndim)
    args = (w1p, b1.reshape(1, fh), w2, b2.reshape(1, fh), w3, b3.reshape(1, fh), w4, freq.reshape(1, fh),
            jnp.asarray(deltas))
    return pl.pallas_call(
        functools.partial(_hy_filter_kernel, seq=seq, group=group),
        grid=(seq // tl,),
        in_specs=[pl.BlockSpec((tl, LANE), lambda i: (i, 0))] * 2 + [full(a) for a in args],
        out_specs=[pl.BlockSpec((n // LANE, tl, LANE), lambda i: (0, i, 0)),
                   pl.BlockSpec((1, n), lambda i: (0, 0))],
        out_shape=[jax.ShapeDtypeStruct((n // LANE, seq, LANE), F32), jax.ShapeDtypeStruct((1, n), F32)],
        compiler_params=_cparams(1),
        name="hyena_filter_taps",
    )(jnp.asarray(emb), jnp.asarray(embr), *args)


def _fft_split(seq):
    n = 2 * seq
    n1 = FFT_N1 if n > 1024 else 1
    return n, n1, n // n1


def _cis(idx, n):
    ph = 2.0 * np.pi * (idx % n) / n
    return np.cos(ph), -np.sin(ph)


def _dft_tables_short(seq):
    n = 2 * seq
    cr, ci = _cis(np.arange(n)[:, None] * np.arange(seq)[None, :], n)
    w_fwd = np.concatenate([cr, ci], axis=0)
    w_inv = np.concatenate([cr.T, ci.T], axis=1) / n
    return dict(w_fwd=jnp.asarray(w_fwd, BF16), w_inv=jnp.asarray(w_inv, BF16))


def _dft_tables_long(seq):
    n, n1, n2 = _fft_split(seq)
    t1 = np.arange(n1)[:, None, None]
    f2 = np.arange(n2)[None, :, None]
    t2 = np.arange(n2 // 2)[None, None, :]
    cr, ci = _cis(f2 * (t1 + n1 * t2), n)
    w1 = np.concatenate([cr, ci], axis=1)
    w4 = np.concatenate([np.swapaxes(cr, 1, 2), np.swapaxes(ci, 1, 2)], axis=2) / n
    j = (n1 - t1) % n1 + n1 * t2
    br, bi = _cis(f2 * (n - j), n)
    w1f = np.concatenate([np.concatenate([cr, br], axis=2), np.concatenate([ci, bi], axis=2)], axis=1)
    gr, gi = _cis(np.arange(n1)[:, None] * np.arange(n1)[None, :], n1)
    wd = np.concatenate([np.concatenate([gr, -gi], axis=1), np.concatenate([gi, gr], axis=1)], axis=0)
    return dict(w1=jnp.asarray(w1, BF16), w4=jnp.asarray(w4, BF16), w1f=jnp.asarray(w1f, BF16),
                wd=jnp.asarray(wd, BF16), wdi=jnp.asarray(wd.T, BF16))


def _dft_rows_kernel(w_ref, x_ref, o_ref):
    o_ref[0] = _dot(w_ref[...], x_ref[0].astype(BF16)).astype(o_ref.dtype)


def _dft_rows(w, x, tn):
    nb, k, cols = x.shape
    m = w.shape[0]
    return pl.pallas_call(
        _dft_rows_kernel,
        grid=(nb, cols // tn),
        in_specs=[pl.BlockSpec((m, k), lambda b, j: (0, 0)),
                  pl.BlockSpec((1, k, tn), lambda b, j: (b, 0, j))],
        out_specs=pl.BlockSpec((1, m, tn), lambda b, j: (b, 0, j)),
        out_shape=jax.ShapeDtypeStruct((nb, m, cols), BF16),
        compiler_params=_cparams(2),
        name="hyena_dft_rows",
    )(w, x)


def _idft_gate_kernel(w_ref, b_ref, xg_ref, z_ref, sk_ref, o_ref):
    y = _dot(w_ref[...], b_ref[0])
    z = z_ref[0]
    o_ref[0] = xg_ref[0] * (y + z * sk_ref[...])


def _idft_gate(w, bc, xg, z, skip_t, tn):
    nb, k, cols = bc.shape
    m = w.shape[0]
    return pl.pallas_call(
        _idft_gate_kernel,
        grid=(nb, cols // tn),
        in_specs=[pl.BlockSpec((m, k), lambda b, j: (0, 0)),
                  pl.BlockSpec((1, k, tn), lambda b, j: (b, 0, j)),
                  pl.BlockSpec((1, m, tn), lambda b, j: (b, 0, j)),
                  pl.BlockSpec((1, m, tn), lambda b, j: (b, 0, j)),
                  pl.BlockSpec((1, tn), lambda b, j: (0, j))],
        out_specs=pl.BlockSpec((1, m, tn), lambda b, j: (b, 0, j)),
        out_shape=jax.ShapeDtypeStruct((nb, m, cols), F32),
        compiler_params=_cparams(2),
        name="hyena_idft_gate",
    )(w, bc, xg, z, skip_t)


def _spec_combine_kernel(af_ref, ab_ref, nf_ref, nb_ref, o_ref):
    inv = 1.0 / (nf_ref[...] + nb_ref[...])
    n = o_ref.shape[1]
    o_ref[0] = (af_ref[0, :n].astype(F32) + ab_ref[0, :n].astype(F32)) * inv
    o_ref[1] = (af_ref[0, n:].astype(F32) - ab_ref[0, n:].astype(F32)) * inv


def _spec_mul_kernel(a_ref, k_ref, o_ref):
    n = k_ref.shape[1]
    xr, xi = a_ref[0, :n].astype(F32), a_ref[0, n:].astype(F32)
    kr, ki = k_ref[0], k_ref[1]
    o_ref[0, :n] = (xr * kr - xi * ki).astype(o_ref.dtype)
    o_ref[0, n:] = (xr * ki + xi * kr).astype(o_ref.dtype)


def _hyena_short(seq, taps, nrm, uc, skip):
    n = 2 * seq
    uc = jnp.swapaxes(uc, 2, 3).reshape(3, uc.shape[1], seq, HY_W)
    tabs = _dft_tables_short(seq)
    cf = taps.shape[1]
    c = HY_W
    a = _dft_rows(tabs["w_fwd"], taps.reshape(1, seq, cf), tn=cf)
    nblk = cf // 2 // c
    spec = pl.pallas_call(
        _spec_combine_kernel,
        grid=(nblk,),
        in_specs=[pl.BlockSpec((1, 2 * n, c), lambda j: (0, 0, j)),
                  pl.BlockSpec((1, 2 * n, c), lambda j: (0, 0, j + nblk)),
                  pl.BlockSpec((1, c), lambda j: (0, j)),
                  pl.BlockSpec((1, c), lambda j: (0, j + nblk))],
        out_specs=pl.BlockSpec((2, n, c), lambda j: (0, 0, j)),
        out_shape=jax.ShapeDtypeStruct((2, n, cf // 2), F32),
        compiler_params=_cparams(1),
        name="hyena_spec_combine",
    )(a, a, nrm, nrm)
    nb = uc.shape[1]
    z = uc[2]
    for order in range(2):
        a = _dft_rows(tabs["w_fwd"], z, tn=c)
        bc = pl.pallas_call(
            _spec_mul_kernel,
            grid=(nb,),
            in_specs=[pl.BlockSpec((1, 2 * n, c), lambda b: (b, 0, 0)),
                      pl.BlockSpec((2, n, c), lambda b: (0, 0, order))],
            out_specs=pl.BlockSpec((1, 2 * n, c), lambda b: (b, 0, 0)),
            out_shape=jax.ShapeDtypeStruct((nb, 2 * n, c), BF16),
            compiler_params=_cparams(1),
            name="hyena_spec_mul",
        )(a, spec)
        z = _idft_gate(tabs["w_inv"], bc, uc[order], z, skip[order].reshape(1, c), tn=c)
    return z


HY_SLABS = 2


def _slab_pitch(n2):
    return n2 + 8


def _pack_c(re, im):
    hi = lax.bitcast_convert_type(re.astype(BF16).astype(F32), jnp.uint32)
    lo = lax.bitcast_convert_type(im.astype(BF16).astype(F32), jnp.uint32)
    return hi | (lo >> 16)


def _unpack_c(w):
    re = lax.bitcast_convert_type(w & jnp.uint32(0xFFFF0000), F32)
    im = lax.bitcast_convert_type(w << 16, F32)
    return re.astype(BF16), im.astype(BF16)


def _store_slab(a_scr, row, words):
    for s in range(HY_SLABS):
        a_scr[s, pl.ds(row, words.shape[0]), :] = words[:, s * LANE:(s + 1) * LANE]


def _stage1(a_scr, w_ref, xs_of, i, tb, n2, pitch):
    for j in range(tb):
        a = _dot(w_ref[j], xs_of(j))
        _store_slab(a_scr, pl.multiple_of((i * tb + j) * pitch, 8), _pack_c(a[:n2], a[n2:]))


def _stage2(a_scr, wd_ref, f2, n1, pitch):
    w = jnp.concatenate([a_scr[s, pl.ds(f2, n1, stride=pitch), :] for s in range(HY_SLABS)], axis=1)
    re, im = _unpack_c(w)
    return _dot(wd_ref[...], jnp.concatenate([re, im], axis=0))


def _time_col(ref, j, half):
    lead = (0,) * (len(ref.shape) - 3)
    return jnp.concatenate([ref[lead + (s, slice(j * half, (j + 1) * half), slice(None))]
                            for s in range(HY_SLABS)], axis=1)


def _hy_spectrum_kernel(hf_ref, hb_ref, nf_ref, nb_ref, w1f_ref, wd_ref, o_ref, a_scr, *, n1, n2, tb, fb):
    i = pl.program_id(1)
    pitch = _slab_pitch(n2)
    half = n2 // 2
    nt = n1 // tb

    @pl.when(i < nt)
    def _():
        def xs_of(j):
            return jnp.concatenate([_time_col(hf_ref, j, half), _time_col(hb_ref, j, half)], axis=0).astype(BF16)
        _stage1(a_scr, w1f_ref, xs_of, i, tb, n2, pitch)

    @pl.when(i >= nt)
    def _():
        inv = 1.0 / (nf_ref[...] + nb_ref[...])
        for jj in range(fb):
            x = _stage2(a_scr, wd_ref, (i - nt) * fb + jj, n1, pitch)
            o_ref[0, 0, jj] = x[:n1] * inv
            o_ref[0, 1, jj] = x[n1:] * inv


def _hy_fftconv_kernel(z_ref, xg_ref, k_ref, sk_ref, w1_ref, wd_ref, wdi_ref, w4_ref, o_ref, a_scr,
                       *, n1, n2, tb, fb):
    i = pl.program_id(2)
    pitch = _slab_pitch(n2)
    half = n2 // 2
    nt, nf = n1 // tb, n2 // fb

    @pl.when(i < nt)
    def _():
        _stage1(a_scr, w1_ref, lambda j: _time_col(z_ref, j, half).astype(BF16), i, tb, n2, pitch)

    @pl.when((i >= nt) & (i < nt + nf))
    def _():
        for jj in range(fb):
            f2 = (i - nt) * fb + jj
            x = _stage2(a_scr, wd_ref, f2, n1, pitch)
            xr, xi = x[:n1], x[n1:]
            kr, ki = k_ref[0, 0, jj], k_ref[0, 1, jj]
            y = jnp.concatenate([xr * kr - xi * ki, xr * ki + xi * kr], axis=0).astype(BF16)
            bv = _dot(wdi_ref[...], y)
            words = _pack_c(bv[:n1], bv[n1:])
            for s in range(HY_SLABS):
                a_scr[s, pl.ds(f2, n1, stride=pitch), :] = words[:, s * LANE:(s + 1) * LANE]

    @pl.when(i >= nt + nf)
    def _():
        sk = sk_ref[...]
        for j in range(tb):
            row = pl.multiple_of(((i - nt - nf) * tb + j) * pitch, 8)
            w = jnp.concatenate([a_scr[s, pl.ds(row, n2), :] for s in range(HY_SLABS)], axis=1)
            re, im = _unpack_c(w)
            y = _dot(w4_ref[j], jnp.concatenate([re, im], axis=0))
            out = _time_col(xg_ref, j, half) * (y + _time_col(z_ref, j, half) * sk)
            for s in range(HY_SLABS):
                o_ref[0, s, j * half:(j + 1) * half, :] = out[:, s * LANE:(s + 1) * LANE]


def _hyena_long(seq, taps, nrm, uc, skip, tb=32, fb=16):
    n, n1, n2 = _fft_split(seq)
    tabs = _dft_tables_long(seq)
    pitch = _slab_pitch(n2)
    half = n2 // 2
    cb = HY_SLABS * LANE
    tb, fb = min(tb, n1), min(fb, n2)
    nt, nf = n1 // tb, n2 // fb
    cf = taps.shape[0] * LANE
    ngrp = cf // 2 // cb
    scratch = [pltpu.VMEM((HY_SLABS, n1 * pitch, LANE), jnp.uint32)]
    tcol = lambda i: jnp.minimum(i, nt - 1)
    spec = pl.pallas_call(
        functools.partial(_hy_spectrum_kernel, n1=n1, n2=n2, tb=tb, fb=fb),
        grid=(ngrp, nt + nf),
        in_specs=[pl.BlockSpec((HY_SLABS, tb * half, LANE), lambda g, i: (g, tcol(i), 0)),
                  pl.BlockSpec((HY_SLABS, tb * half, LANE), lambda g, i: (ngrp + g, tcol(i), 0)),
                  pl.BlockSpec((1, cb), lambda g, i: (0, g)),
                  pl.BlockSpec((1, cb), lambda g, i: (0, ngrp + g)),
                  pl.BlockSpec((tb, 2 * n2, n2), lambda g, i: (tcol(i), 0, 0)),
                  pl.BlockSpec(tabs["wd"].shape, lambda g, i: (0, 0))],
        out_specs=pl.BlockSpec((1, 2, fb, n1, cb), lambda g, i: (g, 0, jnp.maximum(i - nt, 0), 0, 0)),
        out_shape=jax.ShapeDtypeStruct((ngrp, 2, n2, n1, cb), F32),
        scratch_shapes=scratch,
        compiler_params=_cparams(2),
        name="hyena_filter_spectrum",
    )(taps, taps, nrm, nrm, tabs["w1f"], tabs["wd"])
    nb, nblk = uc.shape[1], uc.shape[2]
    ngc = nblk // HY_SLABS
    z, zsel = uc, 2

    def tblk(b, g, i):
        return (b, g, jnp.where(i < nt, i, jnp.maximum(i - nt - nf, 0)), 0)

    def tblk_late(b, g, i):
        return (b, g, jnp.maximum(i - nt - nf, 0), 0)

    for order in range(2):
        stacked = lambda sel, f: (lambda b, g, i: (sel,) + f(b, g, i))
        z = pl.pallas_call(
            functools.partial(_hy_fftconv_kernel, n1=n1, n2=n2, tb=tb, fb=fb),
            grid=(nb, ngc, 2 * nt + nf),
            in_specs=[pl.BlockSpec((1, 1, HY_SLABS, tb * half, LANE), stacked(zsel, tblk)),
                      pl.BlockSpec((1, 1, HY_SLABS, tb * half, LANE), stacked(order, tblk_late)),
                      pl.BlockSpec((1, 2, fb, n1, cb),
                                   lambda b, g, i: (order * ngc + g, 0, jnp.clip(i - nt, 0, nf - 1), 0, 0)),
                      pl.BlockSpec((1, cb), lambda b, g, i: (0, g)),
                      pl.BlockSpec((tb, 2 * n2, half), lambda b, g, i: (tcol(i), 0, 0)),
                      pl.BlockSpec(tabs["wd"].shape, lambda b, g, i: (0, 0)),
                      pl.BlockSpec(tabs["wdi"].shape, lambda b, g, i: (0, 0)),
                      pl.BlockSpec((tb, half, 2 * n2), lambda b, g, i: (jnp.maximum(i - nt - nf, 0), 0, 0))],
            out_specs=pl.BlockSpec((1, HY_SLABS, tb * half, LANE), tblk_late),
            out_shape=jax.ShapeDtypeStruct((nb, nblk, seq, LANE), F32),
            scratch_shapes=scratch,
            compiler_params=_cparams(3),
            name="hyena_fft_conv",
        )(z, uc, spec, skip[order].reshape(1, HY_W), tabs["w1"], tabs["wd"], tabs["wdi"], tabs["w4"])
        z, zsel = z.reshape((1,) + z.shape), 0
    return z[0]


def _hyena(p, lw, row0, seq, n_batch):
    long = _fft_split(seq)[1] > 1
    n1 = FFT_N1 if long else 1
    taps, nrm = _hyena_filter_taps(seq, lw["hy_w1"], lw["hy_b1"], lw["hy_w2"], lw["hy_b2"], lw["hy_w3"],
                                   lw["hy_b3"], lw["hy_w4"], lw["hy_freq"], n1)
    uc = _short_conv(p, lw["hy_conv_w"], lw["hy_conv_b"], row0, seq, n_batch, n1)
    if long:
        return _hyena_long(seq, taps, nrm, uc, lw["hy_skip"])
    taps = jnp.swapaxes(taps, 0, 1).reshape(seq, -1)
    return _hyena_short(seq, taps, nrm, uc, lw["hy_skip"]).reshape(n_batch * seq, HY_W)


def _mla_qkv_kernel(qa_ref, kva_ref, kr_ref, cs_ref, wq_ref, wkv_ref, gqa_ref, gkva_ref, gqn_ref, gqr_ref,
                    gkn_ref, gkr_ref, q_ref, k_ref, v_ref):
    cs = cs_ref[...]
    lane = lax.broadcasted_iota(jnp.int32, cs.shape, 1)
    low = lane < MLA_ROPE

    def rope(pair, gain2):
        ms = jnp.sum(jnp.where(low, pair * pair, 0.0), axis=-1, keepdims=True) * (1.0 / MLA_ROPE)
        t = pair * lax.rsqrt(ms + RMS_EPS) * gain2 * cs
        return jnp.where(low, t + pltpu.roll(t, MLA_ROPE, 1), 0.0)

    qq = _dot(_rms(qa_ref[...], gqa_ref[...]).astype(BF16), wq_ref[...])
    kv = _dot(_rms(kva_ref[...], gkva_ref[...]).astype(BF16), wkv_ref[...])
    kr = rope(kr_ref[...], gkr_ref[...])
    hp = MLA_HEAD_PAD
    tm = kr.shape[0]
    qscale = MLA_SCALE * LOG2_E
    ones_row = (lax.broadcasted_iota(jnp.int32, (V_ROWS - MLA_V, tm), 0) == 0).astype(BF16)
    for h in range(MLA_HEADS):
        qn = _rms(qq[:, h * hp:h * hp + MLA_NOPE], gqn_ref[...])
        qr = rope(qq[:, h * hp + MLA_NOPE:(h + 1) * hp], gqr_ref[...])
        q_ref[h, 0:LANE, :] = (qn * qscale).T.astype(BF16)
        q_ref[h, LANE:2 * LANE, :] = (qr * qscale).T.astype(BF16)
        kn = _rms(kv[:, h * hp:h * hp + MLA_NOPE], gkn_ref[...])
        k_ref[h, :, 0:LANE] = kn.astype(BF16)
        k_ref[h, :, LANE:2 * LANE] = kr.astype(BF16)
        v_ref[h, 0:MLA_V, :] = kv[:, h * hp + MLA_NOPE:(h + 1) * hp].T.astype(BF16)
        v_ref[h, MLA_V:V_ROWS, :] = ones_row


def _rope_table(seq, n_batch, ctx_rows):
    rows = seq // GRID_W
    row = np.repeat(np.arange(rows, dtype=np.float32), GRID_W)
    col = np.tile(np.arange(GRID_W, dtype=np.float32), rows)
    half = MLA_ROPE // 2
    inv = (ROPE_THETA ** (-np.arange(0, half, 2, dtype=np.float32) / half)).astype(np.float32)
    ar = (row[:, None] * inv).astype(np.float64)
    ac = (col[:, None] * inv).astype(np.float64)
    cos = np.concatenate([np.cos(ar), np.cos(ar), np.cos(ac), np.cos(ac)], axis=1)
    sin = np.concatenate([-np.sin(ar), np.sin(ar), -np.sin(ac), np.sin(ac)], axis=1)
    lat = np.tile(np.concatenate([cos, sin], axis=1), (n_batch, 1))
    ctx = np.concatenate([np.ones((ctx_rows, MLA_ROPE)), np.zeros((ctx_rows, MLA_ROPE))], axis=1)
    return jnp.asarray(np.concatenate([lat, ctx], axis=0), F32)


def _pair_gain(g):
    return jnp.concatenate([g, _rope_swap(g)]).reshape(1, 2 * MLA_ROPE)


def _mla_qkv(p, cs, lw, tm=512):
    n_rows = p.shape[0]
    hd, hp = MLA_HEADS, MLA_HEAD_PAD
    w_uq = lw["w_uq"].reshape(-1, hd, MLA_QK)
    wq = jnp.concatenate([w_uq, _rope_swap(w_uq[..., MLA_NOPE:])], axis=-1).reshape(-1, hd * hp).astype(BF16)
    wkv = lw["w_ukv"].astype(BF16)
    ql, kvl = wq.shape[0], wkv.shape[0]
    vec = lambda a: a.reshape(1, -1)
    full = lambda a: pl.BlockSpec(a.shape, lambda i: (0,) * a.ndim)
    args = (wq, wkv, vec(lw["q_a_norm"]), vec(lw["kv_a_norm"]), vec(lw["q_nope_norm"]),
            _pair_gain(lw["q_rope_norm"]), vec(lw["k_nope_norm"]), _pair_gain(lw["k_rope_norm"]))
    return pl.pallas_call(
        _mla_qkv_kernel,
        grid=(n_rows // tm,),
        in_specs=[pl.BlockSpec((tm, ql), lambda i: (i, COL_QA // ql)),
                  pl.BlockSpec((tm, kvl), lambda i: (i, COL_KVA // kvl)),
                  pl.BlockSpec((tm, LANE), lambda i: (i, COL_KR // LANE)),
                  pl.BlockSpec((tm, LANE), lambda i: (i, 0))] + [full(a) for a in args],
        out_specs=[pl.BlockSpec((hd, hp, tm), lambda i: (0, 0, i)),
                   pl.BlockSpec((hd, tm, hp), lambda i: (0, i, 0)),
                   pl.BlockSpec((hd, V_ROWS, tm), lambda i: (0, 0, i))],
        out_shape=[jax.ShapeDtypeStruct((hd, hp, n_rows), BF16),
                   jax.ShapeDtypeStruct((hd, n_rows, hp), BF16),
                   jax.ShapeDtypeStruct((hd, V_ROWS, n_rows), BF16)],
        compiler_params=_cparams(1),
        name="mla_qkv",
    )(p, p, p, cs, *args)


def _key_chunks(k_refs, v_refs, tk):
    chunks, s0 = [], 0
    for kr, vr in zip(k_refs, v_refs):
        n = kr.shape[1]
        step = min(tk, n)
        chunks += [(kr, vr, r0, step, s0 + r0) for r0 in range(0, n, step)]
        s0 += n
    return chunks


def _score_pass(qt, chunks, s_scr):
    m = None
    for kr, _, r0, rn, s0 in chunks:
        s = _dot(kr[0, r0:r0 + rn, :], qt)
        s_scr[s0:s0 + rn, :] = s
        mj = jnp.max(s, axis=0, keepdims=True)
        m = mj if m is None else jnp.maximum(m, mj)
    return m


def _value_pass(chunks, s_scr, m, o_ref):
    acc = None
    for _, vr, r0, rn, s0 in chunks:
        p = jnp.exp2((s_scr[s0:s0 + rn, :] - m).astype(BF16))
        part = _dot(vr[0, :, r0:r0 + rn], p)
        acc = part if acc is None else acc + part
    o_ref[...] = (acc[:MLA_V] / acc[MLA_V:MLA_V + 1]).astype(o_ref.dtype)


def _attn_ctx_kernel(q_ref, kc_ref, vc_ref, o_ref, s_scr, *, tk):
    chunks = _key_chunks([kc_ref], [vc_ref], tk)
    _value_pass(chunks, s_scr, _score_pass(q_ref[0], chunks, s_scr), o_ref)


def _attn_kernel(q_ref, k_ref, v_ref, kc_ref, vc_ref, o_ref, s_a, s_b, m_a, m_b, *, tk):
    i = pl.program_id(2)
    chunks = _key_chunks([kc_ref, k_ref], [vc_ref, v_ref], tk)

    @pl.when((pl.program_id(0) == 0) & (pl.program_id(1) == 0) & (i == 0))
    def _():
        s_b[...] = jnp.zeros_like(s_b)
        m_b[...] = jnp.zeros_like(m_b)

    def step(s_cur, m_cur, s_prev, m_prev):
        _value_pass(chunks, s_prev, m_prev[0:1, :], o_ref)
        m_cur[...] = jnp.broadcast_to(_score_pass(q_ref[0], chunks, s_cur), m_cur.shape)

    @pl.when(i % 2 == 0)
    def _():
        step(s_a, m_a, s_b, m_b)

    @pl.when(i % 2 == 1)
    def _():
        step(s_b, m_b, s_a, m_a)


def _attention(qt, k, vt, seq, ctx_len, n_batch, latent, tq=256, tk=1024):
    hd, hp, n_rows = qt.shape
    lat_rows = n_batch * seq
    cblk = lat_rows // ctx_len
    kc_spec = pl.BlockSpec((1, ctx_len, hp), lambda b, h, i: (h, cblk + b, 0))
    vc_spec = pl.BlockSpec((1, V_ROWS, ctx_len), lambda b, h, i: (h, 0, cblk + b))
    if latent:
        nq = seq // tq
        steps = nq + 1
        in_specs = [pl.BlockSpec((1, hp, tq), lambda b, h, i: (h, 0, b * nq + jnp.minimum(i, nq - 1))),
                    pl.BlockSpec((1, seq, hp), lambda b, h, i: (h, b, 0)),
                    pl.BlockSpec((1, V_ROWS, seq), lambda b, h, i: (h, 0, b)), kc_spec, vc_spec]
        args = (qt, k, vt, k, vt)
        out_spec = pl.BlockSpec((MLA_V, tq), lambda b, h, i: (h, b * nq + jnp.maximum(i - 1, 0)))
        out_cols, n_keys = lat_rows, seq + ctx_len
        body = functools.partial(_attn_kernel, tk=tk)
        scratch = [pltpu.VMEM((n_keys, tq), F32)] * 2 + [pltpu.VMEM((8, tq), F32)] * 2
    else:
        steps = 1
        tq = ctx_len
        in_specs = [pl.BlockSpec((1, hp, tq), lambda b, h, i: (h, 0, cblk + b)), kc_spec, vc_spec]
        args = (qt, k, vt)
        out_spec = pl.BlockSpec((MLA_V, tq), lambda b, h, i: (h, b))
        out_cols, n_keys = n_batch * ctx_len, ctx_len
        body = functools.partial(_attn_ctx_kernel, tk=tk)
        scratch = [pltpu.VMEM((n_keys, tq), F32)]
    return pl.pallas_call(
        body,
        grid=(n_batch, hd, steps),
        in_specs=in_specs,
        out_specs=out_spec,
        out_shape=jax.ShapeDtypeStruct((hd * MLA_V, out_cols), BF16),
        scratch_shapes=scratch,
        compiler_params=_cparams(3),
        name="mla_attention" if latent else "mla_attention_ctx",
    )(*args)


def _merge_kernel(*refs, lat_tiles):
    (x_ref, mod_ref, of_ref, ob_ref, g_ref, yb_ref, yc_ref, ga_ref, gb_ref, gc_ref,
     gain_ref, wa_ref, wb_ref, wc_ref, wo_ref) = refs[:15]
    ctx_refs = refs[15:-1]
    o_ref = refs[-1]
    is_ctx = pl.program_id(0) >= lat_tiles
    m = mod_ref[0]
    o = of_ref[...] + ob_ref[...]
    if ctx_refs:
        o = jnp.where(is_ctx, ctx_refs[0][...] + ctx_refs[1][...], o)
    gain = gain_ref[...]
    ya = jnp.concatenate([_rms(o[:, h * HG_D:(h + 1) * HG_D], gain) for h in range(HG_HEADS)], axis=1)
    ya = (ya * _silu(g_ref[...])).astype(BF16)
    nslab, tcols = yb_ref.shape[1], yb_ref.shape[3]
    qn = x_ref.shape[0] // yb_ref.shape[2]
    parts = [jnp.concatenate([jnp.concatenate([yb_ref[0, k, :, q, :] for k in range(nslab)], axis=1)
                              for q in range(h * qn, (h + 1) * qn)], axis=0) for h in range(tcols // qn)]
    sub = pl.program_id(0) % len(parts)
    yb = parts[0]
    for h in range(1, len(parts)):
        yb = jnp.where(sub == h, parts[h], yb)
    yc = yc_ref[...]
    if ctx_refs:
        yb = jnp.where(is_ctx, ctx_refs[2][...], yb)
        yc = jnp.where(is_ctx, ctx_refs[3][...], yc)
    mix = (_sigmoid(ga_ref[...]) * _dot(ya, wa_ref[...])
           + _sigmoid(gb_ref[...]) * _dot(yb.astype(BF16), wb_ref[...])
           + _sigmoid(gc_ref[...]) * _dot_tn(yc, wc_ref[...]))
    o_ref[...] = x_ref[...] + m[5:6] * _dot(mix.astype(BF16), wo_ref[...])


def _merge(xa, mod, o_f, o_b, p, y_b, y_c, ctx_parts, lw, n_rows, seq, n_batch, tm=512):
    d = xa.shape[1]
    hk = HG_HEADS * HG_D
    tpb = seq // tm
    lat_tiles = n_batch * tpb
    n1 = FFT_N1
    tcols = 8
    assert tm % n1 == 0 and tcols % (tm // n1) == 0 and seq % (tcols * n1) == 0
    per_blk = tcols // (tm // n1)
    y_b = y_b.reshape(n_batch, y_b.shape[1], n1, seq // n1, LANE)
    row = lambda w: pl.BlockSpec((tm, w), lambda i: (i, 0))
    pcol = lambda w, c: pl.BlockSpec((tm, w), lambda i: (i, c // w))
    full = lambda a: pl.BlockSpec(a.shape, lambda i: (0,) * a.ndim)
    ws = (lw["hg_out_norm"].reshape(1, HG_D), lw["w_br_a"].astype(BF16), lw["w_br_b"].astype(BF16),
          lw["w_br_c"].astype(BF16), lw["w_out"].astype(BF16))
    yb_spec = pl.BlockSpec((1, y_b.shape[1], n1, tcols, LANE),
                           lambda i: (jnp.minimum(i // tpb, n_batch - 1), 0, 0, (i % tpb) // per_blk, 0))
    lat = lambda i: jnp.minimum(i, lat_tiles - 1)
    late = lambda i: jnp.maximum(i - lat_tiles, 0)
    extra_specs, extra = [], []
    if ctx_parts is not None:
        extra_specs = [pl.BlockSpec((tm, hk), lambda i: (late(i), 0)),
                       pl.BlockSpec((tm, hk), lambda i: (late(i), 0)),
                       pl.BlockSpec((tm, HY_W), lambda i: (late(i), 0)),
                       pl.BlockSpec((MLA_HEADS * MLA_V, tm), lambda i: (0, late(i)))]
        extra = list(ctx_parts)
    return pl.pallas_call(
        functools.partial(_merge_kernel, lat_tiles=lat_tiles),
        grid=(n_rows // tm,),
        in_specs=[row(d), pl.BlockSpec((1, N_MOD, d), _group_map(tpb, n_batch)),
                  pl.BlockSpec((tm, hk), lambda i: (lat(i), 0)), pl.BlockSpec((tm, hk), lambda i: (lat(i), 0)),
                  pcol(hk, COL_G), yb_spec,
                  pl.BlockSpec((MLA_HEADS * MLA_V, tm), lambda i: (0, lat(i))),
                  pcol(d, COL_GA), pcol(d, COL_GB), pcol(d, COL_GC)] + [full(a) for a in ws] + extra_specs,
        out_specs=row(d),
        out_shape=jax.ShapeDtypeStruct((n_rows, d), F32),
        compiler_params=_cparams(1),
        name="merge",
    )(xa, mod, o_f, o_b, p, y_b, y_c, p, p, p, *ws, *extra)


def kernel(x, c, ctx, c_ctx, ada_w, ada_b, ffn1_norm, ffn1_w13, ffn1_w2, mix_norm, w_in, hg_lb_logits, hg_out_norm, hy_conv_w, hy_conv_b, hy_w1, hy_b1, hy_w2, hy_b2, hy_w3, hy_b3, hy_w4, hy_freq, hy_skip, q_a_norm, w_uq, kv_a_norm, w_ukv, q_nope_norm, q_rope_norm, k_nope_norm, k_rope_norm, w_br_a, w_br_b, w_br_c, w_out, ffn2_norm, ffn2_w13, ffn2_w2):
    stacked = dict(
        ada_w=ada_w, ada_b=ada_b, ffn1_norm=ffn1_norm, ffn1_w13=ffn1_w13, ffn1_w2=ffn1_w2, mix_norm=mix_norm,
        w_in=w_in, hg_out_norm=hg_out_norm, hy_conv_w=hy_conv_w, hy_conv_b=hy_conv_b, hy_w1=hy_w1, hy_b1=hy_b1,
        hy_w2=hy_w2, hy_b2=hy_b2, hy_w3=hy_w3, hy_b3=hy_b3, hy_w4=hy_w4, hy_freq=hy_freq, hy_skip=hy_skip,
        q_a_norm=q_a_norm, w_uq=w_uq, kv_a_norm=kv_a_norm, w_ukv=w_ukv, q_nope_norm=q_nope_norm,
        q_rope_norm=q_rope_norm, k_nope_norm=k_nope_norm, k_rope_norm=k_rope_norm, w_br_a=w_br_a,
        w_br_b=w_br_b, w_br_c=w_br_c, w_out=w_out, ffn2_norm=ffn2_norm, ffn2_w13=ffn2_w13, ffn2_w2=ffn2_w2)
    n_batch, seq, d = x.shape
    ctx_len = ctx.shape[1]
    depth = ada_w.shape[0]
    lat_rows, ctx_rows = n_batch * seq, n_batch * ctx_len
    all_rows = lat_rows + ctx_rows
    assert seq % 512 == 0 and ctx_rows % 512 == 0 and seq % ctx_len == 0 and seq % GRID_W == 0
    assert ctx_len % HG_CHUNK == 0 and n_batch < 8

    xa, xc = x.reshape(lat_rows, d), ctx.reshape(ctx_rows, d)
    cs = jnp.concatenate([c, c_ctx.reshape(1, d), jnp.zeros((7 - n_batch, d), F32)], axis=0)
    rope_cs = _rope_table(seq, n_batch, ctx_rows)
    zero_state = jnp.zeros((2, n_batch, HG_HEADS, HG_D, HG_D), F32)
    hk = HG_HEADS * HG_D

    for l in range(depth):
        lw = {name: val[l] for name, val in stacked.items()}
        need_ctx = l < depth - 1
        mod = _modulation(cs, lw["ada_w"], lw["ada_b"])
        xa = _half_ffn(xa, mod, lw["ffn1_norm"], lw["ffn1_w13"], lw["ffn1_w2"], 0, all_rows, seq, n_batch, xc=xc)
        xc = None
        p = _in_projection(xa, mod, lw["mix_norm"], _pack_w_in(lw["w_in"]).astype(BF16), seq, n_batch)

        ocf, ocb, s_c = _hgrn2_scan(p, hg_lb_logits, zero_state, l, lat_rows, ctx_len, n_batch)
        olf, olb, _ = _hgrn2_scan(p, hg_lb_logits, s_c, l, 0, seq, n_batch)
        y_b = _hyena(p, lw, 0, seq, n_batch)
        qt, k, vt = _mla_qkv(p, rope_cs, lw)
        y_c = _attention(qt, k, vt, seq, ctx_len, n_batch, latent=True)

        ctx_parts = None
        if need_ctx:
            ctx_parts = (ocf.reshape(ctx_rows, hk), ocb.reshape(ctx_rows, hk),
                         _hyena(p, lw, lat_rows, ctx_len, n_batch),
                         _attention(qt, k, vt, seq, ctx_len, n_batch, latent=False))
        mix_rows = all_rows if need_ctx else lat_rows
        xa = _merge(xa, mod, olf.reshape(lat_rows, hk), olb.reshape(lat_rows, hk), p, y_b, y_c, ctx_parts, lw,
                    mix_rows, seq, n_batch)
        xa = _half_ffn(xa, mod, lw["ffn2_norm"], lw["ffn2_w13"], lw["ffn2_w2"], 6, mix_rows, seq, n_batch)
    return xa[:lat_rows].reshape(n_batch, seq, d)
```

```python
import functools
import math

import numpy as np
import jax
import jax.numpy as jnp
from jax import lax
from jax.experimental import pallas as pl
from jax.experimental.pallas import tpu as pltpu

F32 = jnp.float32
BF16 = jnp.bfloat16

RMS_EPS = 1e-6
N_MOD = 9
GRID_W = 64
ROPE_THETA = 10000.0
HG_HEADS = 4
HG_D = 128
HG_CHUNK = 128
HY_W = 512
HY_TARGET = 1e-2
HY_MIN_DECAY = math.log(HY_TARGET) / 1.5
HY_MAX_DECAY = math.log(HY_TARGET) / 0.3
HY_SHIFT = 0.05
MLA_HEADS = 4
MLA_NOPE = 128
MLA_ROPE = 64
MLA_V = 128
MLA_QK = MLA_NOPE + MLA_ROPE
MLA_SCALE = MLA_QK ** -0.5
MLA_HEAD_PAD = 256
V_ROWS = MLA_V + 16
LOG2_E = math.log2(math.e)
FFT_N1 = 128
LANE = 128

VMEM_LIMIT = 52 * 1024 * 1024

COL_GA, COL_GB, COL_GC = 0, 1024, 2048
COL_Q, COL_ZF, COL_ZB, COL_IV, COL_G = 3072, 3584, 4096, 4608, 5120
COL_HY = 5632
COL_QA, COL_KVA, COL_KR = 7168, 7424, 7552
IN_PACKED = 7680


def _cparams(n_axes):
    return pltpu.CompilerParams(dimension_semantics=("arbitrary",) * n_axes,
                                vmem_limit_bytes=VMEM_LIMIT)


def _dot(a, b):
    return jnp.dot(a, b, preferred_element_type=F32)


def _dot_nt(a, b):
    return lax.dot_general(a, b, (((1,), (1,)), ((), ())), preferred_element_type=F32)


def _dot_tn(a, b):
    return lax.dot_general(a, b, (((0,), (0,)), ((), ())), preferred_element_type=F32)


def _sigmoid(x):
    return 1.0 / (1.0 + jnp.exp(-x))


def _silu(x):
    return x * _sigmoid(x)


def _rms(x, gain):
    return x * lax.rsqrt(jnp.mean(x * x, axis=-1, keepdims=True) + RMS_EPS) * gain


def _mod_kernel(c_ref, w_ref, b_ref, o_ref):
    o_ref[...] = _dot(_silu(c_ref[...]).astype(BF16), w_ref[...]) + b_ref[...]


def _modulation(cs, ada_w, ada_b):
    g, d = cs.shape
    n = ada_w.shape[1]
    tn = n // 4
    out = pl.pallas_call(
        _mod_kernel,
        grid=(n // tn,),
        in_specs=[pl.BlockSpec((g, d), lambda j: (0, 0)),
                  pl.BlockSpec((d, tn), lambda j: (0, j)),
                  pl.BlockSpec((1, tn), lambda j: (0, j))],
        out_specs=pl.BlockSpec((g, tn), lambda j: (0, j)),
        out_shape=jax.ShapeDtypeStruct((g, n), F32),
        compiler_params=_cparams(1),
        name="modulation",
    )(cs, ada_w.astype(BF16), ada_b.reshape(1, n))
    return out.reshape(g, N_MOD, d)


def _ffn_kernel(*refs, idx, ff, ck, lat_tiles):
    x_ref, mod_ref, g_ref, w13_ref, w2_ref = refs[:5]
    o_ref = refs[-1]
    x = x_ref[...]
    if len(refs) == 7:
        x = jnp.where(pl.program_id(0) >= lat_tiles, refs[5][...], x)
    m = mod_ref[0]
    h = (_rms(x, g_ref[...]) * (1.0 + m[idx + 1:idx + 2]) + m[idx:idx + 1]).astype(BF16)
    acc = jnp.zeros(x.shape, F32)
    for c0 in range(0, ff, ck):
        c1 = min(c0 + ck, ff)
        a = _dot(h, w13_ref[:, c0:c1])
        b = _dot(h, w13_ref[:, ff + c0:ff + c1])
        acc = acc + _dot((_silu(a) * b).astype(BF16), w2_ref[c0:c1, :])
    o_ref[...] = x + (0.5 * m[idx + 2:idx + 3]) * acc


def _group_map(tiles_per_batch, n_batch):
    return lambda i: (jnp.minimum(i // tiles_per_batch, n_batch), 0, 0)


def _half_ffn(xa, mod, gain, w13, w2, idx, n_rows, seq, n_batch, xc=None, tm=512, ck=512):
    d = xa.shape[1]
    ff = w2.shape[0]
    lat_tiles = n_batch * seq // tm
    const = dict(pipeline_mode=pl.Buffered(1))
    x_specs, xs = [pl.BlockSpec((tm, d), lambda i: (jnp.minimum(i, xa.shape[0] // tm - 1), 0))], [xa]
    tail_specs, tail = [], []
    if xc is not None:
        tail_specs, tail = [pl.BlockSpec((tm, d), lambda i: (jnp.maximum(i - lat_tiles, 0), 0))], [xc]
    return pl.pallas_call(
        functools.partial(_ffn_kernel, idx=idx, ff=ff, ck=ck, lat_tiles=lat_tiles),
        grid=(n_rows // tm,),
        in_specs=x_specs + [pl.BlockSpec((1, N_MOD, d), _group_map(seq // tm, n_batch)),
                            pl.BlockSpec((1, d), lambda i: (0, 0)),
                            pl.BlockSpec((d, 2 * ff), lambda i: (0, 0), **const),
                            pl.BlockSpec((ff, d), lambda i: (0, 0), **const)] + tail_specs,
        out_specs=pl.BlockSpec((tm, d), lambda i: (i, 0)),
        out_shape=jax.ShapeDtypeStruct((n_rows, d), F32),
        compiler_params=_cparams(1),
        name="half_ffn",
    )(*xs, mod, gain.reshape(1, d), w13.astype(BF16), w2.astype(BF16), *tail)


def _inproj_kernel(x_ref, mod_ref, g_ref, w_ref, o_ref):
    m = mod_ref[0]
    h = (_rms(x_ref[...], g_ref[...]) * (1.0 + m[4:5]) + m[3:4]).astype(BF16)
    o_ref[...] = _dot(h, w_ref[...])


def _pack_w_in(w_in):
    d = w_in.shape[0]
    hk = HG_HEADS * HG_D
    sizes = (hk, hk, hk, hk, hk, 3 * HY_W, 256, 128, MLA_ROPE, d, d, d)
    offs = np.cumsum((0,) + sizes)
    q, zf, zb, iv, g, hy, qa, kva, kr, ga, gb, gc = (w_in[:, offs[i]:offs[i + 1]] for i in range(12))
    return jnp.concatenate([ga, gb, gc, q, zf, zb, iv, g, hy, qa, kva, kr, _rope_swap(kr)], axis=1)


def _rope_swap(a):
    q = MLA_ROPE // 4
    return jnp.concatenate([a[..., q:2 * q], a[..., :q], a[..., 3 * q:], a[..., 2 * q:3 * q]], axis=-1)


def _in_projection(xa, mod, gain, w_packed, seq, n_batch, tm=512, tn=2560):
    n_rows, d = xa.shape
    n = w_packed.shape[1]
    return pl.pallas_call(
        _inproj_kernel,
        grid=(n // tn, n_rows // tm),
        in_specs=[pl.BlockSpec((tm, d), lambda j, i: (i, 0)),
                  pl.BlockSpec((1, N_MOD, d), lambda j, i: (jnp.minimum(i // (seq // tm), n_batch), 0, 0)),
                  pl.BlockSpec((1, d), lambda j, i: (0, 0)),
                  pl.BlockSpec((d, tn), lambda j, i: (0, j))],
        out_specs=pl.BlockSpec((tm, tn), lambda j, i: (i, j)),
        out_shape=jax.ShapeDtypeStruct((n_rows, n), F32),
        compiler_params=_cparams(2),
        name="in_projection",
    )(xa, mod, gain.reshape(1, d), w_packed)


def _hgrn2_tables(c, rev):
    t = np.arange(c)[:, None]
    u = np.arange(c)[None, :]
    mats = [(u <= t), (u > t)]
    masks = [(t == u)]
    h = c // 2
    while h >= 1:
        mid = (t // (2 * h)) * (2 * h) + h
        mats.append(np.where(t >= mid, (u >= mid) & (u <= t), (u >= t + 1) & (u <= mid - 1)))
        mid_s = (u // (2 * h)) * (2 * h) + h
        masks.append((t // (2 * h) == u // (2 * h)) & (u < mid_s) & (t >= mid))
        h //= 2
    mats = np.stack([m.astype(np.float32) for m in mats])
    masks = np.stack([m.astype(np.float32) for m in masks])
    if rev:
        mats = mats[:, ::-1, ::-1]
        masks = masks[:, ::-1, ::-1]
    mats = np.ascontiguousarray(mats).reshape(-1, c)
    return (jnp.asarray(np.concatenate([mats, mats], axis=1), BF16),
            jnp.asarray(np.ascontiguousarray(masks), F32))


HG_PAIR = 2


def _hgrn2_chunk(q, z, v, lg, dst_ref, msk_ref, st_ref, o_ref, ci, *, layer, rev, n_levels):
    c = q.shape[0]
    hk = HG_HEADS * HG_D
    e = jnp.exp(lg - jnp.max(lg, axis=0, keepdims=True))
    sm = e / jnp.sum(e, axis=0, keepdims=True)
    lb = jnp.zeros((1, hk), F32)
    for i in range(1, layer + 1):
        lb = lb + sm[i:i + 1]

    f = lb + (1.0 - lb) * _sigmoid(z)
    kk = 1.0 - f
    g = jnp.log(f)
    g_hi = g.astype(BF16)
    g_lo = (g - g_hi.astype(F32)).astype(BF16)
    dg = _dot(dst_ref[...], jnp.concatenate([g_hi, g_lo], axis=0))
    qs = _silu(q) * HG_D ** -0.5
    last = 0 if rev else c - 1

    for h in range(HG_HEADS):
        hs = slice(h * HG_D, (h + 1) * HG_D)
        qh, kh, vh = qs[:, hs], kk[:, hs], v[:, hs].astype(BF16)
        b_in = dg[0:c, hs]
        e_out = jnp.exp(dg[c:2 * c, hs])
        a = _dot_nt(qh.astype(BF16), kh.astype(BF16)) * msk_ref[0]
        for l in range(1, n_levels + 1):
            el = jnp.exp(dg[(1 + l) * c:(2 + l) * c, hs])
            a = a + _dot_nt((qh * el).astype(BF16), (kh * el).astype(BF16)) * msk_ref[l]
        st = st_ref[ci, h]
        o = _dot_nt((qh * jnp.exp(b_in)).astype(BF16), st.astype(BF16)) + _dot(a.astype(BF16), vh)
        o_ref[:, hs] = o
        st_ref[ci, h] = jnp.exp(b_in[last:last + 1, :]) * st + _dot_tn(vh, (kh * e_out).astype(BF16))


def _hgrn2_kernel(*refs, layer, n_levels):
    n_in = 6 * HG_PAIR
    ins, (lg_ref, dstf_ref, mskf_ref, dstb_ref, mskb_ref, s0_ref) = refs[:n_in], refs[n_in:n_in + 6]
    of_ref, ob_ref, sf_ref, st_ref = refs[n_in + 6:]
    j = pl.program_id(1)

    @pl.when(j == 0)
    def _():
        st_ref[...] = s0_ref[...].reshape(st_ref.shape)

    for e in range(HG_PAIR):
        qf, zf, vf, qb, zb, vb = (r[...] for r in ins[6 * e:6 * e + 6])
        _hgrn2_chunk(qf, zf, vf, lg_ref[0], dstf_ref, mskf_ref, st_ref, of_ref.at[e], e,
                     layer=layer, rev=False, n_levels=n_levels)
        _hgrn2_chunk(qb, zb, vb, lg_ref[1], dstb_ref, mskb_ref, st_ref, ob_ref.at[e], HG_PAIR + e,
                     layer=layer, rev=True, n_levels=n_levels)

    @pl.when(j == pl.num_programs(1) - 1)
    def _():
        sf_ref[...] = st_ref[...].reshape(sf_ref.shape)


def _hgrn2_scan(p, logits, s0, layer, row0, seq, n_batch):
    c = HG_CHUNK
    hk = HG_HEADS * HG_D
    nc = seq // c
    base = row0 // c
    n_levels = int(math.log2(c))
    tabs = _hgrn2_tables(c, False) + _hgrn2_tables(c, True)
    assert n_batch % HG_PAIR == 0

    def prow(w, e, rev):
        return pl.BlockSpec((c, hk), lambda bp, j: (base + (bp * HG_PAIR + e) * nc + (nc - 1 - j if rev else j),
                                                   w // hk))

    in_specs = []
    for e in range(HG_PAIR):
        in_specs += [prow(COL_Q, e, False), prow(COL_ZF, e, False), prow(COL_IV, e, False),
                     prow(COL_Q, e, True), prow(COL_ZB, e, True), prow(COL_IV, e, True)]
    full = lambda a: pl.BlockSpec(a.shape, lambda bp, j: (0,) * a.ndim)
    st_spec = pl.BlockSpec((2, HG_PAIR, HG_HEADS, HG_D, HG_D), lambda bp, j: (0, bp, 0, 0, 0))
    return pl.pallas_call(
        functools.partial(_hgrn2_kernel, layer=layer, n_levels=n_levels),
        grid=(n_batch // HG_PAIR, nc),
        in_specs=in_specs + [full(logits)] + [full(t) for t in tabs] + [st_spec],
        out_specs=[pl.BlockSpec((HG_PAIR, c, hk), lambda bp, j: (bp, j, 0)),
                   pl.BlockSpec((HG_PAIR, c, hk), lambda bp, j: (bp, nc - 1 - j, 0)),
                   st_spec],
        out_shape=[jax.ShapeDtypeStruct((n_batch, seq, hk), F32),
                   jax.ShapeDtypeStruct((n_batch, seq, hk), F32),
                   jax.ShapeDtypeStruct((2, n_batch, HG_HEADS, HG_D, HG_D), F32)],
        scratch_shapes=[pltpu.VMEM((2 * HG_PAIR, HG_HEADS, HG_D, HG_D), F32)],
        compiler_params=_cparams(2),
        name="hgrn2_scan",
    )(*([p] * (6 * HG_PAIR)), logits, *tabs, s0)


def _shortconv_kernel(u_ref, w_ref, b_ref, o_ref):
    u = u_ref[...]
    n = u.shape[0]
    row = lax.broadcasted_iota(jnp.int32, u.shape, 0)
    prev = jnp.where(row == 0, 0.0, pltpu.roll(u, 1, 0))
    nxt = jnp.where(row == n - 1, 0.0, pltpu.roll(u, n - 1, 0))
    w = w_ref[...]
    o_ref[0, 0, 0] = prev * w[0:1] + u * w[1:2] + nxt * w[2:3] + b_ref[...]


def _shortconv_t1major_kernel(u_ref, w_ref, b_ref, o_ref, u_scr, *, n1):
    half = u_ref.shape[0] // n1
    w = w_ref[...]
    b = b_ref[...]
    pitch = n1 + 8

    def copy(g, carry):
        u_scr[pl.ds(pl.multiple_of(g * pitch, 8), n1), :] = u_ref[pl.ds(pl.multiple_of(g * n1, 8), n1), :]
        return carry

    lax.fori_loop(0, half, copy, 0, unroll=8)
    row = lax.broadcasted_iota(jnp.int32, (half, LANE), 0)
    col = lambda t1: u_scr[pl.ds(t1, half, stride=pitch), :]
    before = jnp.where(row == 0, 0.0, pltpu.roll(col(n1 - 1), 1, 0))
    after = jnp.where(row == half - 1, 0.0, pltpu.roll(col(0), half - 1, 0))

    def body(t1, carry):
        prev, cur = carry
        nxt = jnp.where(t1 == n1 - 1, after, col(jnp.minimum(t1 + 1, n1 - 1)))
        o_ref[0, 0, 0, pl.ds(pl.multiple_of(t1 * half, 8), half), :] = prev * w[0:1] + cur * w[1:2] + nxt * w[2:3] + b
        return cur, nxt

    lax.fori_loop(0, n1, body, (before, col(0)), unroll=16)


def _short_conv(p, w, b, row0, seq, n_batch, n1):
    nb = 3 * HY_W // LANE
    per = HY_W // LANE
    body, scratch = _shortconv_kernel, []
    if n1 > 1:
        body = functools.partial(_shortconv_t1major_kernel, n1=n1)
        scratch = [pltpu.VMEM((seq // n1 * (n1 + 8), LANE), F32)]
    return pl.pallas_call(
        body,
        grid=(n_batch, nb),
        in_specs=[pl.BlockSpec((seq, LANE), lambda bi, j: (row0 // seq + bi, COL_HY // LANE + j)),
                  pl.BlockSpec((3, LANE), lambda bi, j: (0, j)),
                  pl.BlockSpec((1, LANE), lambda bi, j: (0, j))],
        out_specs=pl.BlockSpec((1, 1, 1, seq, LANE), lambda bi, j: (j // per, bi, j % per, 0, 0)),
        out_shape=jax.ShapeDtypeStruct((3, n_batch, per, seq, LANE), F32),
        scratch_shapes=scratch,
        compiler_params=_cparams(2),
        name="hyena_short_conv",
    )(p, w, b.reshape(1, -1))


def _hy_filter_kernel(emb_ref, embr_ref, w1_ref, b1_ref, w2_ref, b2_ref, w3_ref, b3_ref, w4_ref, fr_ref,
                      dl_ref, o_ref, nrm_ref, *, seq, group):
    i = pl.program_id(0)
    hp = lax.Precision.HIGHEST
    fr = fr_ref[...]

    def mlp(emb):
        hid = jnp.sin(fr * (jnp.dot(emb, w1_ref[...], precision=hp) + b1_ref[...]))
        hid = jnp.sin(fr * (jnp.dot(hid, w2_ref[...], precision=hp) + b2_ref[...]))
        return jnp.sin(fr * (jnp.dot(hid, w3_ref[...], precision=hp) + b3_ref[...]))

    tl = emb_ref.shape[0]
    n = w4_ref.shape[1]
    half = n // 2
    r = lax.broadcasted_iota(jnp.int32, (tl, half), 0) + i * tl
    per = seq // group
    pos = (r >> (per.bit_length() - 1)) + group * (r & (per - 1))
    posr = (pos & ~(group - 1)) + ((group - (pos & (group - 1))) & (group - 1))
    for side, (emb, pp) in enumerate(((emb_ref[...], pos), (embr_ref[...], posr))):
        h = jnp.dot(mlp(emb), w4_ref[:, side * half:(side + 1) * half], precision=hp)
        t = pp.astype(F32) * (1.0 / (seq - 1))
        h = h * (jnp.exp(-t * dl_ref[...]) + HY_SHIFT)
        if side == 1:
            h = jnp.where(pp == 0, 0.0, h)
        for k in range(half // LANE):
            o_ref[side * (half // LANE) + k] = h[:, k * LANE:(k + 1) * LANE]

        @pl.when(i == 0)
        def _():
            nrm_ref[:, side * half:(side + 1) * half] = jnp.zeros((1, half), F32)

        nrm_ref[:, side * half:(side + 1) * half] += jnp.sum(jnp.abs(h), axis=0, keepdims=True)


def _hyena_filter_taps(seq, w1, b1, w2, b2, w3, b3, w4, freq, group):
    fh = w1.shape[1]
    n_emb = w1.shape[0]
    bands_n = (n_emb - 1) // 2
    tt = np.linspace(0.0, 1.0, seq, dtype=np.float32)[:, None].astype(np.float64)
    ww = (2.0 * math.pi / seq) * np.arange(seq, dtype=np.float64)[:, None]
    bands = np.linspace(1e-4, bands_n - 1, bands_n, dtype=np.float32)[None, :].astype(np.float64)
    emb = np.concatenate([tt, np.cos(bands * ww), -np.sin(bands * ww)], axis=-1)
    emb = np.pad(emb, ((0, 0), (0, LANE - n_emb))).astype(np.float32)
    assert seq % group == 0 and group & (group - 1) == 0 and (seq // group) & (seq // group - 1) == 0
    r = np.arange(seq)
    pos = r // (seq // group) + group * (r % (seq // group))
    embr = emb[(pos // group) * group + (group - pos % group) % group]
    emb = emb[pos]
    w1p = jnp.pad(w1, ((0, LANE - n_emb), (0, 0)))
    deltas = np.abs(np.linspace(HY_MIN_DECAY, HY_MAX_DECAY, HY_W, dtype=np.float32))
    deltas = np.tile(deltas, 2)[None, :]
    n = w4.shape[1]
    tl = min(seq, 512)
    full = lambda a: pl.BlockSpec(a.shape, lambda i: (0,) * a.ndim)
    args = (w1p, b1.reshape(1, fh), w2, b2.reshape(1, fh), w3, b3.reshape(1, fh), w4, freq.reshape(1, fh),
            jnp.asarray(deltas))
    return pl.pallas_call(
        functools.partial(_hy_filter_kernel, seq=seq, group=group),
        grid=(seq // tl,),
        in_specs=[pl.BlockSpec((tl, LANE), lambda i: (i, 0))] * 2 + [full(a) for a in args],
        out_specs=[pl.BlockSpec((n // LANE, tl, LANE), lambda i: (0, i, 0)),
                   pl.BlockSpec((1, n), lambda i: (0, 0))],
        out_shape=[jax.ShapeDtypeStruct((n // LANE, seq, LANE), F32), jax.ShapeDtypeStruct((1, n), F32)],
        compiler_params=_cparams(1),
        name="hyena_filter_taps",
    )(jnp.asarray(emb), jnp.asarray(embr), *args)


def _fft_split(seq):
    n = 2 * seq
    n1 = FFT_N1 if n > 1024 else 1
    return n, n1, n // n1


def _cis(idx, n):
    ph = 2.0 * np.pi * (idx % n) / n
    return np.cos(ph), -np.sin(ph)


def _dft_tables_short(seq):
    n = 2 * seq
    cr, ci = _cis(np.arange(n)[:, None] * np.arange(seq)[None, :], n)
    w_fwd = np.concatenate([cr, ci], axis=0)
    w_inv = np.concatenate([cr.T, ci.T], axis=1) / n
    return dict(w_fwd=jnp.asarray(w_fwd, BF16), w_inv=jnp.asarray(w_inv, BF16))


def _dft_tables_long(seq):
    n, n1, n2 = _fft_split(seq)
    t1 = np.arange(n1)[:, None, None]
    f2 = np.arange(n2)[None, :, None]
    t2 = np.arange(n2 // 2)[None, None, :]
    cr, ci = _cis(f2 * (t1 + n1 * t2), n)
    w1 = np.concatenate([cr, ci], axis=1)
    w4 = np.concatenate([np.swapaxes(cr, 1, 2), np.swapaxes(ci, 1, 2)], axis=2) / n
    j = (n1 - t1) % n1 + n1 * t2
    br, bi = _cis(f2 * (n - j), n)
    w1f = np.concatenate([np.concatenate([cr, br], axis=2), np.concatenate([ci, bi], axis=2)], axis=1)
    gr, gi = _cis(np.arange(n1)[:, None] * np.arange(n1)[None, :], n1)
    wd = np.concatenate([np.concatenate([gr, -gi], axis=1), np.concatenate([gi, gr], axis=1)], axis=0)
    return dict(w1=jnp.asarray(np.swapaxes(w1, 1, 2), BF16), w4=jnp.asarray(w4, BF16),
                w1f=jnp.asarray(np.swapaxes(w1f, 1, 2), BF16),
                wd=jnp.asarray(wd, BF16), wdi=jnp.asarray(wd.T, BF16))


def _dft_rows_kernel(w_ref, x_ref, o_ref):
    o_ref[0] = _dot(w_ref[...], x_ref[0].astype(BF16)).astype(o_ref.dtype)


def _dft_rows(w, x, tn):
    nb, k, cols = x.shape
    m = w.shape[0]
    return pl.pallas_call(
        _dft_rows_kernel,
        grid=(nb, cols // tn),
        in_specs=[pl.BlockSpec((m, k), lambda b, j: (0, 0)),
                  pl.BlockSpec((1, k, tn), lambda b, j: (b, 0, j))],
        out_specs=pl.BlockSpec((1, m, tn), lambda b, j: (b, 0, j)),
        out_shape=jax.ShapeDtypeStruct((nb, m, cols), BF16),
        compiler_params=_cparams(2),
        name="hyena_dft_rows",
    )(w, x)


def _idft_gate_kernel(w_ref, b_ref, xg_ref, z_ref, sk_ref, o_ref):
    y = _dot(w_ref[...], b_ref[0])
    z = z_ref[0]
    o_ref[0] = xg_ref[0] * (y + z * sk_ref[...])


def _idft_gate(w, bc, xg, z, skip_t, tn):
    nb, k, cols = bc.shape
    m = w.shape[0]
    return pl.pallas_call(
        _idft_gate_kernel,
        grid=(nb, cols // tn),
        in_specs=[pl.BlockSpec((m, k), lambda b, j: (0, 0)),
                  pl.BlockSpec((1, k, tn), lambda b, j: (b, 0, j)),
                  pl.BlockSpec((1, m, tn), lambda b, j: (b, 0, j)),
                  pl.BlockSpec((1, m, tn), lambda b, j: (b, 0, j)),
                  pl.BlockSpec((1, tn), lambda b, j: (0, j))],
        out_specs=pl.BlockSpec((1, m, tn), lambda b, j: (b, 0, j)),
        out_shape=jax.ShapeDtypeStruct((nb, m, cols), F32),
        compiler_params=_cparams(2),
        name="hyena_idft_gate",
    )(w, bc, xg, z, skip_t)


def _spec_combine_kernel(af_ref, ab_ref, nf_ref, nb_ref, o_ref):
    inv = 1.0 / (nf_ref[...] + nb_ref[...])
    n = o_ref.shape[1]
    o_ref[0] = (af_ref[0, :n].astype(F32) + ab_ref[0, :n].astype(F32)) * inv
    o_ref[1] = (af_ref[0, n:].astype(F32) - ab_ref[0, n:].astype(F32)) * inv


def _spec_mul_kernel(a_ref, k_ref, o_ref):
    n = k_ref.shape[1]
    xr, xi = a_ref[0, :n].astype(F32), a_ref[0, n:].astype(F32)
    kr, ki = k_ref[0], k_ref[1]
    o_ref[0, :n] = (xr * kr - xi * ki).astype(o_ref.dtype)
    o_ref[0, n:] = (xr * ki + xi * kr).astype(o_ref.dtype)


def _hyena_short(seq, taps, nrm, uc, skip):
    n = 2 * seq
    uc = jnp.swapaxes(uc, 2, 3).reshape(3, uc.shape[1], seq, HY_W)
    tabs = _dft_tables_short(seq)
    cf = taps.shape[1]
    c = HY_W
    a = _dft_rows(tabs["w_fwd"], taps.reshape(1, seq, cf), tn=cf)
    nblk = cf // 2 // c
    spec = pl.pallas_call(
        _spec_combine_kernel,
        grid=(nblk,),
        in_specs=[pl.BlockSpec((1, 2 * n, c), lambda j: (0, 0, j)),
                  pl.BlockSpec((1, 2 * n, c), lambda j: (0, 0, j + nblk)),
                  pl.BlockSpec((1, c), lambda j: (0, j)),
                  pl.BlockSpec((1, c), lambda j: (0, j + nblk))],
        out_specs=pl.BlockSpec((2, n, c), lambda j: (0, 0, j)),
        out_shape=jax.ShapeDtypeStruct((2, n, cf // 2), F32),
        compiler_params=_cparams(1),
        name="hyena_spec_combine",
    )(a, a, nrm, nrm)
    nb = uc.shape[1]
    z = uc[2]
    for order in range(2):
        a = _dft_rows(tabs["w_fwd"], z, tn=c)
        bc = pl.pallas_call(
            _spec_mul_kernel,
            grid=(nb,),
            in_specs=[pl.BlockSpec((1, 2 * n, c), lambda b: (b, 0, 0)),
                      pl.BlockSpec((2, n, c), lambda b: (0, 0, order))],
            out_specs=pl.BlockSpec((1, 2 * n, c), lambda b: (b, 0, 0)),
            out_shape=jax.ShapeDtypeStruct((nb, 2 * n, c), BF16),
            compiler_params=_cparams(1),
            name="hyena_spec_mul",
        )(a, spec)
        z = _idft_gate(tabs["w_inv"], bc, uc[order], z, skip[order].reshape(1, c), tn=c)
    return z


HY_SLABS = 2


def _slab_pitch(n2):
    return n2 + 8


def _pack_c(re, im):
    hi = lax.bitcast_convert_type(re.astype(BF16).astype(F32), jnp.uint32)
    lo = lax.bitcast_convert_type(im.astype(BF16).astype(F32), jnp.uint32)
    return hi | (lo >> 16)


def _unpack_c(w):
    re = lax.bitcast_convert_type(w & jnp.uint32(0xFFFF0000), F32)
    im = lax.bitcast_convert_type(w << 16, F32)
    return re.astype(BF16), im.astype(BF16)


def _store_slab(a_scr, row, words):
    for s in range(HY_SLABS):
        a_scr[s, pl.ds(row, words.shape[0]), :] = words[:, s * LANE:(s + 1) * LANE]


def _stage1(a_scr, w_ref, xs_of, i, tb, n2, pitch):
    for j in range(tb):
        a = _dot_tn(w_ref[j], xs_of(j))
        _store_slab(a_scr, pl.multiple_of((i * tb + j) * pitch, 8), _pack_c(a[:n2], a[n2:]))


def _stage2(a_scr, wd_ref, f2, n1, pitch):
    w = jnp.concatenate([a_scr[s, pl.ds(f2, n1, stride=pitch), :] for s in range(HY_SLABS)], axis=1)
    re, im = _unpack_c(w)
    return _dot(wd_ref[...], jnp.concatenate([re, im], axis=0))


def _time_col(ref, j, half):
    lead = (0,) * (len(ref.shape) - 3)
    return jnp.concatenate([ref[lead + (s, slice(j * half, (j + 1) * half), slice(None))]
                            for s in range(HY_SLABS)], axis=1)


def _hy_spectrum_kernel(hf_ref, hb_ref, nf_ref, nb_ref, w1f_ref, wd_ref, o_ref, a_scr, *, n1, n2, tb, fb):
    i = pl.program_id(1)
    pitch = _slab_pitch(n2)
    half = n2 // 2
    nt = n1 // tb

    @pl.when(i < nt)
    def _():
        def xs_of(j):
            return jnp.concatenate([_time_col(hf_ref, j, half), _time_col(hb_ref, j, half)], axis=0).astype(BF16)
        _stage1(a_scr, w1f_ref, xs_of, i, tb, n2, pitch)

    @pl.when(i >= nt)
    def _():
        inv = 1.0 / (nf_ref[...] + nb_ref[...])
        for jj in range(fb):
            x = _stage2(a_scr, wd_ref, (i - nt) * fb + jj, n1, pitch)
            o_ref[0, 0, jj] = x[:n1] * inv
            o_ref[0, 1, jj] = x[n1:] * inv


def _hy_fftconv_kernel(z_ref, xg_ref, k_ref, sk_ref, w1_ref, wd_ref, wdi_ref, w4_ref, o_ref, a_scr,
                       *, n1, n2, tb, fb):
    i = pl.program_id(2)
    pitch = _slab_pitch(n2)
    half = n2 // 2
    nt, nf = n1 // tb, n2 // fb

    @pl.when(i < nt)
    def _():
        _stage1(a_scr, w1_ref, lambda j: _time_col(z_ref, j, half).astype(BF16), i, tb, n2, pitch)

    @pl.when((i >= nt) & (i < nt + nf))
    def _():
        for jj in range(fb):
            f2 = (i - nt) * fb + jj
            x = _stage2(a_scr, wd_ref, f2, n1, pitch)
            xr, xi = x[:n1], x[n1:]
            kr, ki = k_ref[0, 0, jj], k_ref[0, 1, jj]
            y = jnp.concatenate([xr * kr - xi * ki, xr * ki + xi * kr], axis=0).astype(BF16)
            bv = _dot(wdi_ref[...], y)
            words = _pack_c(bv[:n1], bv[n1:])
            for s in range(HY_SLABS):
                a_scr[s, pl.ds(f2, n1, stride=pitch), :] = words[:, s * LANE:(s + 1) * LANE]

    @pl.when(i >= nt + nf)
    def _():
        sk = sk_ref[...]
        for j in range(tb):
            row = pl.multiple_of(((i - nt - nf) * tb + j) * pitch, 8)
            w = jnp.concatenate([a_scr[s, pl.ds(row, n2), :] for s in range(HY_SLABS)], axis=1)
            re, im = _unpack_c(w)
            y = _dot(w4_ref[j], jnp.concatenate([re, im], axis=0))
            out = _time_col(xg_ref, j, half) * (y + _time_col(z_ref, j, half) * sk)
            for s in range(HY_SLABS):
                o_ref[0, s, j * half:(j + 1) * half, :] = out[:, s * LANE:(s + 1) * LANE]


def _hyena_long(seq, taps, nrm, uc, skip, tb=32, fb=16):
    n, n1, n2 = _fft_split(seq)
    tabs = _dft_tables_long(seq)
    pitch = _slab_pitch(n2)
    half = n2 // 2
    cb = HY_SLABS * LANE
    tb, fb = min(tb, n1), min(fb, n2)
    nt, nf = n1 // tb, n2 // fb
    cf = taps.shape[0] * LANE
    ngrp = cf // 2 // cb
    scratch = [pltpu.VMEM((HY_SLABS, n1 * pitch, LANE), jnp.uint32)]
    tcol = lambda i: jnp.minimum(i, nt - 1)
    spec = pl.pallas_call(
        functools.partial(_hy_spectrum_kernel, n1=n1, n2=n2, tb=tb, fb=fb),
        grid=(ngrp, nt + nf),
        in_specs=[pl.BlockSpec((HY_SLABS, tb * half, LANE), lambda g, i: (g, tcol(i), 0)),
                  pl.BlockSpec((HY_SLABS, tb * half, LANE), lambda g, i: (ngrp + g, tcol(i), 0)),
                  pl.BlockSpec((1, cb), lambda g, i: (0, g)),
                  pl.BlockSpec((1, cb), lambda g, i: (0, ngrp + g)),
                  pl.BlockSpec((tb, n2, 2 * n2), lambda g, i: (tcol(i), 0, 0)),
                  pl.BlockSpec(tabs["wd"].shape, lambda g, i: (0, 0))],
        out_specs=pl.BlockSpec((1, 2, fb, n1, cb), lambda g, i: (g, 0, jnp.maximum(i - nt, 0), 0, 0)),
        out_shape=jax.ShapeDtypeStruct((ngrp, 2, n2, n1, cb), F32),
        scratch_shapes=scratch,
        compiler_params=_cparams(2),
        name="hyena_filter_spectrum",
    )(taps, taps, nrm, nrm, tabs["w1f"], tabs["wd"])
    nb, nblk = uc.shape[1], uc.shape[2]
    ngc = nblk // HY_SLABS
    z, zsel = uc, 2

    def tblk(b, g, i):
        return (b, g, jnp.where(i < nt, i, jnp.maximum(i - nt - nf, 0)), 0)

    def tblk_late(b, g, i):
        return (b, g, jnp.maximum(i - nt - nf, 0), 0)

    for order in range(2):
        stacked = lambda sel, f: (lambda b, g, i: (sel,) + f(b, g, i))
        z = pl.pallas_call(
            functools.partial(_hy_fftconv_kernel, n1=n1, n2=n2, tb=tb, fb=fb),
            grid=(nb, ngc, 2 * nt + nf),
            in_specs=[pl.BlockSpec((1, 1, HY_SLABS, tb * half, LANE), stacked(zsel, tblk)),
                      pl.BlockSpec((1, 1, HY_SLABS, tb * half, LANE), stacked(order, tblk_late)),
                      pl.BlockSpec((1, 2, fb, n1, cb),
                                   lambda b, g, i: (order * ngc + g, 0, jnp.clip(i - nt, 0, nf - 1), 0, 0)),
                      pl.BlockSpec((1, cb), lambda b, g, i: (0, g)),
                      pl.BlockSpec((tb, half, 2 * n2), lambda b, g, i: (tcol(i), 0, 0)),
                      pl.BlockSpec(tabs["wd"].shape, lambda b, g, i: (0, 0)),
                      pl.BlockSpec(tabs["wdi"].shape, lambda b, g, i: (0, 0)),
                      pl.BlockSpec((tb, half, 2 * n2), lambda b, g, i: (jnp.maximum(i - nt - nf, 0), 0, 0))],
            out_specs=pl.BlockSpec((1, HY_SLABS, tb * half, LANE), tblk_late),
            out_shape=jax.ShapeDtypeStruct((nb, nblk, seq, LANE), F32),
            scratch_shapes=scratch,
            compiler_params=_cparams(3),
            name="hyena_fft_conv",
        )(z, uc, spec, skip[order].reshape(1, HY_W), tabs["w1"], tabs["wd"], tabs["wdi"], tabs["w4"])
        z, zsel = z.reshape((1,) + z.shape), 0
    return z[0]


def _hyena(p, lw, row0, seq, n_batch):
    long = _fft_split(seq)[1] > 1
    n1 = FFT_N1 if long else 1
    taps, nrm = _hyena_filter_taps(seq, lw["hy_w1"], lw["hy_b1"], lw["hy_w2"], lw["hy_b2"], lw["hy_w3"],
                                   lw["hy_b3"], lw["hy_w4"], lw["hy_freq"], n1)
    uc = _short_conv(p, lw["hy_conv_w"], lw["hy_conv_b"], row0, seq, n_batch, n1)
    if long:
        return _hyena_long(seq, taps, nrm, uc, lw["hy_skip"])
    taps = jnp.swapaxes(taps, 0, 1).reshape(seq, -1)
    return _hyena_short(seq, taps, nrm, uc, lw["hy_skip"]).reshape(n_batch * seq, HY_W)


def _mla_qkv_kernel(qa_ref, kva_ref, kr_ref, cs_ref, wq_ref, wkv_ref, gqa_ref, gkva_ref, gqn_ref, gqr_ref,
                    gkn_ref, gkr_ref, q_ref, k_ref, v_ref):
    cs = cs_ref[...]
    lane = lax.broadcasted_iota(jnp.int32, cs.shape, 1)
    low = lane < MLA_ROPE

    def rope(pair, gain2):
        ms = jnp.sum(jnp.where(low, pair * pair, 0.0), axis=-1, keepdims=True) * (1.0 / MLA_ROPE)
        t = pair * lax.rsqrt(ms + RMS_EPS) * gain2 * cs
        return jnp.where(low, t + pltpu.roll(t, MLA_ROPE, 1), 0.0)

    qq = _dot(_rms(qa_ref[...], gqa_ref[...]).astype(BF16), wq_ref[...])
    kv = _dot(_rms(kva_ref[...], gkva_ref[...]).astype(BF16), wkv_ref[...])
    kr = rope(kr_ref[...], gkr_ref[...])
    hp = MLA_HEAD_PAD
    tm = kr.shape[0]
    qscale = MLA_SCALE * LOG2_E
    ones_row = (lax.broadcasted_iota(jnp.int32, (V_ROWS - MLA_V, tm), 0) == 0).astype(BF16)
    for h in range(MLA_HEADS):
        qn = _rms(qq[:, h * hp:h * hp + MLA_NOPE], gqn_ref[...])
        qr = rope(qq[:, h * hp + MLA_NOPE:(h + 1) * hp], gqr_ref[...])
        q_ref[h, 0:LANE, :] = (qn * qscale).T.astype(BF16)
        q_ref[h, LANE:2 * LANE, :] = (qr * qscale).T.astype(BF16)
        kn = _rms(kv[:, h * hp:h * hp + MLA_NOPE], gkn_ref[...])
        k_ref[h, :, 0:LANE] = kn.astype(BF16)
        k_ref[h, :, LANE:2 * LANE] = kr.astype(BF16)
        v_ref[h, 0:MLA_V, :] = kv[:, h * hp + MLA_NOPE:(h + 1) * hp].T.astype(BF16)
        v_ref[h, MLA_V:V_ROWS, :] = ones_row


def _rope_table(seq, n_batch, ctx_rows):
    rows = seq // GRID_W
    row = np.repeat(np.arange(rows, dtype=np.float32), GRID_W)
    col = np.tile(np.arange(GRID_W, dtype=np.float32), rows)
    half = MLA_ROPE // 2
    inv = (ROPE_THETA ** (-np.arange(0, half, 2, dtype=np.float32) / half)).astype(np.float32)
    ar = (row[:, None] * inv).astype(np.float64)
    ac = (col[:, None] * inv).astype(np.float64)
    cos = np.concatenate([np.cos(ar), np.cos(ar), np.cos(ac), np.cos(ac)], axis=1)
    sin = np.concatenate([-np.sin(ar), np.sin(ar), -np.sin(ac), np.sin(ac)], axis=1)
    lat = np.tile(np.concatenate([cos, sin], axis=1), (n_batch, 1))
    ctx = np.concatenate([np.ones((ctx_rows, MLA_ROPE)), np.zeros((ctx_rows, MLA_ROPE))], axis=1)
    return jnp.asarray(np.concatenate([lat, ctx], axis=0), F32)


def _pair_gain(g):
    return jnp.concatenate([g, _rope_swap(g)]).reshape(1, 2 * MLA_ROPE)


def _mla_qkv(p, cs, lw, tm=512):
    n_rows = p.shape[0]
    hd, hp = MLA_HEADS, MLA_HEAD_PAD
    w_uq = lw["w_uq"].reshape(-1, hd, MLA_QK)
    wq = jnp.concatenate([w_uq, _rope_swap(w_uq[..., MLA_NOPE:])], axis=-1).reshape(-1, hd * hp).astype(BF16)
    wkv = lw["w_ukv"].astype(BF16)
    ql, kvl = wq.shape[0], wkv.shape[0]
    vec = lambda a: a.reshape(1, -1)
    full = lambda a: pl.BlockSpec(a.shape, lambda i: (0,) * a.ndim)
    args = (wq, wkv, vec(lw["q_a_norm"]), vec(lw["kv_a_norm"]), vec(lw["q_nope_norm"]),
            _pair_gain(lw["q_rope_norm"]), vec(lw["k_nope_norm"]), _pair_gain(lw["k_rope_norm"]))
    return pl.pallas_call(
        _mla_qkv_kernel,
        grid=(n_rows // tm,),
        in_specs=[pl.BlockSpec((tm, ql), lambda i: (i, COL_QA // ql)),
                  pl.BlockSpec((tm, kvl), lambda i: (i, COL_KVA // kvl)),
                  pl.BlockSpec((tm, LANE), lambda i: (i, COL_KR // LANE)),
                  pl.BlockSpec((tm, LANE), lambda i: (i, 0))] + [full(a) for a in args],
        out_specs=[pl.BlockSpec((hd, hp, tm), lambda i: (0, 0, i)),
                   pl.BlockSpec((hd, tm, hp), lambda i: (0, i, 0)),
                   pl.BlockSpec((hd, V_ROWS, tm), lambda i: (0, 0, i))],
        out_shape=[jax.ShapeDtypeStruct((hd, hp, n_rows), BF16),
                   jax.ShapeDtypeStruct((hd, n_rows, hp), BF16),
                   jax.ShapeDtypeStruct((hd, V_ROWS, n_rows), BF16)],
        compiler_params=_cparams(1),
        name="mla_qkv",
    )(p, p, p, cs, *args)


def _key_chunks(k_refs, v_refs, tk):
    chunks, s0 = [], 0
    for kr, vr in zip(k_refs, v_refs):
        n = kr.shape[1]
        step = min(tk, n)
        chunks += [(kr, vr, r0, step, s0 + r0) for r0 in range(0, n, step)]
        s0 += n
    return chunks


def _score_pass(qt, chunks, s_scr):
    m = None
    for kr, _, r0, rn, s0 in chunks:
        s = _dot(kr[0, r0:r0 + rn, :], qt)
        s_scr[s0:s0 + rn, :] = s
        mj = jnp.max(s, axis=0, keepdims=True)
        m = mj if m is None else jnp.maximum(m, mj)
    return m


def _value_pass(chunks, s_scr, m, o_ref):
    acc = None
    for _, vr, r0, rn, s0 in chunks:
        p = jnp.exp2((s_scr[s0:s0 + rn, :] - m).astype(BF16))
        part = _dot(vr[0, :, r0:r0 + rn], p)
        acc = part if acc is None else acc + part
    o_ref[...] = (acc[:MLA_V] / acc[MLA_V:MLA_V + 1]).astype(o_ref.dtype)


def _attn_ctx_kernel(q_ref, kc_ref, vc_ref, o_ref, s_scr, *, tk):
    chunks = _key_chunks([kc_ref], [vc_ref], tk)
    _value_pass(chunks, s_scr, _score_pass(q_ref[0], chunks, s_scr), o_ref)


def _attn_kernel(q_ref, k_ref, v_ref, kc_ref, vc_ref, o_ref, s_a, s_b, m_a, m_b, *, tk):
    i = pl.program_id(2)
    chunks = _key_chunks([kc_ref, k_ref], [vc_ref, v_ref], tk)

    @pl.when((pl.program_id(0) == 0) & (pl.program_id(1) == 0) & (i == 0))
    def _():
        s_b[...] = jnp.zeros_like(s_b)
        m_b[...] = jnp.zeros_like(m_b)

    def step(s_cur, m_cur, s_prev, m_prev):
        _value_pass(chunks, s_prev, m_prev[0:1, :], o_ref)
        m_cur[...] = jnp.broadcast_to(_score_pass(q_ref[0], chunks, s_cur), m_cur.shape)

    @pl.when(i % 2 == 0)
    def _():
        step(s_a, m_a, s_b, m_b)

    @pl.when(i % 2 == 1)
    def _():
        step(s_b, m_b, s_a, m_a)


def _attention(qt, k, vt, seq, ctx_len, n_batch, latent, tq=256, tk=1024):
    hd, hp, n_rows = qt.shape
    lat_rows = n_batch * seq
    cblk = lat_rows // ctx_len
    kc_spec = pl.BlockSpec((1, ctx_len, hp), lambda b, h, i: (h, cblk + b, 0))
    vc_spec = pl.BlockSpec((1, V_ROWS, ctx_len), lambda b, h, i: (h, 0, cblk + b))
    if latent:
        nq = seq // tq
        steps = nq + 1
        in_specs = [pl.BlockSpec((1, hp, tq), lambda b, h, i: (h, 0, b * nq + jnp.minimum(i, nq - 1))),
                    pl.BlockSpec((1, seq, hp), lambda b, h, i: (h, b, 0)),
                    pl.BlockSpec((1, V_ROWS, seq), lambda b, h, i: (h, 0, b)), kc_spec, vc_spec]
        args = (qt, k, vt, k, vt)
        out_spec = pl.BlockSpec((MLA_V, tq), lambda b, h, i: (h, b * nq + jnp.maximum(i - 1, 0)))
        out_cols, n_keys = lat_rows, seq + ctx_len
        body = functools.partial(_attn_kernel, tk=tk)
        scratch = [pltpu.VMEM((n_keys, tq), F32)] * 2 + [pltpu.VMEM((8, tq), F32)] * 2
    else:
        steps = 1
        tq = ctx_len
        in_specs = [pl.BlockSpec((1, hp, tq), lambda b, h, i: (h, 0, cblk + b)), kc_spec, vc_spec]
        args = (qt, k, vt)
        out_spec = pl.BlockSpec((MLA_V, tq), lambda b, h, i: (h, b))
        out_cols, n_keys = n_batch * ctx_len, ctx_len
        body = functools.partial(_attn_ctx_kernel, tk=tk)
        scratch = [pltpu.VMEM((n_keys, tq), F32)]
    return pl.pallas_call(
        body,
        grid=(n_batch, hd, steps),
        in_specs=in_specs,
        out_specs=out_spec,
        out_shape=jax.ShapeDtypeStruct((hd * MLA_V, out_cols), BF16),
        scratch_shapes=scratch,
        compiler_params=_cparams(3),
        name="mla_attention" if latent else "mla_attention_ctx",
    )(*args)


def _merge_kernel(*refs, lat_tiles):
    (x_ref, mod_ref, of_ref, ob_ref, g_ref, yb_ref, yc_ref, ga_ref, gb_ref, gc_ref,
     gain_ref, wa_ref, wb_ref, wc_ref, wo_ref) = refs[:15]
    ctx_refs = refs[15:-1]
    o_ref = refs[-1]
    is_ctx = pl.program_id(0) >= lat_tiles
    m = mod_ref[0]
    o = of_ref[...] + ob_ref[...]
    if ctx_refs:
        o = jnp.where(is_ctx, ctx_refs[0][...] + ctx_refs[1][...], o)
    gain = gain_ref[...]
    ya = jnp.concatenate([_rms(o[:, h * HG_D:(h + 1) * HG_D], gain) for h in range(HG_HEADS)], axis=1)
    ya = (ya * _silu(g_ref[...])).astype(BF16)
    nslab, tcols = yb_ref.shape[1], yb_ref.shape[3]
    qn = x_ref.shape[0] // yb_ref.shape[2]
    parts = [jnp.concatenate([jnp.concatenate([yb_ref[0, k, :, q, :] for k in range(nslab)], axis=1)
                              for q in range(h * qn, (h + 1) * qn)], axis=0) for h in range(tcols // qn)]
    sub = pl.program_id(0) % len(parts)
    yb = parts[0]
    for h in range(1, len(parts)):
        yb = jnp.where(sub == h, parts[h], yb)
    yc = yc_ref[...]
    if ctx_refs:
        yb = jnp.where(is_ctx, ctx_refs[2][...], yb)
        yc = jnp.where(is_ctx, ctx_refs[3][...], yc)
    mix = (_sigmoid(ga_ref[...]) * _dot(ya, wa_ref[...])
           + _sigmoid(gb_ref[...]) * _dot(yb.astype(BF16), wb_ref[...])
           + _sigmoid(gc_ref[...]) * _dot_tn(yc, wc_ref[...]))
    o_ref[...] = x_ref[...] + m[5:6] * _dot(mix.astype(BF16), wo_ref[...])


def _merge(xa, mod, o_f, o_b, p, y_b, y_c, ctx_parts, lw, n_rows, seq, n_batch, tm=512):
    d = xa.shape[1]
    hk = HG_HEADS * HG_D
    tpb = seq // tm
    lat_tiles = n_batch * tpb
    n1 = FFT_N1
    tcols = 8
    assert tm % n1 == 0 and tcols % (tm // n1) == 0 and seq % (tcols * n1) == 0
    per_blk = tcols // (tm // n1)
    y_b = y_b.reshape(n_batch, y_b.shape[1], n1, seq // n1, LANE)
    row = lambda w: pl.BlockSpec((tm, w), lambda i: (i, 0))
    pcol = lambda w, c: pl.BlockSpec((tm, w), lambda i: (i, c // w))
    full = lambda a: pl.BlockSpec(a.shape, lambda i: (0,) * a.ndim)
    ws = (lw["hg_out_norm"].reshape(1, HG_D), lw["w_br_a"].astype(BF16), lw["w_br_b"].astype(BF16),
          lw["w_br_c"].astype(BF16), lw["w_out"].astype(BF16))
    yb_spec = pl.BlockSpec((1, y_b.shape[1], n1, tcols, LANE),
                           lambda i: (jnp.minimum(i // tpb, n_batch - 1), 0, 0, (i % tpb) // per_blk, 0))
    lat = lambda i: jnp.minimum(i, lat_tiles - 1)
    late = lambda i: jnp.maximum(i - lat_tiles, 0)
    extra_specs, extra = [], []
    if ctx_parts is not None:
        extra_specs = [pl.BlockSpec((tm, hk), lambda i: (late(i), 0)),
                       pl.BlockSpec((tm, hk), lambda i: (late(i), 0)),
                       pl.BlockSpec((tm, HY_W), lambda i: (late(i), 0)),
                       pl.BlockSpec((MLA_HEADS * MLA_V, tm), lambda i: (0, late(i)))]
        extra = list(ctx_parts)
    return pl.pallas_call(
        functools.partial(_merge_kernel, lat_tiles=lat_tiles),
        grid=(n_rows // tm,),
        in_specs=[row(d), pl.BlockSpec((1, N_MOD, d), _group_map(tpb, n_batch)),
                  pl.BlockSpec((tm, hk), lambda i: (lat(i), 0)), pl.BlockSpec((tm, hk), lambda i: (lat(i), 0)),
                  pcol(hk, COL_G), yb_spec,
                  pl.BlockSpec((MLA_HEADS * MLA_V, tm), lambda i: (0, lat(i))),
                  pcol(d, COL_GA), pcol(d, COL_GB), pcol(d, COL_GC)] + [full(a) for a in ws] + extra_specs,
        out_specs=row(d),
        out_shape=jax.ShapeDtypeStruct((n_rows, d), F32),
        compiler_params=_cparams(1),
        name="merge",
    )(xa, mod, o_f, o_b, p, y_b, y_c, p, p, p, *ws, *extra)


def kernel(x, c, ctx, c_ctx, ada_w, ada_b, ffn1_norm, ffn1_w13, ffn1_w2, mix_norm, w_in, hg_lb_logits, hg_out_norm, hy_conv_w, hy_conv_b, hy_w1, hy_b1, hy_w2, hy_b2, hy_w3, hy_b3, hy_w4, hy_freq, hy_skip, q_a_norm, w_uq, kv_a_norm, w_ukv, q_nope_norm, q_rope_norm, k_nope_norm, k_rope_norm, w_br_a, w_br_b, w_br_c, w_out, ffn2_norm, ffn2_w13, ffn2_w2):
    stacked = dict(
        ada_w=ada_w, ada_b=ada_b, ffn1_norm=ffn1_norm, ffn1_w13=ffn1_w13, ffn1_w2=ffn1_w2, mix_norm=mix_norm,
        w_in=w_in, hg_out_norm=hg_out_norm, hy_conv_w=hy_conv_w, hy_conv_b=hy_conv_b, hy_w1=hy_w1, hy_b1=hy_b1,
        hy_w2=hy_w2, hy_b2=hy_b2, hy_w3=hy_w3, hy_b3=hy_b3, hy_w4=hy_w4, hy_freq=hy_freq, hy_skip=hy_skip,
        q_a_norm=q_a_norm, w_uq=w_uq, kv_a_norm=kv_a_norm, w_ukv=w_ukv, q_nope_norm=q_nope_norm,
        q_rope_norm=q_rope_norm, k_nope_norm=k_nope_norm, k_rope_norm=k_rope_norm, w_br_a=w_br_a,
        w_br_b=w_br_b, w_br_c=w_br_c, w_out=w_out, ffn2_norm=ffn2_norm, ffn2_w13=ffn2_w13, ffn2_w2=ffn2_w2)
    n_batch, seq, d = x.shape
    ctx_len = ctx.shape[1]
    depth = ada_w.shape[0]
    lat_rows, ctx_rows = n_batch * seq, n_batch * ctx_len
    all_rows = lat_rows + ctx_rows
    assert seq % 512 == 0 and ctx_rows % 512 == 0 and seq % ctx_len == 0 and seq % GRID_W == 0
    assert ctx_len % HG_CHUNK == 0 and n_batch < 8

    xa, xc = x.reshape(lat_rows, d), ctx.reshape(ctx_rows, d)
    cs = jnp.concatenate([c, c_ctx.reshape(1, d), jnp.zeros((7 - n_batch, d), F32)], axis=0)
    rope_cs = _rope_table(seq, n_batch, ctx_rows)
    zero_state = jnp.zeros((2, n_batch, HG_HEADS, HG_D, HG_D), F32)
    hk = HG_HEADS * HG_D

    for l in range(depth):
        lw = {name: val[l] for name, val in stacked.items()}
        need_ctx = l < depth - 1
        mod = _modulation(cs, lw["ada_w"], lw["ada_b"])
        xa = _half_ffn(xa, mod, lw["ffn1_norm"], lw["ffn1_w13"], lw["ffn1_w2"], 0, all_rows, seq, n_batch, xc=xc)
        xc = None
        p = _in_projection(xa, mod, lw["mix_norm"], _pack_w_in(lw["w_in"]).astype(BF16), seq, n_batch)

        ocf, ocb, s_c = _hgrn2_scan(p, hg_lb_logits, zero_state, l, lat_rows, ctx_len, n_batch)
        olf, olb, _ = _hgrn2_scan(p, hg_lb_logits, s_c, l, 0, seq, n_batch)
        y_b = _hyena(p, lw, 0, seq, n_batch)
        qt, k, vt = _mla_qkv(p, rope_cs, lw)
        y_c = _attention(qt, k, vt, seq, ctx_len, n_batch, latent=True)

        ctx_parts = None
        if need_ctx:
            ctx_parts = (ocf.reshape(ctx_rows, hk), ocb.reshape(ctx_rows, hk),
                         _hyena(p, lw, lat_rows, ctx_len, n_batch),
                         _attention(qt, k, vt, seq, ctx_len, n_batch, latent=False))
        mix_rows = all_rows if need_ctx else lat_rows
        xa = _merge(xa, mod, olf.reshape(lat_rows, hk), olb.reshape(lat_rows, hk), p, y_b, y_c, ctx_parts, lw,
                    mix_rows, seq, n_batch)
        xa = _half_ffn(xa, mod, lw["ffn2_norm"], lw["ffn2_w13"], lw["ffn2_w2"], 6, mix_rows, seq, n_batch)
    return xa[:lat_rows].reshape(n_batch, seq, d)
```

```python
import functools
import math

import numpy as np
import jax
import jax.numpy as jnp
from jax import lax
from jax.experimental import pallas as pl
from jax.experimental.pallas import tpu as pltpu

F32 = jnp.float32
BF16 = jnp.bfloat16

RMS_EPS = 1e-6
N_MOD = 9
GRID_W = 64
ROPE_THETA = 10000.0
HG_HEADS = 4
HG_D = 128
HG_CHUNK = 128
HY_W = 512
HY_TARGET = 1e-2
HY_MIN_DECAY = math.log(HY_TARGET) / 1.5
HY_MAX_DECAY = math.log(HY_TARGET) / 0.3
HY_SHIFT = 0.05
MLA_HEADS = 4
MLA_NOPE = 128
MLA_ROPE = 64
MLA_V = 128
MLA_QK = MLA_NOPE + MLA_ROPE
MLA_SCALE = MLA_QK ** -0.5
MLA_HEAD_PAD = 256
V_ROWS = MLA_V + 16
LOG2_E = math.log2(math.e)
FFT_N1 = 128
LANE = 128

VMEM_LIMIT = 52 * 1024 * 1024

COL_GA, COL_GB, COL_GC = 0, 1024, 2048
COL_Q, COL_ZF, COL_ZB, COL_IV, COL_G = 3072, 3584, 4096, 4608, 5120
COL_HY = 5632
COL_QA, COL_KVA, COL_KR = 7168, 7424, 7552
IN_PACKED = 7680


def _cparams(n_axes):
    return pltpu.CompilerParams(dimension_semantics=("arbitrary",) * n_axes,
                                vmem_limit_bytes=VMEM_LIMIT)


def _dot(a, b):
    return jnp.dot(a, b, preferred_element_type=F32)


def _dot_nt(a, b):
    return lax.dot_general(a, b, (((1,), (1,)), ((), ())), preferred_element_type=F32)


def _dot_tn(a, b):
    return lax.dot_general(a, b, (((0,), (0,)), ((), ())), preferred_element_type=F32)


def _sigmoid(x):
    return 1.0 / (1.0 + jnp.exp(-x))


def _silu(x):
    return x * _sigmoid(x)


def _rms(x, gain):
    return x * lax.rsqrt(jnp.mean(x * x, axis=-1, keepdims=True) + RMS_EPS) * gain


def _mod_kernel(c_ref, w_ref, b_ref, o_ref):
    o_ref[...] = _dot(_silu(c_ref[...]).astype(BF16), w_ref[...]) + b_ref[...]


def _modulation(cs, ada_w, ada_b):
    g, d = cs.shape
    n = ada_w.shape[1]
    tn = n // 4
    out = pl.pallas_call(
        _mod_kernel,
        grid=(n // tn,),
        in_specs=[pl.BlockSpec((g, d), lambda j: (0, 0)),
                  pl.BlockSpec((d, tn), lambda j: (0, j)),
                  pl.BlockSpec((1, tn), lambda j: (0, j))],
        out_specs=pl.BlockSpec((g, tn), lambda j: (0, j)),
        out_shape=jax.ShapeDtypeStruct((g, n), F32),
        compiler_params=_cparams(1),
        name="modulation",
    )(cs, ada_w.astype(BF16), ada_b.reshape(1, n))
    return out.reshape(g, N_MOD, d)


def _ffn_kernel(*refs, idx, ff, ck, lat_tiles):
    x_ref, mod_ref, g_ref, w13_ref, w2_ref = refs[:5]
    o_ref = refs[-1]
    x = x_ref[...]
    if len(refs) == 7:
        x = jnp.where(pl.program_id(0) >= lat_tiles, refs[5][...], x)
    m = mod_ref[0]
    h = (_rms(x, g_ref[...]) * (1.0 + m[idx + 1:idx + 2]) + m[idx:idx + 1]).astype(BF16)
    acc = jnp.zeros(x.shape, F32)
    for c0 in range(0, ff, ck):
        c1 = min(c0 + ck, ff)
        a = _dot(h, w13_ref[:, c0:c1])
        b = _dot(h, w13_ref[:, ff + c0:ff + c1])
        acc = acc + _dot((_silu(a) * b).astype(BF16), w2_ref[c0:c1, :])
    o_ref[...] = x + (0.5 * m[idx + 2:idx + 3]) * acc


def _group_map(tiles_per_batch, n_batch):
    return lambda i: (jnp.minimum(i // tiles_per_batch, n_batch), 0, 0)


def _half_ffn(xa, mod, gain, w13, w2, idx, n_rows, seq, n_batch, xc=None, tm=512, ck=512):
    d = xa.shape[1]
    ff = w2.shape[0]
    lat_tiles = n_batch * seq // tm
    const = dict(pipeline_mode=pl.Buffered(1))
    x_specs, xs = [pl.BlockSpec((tm, d), lambda i: (jnp.minimum(i, xa.shape[0] // tm - 1), 0))], [xa]
    tail_specs, tail = [], []
    if xc is not None:
        tail_specs, tail = [pl.BlockSpec((tm, d), lambda i: (jnp.maximum(i - lat_tiles, 0), 0))], [xc]
    return pl.pallas_call(
        functools.partial(_ffn_kernel, idx=idx, ff=ff, ck=ck, lat_tiles=lat_tiles),
        grid=(n_rows // tm,),
        in_specs=x_specs + [pl.BlockSpec((1, N_MOD, d), _group_map(seq // tm, n_batch)),
                            pl.BlockSpec((1, d), lambda i: (0, 0)),
                            pl.BlockSpec((d, 2 * ff), lambda i: (0, 0), **const),
                            pl.BlockSpec((ff, d), lambda i: (0, 0), **const)] + tail_specs,
        out_specs=pl.BlockSpec((tm, d), lambda i: (i, 0)),
        out_shape=jax.ShapeDtypeStruct((n_rows, d), F32),
        compiler_params=_cparams(1),
        name="half_ffn",
    )(*xs, mod, gain.reshape(1, d), w13.astype(BF16), w2.astype(BF16), *tail)


def _inproj_kernel(x_ref, mod_ref, g_ref, w_ref, o_ref):
    m = mod_ref[0]
    h = (_rms(x_ref[...], g_ref[...]) * (1.0 + m[4:5]) + m[3:4]).astype(BF16)
    o_ref[...] = _dot(h, w_ref[...])


def _pack_w_in(w_in):
    d = w_in.shape[0]
    hk = HG_HEADS * HG_D
    sizes = (hk, hk, hk, hk, hk, 3 * HY_W, 256, 128, MLA_ROPE, d, d, d)
    offs = np.cumsum((0,) + sizes)
    q, zf, zb, iv, g, hy, qa, kva, kr, ga, gb, gc = (w_in[:, offs[i]:offs[i + 1]] for i in range(12))
    return jnp.concatenate([ga, gb, gc, q, zf, zb, iv, g, hy, qa, kva, kr, _rope_swap(kr)], axis=1)


def _rope_swap(a):
    q = MLA_ROPE // 4
    return jnp.concatenate([a[..., q:2 * q], a[..., :q], a[..., 3 * q:], a[..., 2 * q:3 * q]], axis=-1)


def _in_projection(xa, mod, gain, w_packed, seq, n_batch, tm=512, tn=2560):
    n_rows, d = xa.shape
    n = w_packed.shape[1]
    return pl.pallas_call(
        _inproj_kernel,
        grid=(n // tn, n_rows // tm),
        in_specs=[pl.BlockSpec((tm, d), lambda j, i: (i, 0)),
                  pl.BlockSpec((1, N_MOD, d), lambda j, i: (jnp.minimum(i // (seq // tm), n_batch), 0, 0)),
                  pl.BlockSpec((1, d), lambda j, i: (0, 0)),
                  pl.BlockSpec((d, tn), lambda j, i: (0, j))],
        out_specs=pl.BlockSpec((tm, tn), lambda j, i: (i, j)),
        out_shape=jax.ShapeDtypeStruct((n_rows, n), F32),
        compiler_params=_cparams(2),
        name="in_projection",
    )(xa, mod, gain.reshape(1, d), w_packed)


def _hgrn2_tables(c, rev):
    t = np.arange(c)[:, None]
    u = np.arange(c)[None, :]
    mats = [(u <= t), (u > t)]
    masks = [(t == u)]
    h = c // 2
    while h >= 1:
        mid = (t // (2 * h)) * (2 * h) + h
        mats.append(np.where(t >= mid, (u >= mid) & (u <= t), (u >= t + 1) & (u <= mid - 1)))
        mid_s = (u // (2 * h)) * (2 * h) + h
        masks.append((t // (2 * h) == u // (2 * h)) & (u < mid_s) & (t >= mid))
        h //= 2
    mats = np.stack([m.astype(np.float32) for m in mats])
    masks = np.stack([m.astype(np.float32) for m in masks])
    if rev:
        mats = mats[:, ::-1, ::-1]
        masks = masks[:, ::-1, ::-1]
    mats = np.ascontiguousarray(mats).reshape(-1, c)
    return (jnp.asarray(np.concatenate([mats, mats], axis=1), BF16),
            jnp.asarray(np.ascontiguousarray(masks), F32))


HG_PAIR = 4


def _hgrn2_chunk(q, z, v, lg, dst_ref, msk_ref, st_ref, o_ref, ci, *, layer, rev, n_levels):
    c = q.shape[0]
    hk = HG_HEADS * HG_D
    e = jnp.exp(lg - jnp.max(lg, axis=0, keepdims=True))
    sm = e / jnp.sum(e, axis=0, keepdims=True)
    lb = jnp.zeros((1, hk), F32)
    for i in range(1, layer + 1):
        lb = lb + sm[i:i + 1]

    f = lb + (1.0 - lb) * _sigmoid(z)
    kk = 1.0 - f
    g = jnp.log(f)
    g_hi = g.astype(BF16)
    g_lo = (g - g_hi.astype(F32)).astype(BF16)
    dg = _dot(dst_ref[...], jnp.concatenate([g_hi, g_lo], axis=0))
    qs = _silu(q) * HG_D ** -0.5
    last = 0 if rev else c - 1

    for h in range(HG_HEADS):
        hs = slice(h * HG_D, (h + 1) * HG_D)
        qh, kh, vh = qs[:, hs].astype(BF16), kk[:, hs].astype(BF16), v[:, hs].astype(BF16)
        b_in = dg[0:c, hs]
        ex = lambda blk: jnp.exp(dg[blk * c:(blk + 1) * c, hs].astype(BF16))
        a = _dot_nt(qh, kh) * msk_ref[0]
        for l in range(1, n_levels + 1):
            el = ex(1 + l)
            a = a + _dot_nt(qh * el, kh * el) * msk_ref[l]
        st = st_ref[ci, h]
        o = _dot_nt(qh * ex(0), st.astype(BF16)) + _dot(a.astype(BF16), vh)
        o_ref[:, hs] = o
        st_ref[ci, h] = jnp.exp(b_in[last:last + 1, :]) * st + _dot_tn(vh, kh * ex(1))


def _hgrn2_kernel(*refs, layer, n_levels, pair):
    n_in = 6 * pair
    ins, (lg_ref, dstf_ref, mskf_ref, dstb_ref, mskb_ref, s0_ref) = refs[:n_in], refs[n_in:n_in + 6]
    of_ref, ob_ref, sf_ref, st_ref = refs[n_in + 6:]
    j = pl.program_id(1)

    @pl.when(j == 0)
    def _():
        st_ref[...] = s0_ref[...].reshape(st_ref.shape)

    for e in range(pair):
        qf, zf, vf, qb, zb, vb = (r[...] for r in ins[6 * e:6 * e + 6])
        _hgrn2_chunk(qf, zf, vf, lg_ref[0], dstf_ref, mskf_ref, st_ref, of_ref.at[e], e,
                     layer=layer, rev=False, n_levels=n_levels)
        _hgrn2_chunk(qb, zb, vb, lg_ref[1], dstb_ref, mskb_ref, st_ref, ob_ref.at[e], pair + e,
                     layer=layer, rev=True, n_levels=n_levels)

    @pl.when(j == pl.num_programs(1) - 1)
    def _():
        sf_ref[...] = st_ref[...].reshape(sf_ref.shape)


def _hgrn2_scan(p, logits, s0, layer, row0, seq, n_batch):
    c = HG_CHUNK
    hk = HG_HEADS * HG_D
    nc = seq // c
    base = row0 // c
    n_levels = int(math.log2(c))
    tabs = _hgrn2_tables(c, False) + _hgrn2_tables(c, True)
    pair = math.gcd(HG_PAIR, n_batch)

    def prow(w, e, rev):
        return pl.BlockSpec((c, hk), lambda bp, j: (base + (bp * pair + e) * nc + (nc - 1 - j if rev else j),
                                                   w // hk))

    in_specs = []
    for e in range(pair):
        in_specs += [prow(COL_Q, e, False), prow(COL_ZF, e, False), prow(COL_IV, e, False),
                     prow(COL_Q, e, True), prow(COL_ZB, e, True), prow(COL_IV, e, True)]
    full = lambda a: pl.BlockSpec(a.shape, lambda bp, j: (0,) * a.ndim)
    st_spec = pl.BlockSpec((2, pair, HG_HEADS, HG_D, HG_D), lambda bp, j: (0, bp, 0, 0, 0))
    return pl.pallas_call(
        functools.partial(_hgrn2_kernel, layer=layer, n_levels=n_levels, pair=pair),
        grid=(n_batch // pair, nc),
        in_specs=in_specs + [full(logits)] + [full(t) for t in tabs] + [st_spec],
        out_specs=[pl.BlockSpec((pair, c, hk), lambda bp, j: (bp, j, 0)),
                   pl.BlockSpec((pair, c, hk), lambda bp, j: (bp, nc - 1 - j, 0)),
                   st_spec],
        out_shape=[jax.ShapeDtypeStruct((n_batch, seq, hk), F32),
                   jax.ShapeDtypeStruct((n_batch, seq, hk), F32),
                   jax.ShapeDtypeStruct((2, n_batch, HG_HEADS, HG_D, HG_D), F32)],
        scratch_shapes=[pltpu.VMEM((2 * pair, HG_HEADS, HG_D, HG_D), F32)],
        compiler_params=_cparams(2),
        name="hgrn2_scan",
    )(*([p] * (6 * pair)), logits, *tabs, s0)


def _shortconv_kernel(u_ref, w_ref, b_ref, o_ref):
    u = u_ref[...]
    n = u.shape[0]
    row = lax.broadcasted_iota(jnp.int32, u.shape, 0)
    prev = jnp.where(row == 0, 0.0, pltpu.roll(u, 1, 0))
    nxt = jnp.where(row == n - 1, 0.0, pltpu.roll(u, n - 1, 0))
    w = w_ref[...]
    o_ref[0, 0, 0] = prev * w[0:1] + u * w[1:2] + nxt * w[2:3] + b_ref[...]


def _shortconv_t1major_kernel(u_ref, w_ref, b_ref, o_ref, u_scr, *, n1):
    half = u_ref.shape[0] // n1
    w = w_ref[...]
    b = b_ref[...]
    pitch = n1 + 8

    def copy(g, carry):
        u_scr[pl.ds(pl.multiple_of(g * pitch, 8), n1), :] = u_ref[pl.ds(pl.multiple_of(g * n1, 8), n1), :]
        return carry

    lax.fori_loop(0, half, copy, 0, unroll=8)
    row = lax.broadcasted_iota(jnp.int32, (half, LANE), 0)
    col = lambda t1: u_scr[pl.ds(t1, half, stride=pitch), :]
    before = jnp.where(row == 0, 0.0, pltpu.roll(col(n1 - 1), 1, 0))
    after = jnp.where(row == half - 1, 0.0, pltpu.roll(col(0), half - 1, 0))

    def body(t1, carry):
        prev, cur = carry
        nxt = jnp.where(t1 == n1 - 1, after, col(jnp.minimum(t1 + 1, n1 - 1)))
        o_ref[0, 0, 0, pl.ds(pl.multiple_of(t1 * half, 8), half), :] = prev * w[0:1] + cur * w[1:2] + nxt * w[2:3] + b
        return cur, nxt

    lax.fori_loop(0, n1, body, (before, col(0)), unroll=16)


def _short_conv(p, w, b, row0, seq, n_batch, n1):
    nb = 3 * HY_W // LANE
    per = HY_W // LANE
    body, scratch = _shortconv_kernel, []
    if n1 > 1:
        body = functools.partial(_shortconv_t1major_kernel, n1=n1)
        scratch = [pltpu.VMEM((seq // n1 * (n1 + 8), LANE), F32)]
    return pl.pallas_call(
        body,
        grid=(n_batch, nb),
        in_specs=[pl.BlockSpec((seq, LANE), lambda bi, j: (row0 // seq + bi, COL_HY // LANE + j)),
                  pl.BlockSpec((3, LANE), lambda bi, j: (0, j)),
                  pl.BlockSpec((1, LANE), lambda bi, j: (0, j))],
        out_specs=pl.BlockSpec((1, 1, 1, seq, LANE), lambda bi, j: (j // per, bi, j % per, 0, 0)),
        out_shape=jax.ShapeDtypeStruct((3, n_batch, per, seq, LANE), F32),
        scratch_shapes=scratch,
        compiler_params=_cparams(2),
        name="hyena_short_conv",
    )(p, w, b.reshape(1, -1))


def _hy_filter_kernel(emb_ref, embr_ref, w1_ref, b1_ref, w2_ref, b2_ref, w3_ref, b3_ref, w4_ref, fr_ref,
                      dl_ref, o_ref, nrm_ref, *, seq, group):
    i = pl.program_id(0)
    hp = lax.Precision.HIGHEST
    fr = fr_ref[...]

    def mlp(emb):
        hid = jnp.sin(fr * (jnp.dot(emb, w1_ref[...], precision=hp) + b1_ref[...]))
        hid = jnp.sin(fr * (jnp.dot(hid, w2_ref[...], precision=hp) + b2_ref[...]))
        return jnp.sin(fr * (jnp.dot(hid, w3_ref[...], precision=hp) + b3_ref[...]))

    tl = emb_ref.shape[0]
    n = w4_ref.shape[1]
    half = n // 2
    r = lax.broadcasted_iota(jnp.int32, (tl, half), 0) + i * tl
    per = seq // group
    pos = (r >> (per.bit_length() - 1)) + group * (r & (per - 1))
    posr = (pos & ~(group - 1)) + ((group - (pos & (group - 1))) & (group - 1))
    for side, (emb, pp) in enumerate(((emb_ref[...], pos), (embr_ref[...], posr))):
        h = jnp.dot(mlp(emb), w4_ref[:, side * half:(side + 1) * half], precision=hp)
        t = pp.astype(F32) * (1.0 / (seq - 1))
        h = h * (jnp.exp(-t * dl_ref[...]) + HY_SHIFT)
        if side == 1:
            h = jnp.where(pp == 0, 0.0, h)
        for k in range(half // LANE):
            o_ref[side * (half // LANE) + k] = h[:, k * LANE:(k + 1) * LANE]

        @pl.when(i == 0)
        def _():
            nrm_ref[:, side * half:(side + 1) * half] = jnp.zeros((1, half), F32)

        nrm_ref[:, side * half:(side + 1) * half] += jnp.sum(jnp.abs(h), axis=0, keepdims=True)


def _hyena_filter_taps(seq, w1, b1, w2, b2, w3, b3, w4, freq, group):
    fh = w1.shape[1]
    n_emb = w1.shape[0]
    bands_n = (n_emb - 1) // 2
    tt = np.linspace(0.0, 1.0, seq, dtype=np.float32)[:, None].astype(np.float64)
    ww = (2.0 * math.pi / seq) * np.arange(seq, dtype=np.float64)[:, None]
    bands = np.linspace(1e-4, bands_n - 1, bands_n, dtype=np.float32)[None, :].astype(np.float64)
    emb = np.concatenate([tt, np.cos(bands * ww), -np.sin(bands * ww)], axis=-1)
    emb = np.pad(emb, ((0, 0), (0, LANE - n_emb))).astype(np.float32)
    assert seq % group == 0 and group & (group - 1) == 0 and (seq // group) & (seq // group - 1) == 0
    r = np.arange(seq)
    pos = r // (seq // group) + group * (r % (seq // group))
    embr = emb[(pos // group) * group + (group - pos % group) % group]
    emb = emb[pos]
    w1p = jnp.pad(w1, ((0, LANE - n_emb), (0, 0)))
    deltas = np.abs(np.linspace(HY_MIN_DECAY, HY_MAX_DECAY, HY_W, dtype=np.float32))
    deltas = np.tile(deltas, 2)[None, :]
    n = w4.shape[1]
    tl = min(seq, 512)
    full = lambda a: pl.BlockSpec(a.shape, lambda i: (0,) * a.ndim)
    args = (w1p, b1.reshape(1, fh), w2, b2.reshape(1, fh), w3, b3.reshape(1, fh), w4, freq.reshape(1, fh),
            jnp.asarray(deltas))
    return pl.pallas_call(
        functools.partial(_hy_filter_kernel, seq=seq, group=group),
        grid=(seq // tl,),
        in_specs=[pl.BlockSpec((tl, LANE), lambda i: (i, 0))] * 2 + [full(a) for a in args],
        out_specs=[pl.BlockSpec((n // LANE, tl, LANE), lambda i: (0, i, 0)),
                   pl.BlockSpec((1, n), lambda i: (0, 0))],
        out_shape=[jax.ShapeDtypeStruct((n // LANE, seq, LANE), F32), jax.ShapeDtypeStruct((1, n), F32)],
        compiler_params=_cparams(1),
        name="hyena_filter_taps",
    )(jnp.asarray(emb), jnp.asarray(embr), *args)


def _fft_split(seq):
    n = 2 * seq
    n1 = FFT_N1 if n > 1024 else 1
    return n, n1, n // n1


def _cis(idx, n):
    ph = 2.0 * np.pi * (idx % n) / n
    return np.cos(ph), -np.sin(ph)


def _dft_tables_short(seq):
    n = 2 * seq
    cr, ci = _cis(np.arange(n)[:, None] * np.arange(seq)[None, :], n)
    w_fwd = np.concatenate([cr, ci], axis=0)
    w_inv = np.concatenate([cr.T, ci.T], axis=1) / n
    return dict(w_fwd=jnp.asarray(w_fwd, BF16), w_inv=jnp.asarray(w_inv, BF16))


def _dft_tables_long(seq):
    n, n1, n2 = _fft_split(seq)
    t1 = np.arange(n1)[:, None, None]
    f2 = np.arange(n2)[None, :, None]
    t2 = np.arange(n2 // 2)[None, None, :]
    cr, ci = _cis(f2 * (t1 + n1 * t2), n)
    w1 = np.concatenate([cr, ci], axis=1)
    w4 = np.concatenate([np.swapaxes(cr, 1, 2), np.swapaxes(ci, 1, 2)], axis=2) / n
    j = (n1 - t1) % n1 + n1 * t2
    br, bi = _cis(f2 * (n - j), n)
    w1f = np.concatenate([np.concatenate([cr, br], axis=2), np.concatenate([ci, bi], axis=2)], axis=1)
    gr, gi = _cis(np.arange(n1)[:, None] * np.arange(n1)[None, :], n1)
    wd = np.concatenate([np.concatenate([gr, -gi], axis=1), np.concatenate([gi, gr], axis=1)], axis=0)
    return dict(w1=jnp.asarray(np.swapaxes(w1, 1, 2), BF16), w4=jnp.asarray(w4, BF16),
                w1f=jnp.asarray(np.swapaxes(w1f, 1, 2), BF16),
                wd=jnp.asarray(wd, BF16), wdi=jnp.asarray(wd.T, BF16))


def _dft_rows_kernel(w_ref, x_ref, o_ref):
    o_ref[0] = _dot(w_ref[...], x_ref[0].astype(BF16)).astype(o_ref.dtype)


def _dft_rows(w, x, tn):
    nb, k, cols = x.shape
    m = w.shape[0]
    return pl.pallas_call(
        _dft_rows_kernel,
        grid=(nb, cols // tn),
        in_specs=[pl.BlockSpec((m, k), lambda b, j: (0, 0)),
                  pl.BlockSpec((1, k, tn), lambda b, j: (b, 0, j))],
        out_specs=pl.BlockSpec((1, m, tn), lambda b, j: (b, 0, j)),
        out_shape=jax.ShapeDtypeStruct((nb, m, cols), BF16),
        compiler_params=_cparams(2),
        name="hyena_dft_rows",
    )(w, x)


def _idft_gate_kernel(w_ref, b_ref, xg_ref, z_ref, sk_ref, o_ref):
    y = _dot(w_ref[...], b_ref[0])
    z = z_ref[0]
    o_ref[0] = xg_ref[0] * (y + z * sk_ref[...])


def _idft_gate(w, bc, xg, z, skip_t, tn):
    nb, k, cols = bc.shape
    m = w.shape[0]
    return pl.pallas_call(
        _idft_gate_kernel,
        grid=(nb, cols // tn),
        in_specs=[pl.BlockSpec((m, k), lambda b, j: (0, 0)),
                  pl.BlockSpec((1, k, tn), lambda b, j: (b, 0, j)),
                  pl.BlockSpec((1, m, tn), lambda b, j: (b, 0, j)),
                  pl.BlockSpec((1, m, tn), lambda b, j: (b, 0, j)),
                  pl.BlockSpec((1, tn), lambda b, j: (0, j))],
        out_specs=pl.BlockSpec((1, m, tn), lambda b, j: (b, 0, j)),
        out_shape=jax.ShapeDtypeStruct((nb, m, cols), F32),
        compiler_params=_cparams(2),
        name="hyena_idft_gate",
    )(w, bc, xg, z, skip_t)


def _spec_combine_kernel(af_ref, ab_ref, nf_ref, nb_ref, o_ref):
    inv = 1.0 / (nf_ref[...] + nb_ref[...])
    n = o_ref.shape[1]
    o_ref[0] = (af_ref[0, :n].astype(F32) + ab_ref[0, :n].astype(F32)) * inv
    o_ref[1] = (af_ref[0, n:].astype(F32) - ab_ref[0, n:].astype(F32)) * inv


def _spec_mul_kernel(a_ref, k_ref, o_ref):
    n = k_ref.shape[1]
    xr, xi = a_ref[0, :n].astype(F32), a_ref[0, n:].astype(F32)
    kr, ki = k_ref[0], k_ref[1]
    o_ref[0, :n] = (xr * kr - xi * ki).astype(o_ref.dtype)
    o_ref[0, n:] = (xr * ki + xi * kr).astype(o_ref.dtype)


def _hyena_short(seq, taps, nrm, uc, skip):
    n = 2 * seq
    uc = jnp.swapaxes(uc, 2, 3).reshape(3, uc.shape[1], seq, HY_W)
    tabs = _dft_tables_short(seq)
    cf = taps.shape[1]
    c = HY_W
    a = _dft_rows(tabs["w_fwd"], taps.reshape(1, seq, cf), tn=cf)
    nblk = cf // 2 // c
    spec = pl.pallas_call(
        _spec_combine_kernel,
        grid=(nblk,),
        in_specs=[pl.BlockSpec((1, 2 * n, c), lambda j: (0, 0, j)),
                  pl.BlockSpec((1, 2 * n, c), lambda j: (0, 0, j + nblk)),
                  pl.BlockSpec((1, c), lambda j: (0, j)),
                  pl.BlockSpec((1, c), lambda j: (0, j + nblk))],
        out_specs=pl.BlockSpec((2, n, c), lambda j: (0, 0, j)),
        out_shape=jax.ShapeDtypeStruct((2, n, cf // 2), F32),
        compiler_params=_cparams(1),
        name="hyena_spec_combine",
    )(a, a, nrm, nrm)
    nb = uc.shape[1]
    z = uc[2]
    for order in range(2):
        a = _dft_rows(tabs["w_fwd"], z, tn=c)
        bc = pl.pallas_call(
            _spec_mul_kernel,
            grid=(nb,),
            in_specs=[pl.BlockSpec((1, 2 * n, c), lambda b: (b, 0, 0)),
                      pl.BlockSpec((2, n, c), lambda b: (0, 0, order))],
            out_specs=pl.BlockSpec((1, 2 * n, c), lambda b: (b, 0, 0)),
            out_shape=jax.ShapeDtypeStruct((nb, 2 * n, c), BF16),
            compiler_params=_cparams(1),
            name="hyena_spec_mul",
        )(a, spec)
        z = _idft_gate(tabs["w_inv"], bc, uc[order], z, skip[order].reshape(1, c), tn=c)
    return z


HY_SLABS = 2


def _slab_pitch(n2):
    return n2 + 8


def _pack_c(re, im):
    hi = lax.bitcast_convert_type(re.astype(BF16).astype(F32), jnp.uint32)
    lo = lax.bitcast_convert_type(im.astype(BF16).astype(F32), jnp.uint32)
    return hi | (lo >> 16)


def _unpack_c(w):
    re = lax.bitcast_convert_type(w & jnp.uint32(0xFFFF0000), F32)
    im = lax.bitcast_convert_type(w << 16, F32)
    return re.astype(BF16), im.astype(BF16)


def _store_slab(a_scr, row, words):
    for s in range(HY_SLABS):
        a_scr[s, pl.ds(row, words.shape[0]), :] = words[:, s * LANE:(s + 1) * LANE]


def _stage1(a_scr, w_ref, xs_of, i, tb, n2, pitch):
    for j in range(tb):
        a = _dot_tn(w_ref[j], xs_of(j))
        _store_slab(a_scr, pl.multiple_of((i * tb + j) * pitch, 8), _pack_c(a[:n2], a[n2:]))


def _stage2(a_scr, wd_ref, f2, n1, pitch):
    w = jnp.concatenate([a_scr[s, pl.ds(f2, n1, stride=pitch), :] for s in range(HY_SLABS)], axis=1)
    re, im = _unpack_c(w)
    return _dot(wd_ref[...], jnp.concatenate([re, im], axis=0))


def _time_col(ref, j, half):
    lead = (0,) * (len(ref.shape) - 3)
    return jnp.concatenate([ref[lead + (s, slice(j * half, (j + 1) * half), slice(None))]
                            for s in range(HY_SLABS)], axis=1)


def _hy_spectrum_kernel(hf_ref, hb_ref, nf_ref, nb_ref, w1f_ref, wd_ref, o_ref, a_scr, *, n1, n2, tb, fb):
    i = pl.program_id(1)
    pitch = _slab_pitch(n2)
    half = n2 // 2
    nt = n1 // tb

    @pl.when(i < nt)
    def _():
        def xs_of(j):
            return jnp.concatenate([_time_col(hf_ref, j, half), _time_col(hb_ref, j, half)], axis=0).astype(BF16)
        _stage1(a_scr, w1f_ref, xs_of, i, tb, n2, pitch)

    @pl.when(i >= nt)
    def _():
        inv = 1.0 / (nf_ref[...] + nb_ref[...])
        for jj in range(fb):
            x = _stage2(a_scr, wd_ref, (i - nt) * fb + jj, n1, pitch)
            o_ref[0, 0, jj] = x[:n1] * inv
            o_ref[0, 1, jj] = x[n1:] * inv


def _hy_fftconv_kernel(z_ref, xg_ref, k_ref, sk_ref, w1_ref, wd_ref, wdi_ref, w4_ref, o_ref, a_scr,
                       *, n1, n2, tb, fb):
    i = pl.program_id(2)
    pitch = _slab_pitch(n2)
    half = n2 // 2
    nt, nf = n1 // tb, n2 // fb

    @pl.when(i < nt)
    def _():
        _stage1(a_scr, w1_ref, lambda j: _time_col(z_ref, j, half).astype(BF16), i, tb, n2, pitch)

    @pl.when((i >= nt) & (i < nt + nf))
    def _():
        for jj in range(fb):
            f2 = (i - nt) * fb + jj
            x = _stage2(a_scr, wd_ref, f2, n1, pitch)
            xr, xi = x[:n1], x[n1:]
            kr, ki = k_ref[0, 0, jj], k_ref[0, 1, jj]
            y = jnp.concatenate([xr * kr - xi * ki, xr * ki + xi * kr], axis=0).astype(BF16)
            bv = _dot(wdi_ref[...], y)
            words = _pack_c(bv[:n1], bv[n1:])
            for s in range(HY_SLABS):
                a_scr[s, pl.ds(f2, n1, stride=pitch), :] = words[:, s * LANE:(s + 1) * LANE]

    @pl.when(i >= nt + nf)
    def _():
        sk = sk_ref[...]
        for j in range(tb):
            row = pl.multiple_of(((i - nt - nf) * tb + j) * pitch, 8)
            w = jnp.concatenate([a_scr[s, pl.ds(row, n2), :] for s in range(HY_SLABS)], axis=1)
            re, im = _unpack_c(w)
            y = _dot(w4_ref[j], jnp.concatenate([re, im], axis=0))
            out = _time_col(xg_ref, j, half) * (y + _time_col(z_ref, j, half) * sk)
            for s in range(HY_SLABS):
                o_ref[0, s, j * half:(j + 1) * half, :] = out[:, s * LANE:(s + 1) * LANE]


def _hyena_long(seq, taps, nrm, uc, skip, tb=32, fb=16):
    n, n1, n2 = _fft_split(seq)
    tabs = _dft_tables_long(seq)
    pitch = _slab_pitch(n2)
    half = n2 // 2
    cb = HY_SLABS * LANE
    tb, fb = min(tb, n1), min(fb, n2)
    nt, nf = n1 // tb, n2 // fb
    cf = taps.shape[0] * LANE
    ngrp = cf // 2 // cb
    scratch = [pltpu.VMEM((HY_SLABS, n1 * pitch, LANE), jnp.uint32)]
    tcol = lambda i: jnp.minimum(i, nt - 1)
    spec = pl.pallas_call(
        functools.partial(_hy_spectrum_kernel, n1=n1, n2=n2, tb=tb, fb=fb),
        grid=(ngrp, nt + nf),
        in_specs=[pl.BlockSpec((HY_SLABS, tb * half, LANE), lambda g, i: (g, tcol(i), 0)),
                  pl.BlockSpec((HY_SLABS, tb * half, LANE), lambda g, i: (ngrp + g, tcol(i), 0)),
                  pl.BlockSpec((1, cb), lambda g, i: (0, g)),
                  pl.BlockSpec((1, cb), lambda g, i: (0, ngrp + g)),
                  pl.BlockSpec((tb, n2, 2 * n2), lambda g, i: (tcol(i), 0, 0)),
                  pl.BlockSpec(tabs["wd"].shape, lambda g, i: (0, 0))],
        out_specs=pl.BlockSpec((1, 2, fb, n1, cb), lambda g, i: (g, 0, jnp.maximum(i - nt, 0), 0, 0)),
        out_shape=jax.ShapeDtypeStruct((ngrp, 2, n2, n1, cb), F32),
        scratch_shapes=scratch,
        compiler_params=_cparams(2),
        name="hyena_filter_spectrum",
    )(taps, taps, nrm, nrm, tabs["w1f"], tabs["wd"])
    nb, nblk = uc.shape[1], uc.shape[2]
    ngc = nblk // HY_SLABS
    z, zsel = uc, 2

    def tblk(b, g, i):
        return (b, g, jnp.where(i < nt, i, jnp.maximum(i - nt - nf, 0)), 0)

    def tblk_late(b, g, i):
        return (b, g, jnp.maximum(i - nt - nf, 0), 0)

    for order in range(2):
        stacked = lambda sel, f: (lambda b, g, i: (sel,) + f(b, g, i))
        z = pl.pallas_call(
            functools.partial(_hy_fftconv_kernel, n1=n1, n2=n2, tb=tb, fb=fb),
            grid=(nb, ngc, 2 * nt + nf),
            in_specs=[pl.BlockSpec((1, 1, HY_SLABS, tb * half, LANE), stacked(zsel, tblk)),
                      pl.BlockSpec((1, 1, HY_SLABS, tb * half, LANE), stacked(order, tblk_late)),
                      pl.BlockSpec((1, 2, fb, n1, cb),
                                   lambda b, g, i: (order * ngc + g, 0, jnp.clip(i - nt, 0, nf - 1), 0, 0)),
                      pl.BlockSpec((1, cb), lambda b, g, i: (0, g)),
                      pl.BlockSpec((tb, half, 2 * n2), lambda b, g, i: (tcol(i), 0, 0)),
                      pl.BlockSpec(tabs["wd"].shape, lambda b, g, i: (0, 0)),
                      pl.BlockSpec(tabs["wdi"].shape, lambda b, g, i: (0, 0)),
                      pl.BlockSpec((tb, half, 2 * n2), lambda b, g, i: (jnp.maximum(i - nt - nf, 0), 0, 0))],
            out_specs=pl.BlockSpec((1, HY_SLABS, tb * half, LANE), tblk_late),
            out_shape=jax.ShapeDtypeStruct((nb, nblk, seq, LANE), F32),
            scratch_shapes=scratch,
            compiler_params=_cparams(3),
            name="hyena_fft_conv",
        )(z, uc, spec, skip[order].reshape(1, HY_W), tabs["w1"], tabs["wd"], tabs["wdi"], tabs["w4"])
        z, zsel = z.reshape((1,) + z.shape), 0
    return z[0]


def _hyena(p, lw, row0, seq, n_batch):
    long = _fft_split(seq)[1] > 1
    n1 = FFT_N1 if long else 1
    taps, nrm = _hyena_filter_taps(seq, lw["hy_w1"], lw["hy_b1"], lw["hy_w2"], lw["hy_b2"], lw["hy_w3"],
                                   lw["hy_b3"], lw["hy_w4"], lw["hy_freq"], n1)
    uc = _short_conv(p, lw["hy_conv_w"], lw["hy_conv_b"], row0, seq, n_batch, n1)
    if long:
        return _hyena_long(seq, taps, nrm, uc, lw["hy_skip"])
    taps = jnp.swapaxes(taps, 0, 1).reshape(seq, -1)
    return _hyena_short(seq, taps, nrm, uc, lw["hy_skip"]).reshape(n_batch * seq, HY_W)


def _mla_qkv_kernel(qa_ref, kva_ref, kr_ref, cs_ref, wq_ref, wkv_ref, gqa_ref, gkva_ref, gqn_ref, gqr_ref,
                    gkn_ref, gkr_ref, q_ref, k_ref, v_ref):
    cs = cs_ref[...]
    lane = lax.broadcasted_iota(jnp.int32, cs.shape, 1)
    low = lane < MLA_ROPE

    def rope(pair, gain2):
        ms = jnp.sum(jnp.where(low, pair * pair, 0.0), axis=-1, keepdims=True) * (1.0 / MLA_ROPE)
        t = pair * lax.rsqrt(ms + RMS_EPS) * gain2 * cs
        return jnp.where(low, t + pltpu.roll(t, MLA_ROPE, 1), 0.0)

    qq = _dot(_rms(qa_ref[...], gqa_ref[...]).astype(BF16), wq_ref[...])
    kv = _dot(_rms(kva_ref[...], gkva_ref[...]).astype(BF16), wkv_ref[...])
    kr = rope(kr_ref[...], gkr_ref[...])
    hp = MLA_HEAD_PAD
    tm = kr.shape[0]
    qscale = MLA_SCALE * LOG2_E
    ones_row = (lax.broadcasted_iota(jnp.int32, (V_ROWS - MLA_V, tm), 0) == 0).astype(BF16)
    for h in range(MLA_HEADS):
        qn = _rms(qq[:, h * hp:h * hp + MLA_NOPE], gqn_ref[...])
        qr = rope(qq[:, h * hp + MLA_NOPE:(h + 1) * hp], gqr_ref[...])
        q_ref[h, 0:LANE, :] = (qn * qscale).T.astype(BF16)
        q_ref[h, LANE:2 * LANE, :] = (qr * qscale).T.astype(BF16)
        kn = _rms(kv[:, h * hp:h * hp + MLA_NOPE], gkn_ref[...])
        k_ref[h, :, 0:LANE] = kn.astype(BF16)
        k_ref[h, :, LANE:2 * LANE] = kr.astype(BF16)
        v_ref[h, 0:MLA_V, :] = kv[:, h * hp + MLA_NOPE:(h + 1) * hp].T.astype(BF16)
        v_ref[h, MLA_V:V_ROWS, :] = ones_row


def _rope_table(seq, n_batch, ctx_rows):
    rows = seq // GRID_W
    row = np.repeat(np.arange(rows, dtype=np.float32), GRID_W)
    col = np.tile(np.arange(GRID_W, dtype=np.float32), rows)
    half = MLA_ROPE // 2
    inv = (ROPE_THETA ** (-np.arange(0, half, 2, dtype=np.float32) / half)).astype(np.float32)
    ar = (row[:, None] * inv).astype(np.float64)
    ac = (col[:, None] * inv).astype(np.float64)
    cos = np.concatenate([np.cos(ar), np.cos(ar), np.cos(ac), np.cos(ac)], axis=1)
    sin = np.concatenate([-np.sin(ar), np.sin(ar), -np.sin(ac), np.sin(ac)], axis=1)
    lat = np.tile(np.concatenate([cos, sin], axis=1), (n_batch, 1))
    ctx = np.concatenate([np.ones((ctx_rows, MLA_ROPE)), np.zeros((ctx_rows, MLA_ROPE))], axis=1)
    return jnp.asarray(np.concatenate([lat, ctx], axis=0), F32)


def _pair_gain(g):
    return jnp.concatenate([g, _rope_swap(g)]).reshape(1, 2 * MLA_ROPE)


def _mla_qkv(p, cs, lw, tm=512):
    n_rows = p.shape[0]
    hd, hp = MLA_HEADS, MLA_HEAD_PAD
    w_uq = lw["w_uq"].reshape(-1, hd, MLA_QK)
    wq = jnp.concatenate([w_uq, _rope_swap(w_uq[..., MLA_NOPE:])], axis=-1).reshape(-1, hd * hp).astype(BF16)
    wkv = lw["w_ukv"].astype(BF16)
    ql, kvl = wq.shape[0], wkv.shape[0]
    vec = lambda a: a.reshape(1, -1)
    full = lambda a: pl.BlockSpec(a.shape, lambda i: (0,) * a.ndim)
    args = (wq, wkv, vec(lw["q_a_norm"]), vec(lw["kv_a_norm"]), vec(lw["q_nope_norm"]),
            _pair_gain(lw["q_rope_norm"]), vec(lw["k_nope_norm"]), _pair_gain(lw["k_rope_norm"]))
    return pl.pallas_call(
        _mla_qkv_kernel,
        grid=(n_rows // tm,),
        in_specs=[pl.BlockSpec((tm, ql), lambda i: (i, COL_QA // ql)),
                  pl.BlockSpec((tm, kvl), lambda i: (i, COL_KVA // kvl)),
                  pl.BlockSpec((tm, LANE), lambda i: (i, COL_KR // LANE)),
                  pl.BlockSpec((tm, LANE), lambda i: (i, 0))] + [full(a) for a in args],
        out_specs=[pl.BlockSpec((hd, hp, tm), lambda i: (0, 0, i)),
                   pl.BlockSpec((hd, tm, hp), lambda i: (0, i, 0)),
                   pl.BlockSpec((hd, V_ROWS, tm), lambda i: (0, 0, i))],
        out_shape=[jax.ShapeDtypeStruct((hd, hp, n_rows), BF16),
                   jax.ShapeDtypeStruct((hd, n_rows, hp), BF16),
                   jax.ShapeDtypeStruct((hd, V_ROWS, n_rows), BF16)],
        compiler_params=_cparams(1),
        name="mla_qkv",
    )(p, p, p, cs, *args)


def _key_chunks(k_refs, v_refs, tk):
    chunks, s0 = [], 0
    for kr, vr in zip(k_refs, v_refs):
        n = kr.shape[1]
        step = min(tk, n)
        chunks += [(kr, vr, r0, step, s0 + r0) for r0 in range(0, n, step)]
        s0 += n
    return chunks


def _score_pass(qt, chunks, s_scr):
    m = None
    for kr, _, r0, rn, s0 in chunks:
        s = _dot(kr[0, r0:r0 + rn, :], qt)
        s_scr[s0:s0 + rn, :] = s
        mj = jnp.max(s, axis=0, keepdims=True)
        m = mj if m is None else jnp.maximum(m, mj)
    return m


def _value_pass(chunks, s_scr, m, o_ref):
    acc = None
    for _, vr, r0, rn, s0 in chunks:
        p = jnp.exp2((s_scr[s0:s0 + rn, :] - m).astype(BF16))
        part = _dot(vr[0, :, r0:r0 + rn], p)
        acc = part if acc is None else acc + part
    o_ref[...] = (acc[:MLA_V] / acc[MLA_V:MLA_V + 1]).astype(o_ref.dtype)


def _attn_ctx_kernel(q_ref, kc_ref, vc_ref, o_ref, s_scr, *, tk):
    chunks = _key_chunks([kc_ref], [vc_ref], tk)
    _value_pass(chunks, s_scr, _score_pass(q_ref[0], chunks, s_scr), o_ref)


def _attn_kernel(q_ref, k_ref, v_ref, kc_ref, vc_ref, o_ref, s_a, s_b, m_a, m_b, *, tk):
    i = pl.program_id(2)
    chunks = _key_chunks([kc_ref, k_ref], [vc_ref, v_ref], tk)

    @pl.when((pl.program_id(0) == 0) & (pl.program_id(1) == 0) & (i == 0))
    def _():
        s_b[...] = jnp.zeros_like(s_b)
        m_b[...] = jnp.zeros_like(m_b)

    def step(s_cur, m_cur, s_prev, m_prev):
        _value_pass(chunks, s_prev, m_prev[0:1, :], o_ref)
        m_cur[...] = jnp.broadcast_to(_score_pass(q_ref[0], chunks, s_cur), m_cur.shape)

    @pl.when(i % 2 == 0)
    def _():
        step(s_a, m_a, s_b, m_b)

    @pl.when(i % 2 == 1)
    def _():
        step(s_b, m_b, s_a, m_a)


def _attention(qt, k, vt, seq, ctx_len, n_batch, latent, tq=256, tk=1024):
    hd, hp, n_rows = qt.shape
    lat_rows = n_batch * seq
    cblk = lat_rows // ctx_len
    kc_spec = pl.BlockSpec((1, ctx_len, hp), lambda b, h, i: (h, cblk + b, 0))
    vc_spec = pl.BlockSpec((1, V_ROWS, ctx_len), lambda b, h, i: (h, 0, cblk + b))
    if latent:
        nq = seq // tq
        steps = nq + 1
        in_specs = [pl.BlockSpec((1, hp, tq), lambda b, h, i: (h, 0, b * nq + jnp.minimum(i, nq - 1))),
                    pl.BlockSpec((1, seq, hp), lambda b, h, i: (h, b, 0)),
                    pl.BlockSpec((1, V_ROWS, seq), lambda b, h, i: (h, 0, b)), kc_spec, vc_spec]
        args = (qt, k, vt, k, vt)
        out_spec = pl.BlockSpec((MLA_V, tq), lambda b, h, i: (h, b * nq + jnp.maximum(i - 1, 0)))
        out_cols, n_keys = lat_rows, seq + ctx_len
        body = functools.partial(_attn_kernel, tk=tk)
        scratch = [pltpu.VMEM((n_keys, tq), F32)] * 2 + [pltpu.VMEM((8, tq), F32)] * 2
    else:
        steps = 1
        tq = ctx_len
        in_specs = [pl.BlockSpec((1, hp, tq), lambda b, h, i: (h, 0, cblk + b)), kc_spec, vc_spec]
        args = (qt, k, vt)
        out_spec = pl.BlockSpec((MLA_V, tq), lambda b, h, i: (h, b))
        out_cols, n_keys = n_batch * ctx_len, ctx_len
        body = functools.partial(_attn_ctx_kernel, tk=tk)
        scratch = [pltpu.VMEM((n_keys, tq), F32)]
    return pl.pallas_call(
        body,
        grid=(n_batch, hd, steps),
        in_specs=in_specs,
        out_specs=out_spec,
        out_shape=jax.ShapeDtypeStruct((hd * MLA_V, out_cols), BF16),
        scratch_shapes=scratch,
        compiler_params=_cparams(3),
        name="mla_attention" if latent else "mla_attention_ctx",
    )(*args)


def _merge_kernel(*refs, lat_tiles):
    (x_ref, mod_ref, of_ref, ob_ref, g_ref, yb_ref, yc_ref, ga_ref, gb_ref, gc_ref,
     gain_ref, wa_ref, wb_ref, wc_ref, wo_ref) = refs[:15]
    ctx_refs = refs[15:-1]
    o_ref = refs[-1]
    is_ctx = pl.program_id(0) >= lat_tiles
    m = mod_ref[0]
    o = of_ref[...] + ob_ref[...]
    if ctx_refs:
        o = jnp.where(is_ctx, ctx_refs[0][...] + ctx_refs[1][...], o)
    gain = gain_ref[...]
    ya = jnp.concatenate([_rms(o[:, h * HG_D:(h + 1) * HG_D], gain) for h in range(HG_HEADS)], axis=1)
    ya = (ya * _silu(g_ref[...])).astype(BF16)
    nslab, tcols = yb_ref.shape[1], yb_ref.shape[3]
    qn = x_ref.shape[0] // yb_ref.shape[2]
    parts = [jnp.concatenate([jnp.concatenate([yb_ref[0, k, :, q, :] for k in range(nslab)], axis=1)
                              for q in range(h * qn, (h + 1) * qn)], axis=0) for h in range(tcols // qn)]
    sub = pl.program_id(0) % len(parts)
    yb = parts[0]
    for h in range(1, len(parts)):
        yb = jnp.where(sub == h, parts[h], yb)
    yc = yc_ref[...]
    if ctx_refs:
        yb = jnp.where(is_ctx, ctx_refs[2][...], yb)
        yc = jnp.where(is_ctx, ctx_refs[3][...], yc)
    mix = (_sigmoid(ga_ref[...]) * _dot(ya, wa_ref[...])
           + _sigmoid(gb_ref[...]) * _dot(yb.astype(BF16), wb_ref[...])
           + _sigmoid(gc_ref[...]) * _dot_tn(yc, wc_ref[...]))
    o_ref[...] = x_ref[...] + m[5:6] * _dot(mix.astype(BF16), wo_ref[...])


def _merge(xa, mod, o_f, o_b, p, y_b, y_c, ctx_parts, lw, n_rows, seq, n_batch, tm=512):
    d = xa.shape[1]
    hk = HG_HEADS * HG_D
    tpb = seq // tm
    lat_tiles = n_batch * tpb
    n1 = FFT_N1
    tcols = 8
    assert tm % n1 == 0 and tcols % (tm // n1) == 0 and seq % (tcols * n1) == 0
    per_blk = tcols // (tm // n1)
    y_b = y_b.reshape(n_batch, y_b.shape[1], n1, seq // n1, LANE)
    row = lambda w: pl.BlockSpec((tm, w), lambda i: (i, 0))
    pcol = lambda w, c: pl.BlockSpec((tm, w), lambda i: (i, c // w))
    full = lambda a: pl.BlockSpec(a.shape, lambda i: (0,) * a.ndim)
    ws = (lw["hg_out_norm"].reshape(1, HG_D), lw["w_br_a"].astype(BF16), lw["w_br_b"].astype(BF16),
          lw["w_br_c"].astype(BF16), lw["w_out"].astype(BF16))
    yb_spec = pl.BlockSpec((1, y_b.shape[1], n1, tcols, LANE),
                           lambda i: (jnp.minimum(i // tpb, n_batch - 1), 0, 0, (i % tpb) // per_blk, 0))
    lat = lambda i: jnp.minimum(i, lat_tiles - 1)
    late = lambda i: jnp.maximum(i - lat_tiles, 0)
    extra_specs, extra = [], []
    if ctx_parts is not None:
        extra_specs = [pl.BlockSpec((tm, hk), lambda i: (late(i), 0)),
                       pl.BlockSpec((tm, hk), lambda i: (late(i), 0)),
                       pl.BlockSpec((tm, HY_W), lambda i: (late(i), 0)),
                       pl.BlockSpec((MLA_HEADS * MLA_V, tm), lambda i: (0, late(i)))]
        extra = list(ctx_parts)
    return pl.pallas_call(
        functools.partial(_merge_kernel, lat_tiles=lat_tiles),
        grid=(n_rows // tm,),
        in_specs=[row(d), pl.BlockSpec((1, N_MOD, d), _group_map(tpb, n_batch)),
                  pl.BlockSpec((tm, hk), lambda i: (lat(i), 0)), pl.BlockSpec((tm, hk), lambda i: (lat(i), 0)),
                  pcol(hk, COL_G), yb_spec,
                  pl.BlockSpec((MLA_HEADS * MLA_V, tm), lambda i: (0, lat(i))),
                  pcol(d, COL_GA), pcol(d, COL_GB), pcol(d, COL_GC)] + [full(a) for a in ws] + extra_specs,
        out_specs=row(d),
        out_shape=jax.ShapeDtypeStruct((n_rows, d), F32),
        compiler_params=_cparams(1),
        name="merge",
    )(xa, mod, o_f, o_b, p, y_b, y_c, p, p, p, *ws, *extra)


def kernel(x, c, ctx, c_ctx, ada_w, ada_b, ffn1_norm, ffn1_w13, ffn1_w2, mix_norm, w_in, hg_lb_logits, hg_out_norm, hy_conv_w, hy_conv_b, hy_w1, hy_b1, hy_w2, hy_b2, hy_w3, hy_b3, hy_w4, hy_freq, hy_skip, q_a_norm, w_uq, kv_a_norm, w_ukv, q_nope_norm, q_rope_norm, k_nope_norm, k_rope_norm, w_br_a, w_br_b, w_br_c, w_out, ffn2_norm, ffn2_w13, ffn2_w2):
    stacked = dict(
        ada_w=ada_w, ada_b=ada_b, ffn1_norm=ffn1_norm, ffn1_w13=ffn1_w13, ffn1_w2=ffn1_w2, mix_norm=mix_norm,
        w_in=w_in, hg_out_norm=hg_out_norm, hy_conv_w=hy_conv_w, hy_conv_b=hy_conv_b, hy_w1=hy_w1, hy_b1=hy_b1,
        hy_w2=hy_w2, hy_b2=hy_b2, hy_w3=hy_w3, hy_b3=hy_b3, hy_w4=hy_w4, hy_freq=hy_freq, hy_skip=hy_skip,
        q_a_norm=q_a_norm, w_uq=w_uq, kv_a_norm=kv_a_norm, w_ukv=w_ukv, q_nope_norm=q_nope_norm,
        q_rope_norm=q_rope_norm, k_nope_norm=k_nope_norm, k_rope_norm=k_rope_norm, w_br_a=w_br_a,
        w_br_b=w_br_b, w_br_c=w_br_c, w_out=w_out, ffn2_norm=ffn2_norm, ffn2_w13=ffn2_w13, ffn2_w2=ffn2_w2)
    n_batch, seq, d = x.shape
    ctx_len = ctx.shape[1]
    depth = ada_w.shape[0]
    lat_rows, ctx_rows = n_batch * seq, n_batch * ctx_len
    all_rows = lat_rows + ctx_rows
    assert seq % 512 == 0 and ctx_rows % 512 == 0 and seq % ctx_len == 0 and seq % GRID_W == 0
    assert ctx_len % HG_CHUNK == 0 and n_batch < 8

    xa, xc = x.reshape(lat_rows, d), ctx.reshape(ctx_rows, d)
    cs = jnp.concatenate([c, c_ctx.reshape(1, d), jnp.zeros((7 - n_batch, d), F32)], axis=0)
    rope_cs = _rope_table(seq, n_batch, ctx_rows)
    zero_state = jnp.zeros((2, n_batch, HG_HEADS, HG_D, HG_D), F32)
    hk = HG_HEADS * HG_D

    for l in range(depth):
        lw = {name: val[l] for name, val in stacked.items()}
        need_ctx = l < depth - 1
        mod = _modulation(cs, lw["ada_w"], lw["ada_b"])
        xa = _half_ffn(xa, mod, lw["ffn1_norm"], lw["ffn1_w13"], lw["ffn1_w2"], 0, all_rows, seq, n_batch, xc=xc)
        xc = None
        p = _in_projection(xa, mod, lw["mix_norm"], _pack_w_in(lw["w_in"]).astype(BF16), seq, n_batch)

        ocf, ocb, s_c = _hgrn2_scan(p, hg_lb_logits, zero_state, l, lat_rows, ctx_len, n_batch)
        olf, olb, _ = _hgrn2_scan(p, hg_lb_logits, s_c, l, 0, seq, n_batch)
        y_b = _hyena(p, lw, 0, seq, n_batch)
        qt, k, vt = _mla_qkv(p, rope_cs, lw)
        y_c = _attention(qt, k, vt, seq, ctx_len, n_batch, latent=True)

        ctx_parts = None
        if need_ctx:
            ctx_parts = (ocf.reshape(ctx_rows, hk), ocb.reshape(ctx_rows, hk),
                         _hyena(p, lw, lat_rows, ctx_len, n_batch),
                         _attention(qt, k, vt, seq, ctx_len, n_batch, latent=False))
        mix_rows = all_rows if need_ctx else lat_rows
        xa = _merge(xa, mod, olf.reshape(lat_rows, hk), olb.reshape(lat_rows, hk), p, y_b, y_c, ctx_parts, lw,
                    mix_rows, seq, n_batch)
        xa = _half_ffn(xa, mod, lw["ffn2_norm"], lw["ffn2_w13"], lw["ffn2_w2"], 6, mix_rows, seq, n_batch)
    return xa[:lat_rows].reshape(n_batch, seq, d)
```

```python
import functools
import math

import numpy as np
import jax
import jax.numpy as jnp
from jax import lax
from jax.experimental import pallas as pl
from jax.experimental.pallas import tpu as pltpu

F32 = jnp.float32
BF16 = jnp.bfloat16

RMS_EPS = 1e-6
N_MOD = 9
GRID_W = 64
ROPE_THETA = 10000.0
HG_HEADS = 4
HG_D = 128
HG_CHUNK = 128
HY_W = 512
HY_TARGET = 1e-2
HY_MIN_DECAY = math.log(HY_TARGET) / 1.5
HY_MAX_DECAY = math.log(HY_TARGET) / 0.3
HY_SHIFT = 0.05
MLA_HEADS = 4
MLA_NOPE = 128
MLA_ROPE = 64
MLA_V = 128
MLA_QK = MLA_NOPE + MLA_ROPE
MLA_SCALE = MLA_QK ** -0.5
MLA_HEAD_PAD = 256
V_ROWS = MLA_V + 16
LOG2_E = math.log2(math.e)
FFT_N1 = 128
LANE = 128

VMEM_LIMIT = 52 * 1024 * 1024

COL_GA, COL_GB, COL_GC = 0, 1024, 2048
COL_Q, COL_ZF, COL_ZB, COL_IV, COL_G = 3072, 3584, 4096, 4608, 5120
COL_HY = 5632
COL_QA, COL_KVA, COL_KR = 7168, 7424, 7552
IN_PACKED = 7680


def _cparams(n_axes):
    return pltpu.CompilerParams(dimension_semantics=("arbitrary",) * n_axes,
                                vmem_limit_bytes=VMEM_LIMIT)


def _dot(a, b):
    return jnp.dot(a, b, preferred_element_type=F32)


def _dot_nt(a, b):
    return lax.dot_general(a, b, (((1,), (1,)), ((), ())), preferred_element_type=F32)


def _dot_tn(a, b):
    return lax.dot_general(a, b, (((0,), (0,)), ((), ())), preferred_element_type=F32)


def _sigmoid(x):
    return 1.0 / (1.0 + jnp.exp(-x))


def _silu(x):
    return x * _sigmoid(x)


def _rms(x, gain):
    return x * lax.rsqrt(jnp.mean(x * x, axis=-1, keepdims=True) + RMS_EPS) * gain


def _mod_kernel(c_ref, w_ref, b_ref, o_ref):
    o_ref[...] = _dot(_silu(c_ref[...]).astype(BF16), w_ref[...]) + b_ref[...]


def _modulation(cs, ada_w, ada_b):
    g, d = cs.shape
    n = ada_w.shape[1]
    tn = n // 4
    out = pl.pallas_call(
        _mod_kernel,
        grid=(n // tn,),
        in_specs=[pl.BlockSpec((g, d), lambda j: (0, 0)),
                  pl.BlockSpec((d, tn), lambda j: (0, j)),
                  pl.BlockSpec((1, tn), lambda j: (0, j))],
        out_specs=pl.BlockSpec((g, tn), lambda j: (0, j)),
        out_shape=jax.ShapeDtypeStruct((g, n), F32),
        compiler_params=_cparams(1),
        name="modulation",
    )(cs, ada_w.astype(BF16), ada_b.reshape(1, n))
    return out.reshape(g, N_MOD, d)


def _ffn_kernel(*refs, idx, ff, ck, lat_tiles):
    x_ref, mod_ref, g_ref, w13_ref, w2_ref = refs[:5]
    o_ref = refs[-1]
    x = x_ref[...]
    if len(refs) == 7:
        x = jnp.where(pl.program_id(0) >= lat_tiles, refs[5][...], x)
    m = mod_ref[0]
    h = (_rms(x, g_ref[...]) * (1.0 + m[idx + 1:idx + 2]) + m[idx:idx + 1]).astype(BF16)
    acc = jnp.zeros(x.shape, F32)
    for c0 in range(0, ff, ck):
        c1 = min(c0 + ck, ff)
        a = _dot(h, w13_ref[:, c0:c1])
        b = _dot(h, w13_ref[:, ff + c0:ff + c1])
        acc = acc + _dot((_silu(a) * b).astype(BF16), w2_ref[c0:c1, :])
    o_ref[...] = x + (0.5 * m[idx + 2:idx + 3]) * acc


def _group_map(tiles_per_batch, n_batch):
    return lambda i: (jnp.minimum(i // tiles_per_batch, n_batch), 0, 0)


def _half_ffn(xa, mod, gain, w13, w2, idx, n_rows, seq, n_batch, xc=None, tm=512, ck=512):
    d = xa.shape[1]
    ff = w2.shape[0]
    lat_tiles = n_batch * seq // tm
    const = dict(pipeline_mode=pl.Buffered(1))
    x_specs, xs = [pl.BlockSpec((tm, d), lambda i: (jnp.minimum(i, xa.shape[0] // tm - 1), 0))], [xa]
    tail_specs, tail = [], []
    if xc is not None:
        tail_specs, tail = [pl.BlockSpec((tm, d), lambda i: (jnp.maximum(i - lat_tiles, 0), 0))], [xc]
    return pl.pallas_call(
        functools.partial(_ffn_kernel, idx=idx, ff=ff, ck=ck, lat_tiles=lat_tiles),
        grid=(n_rows // tm,),
        in_specs=x_specs + [pl.BlockSpec((1, N_MOD, d), _group_map(seq // tm, n_batch)),
                            pl.BlockSpec((1, d), lambda i: (0, 0)),
                            pl.BlockSpec((d, 2 * ff), lambda i: (0, 0), **const),
                            pl.BlockSpec((ff, d), lambda i: (0, 0), **const)] + tail_specs,
        out_specs=pl.BlockSpec((tm, d), lambda i: (i, 0)),
        out_shape=jax.ShapeDtypeStruct((n_rows, d), F32),
        compiler_params=_cparams(1),
        name="half_ffn",
    )(*xs, mod, gain.reshape(1, d), w13.astype(BF16), w2.astype(BF16), *tail)


def _inproj_kernel(x_ref, mod_ref, g_ref, w_ref, o_ref):
    m = mod_ref[0]
    h = (_rms(x_ref[...], g_ref[...]) * (1.0 + m[4:5]) + m[3:4]).astype(BF16)
    o_ref[...] = _dot(h, w_ref[...])


def _pack_w_in(w_in):
    d = w_in.shape[0]
    hk = HG_HEADS * HG_D
    sizes = (hk, hk, hk, hk, hk, 3 * HY_W, 256, 128, MLA_ROPE, d, d, d)
    offs = np.cumsum((0,) + sizes)
    q, zf, zb, iv, g, hy, qa, kva, kr, ga, gb, gc = (w_in[:, offs[i]:offs[i + 1]] for i in range(12))
    return jnp.concatenate([ga, gb, gc, q, zf, zb, iv, g, hy, qa, kva, kr, _rope_swap(kr)], axis=1)


def _rope_swap(a):
    q = MLA_ROPE // 4
    return jnp.concatenate([a[..., q:2 * q], a[..., :q], a[..., 3 * q:], a[..., 2 * q:3 * q]], axis=-1)


def _in_projection(xa, mod, gain, w_packed, seq, n_batch, tm=512, tn=2560):
    n_rows, d = xa.shape
    n = w_packed.shape[1]
    return pl.pallas_call(
        _inproj_kernel,
        grid=(n // tn, n_rows // tm),
        in_specs=[pl.BlockSpec((tm, d), lambda j, i: (i, 0)),
                  pl.BlockSpec((1, N_MOD, d), lambda j, i: (jnp.minimum(i // (seq // tm), n_batch), 0, 0)),
                  pl.BlockSpec((1, d), lambda j, i: (0, 0)),
                  pl.BlockSpec((d, tn), lambda j, i: (0, j))],
        out_specs=pl.BlockSpec((tm, tn), lambda j, i: (i, j)),
        out_shape=jax.ShapeDtypeStruct((n_rows, n), F32),
        compiler_params=_cparams(2),
        name="in_projection",
    )(xa, mod, gain.reshape(1, d), w_packed)


def _hgrn2_tables(c, rev):
    t = np.arange(c)[:, None]
    u = np.arange(c)[None, :]
    mats = [(u <= t), (u > t)]
    masks = [(t == u)]
    h = c // 2
    while h >= 1:
        mid = (t // (2 * h)) * (2 * h) + h
        mats.append(np.where(t >= mid, (u >= mid) & (u <= t), (u >= t + 1) & (u <= mid - 1)))
        mid_s = (u // (2 * h)) * (2 * h) + h
        masks.append((t // (2 * h) == u // (2 * h)) & (u < mid_s) & (t >= mid))
        h //= 2
    mats = np.stack([m.astype(np.float32) for m in mats])
    masks = np.stack([m.astype(np.float32) for m in masks])
    if rev:
        mats = mats[:, ::-1, ::-1]
        masks = masks[:, ::-1, ::-1]
    mats = np.ascontiguousarray(mats).reshape(-1, c)
    return (jnp.asarray(np.concatenate([mats, mats], axis=1), BF16),
            jnp.asarray(np.ascontiguousarray(masks), F32))


HG_PAIR = 4


def _hgrn2_chunk(q, z, v, lg, dst_ref, msk_ref, st_ref, o_ref, ci, *, layer, rev, n_levels):
    c = q.shape[0]
    hk = HG_HEADS * HG_D
    e = jnp.exp(lg - jnp.max(lg, axis=0, keepdims=True))
    sm = e / jnp.sum(e, axis=0, keepdims=True)
    lb = jnp.zeros((1, hk), F32)
    for i in range(1, layer + 1):
        lb = lb + sm[i:i + 1]

    f = lb + (1.0 - lb) * _sigmoid(z)
    kk = 1.0 - f
    g = jnp.log(f)
    g_hi = g.astype(BF16)
    g_lo = (g - g_hi.astype(F32)).astype(BF16)
    dg = _dot(dst_ref[...], jnp.concatenate([g_hi, g_lo], axis=0))
    qs = _silu(q) * HG_D ** -0.5
    last = 0 if rev else c - 1

    for h in range(HG_HEADS):
        hs = slice(h * HG_D, (h + 1) * HG_D)
        qh, kh, vh = qs[:, hs].astype(BF16), kk[:, hs].astype(BF16), v[:, hs].astype(BF16)
        b_in = dg[0:c, hs]
        ex = lambda blk: jnp.exp(dg[blk * c:(blk + 1) * c, hs].astype(BF16))
        a = _dot_nt(qh, kh) * msk_ref[0]
        for l in range(1, n_levels + 1):
            el = ex(1 + l)
            a = a + _dot_nt(qh * el, kh * el) * msk_ref[l]
        st = st_ref[ci, h]
        o = _dot_nt(qh * ex(0), st.astype(BF16)) + _dot(a.astype(BF16), vh)
        o_ref[:, hs] = o
        st_ref[ci, h] = jnp.exp(b_in[last:last + 1, :]) * st + _dot_tn(vh, kh * ex(1))


def _hgrn2_kernel(*refs, layer, n_levels, pair):
    n_in = 6 * pair
    ins, (lg_ref, dstf_ref, mskf_ref, dstb_ref, mskb_ref, s0_ref) = refs[:n_in], refs[n_in:n_in + 6]
    of_ref, ob_ref, sf_ref, st_ref = refs[n_in + 6:]
    j = pl.program_id(1)

    @pl.when(j == 0)
    def _():
        st_ref[...] = s0_ref[...].reshape(st_ref.shape)

    for e in range(pair):
        qf, zf, vf, qb, zb, vb = (r[...] for r in ins[6 * e:6 * e + 6])
        _hgrn2_chunk(qf, zf, vf, lg_ref[0], dstf_ref, mskf_ref, st_ref, of_ref.at[e], e,
                     layer=layer, rev=False, n_levels=n_levels)
        _hgrn2_chunk(qb, zb, vb, lg_ref[1], dstb_ref, mskb_ref, st_ref, ob_ref.at[e], pair + e,
                     layer=layer, rev=True, n_levels=n_levels)

    @pl.when(j == pl.num_programs(1) - 1)
    def _():
        sf_ref[...] = st_ref[...].reshape(sf_ref.shape)


def _hgrn2_scan(p, logits, s0, layer, row0, seq, n_batch):
    c = HG_CHUNK
    hk = HG_HEADS * HG_D
    nc = seq // c
    base = row0 // c
    n_levels = int(math.log2(c))
    tabs = _hgrn2_tables(c, False) + _hgrn2_tables(c, True)
    pair = math.gcd(HG_PAIR, n_batch)

    def prow(w, e, rev):
        return pl.BlockSpec((c, hk), lambda bp, j: (base + (bp * pair + e) * nc + (nc - 1 - j if rev else j),
                                                   w // hk))

    in_specs = []
    for e in range(pair):
        in_specs += [prow(COL_Q, e, False), prow(COL_ZF, e, False), prow(COL_IV, e, False),
                     prow(COL_Q, e, True), prow(COL_ZB, e, True), prow(COL_IV, e, True)]
    full = lambda a: pl.BlockSpec(a.shape, lambda bp, j: (0,) * a.ndim)
    st_spec = pl.BlockSpec((2, pair, HG_HEADS, HG_D, HG_D), lambda bp, j: (0, bp, 0, 0, 0))
    return pl.pallas_call(
        functools.partial(_hgrn2_kernel, layer=layer, n_levels=n_levels, pair=pair),
        grid=(n_batch // pair, nc),
        in_specs=in_specs + [full(logits)] + [full(t) for t in tabs] + [st_spec],
        out_specs=[pl.BlockSpec((pair, c, hk), lambda bp, j: (bp, j, 0)),
                   pl.BlockSpec((pair, c, hk), lambda bp, j: (bp, nc - 1 - j, 0)),
                   st_spec],
        out_shape=[jax.ShapeDtypeStruct((n_batch, seq, hk), F32),
                   jax.ShapeDtypeStruct((n_batch, seq, hk), F32),
                   jax.ShapeDtypeStruct((2, n_batch, HG_HEADS, HG_D, HG_D), F32)],
        scratch_shapes=[pltpu.VMEM((2 * pair, HG_HEADS, HG_D, HG_D), F32)],
        compiler_params=_cparams(2),
        name="hgrn2_scan",
    )(*([p] * (6 * pair)), logits, *tabs, s0)


def _shortconv_kernel(u_ref, w_ref, b_ref, o_ref):
    u = u_ref[...]
    n = u.shape[0]
    row = lax.broadcasted_iota(jnp.int32, u.shape, 0)
    prev = jnp.where(row == 0, 0.0, pltpu.roll(u, 1, 0))
    nxt = jnp.where(row == n - 1, 0.0, pltpu.roll(u, n - 1, 0))
    w = w_ref[...]
    o_ref[0, 0, 0] = prev * w[0:1] + u * w[1:2] + nxt * w[2:3] + b_ref[...]


def _shortconv_t1major_kernel(u_ref, w_ref, b_ref, o_ref, u_scr, *, n1):
    half = u_ref.shape[0] // n1
    w = w_ref[...]
    b = b_ref[...]
    pitch = n1 + 8

    def copy(g, carry):
        u_scr[pl.ds(pl.multiple_of(g * pitch, 8), n1), :] = u_ref[pl.ds(pl.multiple_of(g * n1, 8), n1), :]
        return carry

    lax.fori_loop(0, half, copy, 0, unroll=8)
    row = lax.broadcasted_iota(jnp.int32, (half, LANE), 0)
    col = lambda t1: u_scr[pl.ds(t1, half, stride=pitch), :]
    before = jnp.where(row == 0, 0.0, pltpu.roll(col(n1 - 1), 1, 0))
    after = jnp.where(row == half - 1, 0.0, pltpu.roll(col(0), half - 1, 0))

    def body(t1, carry):
        prev, cur = carry
        nxt = jnp.where(t1 == n1 - 1, after, col(jnp.minimum(t1 + 1, n1 - 1)))
        o_ref[0, 0, 0, pl.ds(pl.multiple_of(t1 * half, 8), half), :] = prev * w[0:1] + cur * w[1:2] + nxt * w[2:3] + b
        return cur, nxt

    lax.fori_loop(0, n1, body, (before, col(0)), unroll=16)


def _short_conv(p, w, b, row0, seq, n_batch, n1):
    nb = 3 * HY_W // LANE
    per = HY_W // LANE
    body, scratch = _shortconv_kernel, []
    if n1 > 1:
        body = functools.partial(_shortconv_t1major_kernel, n1=n1)
        scratch = [pltpu.VMEM((seq // n1 * (n1 + 8), LANE), F32)]
    return pl.pallas_call(
        body,
        grid=(n_batch, nb),
        in_specs=[pl.BlockSpec((seq, LANE), lambda bi, j: (row0 // seq + bi, COL_HY // LANE + j)),
                  pl.BlockSpec((3, LANE), lambda bi, j: (0, j)),
                  pl.BlockSpec((1, LANE), lambda bi, j: (0, j))],
        out_specs=pl.BlockSpec((1, 1, 1, seq, LANE), lambda bi, j: (j // per, bi, j % per, 0, 0)),
        out_shape=jax.ShapeDtypeStruct((3, n_batch, per, seq, LANE), F32),
        scratch_shapes=scratch,
        compiler_params=_cparams(2),
        name="hyena_short_conv",
    )(p, w, b.reshape(1, -1))


def _hy_filter_kernel(emb_ref, w1_ref, b1_ref, w2_ref, b2_ref, w3_ref, b3_ref, w4_ref, fr_ref,
                      dl_ref, o_ref, nrm_ref, *, seq, group):
    i = pl.program_id(0)
    hp = lax.Precision.HIGHEST
    fr = fr_ref[...]
    hid = jnp.sin(fr * (jnp.dot(emb_ref[...], w1_ref[...], precision=hp) + b1_ref[...]))
    hid = jnp.sin(fr * (jnp.dot(hid, w2_ref[...], precision=hp) + b2_ref[...]))
    hid = jnp.sin(fr * (jnp.dot(hid, w3_ref[...], precision=hp) + b3_ref[...]))
    h = jnp.dot(hid, w4_ref[...], precision=hp)
    tl, n = h.shape
    r = lax.broadcasted_iota(jnp.int32, h.shape, 0) + i * tl
    col = lax.broadcasted_iota(jnp.int32, h.shape, 1)
    per = seq // group
    pos = (r >> (per.bit_length() - 1)) + group * (r & (per - 1))
    t = pos.astype(F32) * (1.0 / (seq - 1))
    h = h * (jnp.exp(-t * dl_ref[...]) + HY_SHIFT)
    h = jnp.where((pos == 0) & (col >= n // 2), 0.0, h)
    for k in range(n // LANE):
        o_ref[k] = h[:, k * LANE:(k + 1) * LANE]

    @pl.when(i == 0)
    def _():
        nrm_ref[...] = jnp.zeros_like(nrm_ref)

    nrm_ref[...] += jnp.sum(jnp.abs(h), axis=0, keepdims=True)


def _hyena_filter_taps(seq, w1, b1, w2, b2, w3, b3, w4, freq, group):
    fh = w1.shape[1]
    n_emb = w1.shape[0]
    bands_n = (n_emb - 1) // 2
    tt = np.linspace(0.0, 1.0, seq, dtype=np.float32)[:, None].astype(np.float64)
    ww = (2.0 * math.pi / seq) * np.arange(seq, dtype=np.float64)[:, None]
    bands = np.linspace(1e-4, bands_n - 1, bands_n, dtype=np.float32)[None, :].astype(np.float64)
    emb = np.concatenate([tt, np.cos(bands * ww), -np.sin(bands * ww)], axis=-1)
    emb = np.pad(emb, ((0, 0), (0, LANE - n_emb))).astype(np.float32)
    assert seq % group == 0 and group & (group - 1) == 0 and (seq // group) & (seq // group - 1) == 0
    r = np.arange(seq)
    pos = r // (seq // group) + group * (r % (seq // group))
    emb = emb[pos]
    w1p = jnp.pad(w1, ((0, LANE - n_emb), (0, 0)))
    deltas = np.abs(np.linspace(HY_MIN_DECAY, HY_MAX_DECAY, HY_W, dtype=np.float32))
    deltas = np.tile(deltas, 4)[None, :]
    n = w4.shape[1]
    tl = min(seq, 512)
    full = lambda a: pl.BlockSpec(a.shape, lambda i: (0,) * a.ndim)
    args = (w1p, b1.reshape(1, fh), w2, b2.reshape(1, fh), w3, b3.reshape(1, fh), w4, freq.reshape(1, fh),
            jnp.asarray(deltas))
    return pl.pallas_call(
        functools.partial(_hy_filter_kernel, seq=seq, group=group),
        grid=(seq // tl,),
        in_specs=[pl.BlockSpec((tl, LANE), lambda i: (i, 0))] + [full(a) for a in args],
        out_specs=[pl.BlockSpec((n // LANE, tl, LANE), lambda i: (0, i, 0)),
                   pl.BlockSpec((1, n), lambda i: (0, 0))],
        out_shape=[jax.ShapeDtypeStruct((n // LANE, seq, LANE), F32), jax.ShapeDtypeStruct((1, n), F32)],
        compiler_params=_cparams(1),
        name="hyena_filter_taps",
    )(jnp.asarray(emb), *args)


def _fft_split(seq):
    n = 2 * seq
    n1 = FFT_N1 if n > 1024 else 1
    return n, n1, n // n1


def _cis(idx, n):
    ph = 2.0 * np.pi * (idx % n) / n
    return np.cos(ph), -np.sin(ph)


def _dft_tables_short(seq):
    n = 2 * seq
    cr, ci = _cis(np.arange(n)[:, None] * np.arange(seq)[None, :], n)
    w_fwd = np.concatenate([cr, ci], axis=0)
    w_inv = np.concatenate([cr.T, ci.T], axis=1) / n
    return dict(w_fwd=jnp.asarray(w_fwd, BF16), w_inv=jnp.asarray(w_inv, BF16))


def _dft_tables_long(seq):
    n, n1, n2 = _fft_split(seq)
    t1 = np.arange(n1)[:, None, None]
    f2 = np.arange(n2)[None, :, None]
    t2 = np.arange(n2 // 2)[None, None, :]
    cr, ci = _cis(f2 * (t1 + n1 * t2), n)
    w1 = np.concatenate([cr, ci], axis=1)
    w4 = np.concatenate([np.swapaxes(cr, 1, 2), np.swapaxes(ci, 1, 2)], axis=2) / n
    j = (n1 - t1) % n1 + n1 * t2
    br, bi = _cis(f2 * (n - j), n)
    w1f = np.concatenate([np.concatenate([cr, br], axis=2), np.concatenate([ci, bi], axis=2)], axis=1)
    gr, gi = _cis(np.arange(n1)[:, None] * np.arange(n1)[None, :], n1)
    wd = np.concatenate([np.concatenate([gr, -gi], axis=1), np.concatenate([gi, gr], axis=1)], axis=0)
    return dict(w1=jnp.asarray(np.swapaxes(w1, 1, 2), BF16), w4=jnp.asarray(w4, BF16),
                w1f=jnp.asarray(np.swapaxes(w1f, 1, 2), BF16),
                wd=jnp.asarray(wd, BF16), wdi=jnp.asarray(wd.T, BF16))


def _dft_rows_kernel(w_ref, x_ref, o_ref):
    o_ref[0] = _dot(w_ref[...], x_ref[0].astype(BF16)).astype(o_ref.dtype)


def _dft_rows(w, x, tn):
    nb, k, cols = x.shape
    m = w.shape[0]
    return pl.pallas_call(
        _dft_rows_kernel,
        grid=(nb, cols // tn),
        in_specs=[pl.BlockSpec((m, k), lambda b, j: (0, 0)),
                  pl.BlockSpec((1, k, tn), lambda b, j: (b, 0, j))],
        out_specs=pl.BlockSpec((1, m, tn), lambda b, j: (b, 0, j)),
        out_shape=jax.ShapeDtypeStruct((nb, m, cols), BF16),
        compiler_params=_cparams(2),
        name="hyena_dft_rows",
    )(w, x)


def _idft_gate_kernel(w_ref, b_ref, xg_ref, z_ref, sk_ref, o_ref):
    y = _dot(w_ref[...], b_ref[0])
    z = z_ref[0]
    o_ref[0] = xg_ref[0] * (y + z * sk_ref[...])


def _idft_gate(w, bc, xg, z, skip_t, tn):
    nb, k, cols = bc.shape
    m = w.shape[0]
    return pl.pallas_call(
        _idft_gate_kernel,
        grid=(nb, cols // tn),
        in_specs=[pl.BlockSpec((m, k), lambda b, j: (0, 0)),
                  pl.BlockSpec((1, k, tn), lambda b, j: (b, 0, j)),
                  pl.BlockSpec((1, m, tn), lambda b, j: (b, 0, j)),
                  pl.BlockSpec((1, m, tn), lambda b, j: (b, 0, j)),
                  pl.BlockSpec((1, tn), lambda b, j: (0, j))],
        out_specs=pl.BlockSpec((1, m, tn), lambda b, j: (b, 0, j)),
        out_shape=jax.ShapeDtypeStruct((nb, m, cols), F32),
        compiler_params=_cparams(2),
        name="hyena_idft_gate",
    )(w, bc, xg, z, skip_t)


def _spec_combine_kernel(af_ref, ab_ref, nf_ref, nb_ref, o_ref):
    inv = 1.0 / (nf_ref[...] + nb_ref[...])
    n = o_ref.shape[1]
    o_ref[0] = (af_ref[0, :n].astype(F32) + ab_ref[0, :n].astype(F32)) * inv
    o_ref[1] = (af_ref[0, n:].astype(F32) - ab_ref[0, n:].astype(F32)) * inv


def _spec_mul_kernel(a_ref, k_ref, o_ref):
    n = k_ref.shape[1]
    xr, xi = a_ref[0, :n].astype(F32), a_ref[0, n:].astype(F32)
    kr, ki = k_ref[0], k_ref[1]
    o_ref[0, :n] = (xr * kr - xi * ki).astype(o_ref.dtype)
    o_ref[0, n:] = (xr * ki + xi * kr).astype(o_ref.dtype)


def _hyena_short(seq, taps, nrm, uc, skip):
    n = 2 * seq
    uc = jnp.swapaxes(uc, 2, 3).reshape(3, uc.shape[1], seq, HY_W)
    tabs = _dft_tables_short(seq)
    cf = taps.shape[1]
    c = HY_W
    a = _dft_rows(tabs["w_fwd"], taps.reshape(1, seq, cf), tn=cf)
    nblk = cf // 2 // c
    spec = pl.pallas_call(
        _spec_combine_kernel,
        grid=(nblk,),
        in_specs=[pl.BlockSpec((1, 2 * n, c), lambda j: (0, 0, j)),
                  pl.BlockSpec((1, 2 * n, c), lambda j: (0, 0, j + nblk)),
                  pl.BlockSpec((1, c), lambda j: (0, j)),
                  pl.BlockSpec((1, c), lambda j: (0, j + nblk))],
        out_specs=pl.BlockSpec((2, n, c), lambda j: (0, 0, j)),
        out_shape=jax.ShapeDtypeStruct((2, n, cf // 2), F32),
        compiler_params=_cparams(1),
        name="hyena_spec_combine",
    )(a, a, nrm, nrm)
    nb = uc.shape[1]
    z = uc[2]
    for order in range(2):
        a = _dft_rows(tabs["w_fwd"], z, tn=c)
        bc = pl.pallas_call(
            _spec_mul_kernel,
            grid=(nb,),
            in_specs=[pl.BlockSpec((1, 2 * n, c), lambda b: (b, 0, 0)),
                      pl.BlockSpec((2, n, c), lambda b: (0, 0, order))],
            out_specs=pl.BlockSpec((1, 2 * n, c), lambda b: (b, 0, 0)),
            out_shape=jax.ShapeDtypeStruct((nb, 2 * n, c), BF16),
            compiler_params=_cparams(1),
            name="hyena_spec_mul",
        )(a, spec)
        z = _idft_gate(tabs["w_inv"], bc, uc[order], z, skip[order].reshape(1, c), tn=c)
    return z


HY_SLABS = 2


def _slab_pitch(n2):
    return n2 + 8


def _pack_c(re, im):
    hi = lax.bitcast_convert_type(re.astype(BF16).astype(F32), jnp.uint32)
    lo = lax.bitcast_convert_type(im.astype(BF16).astype(F32), jnp.uint32)
    return hi | (lo >> 16)


def _unpack_c(w):
    re = lax.bitcast_convert_type(w & jnp.uint32(0xFFFF0000), F32)
    im = lax.bitcast_convert_type(w << 16, F32)
    return re.astype(BF16), im.astype(BF16)


def _store_slab(a_scr, row, words):
    for s in range(HY_SLABS):
        a_scr[s, pl.ds(row, words.shape[0]), :] = words[:, s * LANE:(s + 1) * LANE]


def _stage1(a_scr, w_ref, xs_of, i, tb, n2, pitch):
    for j in range(tb):
        a = _dot_tn(w_ref[j], xs_of(j))
        _store_slab(a_scr, pl.multiple_of((i * tb + j) * pitch, 8), _pack_c(a[:n2], a[n2:]))


def _stage2(a_scr, wd_ref, f2, n1, pitch):
    w = jnp.concatenate([a_scr[s, pl.ds(f2, n1, stride=pitch), :] for s in range(HY_SLABS)], axis=1)
    re, im = _unpack_c(w)
    return _dot(wd_ref[...], jnp.concatenate([re, im], axis=0))


def _time_col(ref, j, half):
    lead = (0,) * (len(ref.shape) - 3)
    return jnp.concatenate([ref[lead + (s, slice(j * half, (j + 1) * half), slice(None))]
                            for s in range(HY_SLABS)], axis=1)


def _hy_spectrum_kernel(hf_ref, hb_ref, hb0_ref, nf_ref, nb_ref, w1f_ref, wd_ref, o_ref, a_scr,
                        *, n1, n2, tb, fb):
    i = pl.program_id(1)
    pitch = _slab_pitch(n2)
    half = n2 // 2
    nt = n1 // tb

    @pl.when(i < nt)
    def _():
        def xs_of(j):
            fut = _time_col(hb0_ref, 0, half) if j == 0 else _time_col(hb_ref, tb - j, half)
            return jnp.concatenate([_time_col(hf_ref, j, half), fut], axis=0).astype(BF16)
        _stage1(a_scr, w1f_ref, xs_of, i, tb, n2, pitch)

    @pl.when(i >= nt)
    def _():
        inv = 1.0 / (nf_ref[...] + nb_ref[...])
        for jj in range(fb):
            x = _stage2(a_scr, wd_ref, (i - nt) * fb + jj, n1, pitch)
            o_ref[0, 0, jj] = x[:n1] * inv
            o_ref[0, 1, jj] = x[n1:] * inv


def _hy_fftconv_kernel(z_ref, xg_ref, k_ref, sk_ref, w1_ref, wd_ref, wdi_ref, w4_ref, o_ref, a_scr,
                       *, n1, n2, tb, fb):
    i = pl.program_id(2)
    pitch = _slab_pitch(n2)
    half = n2 // 2
    nt, nf = n1 // tb, n2 // fb

    @pl.when(i < nt)
    def _():
        _stage1(a_scr, w1_ref, lambda j: _time_col(z_ref, j, half).astype(BF16), i, tb, n2, pitch)

    @pl.when((i >= nt) & (i < nt + nf))
    def _():
        for jj in range(fb):
            f2 = (i - nt) * fb + jj
            x = _stage2(a_scr, wd_ref, f2, n1, pitch)
            xr, xi = x[:n1], x[n1:]
            kr, ki = k_ref[0, 0, jj], k_ref[0, 1, jj]
            y = jnp.concatenate([xr * kr - xi * ki, xr * ki + xi * kr], axis=0).astype(BF16)
            bv = _dot(wdi_ref[...], y)
            words = _pack_c(bv[:n1], bv[n1:])
            for s in range(HY_SLABS):
                a_scr[s, pl.ds(f2, n1, stride=pitch), :] = words[:, s * LANE:(s + 1) * LANE]

    @pl.when(i >= nt + nf)
    def _():
        sk = sk_ref[...]
        for j in range(tb):
            row = pl.multiple_of(((i - nt - nf) * tb + j) * pitch, 8)
            w = jnp.concatenate([a_scr[s, pl.ds(row, n2), :] for s in range(HY_SLABS)], axis=1)
            re, im = _unpack_c(w)
            y = _dot(w4_ref[j], jnp.concatenate([re, im], axis=0))
            out = _time_col(xg_ref, j, half) * (y + _time_col(z_ref, j, half) * sk)
            for s in range(HY_SLABS):
                o_ref[0, s, j * half:(j + 1) * half, :] = out[:, s * LANE:(s + 1) * LANE]


def _hyena_long(seq, taps, nrm, uc, skip, tb=32, fb=16):
    n, n1, n2 = _fft_split(seq)
    tabs = _dft_tables_long(seq)
    pitch = _slab_pitch(n2)
    half = n2 // 2
    cb = HY_SLABS * LANE
    tb, fb = min(tb, n1), min(fb, n2)
    nt, nf = n1 // tb, n2 // fb
    cf = taps.shape[0] * LANE
    ngrp = cf // 2 // cb
    scratch = [pltpu.VMEM((HY_SLABS, n1 * pitch, LANE), jnp.uint32)]
    tcol = lambda i: jnp.minimum(i, nt - 1)
    spec = pl.pallas_call(
        functools.partial(_hy_spectrum_kernel, n1=n1, n2=n2, tb=tb, fb=fb),
        grid=(ngrp, nt + nf),
        in_specs=[pl.BlockSpec((HY_SLABS, tb * half, LANE), lambda g, i: (g, tcol(i), 0)),
                  pl.BlockSpec((HY_SLABS, tb * half, LANE), lambda g, i: (ngrp + g, nt - 1 - tcol(i), 0)),
                  pl.BlockSpec((HY_SLABS, tb * half, LANE), lambda g, i: (ngrp + g, (nt - tcol(i)) % nt, 0)),
                  pl.BlockSpec((1, cb), lambda g, i: (0, g)),
                  pl.BlockSpec((1, cb), lambda g, i: (0, ngrp + g)),
                  pl.BlockSpec((tb, n2, 2 * n2), lambda g, i: (tcol(i), 0, 0)),
                  pl.BlockSpec(tabs["wd"].shape, lambda g, i: (0, 0))],
        out_specs=pl.BlockSpec((1, 2, fb, n1, cb), lambda g, i: (g, 0, jnp.maximum(i - nt, 0), 0, 0)),
        out_shape=jax.ShapeDtypeStruct((ngrp, 2, n2, n1, cb), F32),
        scratch_shapes=scratch,
        compiler_params=_cparams(2),
        name="hyena_filter_spectrum",
    )(taps, taps, taps, nrm, nrm, tabs["w1f"], tabs["wd"])
    nb, nblk = uc.shape[1], uc.shape[2]
    ngc = nblk // HY_SLABS
    z, zsel = uc, 2

    def tblk(b, g, i):
        return (b, g, jnp.where(i < nt, i, jnp.maximum(i - nt - nf, 0)), 0)

    def tblk_late(b, g, i):
        return (b, g, jnp.maximum(i - nt - nf, 0), 0)

    for order in range(2):
        stacked = lambda sel, f: (lambda b, g, i: (sel,) + f(b, g, i))
        z = pl.pallas_call(
            functools.partial(_hy_fftconv_kernel, n1=n1, n2=n2, tb=tb, fb=fb),
            grid=(nb, ngc, 2 * nt + nf),
            in_specs=[pl.BlockSpec((1, 1, HY_SLABS, tb * half, LANE), stacked(zsel, tblk)),
                      pl.BlockSpec((1, 1, HY_SLABS, tb * half, LANE), stacked(order, tblk_late)),
                      pl.BlockSpec((1, 2, fb, n1, cb),
                                   lambda b, g, i: (order * ngc + g, 0, jnp.clip(i - nt, 0, nf - 1), 0, 0)),
                      pl.BlockSpec((1, cb), lambda b, g, i: (0, g)),
                      pl.BlockSpec((tb, half, 2 * n2), lambda b, g, i: (tcol(i), 0, 0)),
                      pl.BlockSpec(tabs["wd"].shape, lambda b, g, i: (0, 0)),
                      pl.BlockSpec(tabs["wdi"].shape, lambda b, g, i: (0, 0)),
                      pl.BlockSpec((tb, half, 2 * n2), lambda b, g, i: (jnp.maximum(i - nt - nf, 0), 0, 0))],
            out_specs=pl.BlockSpec((1, HY_SLABS, tb * half, LANE), tblk_late),
            out_shape=jax.ShapeDtypeStruct((nb, nblk, seq, LANE), F32),
            scratch_shapes=scratch,
            compiler_params=_cparams(3),
            name="hyena_fft_conv",
        )(z, uc, spec, skip[order].reshape(1, HY_W), tabs["w1"], tabs["wd"], tabs["wdi"], tabs["w4"])
        z, zsel = z.reshape((1,) + z.shape), 0
    return z[0]


def _hyena(p, lw, row0, seq, n_batch):
    long = _fft_split(seq)[1] > 1
    n1 = FFT_N1 if long else 1
    taps, nrm = _hyena_filter_taps(seq, lw["hy_w1"], lw["hy_b1"], lw["hy_w2"], lw["hy_b2"], lw["hy_w3"],
                                   lw["hy_b3"], lw["hy_w4"], lw["hy_freq"], n1)
    uc = _short_conv(p, lw["hy_conv_w"], lw["hy_conv_b"], row0, seq, n_batch, n1)
    if long:
        return _hyena_long(seq, taps, nrm, uc, lw["hy_skip"])
    taps = jnp.swapaxes(taps, 0, 1).reshape(seq, -1)
    return _hyena_short(seq, taps, nrm, uc, lw["hy_skip"]).reshape(n_batch * seq, HY_W)


def _mla_qkv_kernel(qa_ref, kva_ref, kr_ref, cs_ref, wq_ref, wkv_ref, gqa_ref, gkva_ref, gqn_ref, gqr_ref,
                    gkn_ref, gkr_ref, q_ref, k_ref, v_ref):
    cs = cs_ref[...]
    lane = lax.broadcasted_iota(jnp.int32, cs.shape, 1)
    low = lane < MLA_ROPE

    def rope(pair, gain2):
        ms = jnp.sum(jnp.where(low, pair * pair, 0.0), axis=-1, keepdims=True) * (1.0 / MLA_ROPE)
        t = pair * lax.rsqrt(ms + RMS_EPS) * gain2 * cs
        return jnp.where(low, t + pltpu.roll(t, MLA_ROPE, 1), 0.0)

    qq = _dot(_rms(qa_ref[...], gqa_ref[...]).astype(BF16), wq_ref[...])
    kv = _dot(_rms(kva_ref[...], gkva_ref[...]).astype(BF16), wkv_ref[...])
    kr = rope(kr_ref[...], gkr_ref[...])
    hp = MLA_HEAD_PAD
    tm = kr.shape[0]
    qscale = MLA_SCALE * LOG2_E
    ones_row = (lax.broadcasted_iota(jnp.int32, (V_ROWS - MLA_V, tm), 0) == 0).astype(BF16)
    for h in range(MLA_HEADS):
        qn = _rms(qq[:, h * hp:h * hp + MLA_NOPE], gqn_ref[...])
        qr = rope(qq[:, h * hp + MLA_NOPE:(h + 1) * hp], gqr_ref[...])
        q_ref[h, 0:LANE, :] = (qn * qscale).T.astype(BF16)
        q_ref[h, LANE:2 * LANE, :] = (qr * qscale).T.astype(BF16)
        kn = _rms(kv[:, h * hp:h * hp + MLA_NOPE], gkn_ref[...])
        k_ref[h, :, 0:LANE] = kn.astype(BF16)
        k_ref[h, :, LANE:2 * LANE] = kr.astype(BF16)
        v_ref[h, 0:MLA_V, :] = kv[:, h * hp + MLA_NOPE:(h + 1) * hp].T.astype(BF16)
        v_ref[h, MLA_V:V_ROWS, :] = ones_row


def _rope_table(seq, n_batch, ctx_rows):
    rows = seq // GRID_W
    row = np.repeat(np.arange(rows, dtype=np.float32), GRID_W)
    col = np.tile(np.arange(GRID_W, dtype=np.float32), rows)
    half = MLA_ROPE // 2
    inv = (ROPE_THETA ** (-np.arange(0, half, 2, dtype=np.float32) / half)).astype(np.float32)
    ar = (row[:, None] * inv).astype(np.float64)
    ac = (col[:, None] * inv).astype(np.float64)
    cos = np.concatenate([np.cos(ar), np.cos(ar), np.cos(ac), np.cos(ac)], axis=1)
    sin = np.concatenate([-np.sin(ar), np.sin(ar), -np.sin(ac), np.sin(ac)], axis=1)
    lat = np.tile(np.concatenate([cos, sin], axis=1), (n_batch, 1))
    ctx = np.concatenate([np.ones((ctx_rows, MLA_ROPE)), np.zeros((ctx_rows, MLA_ROPE))], axis=1)
    return jnp.asarray(np.concatenate([lat, ctx], axis=0), F32)


def _pair_gain(g):
    return jnp.concatenate([g, _rope_swap(g)]).reshape(1, 2 * MLA_ROPE)


def _mla_qkv(p, cs, lw, tm=512):
    n_rows = p.shape[0]
    hd, hp = MLA_HEADS, MLA_HEAD_PAD
    w_uq = lw["w_uq"].reshape(-1, hd, MLA_QK)
    wq = jnp.concatenate([w_uq, _rope_swap(w_uq[..., MLA_NOPE:])], axis=-1).reshape(-1, hd * hp).astype(BF16)
    wkv = lw["w_ukv"].astype(BF16)
    ql, kvl = wq.shape[0], wkv.shape[0]
    vec = lambda a: a.reshape(1, -1)
    full = lambda a: pl.BlockSpec(a.shape, lambda i: (0,) * a.ndim)
    args = (wq, wkv, vec(lw["q_a_norm"]), vec(lw["kv_a_norm"]), vec(lw["q_nope_norm"]),
            _pair_gain(lw["q_rope_norm"]), vec(lw["k_nope_norm"]), _pair_gain(lw["k_rope_norm"]))
    return pl.pallas_call(
        _mla_qkv_kernel,
        grid=(n_rows // tm,),
        in_specs=[pl.BlockSpec((tm, ql), lambda i: (i, COL_QA // ql)),
                  pl.BlockSpec((tm, kvl), lambda i: (i, COL_KVA // kvl)),
                  pl.BlockSpec((tm, LANE), lambda i: (i, COL_KR // LANE)),
                  pl.BlockSpec((tm, LANE), lambda i: (i, 0))] + [full(a) for a in args],
        out_specs=[pl.BlockSpec((hd, hp, tm), lambda i: (0, 0, i)),
                   pl.BlockSpec((hd, tm, hp), lambda i: (0, i, 0)),
                   pl.BlockSpec((hd, V_ROWS, tm), lambda i: (0, 0, i))],
        out_shape=[jax.ShapeDtypeStruct((hd, hp, n_rows), BF16),
                   jax.ShapeDtypeStruct((hd, n_rows, hp), BF16),
                   jax.ShapeDtypeStruct((hd, V_ROWS, n_rows), BF16)],
        compiler_params=_cparams(1),
        name="mla_qkv",
    )(p, p, p, cs, *args)


def _key_chunks(k_refs, v_refs, tk):
    chunks, s0 = [], 0
    for kr, vr in zip(k_refs, v_refs):
        n = kr.shape[1]
        step = min(tk, n)
        chunks += [(kr, vr, r0, step, s0 + r0) for r0 in range(0, n, step)]
        s0 += n
    return chunks


def _score_pass(qt, chunks, s_scr):
    m = None
    for kr, _, r0, rn, s0 in chunks:
        s = _dot(kr[0, r0:r0 + rn, :], qt)
        s_scr[s0:s0 + rn, :] = s
        mj = jnp.max(s, axis=0, keepdims=True)
        m = mj if m is None else jnp.maximum(m, mj)
    return m


def _value_pass(chunks, s_scr, m, o_ref):
    acc = None
    for _, vr, r0, rn, s0 in chunks:
        p = jnp.exp2((s_scr[s0:s0 + rn, :] - m).astype(BF16))
        part = _dot(vr[0, :, r0:r0 + rn], p)
        acc = part if acc is None else acc + part
    o_ref[...] = (acc[:MLA_V] / acc[MLA_V:MLA_V + 1]).astype(o_ref.dtype)


def _attn_ctx_kernel(q_ref, kc_ref, vc_ref, o_ref, s_scr, *, tk):
    chunks = _key_chunks([kc_ref], [vc_ref], tk)
    _value_pass(chunks, s_scr, _score_pass(q_ref[0], chunks, s_scr), o_ref)


def _attn_kernel(q_ref, k_ref, v_ref, kc_ref, vc_ref, o_ref, s_a, s_b, m_a, m_b, *, tk):
    i = pl.program_id(2)
    chunks = _key_chunks([kc_ref, k_ref], [vc_ref, v_ref], tk)

    @pl.when((pl.program_id(0) == 0) & (pl.program_id(1) == 0) & (i == 0))
    def _():
        s_b[...] = jnp.zeros_like(s_b)
        m_b[...] = jnp.zeros_like(m_b)

    def step(s_cur, m_cur, s_prev, m_prev):
        _value_pass(chunks, s_prev, m_prev[0:1, :], o_ref)
        m_cur[...] = jnp.broadcast_to(_score_pass(q_ref[0], chunks, s_cur), m_cur.shape)

    @pl.when(i % 2 == 0)
    def _():
        step(s_a, m_a, s_b, m_b)

    @pl.when(i % 2 == 1)
    def _():
        step(s_b, m_b, s_a, m_a)


def _attention(qt, k, vt, seq, ctx_len, n_batch, latent, tq=256, tk=1024):
    hd, hp, n_rows = qt.shape
    lat_rows = n_batch * seq
    cblk = lat_rows // ctx_len
    kc_spec = pl.BlockSpec((1, ctx_len, hp), lambda b, h, i: (h, cblk + b, 0))
    vc_spec = pl.BlockSpec((1, V_ROWS, ctx_len), lambda b, h, i: (h, 0, cblk + b))
    if latent:
        nq = seq // tq
        steps = nq + 1
        in_specs = [pl.BlockSpec((1, hp, tq), lambda b, h, i: (h, 0, b * nq + jnp.minimum(i, nq - 1))),
                    pl.BlockSpec((1, seq, hp), lambda b, h, i: (h, b, 0)),
                    pl.BlockSpec((1, V_ROWS, seq), lambda b, h, i: (h, 0, b)), kc_spec, vc_spec]
        args = (qt, k, vt, k, vt)
        out_spec = pl.BlockSpec((MLA_V, tq), lambda b, h, i: (h, b * nq + jnp.maximum(i - 1, 0)))
        out_cols, n_keys = lat_rows, seq + ctx_len
        body = functools.partial(_attn_kernel, tk=tk)
        scratch = [pltpu.VMEM((n_keys, tq), F32)] * 2 + [pltpu.VMEM((8, tq), F32)] * 2
    else:
        steps = 1
        tq = ctx_len
        in_specs = [pl.BlockSpec((1, hp, tq), lambda b, h, i: (h, 0, cblk + b)), kc_spec, vc_spec]
        args = (qt, k, vt)
        out_spec = pl.BlockSpec((MLA_V, tq), lambda b, h, i: (h, b))
        out_cols, n_keys = n_batch * ctx_len, ctx_len
        body = functools.partial(_attn_ctx_kernel, tk=tk)
        scratch = [pltpu.VMEM((n_keys, tq), F32)]
    return pl.pallas_call(
        body,
        grid=(n_batch, hd, steps),
        in_specs=in_specs,
        out_specs=out_spec,
        out_shape=jax.ShapeDtypeStruct((hd * MLA_V, out_cols), BF16),
        scratch_shapes=scratch,
        compiler_params=_cparams(3),
        name="mla_attention" if latent else "mla_attention_ctx",
    )(*args)


def _merge_kernel(*refs, lat_tiles):
    (x_ref, mod_ref, of_ref, ob_ref, g_ref, yb_ref, yc_ref, ga_ref, gb_ref, gc_ref,
     gain_ref, wa_ref, wb_ref, wc_ref, wo_ref) = refs[:15]
    ctx_refs = refs[15:-1]
    o_ref = refs[-1]
    is_ctx = pl.program_id(0) >= lat_tiles
    m = mod_ref[0]
    o = of_ref[...] + ob_ref[...]
    if ctx_refs:
        o = jnp.where(is_ctx, ctx_refs[0][...] + ctx_refs[1][...], o)
    gain = gain_ref[...]
    ya = jnp.concatenate([_rms(o[:, h * HG_D:(h + 1) * HG_D], gain) for h in range(HG_HEADS)], axis=1)
    ya = (ya * _silu(g_ref[...])).astype(BF16)
    nslab, tcols = yb_ref.shape[1], yb_ref.shape[3]
    qn = x_ref.shape[0] // yb_ref.shape[2]
    q0 = (pl.program_id(0) % (tcols // qn)) * qn
    yb = jnp.concatenate([jnp.concatenate([yb_ref[0, k, :, q0 + q, :] for k in range(nslab)], axis=1)
                          for q in range(qn)], axis=0)
    yc = yc_ref[...]
    if ctx_refs:
        yb = jnp.where(is_ctx, ctx_refs[2][...], yb)
        yc = jnp.where(is_ctx, ctx_refs[3][...], yc)
    mix = (_sigmoid(ga_ref[...]) * _dot(ya, wa_ref[...])
           + _sigmoid(gb_ref[...]) * _dot(yb.astype(BF16), wb_ref[...])
           + _sigmoid(gc_ref[...]) * _dot_tn(yc, wc_ref[...]))
    o_ref[...] = x_ref[...] + m[5:6] * _dot(mix.astype(BF16), wo_ref[...])


def _merge(xa, mod, o_f, o_b, p, y_b, y_c, ctx_parts, lw, n_rows, seq, n_batch, tm=512):
    d = xa.shape[1]
    hk = HG_HEADS * HG_D
    tpb = seq // tm
    lat_tiles = n_batch * tpb
    n1 = FFT_N1
    tcols = 8
    assert tm % n1 == 0 and tcols % (tm // n1) == 0 and seq % (tcols * n1) == 0
    per_blk = tcols // (tm // n1)
    y_b = y_b.reshape(n_batch, y_b.shape[1], n1, seq // n1, LANE)
    row = lambda w: pl.BlockSpec((tm, w), lambda i: (i, 0))
    pcol = lambda w, c: pl.BlockSpec((tm, w), lambda i: (i, c // w))
    full = lambda a: pl.BlockSpec(a.shape, lambda i: (0,) * a.ndim)
    ws = (lw["hg_out_norm"].reshape(1, HG_D), lw["w_br_a"].astype(BF16), lw["w_br_b"].astype(BF16),
          lw["w_br_c"].astype(BF16), lw["w_out"].astype(BF16))
    yb_spec = pl.BlockSpec((1, y_b.shape[1], n1, tcols, LANE),
                           lambda i: (jnp.minimum(i // tpb, n_batch - 1), 0, 0, (i % tpb) // per_blk, 0))
    lat = lambda i: jnp.minimum(i, lat_tiles - 1)
    late = lambda i: jnp.maximum(i - lat_tiles, 0)
    extra_specs, extra = [], []
    if ctx_parts is not None:
        extra_specs = [pl.BlockSpec((tm, hk), lambda i: (late(i), 0)),
                       pl.BlockSpec((tm, hk), lambda i: (late(i), 0)),
                       pl.BlockSpec((tm, HY_W), lambda i: (late(i), 0)),
                       pl.BlockSpec((MLA_HEADS * MLA_V, tm), lambda i: (0, late(i)))]
        extra = list(ctx_parts)
    return pl.pallas_call(
        functools.partial(_merge_kernel, lat_tiles=lat_tiles),
        grid=(n_rows // tm,),
        in_specs=[row(d), pl.BlockSpec((1, N_MOD, d), _group_map(tpb, n_batch)),
                  pl.BlockSpec((tm, hk), lambda i: (lat(i), 0)), pl.BlockSpec((tm, hk), lambda i: (lat(i), 0)),
                  pcol(hk, COL_G), yb_spec,
                  pl.BlockSpec((MLA_HEADS * MLA_V, tm), lambda i: (0, lat(i))),
                  pcol(d, COL_GA), pcol(d, COL_GB), pcol(d, COL_GC)] + [full(a) for a in ws] + extra_specs,
        out_specs=row(d),
        out_shape=jax.ShapeDtypeStruct((n_rows, d), F32),
        compiler_params=_cparams(1),
        name="merge",
    )(xa, mod, o_f, o_b, p, y_b, y_c, p, p, p, *ws, *extra)


def kernel(x, c, ctx, c_ctx, ada_w, ada_b, ffn1_norm, ffn1_w13, ffn1_w2, mix_norm, w_in, hg_lb_logits, hg_out_norm, hy_conv_w, hy_conv_b, hy_w1, hy_b1, hy_w2, hy_b2, hy_w3, hy_b3, hy_w4, hy_freq, hy_skip, q_a_norm, w_uq, kv_a_norm, w_ukv, q_nope_norm, q_rope_norm, k_nope_norm, k_rope_norm, w_br_a, w_br_b, w_br_c, w_out, ffn2_norm, ffn2_w13, ffn2_w2):
    stacked = dict(
        ada_w=ada_w, ada_b=ada_b, ffn1_norm=ffn1_norm, ffn1_w13=ffn1_w13, ffn1_w2=ffn1_w2, mix_norm=mix_norm,
        w_in=w_in, hg_out_norm=hg_out_norm, hy_conv_w=hy_conv_w, hy_conv_b=hy_conv_b, hy_w1=hy_w1, hy_b1=hy_b1,
        hy_w2=hy_w2, hy_b2=hy_b2, hy_w3=hy_w3, hy_b3=hy_b3, hy_w4=hy_w4, hy_freq=hy_freq, hy_skip=hy_skip,
        q_a_norm=q_a_norm, w_uq=w_uq, kv_a_norm=kv_a_norm, w_ukv=w_ukv, q_nope_norm=q_nope_norm,
        q_rope_norm=q_rope_norm, k_nope_norm=k_nope_norm, k_rope_norm=k_rope_norm, w_br_a=w_br_a,
        w_br_b=w_br_b, w_br_c=w_br_c, w_out=w_out, ffn2_norm=ffn2_norm, ffn2_w13=ffn2_w13, ffn2_w2=ffn2_w2)
    n_batch, seq, d = x.shape
    ctx_len = ctx.shape[1]
    depth = ada_w.shape[0]
    lat_rows, ctx_rows = n_batch * seq, n_batch * ctx_len
    all_rows = lat_rows + ctx_rows
    assert seq % 512 == 0 and ctx_rows % 512 == 0 and seq % ctx_len == 0 and seq % GRID_W == 0
    assert ctx_len % HG_CHUNK == 0 and n_batch < 8

    xa, xc = x.reshape(lat_rows, d), ctx.reshape(ctx_rows, d)
    cs = jnp.concatenate([c, c_ctx.reshape(1, d), jnp.zeros((7 - n_batch, d), F32)], axis=0)
    rope_cs = _rope_table(seq, n_batch, ctx_rows)
    zero_state = jnp.zeros((2, n_batch, HG_HEADS, HG_D, HG_D), F32)
    hk = HG_HEADS * HG_D

    for l in range(depth):
        lw = {name: val[l] for name, val in stacked.items()}
        need_ctx = l < depth - 1
        mod = _modulation(cs, lw["ada_w"], lw["ada_b"])
        xa = _half_ffn(xa, mod, lw["ffn1_norm"], lw["ffn1_w13"], lw["ffn1_w2"], 0, all_rows, seq, n_batch, xc=xc)
        xc = None
        p = _in_projection(xa, mod, lw["mix_norm"], _pack_w_in(lw["w_in"]).astype(BF16), seq, n_batch)

        ocf, ocb, s_c = _hgrn2_scan(p, hg_lb_logits, zero_state, l, lat_rows, ctx_len, n_batch)
        olf, olb, _ = _hgrn2_scan(p, hg_lb_logits, s_c, l, 0, seq, n_batch)
        y_b = _hyena(p, lw, 0, seq, n_batch)
        qt, k, vt = _mla_qkv(p, rope_cs, lw)
        y_c = _attention(qt, k, vt, seq, ctx_len, n_batch, latent=True)

        ctx_parts = None
        if need_ctx:
            ctx_parts = (ocf.reshape(ctx_rows, hk), ocb.reshape(ctx_rows, hk),
                         _hyena(p, lw, lat_rows, ctx_len, n_batch),
                         _attention(qt, k, vt, seq, ctx_len, n_batch, latent=False))
        mix_rows = all_rows if need_ctx else lat_rows
        xa = _merge(xa, mod, olf.reshape(lat_rows, hk), olb.reshape(lat_rows, hk), p, y_b, y_c, ctx_parts, lw,
                    mix_rows, seq, n_batch)
        xa = _half_ffn(xa, mod, lw["ffn2_norm"], lw["ffn2_w13"], lw["ffn2_w2"], 6, mix_rows, seq, n_batch)
    return xa[:lat_rows].reshape(n_batch, seq, d)
```

```python
import functools
import math

import numpy as np
import jax
import jax.numpy as jnp
from jax import lax
from jax.experimental import pallas as pl
from jax.experimental.pallas import tpu as pltpu

F32 = jnp.float32
BF16 = jnp.bfloat16

RMS_EPS = 1e-6
N_MOD = 9
GRID_W = 64
ROPE_THETA = 10000.0
HG_HEADS = 4
HG_D = 128
HG_CHUNK = 128
HY_W = 512
HY_TARGET = 1e-2
HY_MIN_DECAY = math.log(HY_TARGET) / 1.5
HY_MAX_DECAY = math.log(HY_TARGET) / 0.3
HY_SHIFT = 0.05
MLA_HEADS = 4
MLA_NOPE = 128
MLA_ROPE = 64
MLA_V = 128
MLA_QK = MLA_NOPE + MLA_ROPE
MLA_SCALE = MLA_QK ** -0.5
MLA_HEAD_PAD = 256
V_ROWS = MLA_V + 16
LOG2_E = math.log2(math.e)
FFT_N1 = 128
LANE = 128

VMEM_LIMIT = 52 * 1024 * 1024

COL_GA, COL_GB, COL_GC = 0, 1024, 2048
COL_Q, COL_ZF, COL_ZB, COL_IV, COL_G = 3072, 3584, 4096, 4608, 5120
COL_HY = 5632
COL_QA, COL_KVA, COL_KR = 7168, 7424, 7552
IN_PACKED = 7680


def _cparams(n_axes):
    return pltpu.CompilerParams(dimension_semantics=("arbitrary",) * n_axes,
                                vmem_limit_bytes=VMEM_LIMIT)


def _dot(a, b):
    return jnp.dot(a, b, preferred_element_type=F32)


def _dot_nt(a, b):
    return lax.dot_general(a, b, (((1,), (1,)), ((), ())), preferred_element_type=F32)


def _dot_tn(a, b):
    return lax.dot_general(a, b, (((0,), (0,)), ((), ())), preferred_element_type=F32)


def _sigmoid(x):
    return 1.0 / (1.0 + jnp.exp(-x))


def _silu(x):
    return x * _sigmoid(x)


def _rms(x, gain):
    return x * lax.rsqrt(jnp.mean(x * x, axis=-1, keepdims=True) + RMS_EPS) * gain


def _mod_kernel(c_ref, w_ref, b_ref, o_ref):
    o_ref[...] = _dot(_silu(c_ref[...]).astype(BF16), w_ref[...]) + b_ref[...]


def _modulation(cs, ada_w, ada_b):
    g, d = cs.shape
    n = ada_w.shape[1]
    tn = n // 4
    out = pl.pallas_call(
        _mod_kernel,
        grid=(n // tn,),
        in_specs=[pl.BlockSpec((g, d), lambda j: (0, 0)),
                  pl.BlockSpec((d, tn), lambda j: (0, j)),
                  pl.BlockSpec((1, tn), lambda j: (0, j))],
        out_specs=pl.BlockSpec((g, tn), lambda j: (0, j)),
        out_shape=jax.ShapeDtypeStruct((g, n), F32),
        compiler_params=_cparams(1),
        name="modulation",
    )(cs, ada_w.astype(BF16), ada_b.reshape(1, n))
    return out.reshape(g, N_MOD, d)


def _ffn_kernel(*refs, idx, ff, ck, lat_tiles):
    x_ref, mod_ref, g_ref, w13_ref, w2_ref = refs[:5]
    o_ref = refs[-1]
    x = x_ref[...]
    if len(refs) == 7:
        x = jnp.where(pl.program_id(0) >= lat_tiles, refs[5][...], x)
    m = mod_ref[0]
    h = (_rms(x, g_ref[...]) * (1.0 + m[idx + 1:idx + 2]) + m[idx:idx + 1]).astype(BF16)
    acc = jnp.zeros(x.shape, F32)
    for c0 in range(0, ff, ck):
        c1 = min(c0 + ck, ff)
        a = _dot(h, w13_ref[:, c0:c1])
        b = _dot(h, w13_ref[:, ff + c0:ff + c1])
        acc = acc + _dot((_silu(a) * b).astype(BF16), w2_ref[c0:c1, :])
    o_ref[...] = x + (0.5 * m[idx + 2:idx + 3]) * acc


def _group_map(tiles_per_batch, n_batch):
    return lambda i: (jnp.minimum(i // tiles_per_batch, n_batch), 0, 0)


def _half_ffn(xa, mod, gain, w13, w2, idx, n_rows, seq, n_batch, xc=None, tm=512, ck=512):
    d = xa.shape[1]
    ff = w2.shape[0]
    lat_tiles = n_batch * seq // tm
    const = dict(pipeline_mode=pl.Buffered(1))
    x_specs, xs = [pl.BlockSpec((tm, d), lambda i: (jnp.minimum(i, xa.shape[0] // tm - 1), 0))], [xa]
    tail_specs, tail = [], []
    if xc is not None:
        tail_specs, tail = [pl.BlockSpec((tm, d), lambda i: (jnp.maximum(i - lat_tiles, 0), 0))], [xc]
    return pl.pallas_call(
        functools.partial(_ffn_kernel, idx=idx, ff=ff, ck=ck, lat_tiles=lat_tiles),
        grid=(n_rows // tm,),
        in_specs=x_specs + [pl.BlockSpec((1, N_MOD, d), _group_map(seq // tm, n_batch)),
                            pl.BlockSpec((1, d), lambda i: (0, 0)),
                            pl.BlockSpec((d, 2 * ff), lambda i: (0, 0), **const),
                            pl.BlockSpec((ff, d), lambda i: (0, 0), **const)] + tail_specs,
        out_specs=pl.BlockSpec((tm, d), lambda i: (i, 0)),
        out_shape=jax.ShapeDtypeStruct((n_rows, d), F32),
        compiler_params=_cparams(1),
        name="half_ffn",
    )(*xs, mod, gain.reshape(1, d), w13.astype(BF16), w2.astype(BF16), *tail)


def _inproj_kernel(x_ref, mod_ref, g_ref, w_ref, o_ref):
    m = mod_ref[0]
    h = (_rms(x_ref[...], g_ref[...]) * (1.0 + m[4:5]) + m[3:4]).astype(BF16)
    o_ref[...] = _dot(h, w_ref[...])


def _pack_w_in(w_in):
    d = w_in.shape[0]
    hk = HG_HEADS * HG_D
    sizes = (hk, hk, hk, hk, hk, 3 * HY_W, 256, 128, MLA_ROPE, d, d, d)
    offs = np.cumsum((0,) + sizes)
    q, zf, zb, iv, g, hy, qa, kva, kr, ga, gb, gc = (w_in[:, offs[i]:offs[i + 1]] for i in range(12))
    return jnp.concatenate([ga, gb, gc, q, zf, zb, iv, g, hy, qa, kva, kr, _rope_swap(kr)], axis=1)


def _rope_swap(a):
    q = MLA_ROPE // 4
    return jnp.concatenate([a[..., q:2 * q], a[..., :q], a[..., 3 * q:], a[..., 2 * q:3 * q]], axis=-1)


def _in_projection(xa, mod, gain, w_packed, seq, n_batch, tm=512, tn=2560):
    n_rows, d = xa.shape
    n = w_packed.shape[1]
    return pl.pallas_call(
        _inproj_kernel,
        grid=(n // tn, n_rows // tm),
        in_specs=[pl.BlockSpec((tm, d), lambda j, i: (i, 0)),
                  pl.BlockSpec((1, N_MOD, d), lambda j, i: (jnp.minimum(i // (seq // tm), n_batch), 0, 0)),
                  pl.BlockSpec((1, d), lambda j, i: (0, 0)),
                  pl.BlockSpec((d, tn), lambda j, i: (0, j))],
        out_specs=pl.BlockSpec((tm, tn), lambda j, i: (i, j)),
        out_shape=jax.ShapeDtypeStruct((n_rows, n), F32),
        compiler_params=_cparams(2),
        name="in_projection",
    )(xa, mod, gain.reshape(1, d), w_packed)


def _hgrn2_tables(c, rev):
    t = np.arange(c)[:, None]
    u = np.arange(c)[None, :]
    mats = [(u <= t), (u > t)]
    masks = [(t == u)]
    h = c // 2
    while h >= 1:
        mid = (t // (2 * h)) * (2 * h) + h
        mats.append(np.where(t >= mid, (u >= mid) & (u <= t), (u >= t + 1) & (u <= mid - 1)))
        mid_s = (u // (2 * h)) * (2 * h) + h
        masks.append((t // (2 * h) == u // (2 * h)) & (u < mid_s) & (t >= mid))
        h //= 2
    mats = np.stack([m.astype(np.float32) for m in mats])
    masks = np.stack([m.astype(np.float32) for m in masks])
    if rev:
        mats = mats[:, ::-1, ::-1]
        masks = masks[:, ::-1, ::-1]
    mats = np.ascontiguousarray(mats).reshape(-1, c)
    return (jnp.asarray(np.concatenate([mats, mats], axis=1), BF16),
            jnp.asarray(np.ascontiguousarray(masks), F32))


HG_PAIR = 4


def _hgrn2_chunk(q, z, v, lg, dst_ref, msk_ref, st_ref, o_ref, ci, *, layer, rev, n_levels):
    c = q.shape[0]
    hk = HG_HEADS * HG_D
    e = jnp.exp(lg - jnp.max(lg, axis=0, keepdims=True))
    sm = e / jnp.sum(e, axis=0, keepdims=True)
    lb = jnp.zeros((1, hk), F32)
    for i in range(1, layer + 1):
        lb = lb + sm[i:i + 1]

    f = lb + (1.0 - lb) * _sigmoid(z)
    kk = 1.0 - f
    g = jnp.log(f)
    g_hi = g.astype(BF16)
    g_lo = (g - g_hi.astype(F32)).astype(BF16)
    dg = _dot(dst_ref[...], jnp.concatenate([g_hi, g_lo], axis=0))
    qs = _silu(q) * HG_D ** -0.5
    last = 0 if rev else c - 1

    for h in range(HG_HEADS):
        hs = slice(h * HG_D, (h + 1) * HG_D)
        qh, kh, vh = qs[:, hs].astype(BF16), kk[:, hs].astype(BF16), v[:, hs].astype(BF16)
        b_in = dg[0:c, hs]
        ex = lambda blk: jnp.exp(dg[blk * c:(blk + 1) * c, hs].astype(BF16))
        a = _dot_nt(qh, kh) * msk_ref[0]
        for l in range(1, n_levels + 1):
            el = ex(1 + l)
            a = a + _dot_nt(qh * el, kh * el) * msk_ref[l]
        st = st_ref[ci, h]
        o = _dot_nt(qh * ex(0), st.astype(BF16)) + _dot(a.astype(BF16), vh)
        o_ref[:, hs] = o.astype(o_ref.dtype)
        st_ref[ci, h] = jnp.exp(b_in[last:last + 1, :]) * st + _dot_tn(vh, kh * ex(1))


def _hgrn2_kernel(*refs, layer, n_levels, pair):
    n_in = 6 * pair
    ins, (lg_ref, dstf_ref, mskf_ref, dstb_ref, mskb_ref, s0_ref) = refs[:n_in], refs[n_in:n_in + 6]
    of_ref, ob_ref, sf_ref, st_ref = refs[n_in + 6:]
    j = pl.program_id(1)

    @pl.when(j == 0)
    def _():
        st_ref[...] = s0_ref[...].reshape(st_ref.shape)

    for e in range(pair):
        qf, zf, vf, qb, zb, vb = (r[...] for r in ins[6 * e:6 * e + 6])
        _hgrn2_chunk(qf, zf, vf, lg_ref[0], dstf_ref, mskf_ref, st_ref, of_ref.at[e], e,
                     layer=layer, rev=False, n_levels=n_levels)
        _hgrn2_chunk(qb, zb, vb, lg_ref[1], dstb_ref, mskb_ref, st_ref, ob_ref.at[e], pair + e,
                     layer=layer, rev=True, n_levels=n_levels)

    @pl.when(j == pl.num_programs(1) - 1)
    def _():
        sf_ref[...] = st_ref[...].reshape(sf_ref.shape)


def _hgrn2_scan(p, logits, s0, layer, row0, seq, n_batch):
    c = HG_CHUNK
    hk = HG_HEADS * HG_D
    nc = seq // c
    base = row0 // c
    n_levels = int(math.log2(c))
    tabs = _hgrn2_tables(c, False) + _hgrn2_tables(c, True)
    pair = math.gcd(HG_PAIR, n_batch)

    def prow(w, e, rev):
        return pl.BlockSpec((c, hk), lambda bp, j: (base + (bp * pair + e) * nc + (nc - 1 - j if rev else j),
                                                   w // hk))

    in_specs = []
    for e in range(pair):
        in_specs += [prow(COL_Q, e, False), prow(COL_ZF, e, False), prow(COL_IV, e, False),
                     prow(COL_Q, e, True), prow(COL_ZB, e, True), prow(COL_IV, e, True)]
    full = lambda a: pl.BlockSpec(a.shape, lambda bp, j: (0,) * a.ndim)
    st_spec = pl.BlockSpec((2, pair, HG_HEADS, HG_D, HG_D), lambda bp, j: (0, bp, 0, 0, 0))
    return pl.pallas_call(
        functools.partial(_hgrn2_kernel, layer=layer, n_levels=n_levels, pair=pair),
        grid=(n_batch // pair, nc),
        in_specs=in_specs + [full(logits)] + [full(t) for t in tabs] + [st_spec],
        out_specs=[pl.BlockSpec((pair, c, hk), lambda bp, j: (bp, j, 0)),
                   pl.BlockSpec((pair, c, hk), lambda bp, j: (bp, nc - 1 - j, 0)),
                   st_spec],
        out_shape=[jax.ShapeDtypeStruct((n_batch, seq, hk), BF16),
                   jax.ShapeDtypeStruct((n_batch, seq, hk), BF16),
                   jax.ShapeDtypeStruct((2, n_batch, HG_HEADS, HG_D, HG_D), F32)],
        scratch_shapes=[pltpu.VMEM((2 * pair, HG_HEADS, HG_D, HG_D), F32)],
        compiler_params=_cparams(2),
        name="hgrn2_scan",
    )(*([p] * (6 * pair)), logits, *tabs, s0)


def _shortconv_kernel(u_ref, w_ref, b_ref, o_ref):
    u = u_ref[...]
    n = u.shape[0]
    row = lax.broadcasted_iota(jnp.int32, u.shape, 0)
    prev = jnp.where(row == 0, 0.0, pltpu.roll(u, 1, 0))
    nxt = jnp.where(row == n - 1, 0.0, pltpu.roll(u, n - 1, 0))
    w = w_ref[...]
    o_ref[0, 0, 0] = prev * w[0:1] + u * w[1:2] + nxt * w[2:3] + b_ref[...]


def _shortconv_t1major_kernel(u_ref, w_ref, b_ref, o_ref, u_scr, *, n1):
    half = u_ref.shape[0] // n1
    w = w_ref[...]
    b = b_ref[...]
    pitch = n1 + 8

    def copy(g, carry):
        u_scr[pl.ds(pl.multiple_of(g * pitch, 8), n1), :] = u_ref[pl.ds(pl.multiple_of(g * n1, 8), n1), :]
        return carry

    lax.fori_loop(0, half, copy, 0, unroll=8)
    row = lax.broadcasted_iota(jnp.int32, (half, LANE), 0)
    col = lambda t1: u_scr[pl.ds(t1, half, stride=pitch), :]
    before = jnp.where(row == 0, 0.0, pltpu.roll(col(n1 - 1), 1, 0))
    after = jnp.where(row == half - 1, 0.0, pltpu.roll(col(0), half - 1, 0))

    def body(t1, carry):
        prev, cur = carry
        nxt = jnp.where(t1 == n1 - 1, after, col(jnp.minimum(t1 + 1, n1 - 1)))
        o_ref[0, 0, 0, pl.ds(pl.multiple_of(t1 * half, 8), half), :] = prev * w[0:1] + cur * w[1:2] + nxt * w[2:3] + b
        return cur, nxt

    lax.fori_loop(0, n1, body, (before, col(0)), unroll=16)


def _short_conv(p, w, b, row0, seq, n_batch, n1):
    nb = 3 * HY_W // LANE
    per = HY_W // LANE
    body, scratch = _shortconv_kernel, []
    if n1 > 1:
        body = functools.partial(_shortconv_t1major_kernel, n1=n1)
        scratch = [pltpu.VMEM((seq // n1 * (n1 + 8), LANE), F32)]
    return pl.pallas_call(
        body,
        grid=(n_batch, nb),
        in_specs=[pl.BlockSpec((seq, LANE), lambda bi, j: (row0 // seq + bi, COL_HY // LANE + j)),
                  pl.BlockSpec((3, LANE), lambda bi, j: (0, j)),
                  pl.BlockSpec((1, LANE), lambda bi, j: (0, j))],
        out_specs=pl.BlockSpec((1, 1, 1, seq, LANE), lambda bi, j: (j // per, bi, j % per, 0, 0)),
        out_shape=jax.ShapeDtypeStruct((3, n_batch, per, seq, LANE), F32),
        scratch_shapes=scratch,
        compiler_params=_cparams(2),
        name="hyena_short_conv",
    )(p, w, b.reshape(1, -1))


def _hy_filter_kernel(emb_ref, w1_ref, b1_ref, w2_ref, b2_ref, w3_ref, b3_ref, w4_ref, fr_ref,
                      dl_ref, o_ref, nrm_ref, *, seq, group):
    i = pl.program_id(0)
    hp = lax.Precision.HIGHEST
    fr = fr_ref[...]
    hid = jnp.sin(fr * (jnp.dot(emb_ref[...], w1_ref[...], precision=hp) + b1_ref[...]))
    hid = jnp.sin(fr * (jnp.dot(hid, w2_ref[...], precision=hp) + b2_ref[...]))
    hid = jnp.sin(fr * (jnp.dot(hid, w3_ref[...], precision=hp) + b3_ref[...]))
    h = jnp.dot(hid, w4_ref[...], precision=hp)
    tl, n = h.shape
    r = lax.broadcasted_iota(jnp.int32, h.shape, 0) + i * tl
    col = lax.broadcasted_iota(jnp.int32, h.shape, 1)
    per = seq // group
    pos = (r >> (per.bit_length() - 1)) + group * (r & (per - 1))
    t = pos.astype(F32) * (1.0 / (seq - 1))
    h = h * (jnp.exp(-t * dl_ref[...]) + HY_SHIFT)
    h = jnp.where((pos == 0) & (col >= n // 2), 0.0, h)
    for k in range(n // LANE):
        o_ref[k] = h[:, k * LANE:(k + 1) * LANE]

    @pl.when(i == 0)
    def _():
        nrm_ref[...] = jnp.zeros_like(nrm_ref)

    nrm_ref[...] += jnp.sum(jnp.abs(h), axis=0, keepdims=True)


def _hyena_filter_taps(seq, w1, b1, w2, b2, w3, b3, w4, freq, group):
    fh = w1.shape[1]
    n_emb = w1.shape[0]
    bands_n = (n_emb - 1) // 2
    tt = np.linspace(0.0, 1.0, seq, dtype=np.float32)[:, None].astype(np.float64)
    ww = (2.0 * math.pi / seq) * np.arange(seq, dtype=np.float64)[:, None]
    bands = np.linspace(1e-4, bands_n - 1, bands_n, dtype=np.float32)[None, :].astype(np.float64)
    emb = np.concatenate([tt, np.cos(bands * ww), -np.sin(bands * ww)], axis=-1)
    emb = np.pad(emb, ((0, 0), (0, LANE - n_emb))).astype(np.float32)
    assert seq % group == 0 and group & (group - 1) == 0 and (seq // group) & (seq // group - 1) == 0
    r = np.arange(seq)
    pos = r // (seq // group) + group * (r % (seq // group))
    emb = emb[pos]
    w1p = jnp.pad(w1, ((0, LANE - n_emb), (0, 0)))
    deltas = np.abs(np.linspace(HY_MIN_DECAY, HY_MAX_DECAY, HY_W, dtype=np.float32))
    deltas = np.tile(deltas, 4)[None, :]
    n = w4.shape[1]
    tl = min(seq, 512)
    full = lambda a: pl.BlockSpec(a.shape, lambda i: (0,) * a.ndim)
    args = (w1p, b1.reshape(1, fh), w2, b2.reshape(1, fh), w3, b3.reshape(1, fh), w4, freq.reshape(1, fh),
            jnp.asarray(deltas))
    return pl.pallas_call(
        functools.partial(_hy_filter_kernel, seq=seq, group=group),
        grid=(seq // tl,),
        in_specs=[pl.BlockSpec((tl, LANE), lambda i: (i, 0))] + [full(a) for a in args],
        out_specs=[pl.BlockSpec((n // LANE, tl, LANE), lambda i: (0, i, 0)),
                   pl.BlockSpec((1, n), lambda i: (0, 0))],
        out_shape=[jax.ShapeDtypeStruct((n // LANE, seq, LANE), F32), jax.ShapeDtypeStruct((1, n), F32)],
        compiler_params=_cparams(1),
        name="hyena_filter_taps",
    )(jnp.asarray(emb), *args)


def _fft_split(seq):
    n = 2 * seq
    n1 = FFT_N1 if n > 1024 else 1
    return n, n1, n // n1


def _cis(idx, n):
    ph = 2.0 * np.pi * (idx % n) / n
    return np.cos(ph), -np.sin(ph)


def _dft_tables_short(seq):
    n = 2 * seq
    cr, ci = _cis(np.arange(n)[:, None] * np.arange(seq)[None, :], n)
    w_fwd = np.concatenate([cr, ci], axis=0)
    w_inv = np.concatenate([cr.T, ci.T], axis=1) / n
    return dict(w_fwd=jnp.asarray(w_fwd, BF16), w_inv=jnp.asarray(w_inv, BF16))


def _dft_tables_long(seq):
    n, n1, n2 = _fft_split(seq)
    t1 = np.arange(n1)[:, None, None]
    f2 = np.arange(n2)[None, :, None]
    t2 = np.arange(n2 // 2)[None, None, :]
    cr, ci = _cis(f2 * (t1 + n1 * t2), n)
    w1 = np.concatenate([cr, ci], axis=1)
    w4 = np.concatenate([np.swapaxes(cr, 1, 2), np.swapaxes(ci, 1, 2)], axis=2) / n
    j = (n1 - t1) % n1 + n1 * t2
    br, bi = _cis(f2 * (n - j), n)
    w1f = np.concatenate([np.concatenate([cr, br], axis=2), np.concatenate([ci, bi], axis=2)], axis=1)
    gr, gi = _cis(np.arange(n1)[:, None] * np.arange(n1)[None, :], n1)
    wd = np.concatenate([np.concatenate([gr, -gi], axis=1), np.concatenate([gi, gr], axis=1)], axis=0)
    return dict(w1=jnp.asarray(np.swapaxes(w1, 1, 2), BF16), w4=jnp.asarray(w4, BF16),
                w1f=jnp.asarray(np.swapaxes(w1f, 1, 2), BF16),
                wd=jnp.asarray(wd, BF16), wdi=jnp.asarray(wd.T, BF16))


def _dft_rows_kernel(w_ref, x_ref, o_ref):
    o_ref[0] = _dot(w_ref[...], x_ref[0].astype(BF16)).astype(o_ref.dtype)


def _dft_rows(w, x, tn):
    nb, k, cols = x.shape
    m = w.shape[0]
    return pl.pallas_call(
        _dft_rows_kernel,
        grid=(nb, cols // tn),
        in_specs=[pl.BlockSpec((m, k), lambda b, j: (0, 0)),
                  pl.BlockSpec((1, k, tn), lambda b, j: (b, 0, j))],
        out_specs=pl.BlockSpec((1, m, tn), lambda b, j: (b, 0, j)),
        out_shape=jax.ShapeDtypeStruct((nb, m, cols), BF16),
        compiler_params=_cparams(2),
        name="hyena_dft_rows",
    )(w, x)


def _idft_gate_kernel(w_ref, b_ref, xg_ref, z_ref, sk_ref, o_ref):
    y = _dot(w_ref[...], b_ref[0])
    z = z_ref[0]
    o_ref[0] = xg_ref[0] * (y + z * sk_ref[...])


def _idft_gate(w, bc, xg, z, skip_t, tn):
    nb, k, cols = bc.shape
    m = w.shape[0]
    return pl.pallas_call(
        _idft_gate_kernel,
        grid=(nb, cols // tn),
        in_specs=[pl.BlockSpec((m, k), lambda b, j: (0, 0)),
                  pl.BlockSpec((1, k, tn), lambda b, j: (b, 0, j)),
                  pl.BlockSpec((1, m, tn), lambda b, j: (b, 0, j)),
                  pl.BlockSpec((1, m, tn), lambda b, j: (b, 0, j)),
                  pl.BlockSpec((1, tn), lambda b, j: (0, j))],
        out_specs=pl.BlockSpec((1, m, tn), lambda b, j: (b, 0, j)),
        out_shape=jax.ShapeDtypeStruct((nb, m, cols), F32),
        compiler_params=_cparams(2),
        name="hyena_idft_gate",
    )(w, bc, xg, z, skip_t)


def _spec_combine_kernel(af_ref, ab_ref, nf_ref, nb_ref, o_ref):
    inv = 1.0 / (nf_ref[...] + nb_ref[...])
    n = o_ref.shape[1]
    o_ref[0] = (af_ref[0, :n].astype(F32) + ab_ref[0, :n].astype(F32)) * inv
    o_ref[1] = (af_ref[0, n:].astype(F32) - ab_ref[0, n:].astype(F32)) * inv


def _spec_mul_kernel(a_ref, k_ref, o_ref):
    n = k_ref.shape[1]
    xr, xi = a_ref[0, :n].astype(F32), a_ref[0, n:].astype(F32)
    kr, ki = k_ref[0], k_ref[1]
    o_ref[0, :n] = (xr * kr - xi * ki).astype(o_ref.dtype)
    o_ref[0, n:] = (xr * ki + xi * kr).astype(o_ref.dtype)


def _hyena_short(seq, taps, nrm, uc, skip):
    n = 2 * seq
    uc = jnp.swapaxes(uc, 2, 3).reshape(3, uc.shape[1], seq, HY_W)
    tabs = _dft_tables_short(seq)
    cf = taps.shape[1]
    c = HY_W
    a = _dft_rows(tabs["w_fwd"], taps.reshape(1, seq, cf), tn=cf)
    nblk = cf // 2 // c
    spec = pl.pallas_call(
        _spec_combine_kernel,
        grid=(nblk,),
        in_specs=[pl.BlockSpec((1, 2 * n, c), lambda j: (0, 0, j)),
                  pl.BlockSpec((1, 2 * n, c), lambda j: (0, 0, j + nblk)),
                  pl.BlockSpec((1, c), lambda j: (0, j)),
                  pl.BlockSpec((1, c), lambda j: (0, j + nblk))],
        out_specs=pl.BlockSpec((2, n, c), lambda j: (0, 0, j)),
        out_shape=jax.ShapeDtypeStruct((2, n, cf // 2), F32),
        compiler_params=_cparams(1),
        name="hyena_spec_combine",
    )(a, a, nrm, nrm)
    nb = uc.shape[1]
    z = uc[2]
    for order in range(2):
        a = _dft_rows(tabs["w_fwd"], z, tn=c)
        bc = pl.pallas_call(
            _spec_mul_kernel,
            grid=(nb,),
            in_specs=[pl.BlockSpec((1, 2 * n, c), lambda b: (b, 0, 0)),
                      pl.BlockSpec((2, n, c), lambda b: (0, 0, order))],
            out_specs=pl.BlockSpec((1, 2 * n, c), lambda b: (b, 0, 0)),
            out_shape=jax.ShapeDtypeStruct((nb, 2 * n, c), BF16),
            compiler_params=_cparams(1),
            name="hyena_spec_mul",
        )(a, spec)
        z = _idft_gate(tabs["w_inv"], bc, uc[order], z, skip[order].reshape(1, c), tn=c)
    return z


HY_SLABS = 2


def _slab_pitch(n2):
    return n2 + 8


def _pack_c(re, im):
    hi = lax.bitcast_convert_type(re.astype(BF16).astype(F32), jnp.uint32)
    lo = lax.bitcast_convert_type(im.astype(BF16).astype(F32), jnp.uint32)
    return hi | (lo >> 16)


def _unpack_c(w):
    re = lax.bitcast_convert_type(w & jnp.uint32(0xFFFF0000), F32)
    im = lax.bitcast_convert_type(w << 16, F32)
    return re.astype(BF16), im.astype(BF16)


def _store_slab(a_scr, row, words):
    for s in range(HY_SLABS):
        a_scr[s, pl.ds(row, words.shape[0]), :] = words[:, s * LANE:(s + 1) * LANE]


def _stage1(a_scr, w_ref, xs_of, i, tb, n2, pitch):
    for j in range(tb):
        a = _dot_tn(w_ref[j], xs_of(j))
        _store_slab(a_scr, pl.multiple_of((i * tb + j) * pitch, 8), _pack_c(a[:n2], a[n2:]))


def _stage2(a_scr, wd_ref, f2, n1, pitch):
    w = jnp.concatenate([a_scr[s, pl.ds(f2, n1, stride=pitch), :] for s in range(HY_SLABS)], axis=1)
    re, im = _unpack_c(w)
    return _dot(wd_ref[...], jnp.concatenate([re, im], axis=0))


def _time_col(ref, j, half):
    lead = (0,) * (len(ref.shape) - 3)
    return jnp.concatenate([ref[lead + (s, slice(j * half, (j + 1) * half), slice(None))]
                            for s in range(HY_SLABS)], axis=1)


def _hy_spectrum_kernel(hf_ref, hb_ref, hb0_ref, nf_ref, nb_ref, w1f_ref, wd_ref, o_ref, a_scr,
                        *, n1, n2, tb, fb):
    i = pl.program_id(1)
    pitch = _slab_pitch(n2)
    half = n2 // 2
    nt = n1 // tb

    @pl.when(i < nt)
    def _():
        def xs_of(j):
            fut = _time_col(hb0_ref, 0, half) if j == 0 else _time_col(hb_ref, tb - j, half)
            return jnp.concatenate([_time_col(hf_ref, j, half), fut], axis=0).astype(BF16)
        _stage1(a_scr, w1f_ref, xs_of, i, tb, n2, pitch)

    @pl.when(i >= nt)
    def _():
        inv = 1.0 / (nf_ref[...] + nb_ref[...])
        for jj in range(fb):
            x = _stage2(a_scr, wd_ref, (i - nt) * fb + jj, n1, pitch)
            o_ref[0, 0, jj] = (x[:n1] * inv).astype(o_ref.dtype)
            o_ref[0, 1, jj] = (x[n1:] * inv).astype(o_ref.dtype)


def _hy_fftconv_kernel(z_ref, xg_ref, k_ref, sk_ref, w1_ref, wd_ref, wdi_ref, w4_ref, o_ref, a_scr,
                       *, n1, n2, tb, fb):
    i = pl.program_id(2)
    pitch = _slab_pitch(n2)
    half = n2 // 2
    nt, nf = n1 // tb, n2 // fb

    @pl.when(i < nt)
    def _():
        _stage1(a_scr, w1_ref, lambda j: _time_col(z_ref, j, half).astype(BF16), i, tb, n2, pitch)

    @pl.when((i >= nt) & (i < nt + nf))
    def _():
        for jj in range(fb):
            f2 = (i - nt) * fb + jj
            x = _stage2(a_scr, wd_ref, f2, n1, pitch)
            xr, xi = x[:n1], x[n1:]
            kr, ki = k_ref[0, 0, jj].astype(F32), k_ref[0, 1, jj].astype(F32)
            y = jnp.concatenate([xr * kr - xi * ki, xr * ki + xi * kr], axis=0).astype(BF16)
            bv = _dot(wdi_ref[...], y)
            words = _pack_c(bv[:n1], bv[n1:])
            for s in range(HY_SLABS):
                a_scr[s, pl.ds(f2, n1, stride=pitch), :] = words[:, s * LANE:(s + 1) * LANE]

    @pl.when(i >= nt + nf)
    def _():
        sk = sk_ref[...]
        for j in range(tb):
            row = pl.multiple_of(((i - nt - nf) * tb + j) * pitch, 8)
            w = jnp.concatenate([a_scr[s, pl.ds(row, n2), :] for s in range(HY_SLABS)], axis=1)
            re, im = _unpack_c(w)
            y = _dot(w4_ref[j], jnp.concatenate([re, im], axis=0))
            out = _time_col(xg_ref, j, half) * (y + _time_col(z_ref, j, half) * sk)
            for s in range(HY_SLABS):
                o_ref[0, s, j * half:(j + 1) * half, :] = out[:, s * LANE:(s + 1) * LANE]


def _hyena_long(seq, taps, nrm, uc, skip, tb=32, fb=32):
    n, n1, n2 = _fft_split(seq)
    tabs = _dft_tables_long(seq)
    pitch = _slab_pitch(n2)
    half = n2 // 2
    cb = HY_SLABS * LANE
    tb, fb = min(tb, n1), min(fb, n2)
    nt, nf = n1 // tb, n2 // fb
    cf = taps.shape[0] * LANE
    ngrp = cf // 2 // cb
    scratch = [pltpu.VMEM((HY_SLABS, n1 * pitch, LANE), jnp.uint32)]
    tcol = lambda i: jnp.minimum(i, nt - 1)
    spec = pl.pallas_call(
        functools.partial(_hy_spectrum_kernel, n1=n1, n2=n2, tb=tb, fb=fb),
        grid=(ngrp, nt + nf),
        in_specs=[pl.BlockSpec((HY_SLABS, tb * half, LANE), lambda g, i: (g, tcol(i), 0)),
                  pl.BlockSpec((HY_SLABS, tb * half, LANE), lambda g, i: (ngrp + g, nt - 1 - tcol(i), 0)),
                  pl.BlockSpec((HY_SLABS, tb * half, LANE), lambda g, i: (ngrp + g, (nt - tcol(i)) % nt, 0)),
                  pl.BlockSpec((1, cb), lambda g, i: (0, g)),
                  pl.BlockSpec((1, cb), lambda g, i: (0, ngrp + g)),
                  pl.BlockSpec((tb, n2, 2 * n2), lambda g, i: (tcol(i), 0, 0)),
                  pl.BlockSpec(tabs["wd"].shape, lambda g, i: (0, 0))],
        out_specs=pl.BlockSpec((1, 2, fb, n1, cb), lambda g, i: (g, 0, jnp.maximum(i - nt, 0), 0, 0)),
        out_shape=jax.ShapeDtypeStruct((ngrp, 2, n2, n1, cb), BF16),
        scratch_shapes=scratch,
        compiler_params=_cparams(2),
        name="hyena_filter_spectrum",
    )(taps, taps, taps, nrm, nrm, tabs["w1f"], tabs["wd"])
    nb, nblk = uc.shape[1], uc.shape[2]
    ngc = nblk // HY_SLABS
    z, zsel = uc, 2

    def tblk(b, g, i):
        return (b, g, jnp.where(i < nt, i, jnp.maximum(i - nt - nf, 0)), 0)

    def tblk_late(b, g, i):
        return (b, g, jnp.maximum(i - nt - nf, 0), 0)

    for order in range(2):
        stacked = lambda sel, f: (lambda b, g, i: (sel,) + f(b, g, i))
        z = pl.pallas_call(
            functools.partial(_hy_fftconv_kernel, n1=n1, n2=n2, tb=tb, fb=fb),
            grid=(nb, ngc, 2 * nt + nf),
            in_specs=[pl.BlockSpec((1, 1, HY_SLABS, tb * half, LANE), stacked(zsel, tblk)),
                      pl.BlockSpec((1, 1, HY_SLABS, tb * half, LANE), stacked(order, tblk_late)),
                      pl.BlockSpec((1, 2, fb, n1, cb),
                                   lambda b, g, i: (order * ngc + g, 0, jnp.clip(i - nt, 0, nf - 1), 0, 0)),
                      pl.BlockSpec((1, cb), lambda b, g, i: (0, g)),
                      pl.BlockSpec((tb, half, 2 * n2), lambda b, g, i: (tcol(i), 0, 0)),
                      pl.BlockSpec(tabs["wd"].shape, lambda b, g, i: (0, 0)),
                      pl.BlockSpec(tabs["wdi"].shape, lambda b, g, i: (0, 0)),
                      pl.BlockSpec((tb, half, 2 * n2), lambda b, g, i: (jnp.maximum(i - nt - nf, 0), 0, 0))],
            out_specs=pl.BlockSpec((1, HY_SLABS, tb * half, LANE), tblk_late),
            out_shape=jax.ShapeDtypeStruct((nb, nblk, seq, LANE), F32),
            scratch_shapes=scratch,
            compiler_params=_cparams(3),
            name="hyena_fft_conv",
        )(z, uc, spec, skip[order].reshape(1, HY_W), tabs["w1"], tabs["wd"], tabs["wdi"], tabs["w4"])
        z, zsel = z.reshape((1,) + z.shape), 0
    return z[0]


def _hyena(p, lw, row0, seq, n_batch):
    long = _fft_split(seq)[1] > 1
    n1 = FFT_N1 if long else 1
    taps, nrm = _hyena_filter_taps(seq, lw["hy_w1"], lw["hy_b1"], lw["hy_w2"], lw["hy_b2"], lw["hy_w3"],
                                   lw["hy_b3"], lw["hy_w4"], lw["hy_freq"], n1)
    uc = _short_conv(p, lw["hy_conv_w"], lw["hy_conv_b"], row0, seq, n_batch, n1)
    if long:
        return _hyena_long(seq, taps, nrm, uc, lw["hy_skip"])
    taps = jnp.swapaxes(taps, 0, 1).reshape(seq, -1)
    return _hyena_short(seq, taps, nrm, uc, lw["hy_skip"]).reshape(n_batch * seq, HY_W)


def _mla_qkv_kernel(qa_ref, kva_ref, kr_ref, cs_ref, wq_ref, wkv_ref, gqa_ref, gkva_ref, gqn_ref, gqr_ref,
                    gkn_ref, gkr_ref, q_ref, k_ref, v_ref):
    cs = cs_ref[...]
    lane = lax.broadcasted_iota(jnp.int32, cs.shape, 1)
    low = lane < MLA_ROPE

    def rope(pair, gain2):
        ms = jnp.sum(jnp.where(low, pair * pair, 0.0), axis=-1, keepdims=True) * (1.0 / MLA_ROPE)
        t = pair * lax.rsqrt(ms + RMS_EPS) * gain2 * cs
        return jnp.where(low, t + pltpu.roll(t, MLA_ROPE, 1), 0.0)

    qq = _dot(_rms(qa_ref[...], gqa_ref[...]).astype(BF16), wq_ref[...])
    kv = _dot(_rms(kva_ref[...], gkva_ref[...]).astype(BF16), wkv_ref[...])
    kr = rope(kr_ref[...], gkr_ref[...])
    hp = MLA_HEAD_PAD
    tm = kr.shape[0]
    qscale = MLA_SCALE * LOG2_E
    ones_row = (lax.broadcasted_iota(jnp.int32, (V_ROWS - MLA_V, tm), 0) == 0).astype(BF16)
    for h in range(MLA_HEADS):
        qn = _rms(qq[:, h * hp:h * hp + MLA_NOPE], gqn_ref[...])
        qr = rope(qq[:, h * hp + MLA_NOPE:(h + 1) * hp], gqr_ref[...])
        q_ref[h, 0:LANE, :] = (qn * qscale).T.astype(BF16)
        q_ref[h, LANE:2 * LANE, :] = (qr * qscale).T.astype(BF16)
        kn = _rms(kv[:, h * hp:h * hp + MLA_NOPE], gkn_ref[...])
        k_ref[h, :, 0:LANE] = kn.astype(BF16)
        k_ref[h, :, LANE:2 * LANE] = kr.astype(BF16)
        v_ref[h, 0:MLA_V, :] = kv[:, h * hp + MLA_NOPE:(h + 1) * hp].T.astype(BF16)
        v_ref[h, MLA_V:V_ROWS, :] = ones_row


def _rope_table(seq, n_batch, ctx_rows):
    rows = seq // GRID_W
    row = np.repeat(np.arange(rows, dtype=np.float32), GRID_W)
    col = np.tile(np.arange(GRID_W, dtype=np.float32), rows)
    half = MLA_ROPE // 2
    inv = (ROPE_THETA ** (-np.arange(0, half, 2, dtype=np.float32) / half)).astype(np.float32)
    ar = (row[:, None] * inv).astype(np.float64)
    ac = (col[:, None] * inv).astype(np.float64)
    cos = np.concatenate([np.cos(ar), np.cos(ar), np.cos(ac), np.cos(ac)], axis=1)
    sin = np.concatenate([-np.sin(ar), np.sin(ar), -np.sin(ac), np.sin(ac)], axis=1)
    lat = np.tile(np.concatenate([cos, sin], axis=1), (n_batch, 1))
    ctx = np.concatenate([np.ones((ctx_rows, MLA_ROPE)), np.zeros((ctx_rows, MLA_ROPE))], axis=1)
    return jnp.asarray(np.concatenate([lat, ctx], axis=0), F32)


def _pair_gain(g):
    return jnp.concatenate([g, _rope_swap(g)]).reshape(1, 2 * MLA_ROPE)


def _mla_qkv(p, cs, lw, tm=512):
    n_rows = p.shape[0]
    hd, hp = MLA_HEADS, MLA_HEAD_PAD
    w_uq = lw["w_uq"].reshape(-1, hd, MLA_QK)
    wq = jnp.concatenate([w_uq, _rope_swap(w_uq[..., MLA_NOPE:])], axis=-1).reshape(-1, hd * hp).astype(BF16)
    wkv = lw["w_ukv"].astype(BF16)
    ql, kvl = wq.shape[0], wkv.shape[0]
    vec = lambda a: a.reshape(1, -1)
    full = lambda a: pl.BlockSpec(a.shape, lambda i: (0,) * a.ndim)
    args = (wq, wkv, vec(lw["q_a_norm"]), vec(lw["kv_a_norm"]), vec(lw["q_nope_norm"]),
            _pair_gain(lw["q_rope_norm"]), vec(lw["k_nope_norm"]), _pair_gain(lw["k_rope_norm"]))
    return pl.pallas_call(
        _mla_qkv_kernel,
        grid=(n_rows // tm,),
        in_specs=[pl.BlockSpec((tm, ql), lambda i: (i, COL_QA // ql)),
                  pl.BlockSpec((tm, kvl), lambda i: (i, COL_KVA // kvl)),
                  pl.BlockSpec((tm, LANE), lambda i: (i, COL_KR // LANE)),
                  pl.BlockSpec((tm, LANE), lambda i: (i, 0))] + [full(a) for a in args],
        out_specs=[pl.BlockSpec((hd, hp, tm), lambda i: (0, 0, i)),
                   pl.BlockSpec((hd, tm, hp), lambda i: (0, i, 0)),
                   pl.BlockSpec((hd, V_ROWS, tm), lambda i: (0, 0, i))],
        out_shape=[jax.ShapeDtypeStruct((hd, hp, n_rows), BF16),
                   jax.ShapeDtypeStruct((hd, n_rows, hp), BF16),
                   jax.ShapeDtypeStruct((hd, V_ROWS, n_rows), BF16)],
        compiler_params=_cparams(1),
        name="mla_qkv",
    )(p, p, p, cs, *args)


def _key_chunks(k_refs, v_refs, tk):
    chunks, s0 = [], 0
    for kr, vr in zip(k_refs, v_refs):
        n = kr.shape[1]
        step = min(tk, n)
        chunks += [(kr, vr, r0, step, s0 + r0) for r0 in range(0, n, step)]
        s0 += n
    return chunks


def _score_pass(qt, chunks, s_scr):
    m = None
    for kr, _, r0, rn, s0 in chunks:
        s = _dot(kr[0, r0:r0 + rn, :], qt)
        s_scr[s0:s0 + rn, :] = s
        mj = jnp.max(s, axis=0, keepdims=True)
        m = mj if m is None else jnp.maximum(m, mj)
    return m


def _value_pass(chunks, s_scr, m, o_ref):
    acc = None
    for _, vr, r0, rn, s0 in chunks:
        p = jnp.exp2((s_scr[s0:s0 + rn, :] - m).astype(BF16))
        part = _dot(vr[0, :, r0:r0 + rn], p)
        acc = part if acc is None else acc + part
    o_ref[...] = (acc[:MLA_V] / acc[MLA_V:MLA_V + 1]).astype(o_ref.dtype)


def _attn_ctx_kernel(q_ref, kc_ref, vc_ref, o_ref, s_scr, *, tk):
    chunks = _key_chunks([kc_ref], [vc_ref], tk)
    _value_pass(chunks, s_scr, _score_pass(q_ref[0], chunks, s_scr), o_ref)


def _attn_kernel(q_ref, k_ref, v_ref, kc_ref, vc_ref, o_ref, s_a, s_b, m_a, m_b, *, tk):
    i = pl.program_id(2)
    chunks = _key_chunks([kc_ref, k_ref], [vc_ref, v_ref], tk)

    @pl.when((pl.program_id(0) == 0) & (pl.program_id(1) == 0) & (i == 0))
    def _():
        s_b[...] = jnp.zeros_like(s_b)
        m_b[...] = jnp.zeros_like(m_b)

    def step(s_cur, m_cur, s_prev, m_prev):
        _value_pass(chunks, s_prev, m_prev[0:1, :], o_ref)
        m_cur[...] = jnp.broadcast_to(_score_pass(q_ref[0], chunks, s_cur), m_cur.shape)

    @pl.when(i % 2 == 0)
    def _():
        step(s_a, m_a, s_b, m_b)

    @pl.when(i % 2 == 1)
    def _():
        step(s_b, m_b, s_a, m_a)


def _attention(qt, k, vt, seq, ctx_len, n_batch, latent, tq=256, tk=1024):
    hd, hp, n_rows = qt.shape
    lat_rows = n_batch * seq
    cblk = lat_rows // ctx_len
    kc_spec = pl.BlockSpec((1, ctx_len, hp), lambda b, h, i: (h, cblk + b, 0))
    vc_spec = pl.BlockSpec((1, V_ROWS, ctx_len), lambda b, h, i: (h, 0, cblk + b))
    if latent:
        nq = seq // tq
        steps = nq + 1
        in_specs = [pl.BlockSpec((1, hp, tq), lambda b, h, i: (h, 0, b * nq + jnp.minimum(i, nq - 1))),
                    pl.BlockSpec((1, seq, hp), lambda b, h, i: (h, b, 0)),
                    pl.BlockSpec((1, V_ROWS, seq), lambda b, h, i: (h, 0, b)), kc_spec, vc_spec]
        args = (qt, k, vt, k, vt)
        out_spec = pl.BlockSpec((MLA_V, tq), lambda b, h, i: (h, b * nq + jnp.maximum(i - 1, 0)))
        out_cols, n_keys = lat_rows, seq + ctx_len
        body = functools.partial(_attn_kernel, tk=tk)
        scratch = [pltpu.VMEM((n_keys, tq), F32)] * 2 + [pltpu.VMEM((8, tq), F32)] * 2
    else:
        steps = 1
        tq = ctx_len
        in_specs = [pl.BlockSpec((1, hp, tq), lambda b, h, i: (h, 0, cblk + b)), kc_spec, vc_spec]
        args = (qt, k, vt)
        out_spec = pl.BlockSpec((MLA_V, tq), lambda b, h, i: (h, b))
        out_cols, n_keys = n_batch * ctx_len, ctx_len
        body = functools.partial(_attn_ctx_kernel, tk=tk)
        scratch = [pltpu.VMEM((n_keys, tq), F32)]
    return pl.pallas_call(
        body,
        grid=(n_batch, hd, steps),
        in_specs=in_specs,
        out_specs=out_spec,
        out_shape=jax.ShapeDtypeStruct((hd * MLA_V, out_cols), BF16),
        scratch_shapes=scratch,
        compiler_params=_cparams(3),
        name="mla_attention" if latent else "mla_attention_ctx",
    )(*args)


def _merge_kernel(*refs, lat_tiles):
    (x_ref, mod_ref, of_ref, ob_ref, g_ref, yb_ref, yc_ref, ga_ref, gb_ref, gc_ref,
     gain_ref, wa_ref, wb_ref, wc_ref, wo_ref) = refs[:15]
    ctx_refs = refs[15:-1]
    o_ref = refs[-1]
    is_ctx = pl.program_id(0) >= lat_tiles
    m = mod_ref[0]
    o = of_ref[...].astype(F32) + ob_ref[...].astype(F32)
    if ctx_refs:
        o = jnp.where(is_ctx, ctx_refs[0][...].astype(F32) + ctx_refs[1][...].astype(F32), o)
    gain = gain_ref[...]
    ya = jnp.concatenate([_rms(o[:, h * HG_D:(h + 1) * HG_D], gain) for h in range(HG_HEADS)], axis=1)
    ya = (ya * _silu(g_ref[...])).astype(BF16)
    nslab, tcols = yb_ref.shape[1], yb_ref.shape[3]
    qn = x_ref.shape[0] // yb_ref.shape[2]
    q0 = (pl.program_id(0) % (tcols // qn)) * qn
    yb = jnp.concatenate([jnp.concatenate([yb_ref[0, k, :, q0 + q, :] for k in range(nslab)], axis=1)
                          for q in range(qn)], axis=0)
    yc = yc_ref[...]
    if ctx_refs:
        yb = jnp.where(is_ctx, ctx_refs[2][...], yb)
        yc = jnp.where(is_ctx, ctx_refs[3][...], yc)
    mix = (_sigmoid(ga_ref[...]) * _dot(ya, wa_ref[...])
           + _sigmoid(gb_ref[...]) * _dot(yb.astype(BF16), wb_ref[...])
           + _sigmoid(gc_ref[...]) * _dot_tn(yc, wc_ref[...]))
    o_ref[...] = x_ref[...] + m[5:6] * _dot(mix.astype(BF16), wo_ref[...])


def _merge(xa, mod, o_f, o_b, p, y_b, y_c, ctx_parts, lw, n_rows, seq, n_batch, tm=512):
    d = xa.shape[1]
    hk = HG_HEADS * HG_D
    tpb = seq // tm
    lat_tiles = n_batch * tpb
    n1 = FFT_N1
    tcols = 8
    assert tm % n1 == 0 and tcols % (tm // n1) == 0 and seq % (tcols * n1) == 0
    per_blk = tcols // (tm // n1)
    y_b = y_b.reshape(n_batch, y_b.shape[1], n1, seq // n1, LANE)
    row = lambda w: pl.BlockSpec((tm, w), lambda i: (i, 0))
    pcol = lambda w, c: pl.BlockSpec((tm, w), lambda i: (i, c // w))
    full = lambda a: pl.BlockSpec(a.shape, lambda i: (0,) * a.ndim)
    ws = (lw["hg_out_norm"].reshape(1, HG_D), lw["w_br_a"].astype(BF16), lw["w_br_b"].astype(BF16),
          lw["w_br_c"].astype(BF16), lw["w_out"].astype(BF16))
    yb_spec = pl.BlockSpec((1, y_b.shape[1], n1, tcols, LANE),
                           lambda i: (jnp.minimum(i // tpb, n_batch - 1), 0, 0, (i % tpb) // per_blk, 0))
    lat = lambda i: jnp.minimum(i, lat_tiles - 1)
    late = lambda i: jnp.maximum(i - lat_tiles, 0)
    extra_specs, extra = [], []
    if ctx_parts is not None:
        extra_specs = [pl.BlockSpec((tm, hk), lambda i: (late(i), 0)),
                       pl.BlockSpec((tm, hk), lambda i: (late(i), 0)),
                       pl.BlockSpec((tm, HY_W), lambda i: (late(i), 0)),
                       pl.BlockSpec((MLA_HEADS * MLA_V, tm), lambda i: (0, late(i)))]
        extra = list(ctx_parts)
    return pl.pallas_call(
        functools.partial(_merge_kernel, lat_tiles=lat_tiles),
        grid=(n_rows // tm,),
        in_specs=[row(d), pl.BlockSpec((1, N_MOD, d), _group_map(tpb, n_batch)),
                  pl.BlockSpec((tm, hk), lambda i: (lat(i), 0)), pl.BlockSpec((tm, hk), lambda i: (lat(i), 0)),
                  pcol(hk, COL_G), yb_spec,
                  pl.BlockSpec((MLA_HEADS * MLA_V, tm), lambda i: (0, lat(i))),
                  pcol(d, COL_GA), pcol(d, COL_GB), pcol(d, COL_GC)] + [full(a) for a in ws] + extra_specs,
        out_specs=row(d),
        out_shape=jax.ShapeDtypeStruct((n_rows, d), F32),
        compiler_params=_cparams(1),
        name="merge",
    )(xa, mod, o_f, o_b, p, y_b, y_c, p, p, p, *ws, *extra)


def kernel(x, c, ctx, c_ctx, ada_w, ada_b, ffn1_norm, ffn1_w13, ffn1_w2, mix_norm, w_in, hg_lb_logits, hg_out_norm, hy_conv_w, hy_conv_b, hy_w1, hy_b1, hy_w2, hy_b2, hy_w3, hy_b3, hy_w4, hy_freq, hy_skip, q_a_norm, w_uq, kv_a_norm, w_ukv, q_nope_norm, q_rope_norm, k_nope_norm, k_rope_norm, w_br_a, w_br_b, w_br_c, w_out, ffn2_norm, ffn2_w13, ffn2_w2):
    stacked = dict(
        ada_w=ada_w, ada_b=ada_b, ffn1_norm=ffn1_norm, ffn1_w13=ffn1_w13, ffn1_w2=ffn1_w2, mix_norm=mix_norm,
        w_in=w_in, hg_out_norm=hg_out_norm, hy_conv_w=hy_conv_w, hy_conv_b=hy_conv_b, hy_w1=hy_w1, hy_b1=hy_b1,
        hy_w2=hy_w2, hy_b2=hy_b2, hy_w3=hy_w3, hy_b3=hy_b3, hy_w4=hy_w4, hy_freq=hy_freq, hy_skip=hy_skip,
        q_a_norm=q_a_norm, w_uq=w_uq, kv_a_norm=kv_a_norm, w_ukv=w_ukv, q_nope_norm=q_nope_norm,
        q_rope_norm=q_rope_norm, k_nope_norm=k_nope_norm, k_rope_norm=k_rope_norm, w_br_a=w_br_a,
        w_br_b=w_br_b, w_br_c=w_br_c, w_out=w_out, ffn2_norm=ffn2_norm, ffn2_w13=ffn2_w13, ffn2_w2=ffn2_w2)
    n_batch, seq, d = x.shape
    ctx_len = ctx.shape[1]
    depth = ada_w.shape[0]
    lat_rows, ctx_rows = n_batch * seq, n_batch * ctx_len
    all_rows = lat_rows + ctx_rows
    assert seq % 512 == 0 and ctx_rows % 512 == 0 and seq % ctx_len == 0 and seq % GRID_W == 0
    assert ctx_len % HG_CHUNK == 0 and n_batch < 8

    xa, xc = x.reshape(lat_rows, d), ctx.reshape(ctx_rows, d)
    cs = jnp.concatenate([c, c_ctx.reshape(1, d), jnp.zeros((7 - n_batch, d), F32)], axis=0)
    rope_cs = _rope_table(seq, n_batch, ctx_rows)
    zero_state = jnp.zeros((2, n_batch, HG_HEADS, HG_D, HG_D), F32)
    hk = HG_HEADS * HG_D

    for l in range(depth):
        lw = {name: val[l] for name, val in stacked.items()}
        need_ctx = l < depth - 1
        mod = _modulation(cs, lw["ada_w"], lw["ada_b"])
        xa = _half_ffn(xa, mod, lw["ffn1_norm"], lw["ffn1_w13"], lw["ffn1_w2"], 0, all_rows, seq, n_batch, xc=xc)
        xc = None
        p = _in_projection(xa, mod, lw["mix_norm"], _pack_w_in(lw["w_in"]).astype(BF16), seq, n_batch)

        ocf, ocb, s_c = _hgrn2_scan(p, hg_lb_logits, zero_state, l, lat_rows, ctx_len, n_batch)
        olf, olb, _ = _hgrn2_scan(p, hg_lb_logits, s_c, l, 0, seq, n_batch)
        y_b = _hyena(p, lw, 0, seq, n_batch)
        qt, k, vt = _mla_qkv(p, rope_cs, lw)
        y_c = _attention(qt, k, vt, seq, ctx_len, n_batch, latent=True)

        ctx_parts = None
        if need_ctx:
            ctx_parts = (ocf.reshape(ctx_rows, hk), ocb.reshape(ctx_rows, hk),
                         _hyena(p, lw, lat_rows, ctx_len, n_batch),
                         _attention(qt, k, vt, seq, ctx_len, n_batch, latent=False))
        mix_rows = all_rows if need_ctx else lat_rows
        xa = _merge(xa, mod, olf.reshape(lat_rows, hk), olb.reshape(lat_rows, hk), p, y_b, y_c, ctx_parts, lw,
                    mix_rows, seq, n_batch)
        xa = _half_ffn(xa, mod, lw["ffn2_norm"], lw["ffn2_w13"], lw["ffn2_w2"], 6, mix_rows, seq, n_batch)
    return xa[:lat_rows].reshape(n_batch, seq, d)
```

```python
import functools
import math

import numpy as np
import jax
import jax.numpy as jnp
from jax import lax
from jax.experimental import pallas as pl
from jax.experimental.pallas import tpu as pltpu

F32 = jnp.float32
BF16 = jnp.bfloat16

RMS_EPS = 1e-6
N_MOD = 9
GRID_W = 64
ROPE_THETA = 10000.0
HG_HEADS = 4
HG_D = 128
HG_CHUNK = 128
HY_W = 512
HY_TARGET = 1e-2
HY_MIN_DECAY = math.log(HY_TARGET) / 1.5
HY_MAX_DECAY = math.log(HY_TARGET) / 0.3
HY_SHIFT = 0.05
MLA_HEADS = 4
MLA_NOPE = 128
MLA_ROPE = 64
MLA_V = 128
MLA_QK = MLA_NOPE + MLA_ROPE
MLA_SCALE = MLA_QK ** -0.5
MLA_HEAD_PAD = 256
V_ROWS = MLA_V + 16
LOG2_E = math.log2(math.e)
FFT_N1 = 128
LANE = 128

VMEM_LIMIT = 52 * 1024 * 1024

COL_GA, COL_GB, COL_GC = 0, 1024, 2048
COL_Q, COL_ZF, COL_ZB, COL_IV, COL_G = 3072, 3584, 4096, 4608, 5120
COL_HY = 5632
COL_QA, COL_KVA, COL_KR = 7168, 7424, 7552
IN_PACKED = 7680


def _cparams(n_axes):
    return pltpu.CompilerParams(dimension_semantics=("arbitrary",) * n_axes,
                                vmem_limit_bytes=VMEM_LIMIT)


def _dot(a, b):
    return jnp.dot(a, b, preferred_element_type=F32)


def _dot_nt(a, b):
    return lax.dot_general(a, b, (((1,), (1,)), ((), ())), preferred_element_type=F32)


def _dot_tn(a, b):
    return lax.dot_general(a, b, (((0,), (0,)), ((), ())), preferred_element_type=F32)


def _sigmoid(x):
    return 1.0 / (1.0 + jnp.exp(-x))


def _silu(x):
    return x * _sigmoid(x)


def _rms(x, gain):
    return x * lax.rsqrt(jnp.mean(x * x, axis=-1, keepdims=True) + RMS_EPS) * gain


def _mod_kernel(c_ref, w_ref, b_ref, o_ref):
    o_ref[...] = _dot(_silu(c_ref[...]).astype(BF16), w_ref[...]) + b_ref[...]


def _modulation(cs, ada_w, ada_b):
    g, d = cs.shape
    n = ada_w.shape[1]
    tn = n // 4
    out = pl.pallas_call(
        _mod_kernel,
        grid=(n // tn,),
        in_specs=[pl.BlockSpec((g, d), lambda j: (0, 0)),
                  pl.BlockSpec((d, tn), lambda j: (0, j)),
                  pl.BlockSpec((1, tn), lambda j: (0, j))],
        out_specs=pl.BlockSpec((g, tn), lambda j: (0, j)),
        out_shape=jax.ShapeDtypeStruct((g, n), F32),
        compiler_params=_cparams(1),
        name="modulation",
    )(cs, ada_w.astype(BF16), ada_b.reshape(1, n))
    return out.reshape(g, N_MOD, d)


def _ffn_kernel(*refs, idx, ff, ck, lat_tiles):
    x_ref, mod_ref, g_ref, w13_ref, w2_ref = refs[:5]
    o_ref = refs[-1]
    x = x_ref[...]
    if len(refs) == 7:
        x = jnp.where(pl.program_id(0) >= lat_tiles, refs[5][...], x)
    m = mod_ref[0]
    h = (_rms(x, g_ref[...]) * (1.0 + m[idx + 1:idx + 2]) + m[idx:idx + 1]).astype(BF16)
    acc = jnp.zeros(x.shape, F32)
    for c0 in range(0, ff, ck):
        c1 = min(c0 + ck, ff)
        a = _dot(h, w13_ref[:, c0:c1])
        b = _dot(h, w13_ref[:, ff + c0:ff + c1])
        acc = acc + _dot((_silu(a) * b).astype(BF16), w2_ref[c0:c1, :])
    o_ref[...] = x + (0.5 * m[idx + 2:idx + 3]) * acc


def _group_map(tiles_per_batch, n_batch):
    return lambda i: (jnp.minimum(i // tiles_per_batch, n_batch), 0, 0)


def _half_ffn(xa, mod, gain, w13, w2, idx, n_rows, seq, n_batch, xc=None, tm=512, ck=512):
    d = xa.shape[1]
    ff = w2.shape[0]
    lat_tiles = n_batch * seq // tm
    const = dict(pipeline_mode=pl.Buffered(1))
    x_specs, xs = [pl.BlockSpec((tm, d), lambda i: (jnp.minimum(i, xa.shape[0] // tm - 1), 0))], [xa]
    tail_specs, tail = [], []
    if xc is not None:
        tail_specs, tail = [pl.BlockSpec((tm, d), lambda i: (jnp.maximum(i - lat_tiles, 0), 0))], [xc]
    return pl.pallas_call(
        functools.partial(_ffn_kernel, idx=idx, ff=ff, ck=ck, lat_tiles=lat_tiles),
        grid=(n_rows // tm,),
        in_specs=x_specs + [pl.BlockSpec((1, N_MOD, d), _group_map(seq // tm, n_batch)),
                            pl.BlockSpec((1, d), lambda i: (0, 0)),
                            pl.BlockSpec((d, 2 * ff), lambda i: (0, 0), **const),
                            pl.BlockSpec((ff, d), lambda i: (0, 0), **const)] + tail_specs,
        out_specs=pl.BlockSpec((tm, d), lambda i: (i, 0)),
        out_shape=jax.ShapeDtypeStruct((n_rows, d), F32),
        compiler_params=_cparams(1),
        name="half_ffn",
    )(*xs, mod, gain.reshape(1, d), w13.astype(BF16), w2.astype(BF16), *tail)


def _inproj_kernel(x_ref, mod_ref, g_ref, w_ref, o_ref):
    m = mod_ref[0]
    h = (_rms(x_ref[...], g_ref[...]) * (1.0 + m[4:5]) + m[3:4]).astype(BF16)
    o_ref[...] = _dot(h, w_ref[...])


def _pack_w_in(w_in):
    d = w_in.shape[0]
    hk = HG_HEADS * HG_D
    sizes = (hk, hk, hk, hk, hk, 3 * HY_W, 256, 128, MLA_ROPE, d, d, d)
    offs = np.cumsum((0,) + sizes)
    q, zf, zb, iv, g, hy, qa, kva, kr, ga, gb, gc = (w_in[:, offs[i]:offs[i + 1]] for i in range(12))
    return jnp.concatenate([ga, gb, gc, q, zf, zb, iv, g, hy, qa, kva, kr, _rope_swap(kr)], axis=1)


def _rope_swap(a):
    q = MLA_ROPE // 4
    return jnp.concatenate([a[..., q:2 * q], a[..., :q], a[..., 3 * q:], a[..., 2 * q:3 * q]], axis=-1)


def _in_projection(xa, mod, gain, w_packed, seq, n_batch, tm=512, tn=2560):
    n_rows, d = xa.shape
    n = w_packed.shape[1]
    return pl.pallas_call(
        _inproj_kernel,
        grid=(n // tn, n_rows // tm),
        in_specs=[pl.BlockSpec((tm, d), lambda j, i: (i, 0)),
                  pl.BlockSpec((1, N_MOD, d), lambda j, i: (jnp.minimum(i // (seq // tm), n_batch), 0, 0)),
                  pl.BlockSpec((1, d), lambda j, i: (0, 0)),
                  pl.BlockSpec((d, tn), lambda j, i: (0, j))],
        out_specs=pl.BlockSpec((tm, tn), lambda j, i: (i, j)),
        out_shape=jax.ShapeDtypeStruct((n_rows, n), F32),
        compiler_params=_cparams(2),
        name="in_projection",
    )(xa, mod, gain.reshape(1, d), w_packed)


def _hgrn2_tables(c, rev):
    t = np.arange(c)[:, None]
    u = np.arange(c)[None, :]
    mats = [(u <= t), (u > t)]
    masks = [(t == u)]
    h = c // 2
    while h >= 1:
        mid = (t // (2 * h)) * (2 * h) + h
        mats.append(np.where(t >= mid, (u >= mid) & (u <= t), (u >= t + 1) & (u <= mid - 1)))
        mid_s = (u // (2 * h)) * (2 * h) + h
        masks.append((t // (2 * h) == u // (2 * h)) & (u < mid_s) & (t >= mid))
        h //= 2
    mats = np.stack([m.astype(np.float32) for m in mats])
    masks = np.stack([m.astype(np.float32) for m in masks])
    if rev:
        mats = mats[:, ::-1, ::-1]
        masks = masks[:, ::-1, ::-1]
    mats = np.ascontiguousarray(mats).reshape(-1, c)
    return (jnp.asarray(np.concatenate([mats, mats], axis=1), BF16),
            jnp.asarray(np.ascontiguousarray(masks), F32))


HG_PAIR = 4


def _hgrn2_chunk(q, z, v, lg, dst_ref, msk_ref, st_ref, o_ref, ci, *, layer, rev, n_levels):
    c = q.shape[0]
    hk = HG_HEADS * HG_D
    e = jnp.exp(lg - jnp.max(lg, axis=0, keepdims=True))
    sm = e / jnp.sum(e, axis=0, keepdims=True)
    lb = jnp.zeros((1, hk), F32)
    for i in range(1, layer + 1):
        lb = lb + sm[i:i + 1]

    f = lb + (1.0 - lb) * _sigmoid(z)
    kk = 1.0 - f
    g = jnp.log(f)
    g_hi = g.astype(BF16)
    g_lo = (g - g_hi.astype(F32)).astype(BF16)
    dg = _dot(dst_ref[...], jnp.concatenate([g_hi, g_lo], axis=0))
    qs = _silu(q) * HG_D ** -0.5
    last = 0 if rev else c - 1

    for h in range(HG_HEADS):
        hs = slice(h * HG_D, (h + 1) * HG_D)
        qh, kh, vh = qs[:, hs].astype(BF16), kk[:, hs].astype(BF16), v[:, hs].astype(BF16)
        b_in = dg[0:c, hs]
        ex = lambda blk: jnp.exp(dg[blk * c:(blk + 1) * c, hs].astype(BF16))
        a = _dot_nt(qh, kh) * msk_ref[0]
        for l in range(1, n_levels + 1):
            el = ex(1 + l)
            a = a + _dot_nt(qh * el, kh * el) * msk_ref[l]
        st = st_ref[ci, h]
        o = _dot_nt(qh * ex(0), st.astype(BF16)) + _dot(a.astype(BF16), vh)
        o_ref[:, hs] = o.astype(o_ref.dtype)
        st_ref[ci, h] = jnp.exp(b_in[last:last + 1, :]) * st + _dot_tn(vh, kh * ex(1))


def _hgrn2_kernel(*refs, layer, n_levels, pair):
    n_in = 6 * pair
    ins, (lg_ref, dstf_ref, mskf_ref, dstb_ref, mskb_ref, s0_ref) = refs[:n_in], refs[n_in:n_in + 6]
    of_ref, ob_ref, sf_ref, st_ref = refs[n_in + 6:]
    j = pl.program_id(1)

    @pl.when(j == 0)
    def _():
        st_ref[...] = s0_ref[...].reshape(st_ref.shape)

    for e in range(pair):
        qf, zf, vf, qb, zb, vb = (r[...] for r in ins[6 * e:6 * e + 6])
        _hgrn2_chunk(qf, zf, vf, lg_ref[0], dstf_ref, mskf_ref, st_ref, of_ref.at[e], e,
                     layer=layer, rev=False, n_levels=n_levels)
        _hgrn2_chunk(qb, zb, vb, lg_ref[1], dstb_ref, mskb_ref, st_ref, ob_ref.at[e], pair + e,
                     layer=layer, rev=True, n_levels=n_levels)

    @pl.when(j == pl.num_programs(1) - 1)
    def _():
        sf_ref[...] = st_ref[...].reshape(sf_ref.shape)


def _hgrn2_scan(p, logits, s0, layer, row0, seq, n_batch):
    c = HG_CHUNK
    hk = HG_HEADS * HG_D
    nc = seq // c
    base = row0 // c
    n_levels = int(math.log2(c))
    tabs = _hgrn2_tables(c, False) + _hgrn2_tables(c, True)
    pair = math.gcd(HG_PAIR, n_batch)

    def prow(w, e, rev):
        return pl.BlockSpec((c, hk), lambda bp, j: (base + (bp * pair + e) * nc + (nc - 1 - j if rev else j),
                                                   w // hk))

    in_specs = []
    for e in range(pair):
        in_specs += [prow(COL_Q, e, False), prow(COL_ZF, e, False), prow(COL_IV, e, False),
                     prow(COL_Q, e, True), prow(COL_ZB, e, True), prow(COL_IV, e, True)]
    full = lambda a: pl.BlockSpec(a.shape, lambda bp, j: (0,) * a.ndim)
    st_spec = pl.BlockSpec((2, pair, HG_HEADS, HG_D, HG_D), lambda bp, j: (0, bp, 0, 0, 0))
    return pl.pallas_call(
        functools.partial(_hgrn2_kernel, layer=layer, n_levels=n_levels, pair=pair),
        grid=(n_batch // pair, nc),
        in_specs=in_specs + [full(logits)] + [full(t) for t in tabs] + [st_spec],
        out_specs=[pl.BlockSpec((pair, c, hk), lambda bp, j: (bp, j, 0)),
                   pl.BlockSpec((pair, c, hk), lambda bp, j: (bp, nc - 1 - j, 0)),
                   st_spec],
        out_shape=[jax.ShapeDtypeStruct((n_batch, seq, hk), BF16),
                   jax.ShapeDtypeStruct((n_batch, seq, hk), BF16),
                   jax.ShapeDtypeStruct((2, n_batch, HG_HEADS, HG_D, HG_D), F32)],
        scratch_shapes=[pltpu.VMEM((2 * pair, HG_HEADS, HG_D, HG_D), F32)],
        compiler_params=_cparams(2),
        name="hgrn2_scan",
    )(*([p] * (6 * pair)), logits, *tabs, s0)


def _shortconv_kernel(u_ref, w_ref, b_ref, o_ref):
    u = u_ref[...]
    n = u.shape[0]
    row = lax.broadcasted_iota(jnp.int32, u.shape, 0)
    prev = jnp.where(row == 0, 0.0, pltpu.roll(u, 1, 0))
    nxt = jnp.where(row == n - 1, 0.0, pltpu.roll(u, n - 1, 0))
    w = w_ref[...]
    o_ref[0, 0, 0] = prev * w[0:1] + u * w[1:2] + nxt * w[2:3] + b_ref[...]


def _shortconv_t1major_kernel(u_ref, w_ref, b_ref, o_ref, u_scr, *, n1):
    half = u_ref.shape[0] // n1
    w = w_ref[...]
    b = b_ref[...]
    pitch = n1 + 8

    def copy(g, carry):
        u_scr[pl.ds(pl.multiple_of(g * pitch, 8), n1), :] = u_ref[pl.ds(pl.multiple_of(g * n1, 8), n1), :]
        return carry

    lax.fori_loop(0, half, copy, 0, unroll=8)
    row = lax.broadcasted_iota(jnp.int32, (half, LANE), 0)
    col = lambda t1: u_scr[pl.ds(t1, half, stride=pitch), :]
    before = jnp.where(row == 0, 0.0, pltpu.roll(col(n1 - 1), 1, 0))
    after = jnp.where(row == half - 1, 0.0, pltpu.roll(col(0), half - 1, 0))

    def body(t1, carry):
        prev, cur = carry
        nxt = jnp.where(t1 == n1 - 1, after, col(jnp.minimum(t1 + 1, n1 - 1)))
        o_ref[0, 0, 0, pl.ds(pl.multiple_of(t1 * half, 8), half), :] = prev * w[0:1] + cur * w[1:2] + nxt * w[2:3] + b
        return cur, nxt

    lax.fori_loop(0, n1, body, (before, col(0)), unroll=16)


def _short_conv(p, w, b, row0, seq, n_batch, n1):
    nb = 3 * HY_W // LANE
    per = HY_W // LANE
    body, scratch = _shortconv_kernel, []
    if n1 > 1:
        body = functools.partial(_shortconv_t1major_kernel, n1=n1)
        scratch = [pltpu.VMEM((seq // n1 * (n1 + 8), LANE), F32)]
    return pl.pallas_call(
        body,
        grid=(n_batch, nb),
        in_specs=[pl.BlockSpec((seq, LANE), lambda bi, j: (row0 // seq + bi, COL_HY // LANE + j)),
                  pl.BlockSpec((3, LANE), lambda bi, j: (0, j)),
                  pl.BlockSpec((1, LANE), lambda bi, j: (0, j))],
        out_specs=pl.BlockSpec((1, 1, 1, seq, LANE), lambda bi, j: (j // per, bi, j % per, 0, 0)),
        out_shape=jax.ShapeDtypeStruct((3, n_batch, per, seq, LANE), F32),
        scratch_shapes=scratch,
        compiler_params=_cparams(2),
        name="hyena_short_conv",
    )(p, w, b.reshape(1, -1))


def _hy_filter_kernel(emb_ref, w1_ref, b1_ref, w2_ref, b2_ref, w3_ref, b3_ref, w4_ref, fr_ref,
                      dl_ref, o_ref, nrm_ref, *, seq, group):
    i = pl.program_id(0)
    hp = lax.Precision.HIGHEST
    fr = fr_ref[...]
    hid = jnp.sin(fr * (jnp.dot(emb_ref[...], w1_ref[...], precision=hp) + b1_ref[...]))
    hid = jnp.sin(fr * (jnp.dot(hid, w2_ref[...], precision=hp) + b2_ref[...]))
    hid = jnp.sin(fr * (jnp.dot(hid, w3_ref[...], precision=hp) + b3_ref[...]))
    h = jnp.dot(hid, w4_ref[...], precision=hp)
    tl, n = h.shape
    r = lax.broadcasted_iota(jnp.int32, h.shape, 0) + i * tl
    col = lax.broadcasted_iota(jnp.int32, h.shape, 1)
    per = seq // group
    pos = (r >> (per.bit_length() - 1)) + group * (r & (per - 1))
    t = pos.astype(F32) * (1.0 / (seq - 1))
    h = h * (jnp.exp(-t * dl_ref[...]) + HY_SHIFT)
    h = jnp.where((pos == 0) & (col >= n // 2), 0.0, h)
    for k in range(n // LANE):
        o_ref[k] = h[:, k * LANE:(k + 1) * LANE]

    @pl.when(i == 0)
    def _():
        nrm_ref[...] = jnp.zeros_like(nrm_ref)

    nrm_ref[...] += jnp.sum(jnp.abs(h), axis=0, keepdims=True)


def _hyena_filter_taps(seq, w1, b1, w2, b2, w3, b3, w4, freq, group):
    fh = w1.shape[1]
    n_emb = w1.shape[0]
    bands_n = (n_emb - 1) // 2
    tt = np.linspace(0.0, 1.0, seq, dtype=np.float32)[:, None].astype(np.float64)
    ww = (2.0 * math.pi / seq) * np.arange(seq, dtype=np.float64)[:, None]
    bands = np.linspace(1e-4, bands_n - 1, bands_n, dtype=np.float32)[None, :].astype(np.float64)
    emb = np.concatenate([tt, np.cos(bands * ww), -np.sin(bands * ww)], axis=-1)
    emb = np.pad(emb, ((0, 0), (0, LANE - n_emb))).astype(np.float32)
    assert seq % group == 0 and group & (group - 1) == 0 and (seq // group) & (seq // group - 1) == 0
    r = np.arange(seq)
    pos = r // (seq // group) + group * (r % (seq // group))
    emb = emb[pos]
    w1p = jnp.pad(w1, ((0, LANE - n_emb), (0, 0)))
    deltas = np.abs(np.linspace(HY_MIN_DECAY, HY_MAX_DECAY, HY_W, dtype=np.float32))
    deltas = np.tile(deltas, 4)[None, :]
    n = w4.shape[1]
    tl = min(seq, 512)
    full = lambda a: pl.BlockSpec(a.shape, lambda i: (0,) * a.ndim)
    args = (w1p, b1.reshape(1, fh), w2, b2.reshape(1, fh), w3, b3.reshape(1, fh), w4, freq.reshape(1, fh),
            jnp.asarray(deltas))
    return pl.pallas_call(
        functools.partial(_hy_filter_kernel, seq=seq, group=group),
        grid=(seq // tl,),
        in_specs=[pl.BlockSpec((tl, LANE), lambda i: (i, 0))] + [full(a) for a in args],
        out_specs=[pl.BlockSpec((n // LANE, tl, LANE), lambda i: (0, i, 0)),
                   pl.BlockSpec((1, n), lambda i: (0, 0))],
        out_shape=[jax.ShapeDtypeStruct((n // LANE, seq, LANE), F32), jax.ShapeDtypeStruct((1, n), F32)],
        compiler_params=_cparams(1),
        name="hyena_filter_taps",
    )(jnp.asarray(emb), *args)


def _fft_split(seq):
    n = 2 * seq
    n1 = FFT_N1 if n > 1024 else 1
    return n, n1, n // n1


def _cis(idx, n):
    ph = 2.0 * np.pi * (idx % n) / n
    return np.cos(ph), -np.sin(ph)


def _dft_tables_short(seq):
    n = 2 * seq
    cr, ci = _cis(np.arange(n)[:, None] * np.arange(seq)[None, :], n)
    w_fwd = np.concatenate([cr, ci], axis=0)
    w_inv = np.concatenate([cr.T, ci.T], axis=1) / n
    return dict(w_fwd=jnp.asarray(w_fwd, BF16), w_inv=jnp.asarray(w_inv, BF16))


def _dft_tables_long(seq):
    n, n1, n2 = _fft_split(seq)
    t1 = np.arange(n1)[:, None, None]
    f2 = np.arange(n2)[None, :, None]
    t2 = np.arange(n2 // 2)[None, None, :]
    cr, ci = _cis(f2 * (t1 + n1 * t2), n)
    w1 = np.concatenate([cr, ci], axis=1)
    w4 = np.concatenate([np.swapaxes(cr, 1, 2), np.swapaxes(ci, 1, 2)], axis=2) / n
    j = (n1 - t1) % n1 + n1 * t2
    br, bi = _cis(f2 * (n - j), n)
    w1f = np.concatenate([np.concatenate([cr, br], axis=2), np.concatenate([ci, bi], axis=2)], axis=1)
    gr, gi = _cis(np.arange(n1)[:, None] * np.arange(n1)[None, :], n1)
    wd = np.concatenate([np.concatenate([gr, -gi], axis=1), np.concatenate([gi, gr], axis=1)], axis=0)
    return dict(w1=jnp.asarray(np.swapaxes(w1, 1, 2), BF16), w4=jnp.asarray(w4, BF16),
                w1f=jnp.asarray(np.swapaxes(w1f, 1, 2), BF16),
                wd=jnp.asarray(wd, BF16), wdi=jnp.asarray(wd.T, BF16))


def _dft_rows_kernel(w_ref, x_ref, o_ref):
    o_ref[0] = _dot(w_ref[...], x_ref[0].astype(BF16)).astype(o_ref.dtype)


def _dft_rows(w, x, tn):
    nb, k, cols = x.shape
    m = w.shape[0]
    return pl.pallas_call(
        _dft_rows_kernel,
        grid=(nb, cols // tn),
        in_specs=[pl.BlockSpec((m, k), lambda b, j: (0, 0)),
                  pl.BlockSpec((1, k, tn), lambda b, j: (b, 0, j))],
        out_specs=pl.BlockSpec((1, m, tn), lambda b, j: (b, 0, j)),
        out_shape=jax.ShapeDtypeStruct((nb, m, cols), BF16),
        compiler_params=_cparams(2),
        name="hyena_dft_rows",
    )(w, x)


def _idft_gate_kernel(w_ref, b_ref, xg_ref, z_ref, sk_ref, o_ref):
    y = _dot(w_ref[...], b_ref[0])
    z = z_ref[0]
    o_ref[0] = xg_ref[0] * (y + z * sk_ref[...])


def _idft_gate(w, bc, xg, z, skip_t, tn):
    nb, k, cols = bc.shape
    m = w.shape[0]
    return pl.pallas_call(
        _idft_gate_kernel,
        grid=(nb, cols // tn),
        in_specs=[pl.BlockSpec((m, k), lambda b, j: (0, 0)),
                  pl.BlockSpec((1, k, tn), lambda b, j: (b, 0, j)),
                  pl.BlockSpec((1, m, tn), lambda b, j: (b, 0, j)),
                  pl.BlockSpec((1, m, tn), lambda b, j: (b, 0, j)),
                  pl.BlockSpec((1, tn), lambda b, j: (0, j))],
        out_specs=pl.BlockSpec((1, m, tn), lambda b, j: (b, 0, j)),
        out_shape=jax.ShapeDtypeStruct((nb, m, cols), F32),
        compiler_params=_cparams(2),
        name="hyena_idft_gate",
    )(w, bc, xg, z, skip_t)


def _spec_combine_kernel(af_ref, ab_ref, nf_ref, nb_ref, o_ref):
    inv = 1.0 / (nf_ref[...] + nb_ref[...])
    n = o_ref.shape[1]
    o_ref[0] = (af_ref[0, :n].astype(F32) + ab_ref[0, :n].astype(F32)) * inv
    o_ref[1] = (af_ref[0, n:].astype(F32) - ab_ref[0, n:].astype(F32)) * inv


def _spec_mul_kernel(a_ref, k_ref, o_ref):
    n = k_ref.shape[1]
    xr, xi = a_ref[0, :n].astype(F32), a_ref[0, n:].astype(F32)
    kr, ki = k_ref[0], k_ref[1]
    o_ref[0, :n] = (xr * kr - xi * ki).astype(o_ref.dtype)
    o_ref[0, n:] = (xr * ki + xi * kr).astype(o_ref.dtype)


def _hyena_short(seq, taps, nrm, uc, skip):
    n = 2 * seq
    uc = jnp.swapaxes(uc, 2, 3).reshape(3, uc.shape[1], seq, HY_W)
    tabs = _dft_tables_short(seq)
    cf = taps.shape[1]
    c = HY_W
    a = _dft_rows(tabs["w_fwd"], taps.reshape(1, seq, cf), tn=cf)
    nblk = cf // 2 // c
    spec = pl.pallas_call(
        _spec_combine_kernel,
        grid=(nblk,),
        in_specs=[pl.BlockSpec((1, 2 * n, c), lambda j: (0, 0, j)),
                  pl.BlockSpec((1, 2 * n, c), lambda j: (0, 0, j + nblk)),
                  pl.BlockSpec((1, c), lambda j: (0, j)),
                  pl.BlockSpec((1, c), lambda j: (0, j + nblk))],
        out_specs=pl.BlockSpec((2, n, c), lambda j: (0, 0, j)),
        out_shape=jax.ShapeDtypeStruct((2, n, cf // 2), F32),
        compiler_params=_cparams(1),
        name="hyena_spec_combine",
    )(a, a, nrm, nrm)
    nb = uc.shape[1]
    z = uc[2]
    for order in range(2):
        a = _dft_rows(tabs["w_fwd"], z, tn=c)
        bc = pl.pallas_call(
            _spec_mul_kernel,
            grid=(nb,),
            in_specs=[pl.BlockSpec((1, 2 * n, c), lambda b: (b, 0, 0)),
                      pl.BlockSpec((2, n, c), lambda b: (0, 0, order))],
            out_specs=pl.BlockSpec((1, 2 * n, c), lambda b: (b, 0, 0)),
            out_shape=jax.ShapeDtypeStruct((nb, 2 * n, c), BF16),
            compiler_params=_cparams(1),
            name="hyena_spec_mul",
        )(a, spec)
        z = _idft_gate(tabs["w_inv"], bc, uc[order], z, skip[order].reshape(1, c), tn=c)
    return z


HY_SLABS = 2


def _slab_pitch(n2):
    return n2 + 8


def _pack_c(re, im):
    hi = lax.bitcast_convert_type(re.astype(BF16).astype(F32), jnp.uint32)
    lo = lax.bitcast_convert_type(im.astype(BF16).astype(F32), jnp.uint32)
    return hi | (lo >> 16)


def _unpack_c(w):
    re = lax.bitcast_convert_type(w & jnp.uint32(0xFFFF0000), F32)
    im = lax.bitcast_convert_type(w << 16, F32)
    return re.astype(BF16), im.astype(BF16)


def _store_slab(a_scr, row, words):
    for s in range(HY_SLABS):
        a_scr[s, pl.ds(row, words.shape[0]), :] = words[:, s * LANE:(s + 1) * LANE]


def _stage1(a_scr, w_ref, xs_of, i, tb, n2, pitch):
    for j in range(tb):
        a = _dot_tn(w_ref[j], xs_of(j))
        _store_slab(a_scr, pl.multiple_of((i * tb + j) * pitch, 8), _pack_c(a[:n2], a[n2:]))


def _stage2(a_scr, wd_ref, f2, n1, pitch):
    w = jnp.concatenate([a_scr[s, pl.ds(f2, n1, stride=pitch), :] for s in range(HY_SLABS)], axis=1)
    re, im = _unpack_c(w)
    return _dot(wd_ref[...], jnp.concatenate([re, im], axis=0))


def _time_col(ref, j, half):
    lead = (0,) * (len(ref.shape) - 3)
    return jnp.concatenate([ref[lead + (s, slice(j * half, (j + 1) * half), slice(None))]
                            for s in range(HY_SLABS)], axis=1)


def _hy_spectrum_kernel(hf_ref, hb_ref, hb0_ref, nf_ref, nb_ref, w1f_ref, wd_ref, o_ref, a_scr,
                        *, n1, n2, tb, fb):
    i = pl.program_id(1)
    pitch = _slab_pitch(n2)
    half = n2 // 2
    nt = n1 // tb

    @pl.when(i < nt)
    def _():
        def xs_of(j):
            fut = _time_col(hb0_ref, 0, half) if j == 0 else _time_col(hb_ref, tb - j, half)
            return jnp.concatenate([_time_col(hf_ref, j, half), fut], axis=0).astype(BF16)
        _stage1(a_scr, w1f_ref, xs_of, i, tb, n2, pitch)

    @pl.when(i >= nt)
    def _():
        inv = 1.0 / (nf_ref[...] + nb_ref[...])
        for jj in range(fb):
            x = _stage2(a_scr, wd_ref, (i - nt) * fb + jj, n1, pitch)
            o_ref[0, 0, jj] = (x[:n1] * inv).astype(o_ref.dtype)
            o_ref[0, 1, jj] = (x[n1:] * inv).astype(o_ref.dtype)


def _hy_fftconv_kernel(z_ref, xg_ref, k_ref, sk_ref, w1_ref, wd_ref, wdi_ref, w4_ref, o_ref, a_scr,
                       *, n1, n2, tb, fb):
    i = pl.program_id(2)
    pitch = _slab_pitch(n2)
    half = n2 // 2
    nt, nf = n1 // tb, n2 // fb

    @pl.when(i < nt)
    def _():
        _stage1(a_scr, w1_ref, lambda j: _time_col(z_ref, j, half).astype(BF16), i, tb, n2, pitch)

    @pl.when((i >= nt) & (i < nt + nf))
    def _():
        for jj in range(fb):
            f2 = (i - nt) * fb + jj
            x = _stage2(a_scr, wd_ref, f2, n1, pitch)
            xr, xi = x[:n1], x[n1:]
            kr, ki = k_ref[0, 0, jj].astype(F32), k_ref[0, 1, jj].astype(F32)
            y = jnp.concatenate([xr * kr - xi * ki, xr * ki + xi * kr], axis=0).astype(BF16)
            bv = _dot(wdi_ref[...], y)
            words = _pack_c(bv[:n1], bv[n1:])
            for s in range(HY_SLABS):
                a_scr[s, pl.ds(f2, n1, stride=pitch), :] = words[:, s * LANE:(s + 1) * LANE]

    @pl.when(i >= nt + nf)
    def _():
        sk = sk_ref[...]
        for j in range(tb):
            row = pl.multiple_of(((i - nt - nf) * tb + j) * pitch, 8)
            w = jnp.concatenate([a_scr[s, pl.ds(row, n2), :] for s in range(HY_SLABS)], axis=1)
            re, im = _unpack_c(w)
            y = _dot(w4_ref[j], jnp.concatenate([re, im], axis=0))
            out = _time_col(xg_ref, j, half) * (y + _time_col(z_ref, j, half) * sk)
            for s in range(HY_SLABS):
                o_ref[0, s, j * half:(j + 1) * half, :] = out[:, s * LANE:(s + 1) * LANE]


def _hyena_long(seq, taps, nrm, uc, skip, tb=32, fb=32):
    n, n1, n2 = _fft_split(seq)
    tabs = _dft_tables_long(seq)
    pitch = _slab_pitch(n2)
    half = n2 // 2
    cb = HY_SLABS * LANE
    tb, fb = min(tb, n1), min(fb, n2)
    nt, nf = n1 // tb, n2 // fb
    cf = taps.shape[0] * LANE
    ngrp = cf // 2 // cb
    scratch = [pltpu.VMEM((HY_SLABS, n1 * pitch, LANE), jnp.uint32)]
    tcol = lambda i: jnp.minimum(i, nt - 1)
    spec = pl.pallas_call(
        functools.partial(_hy_spectrum_kernel, n1=n1, n2=n2, tb=tb, fb=fb),
        grid=(ngrp, nt + nf),
        in_specs=[pl.BlockSpec((HY_SLABS, tb * half, LANE), lambda g, i: (g, tcol(i), 0)),
                  pl.BlockSpec((HY_SLABS, tb * half, LANE), lambda g, i: (ngrp + g, nt - 1 - tcol(i), 0)),
                  pl.BlockSpec((HY_SLABS, tb * half, LANE), lambda g, i: (ngrp + g, (nt - tcol(i)) % nt, 0)),
                  pl.BlockSpec((1, cb), lambda g, i: (0, g)),
                  pl.BlockSpec((1, cb), lambda g, i: (0, ngrp + g)),
                  pl.BlockSpec((tb, n2, 2 * n2), lambda g, i: (tcol(i), 0, 0)),
                  pl.BlockSpec(tabs["wd"].shape, lambda g, i: (0, 0))],
        out_specs=pl.BlockSpec((1, 2, fb, n1, cb), lambda g, i: (g, 0, jnp.maximum(i - nt, 0), 0, 0)),
        out_shape=jax.ShapeDtypeStruct((ngrp, 2, n2, n1, cb), BF16),
        scratch_shapes=scratch,
        compiler_params=_cparams(2),
        name="hyena_filter_spectrum",
    )(taps, taps, taps, nrm, nrm, tabs["w1f"], tabs["wd"])
    nb, nblk = uc.shape[1], uc.shape[2]
    ngc = nblk // HY_SLABS
    z, zsel = uc, 2

    def tblk(b, g, i):
        return (b, g, jnp.where(i < nt, i, jnp.maximum(i - nt - nf, 0)), 0)

    def tblk_late(b, g, i):
        return (b, g, jnp.maximum(i - nt - nf, 0), 0)

    for order in range(2):
        stacked = lambda sel, f: (lambda b, g, i: (sel,) + f(b, g, i))
        z = pl.pallas_call(
            functools.partial(_hy_fftconv_kernel, n1=n1, n2=n2, tb=tb, fb=fb),
            grid=(nb, ngc, 2 * nt + nf),
            in_specs=[pl.BlockSpec((1, 1, HY_SLABS, tb * half, LANE), stacked(zsel, tblk)),
                      pl.BlockSpec((1, 1, HY_SLABS, tb * half, LANE), stacked(order, tblk_late)),
                      pl.BlockSpec((1, 2, fb, n1, cb),
                                   lambda b, g, i: (order * ngc + g, 0, jnp.clip(i - nt, 0, nf - 1), 0, 0)),
                      pl.BlockSpec((1, cb), lambda b, g, i: (0, g)),
                      pl.BlockSpec((tb, half, 2 * n2), lambda b, g, i: (tcol(i), 0, 0)),
                      pl.BlockSpec(tabs["wd"].shape, lambda b, g, i: (0, 0)),
                      pl.BlockSpec(tabs["wdi"].shape, lambda b, g, i: (0, 0)),
                      pl.BlockSpec((tb, half, 2 * n2), lambda b, g, i: (jnp.maximum(i - nt - nf, 0), 0, 0))],
            out_specs=pl.BlockSpec((1, HY_SLABS, tb * half, LANE), tblk_late),
            out_shape=jax.ShapeDtypeStruct((nb, nblk, seq, LANE), F32),
            scratch_shapes=scratch,
            compiler_params=_cparams(3),
            name="hyena_fft_conv",
        )(z, uc, spec, skip[order].reshape(1, HY_W), tabs["w1"], tabs["wd"], tabs["wdi"], tabs["w4"])
        z, zsel = z.reshape((1,) + z.shape), 0
    return z[0]


def _hyena(p, lw, row0, seq, n_batch):
    long = _fft_split(seq)[1] > 1
    n1 = FFT_N1 if long else 1
    taps, nrm = _hyena_filter_taps(seq, lw["hy_w1"], lw["hy_b1"], lw["hy_w2"], lw["hy_b2"], lw["hy_w3"],
                                   lw["hy_b3"], lw["hy_w4"], lw["hy_freq"], n1)
    uc = _short_conv(p, lw["hy_conv_w"], lw["hy_conv_b"], row0, seq, n_batch, n1)
    if long:
        return _hyena_long(seq, taps, nrm, uc, lw["hy_skip"])
    taps = jnp.swapaxes(taps, 0, 1).reshape(seq, -1)
    return _hyena_short(seq, taps, nrm, uc, lw["hy_skip"]).reshape(n_batch * seq, HY_W)


def _mla_qkv_kernel(qa_ref, kva_ref, kr_ref, cs_ref, wq_ref, wkv_ref, gqa_ref, gkva_ref, gqn_ref, gqr_ref,
                    gkn_ref, gkr_ref, q_ref, k_ref, v_ref):
    cs = cs_ref[...]
    lane = lax.broadcasted_iota(jnp.int32, cs.shape, 1)
    low = lane < MLA_ROPE

    def rope(pair, gain2):
        ms = jnp.sum(jnp.where(low, pair * pair, 0.0), axis=-1, keepdims=True) * (1.0 / MLA_ROPE)
        t = pair * lax.rsqrt(ms + RMS_EPS) * gain2 * cs
        return jnp.where(low, t + pltpu.roll(t, MLA_ROPE, 1), 0.0)

    qq = _dot(_rms(qa_ref[...], gqa_ref[...]).astype(BF16), wq_ref[...])
    kv = _dot(_rms(kva_ref[...], gkva_ref[...]).astype(BF16), wkv_ref[...])
    kr = rope(kr_ref[...], gkr_ref[...])
    hp = MLA_HEAD_PAD
    tm = kr.shape[0]
    qscale = MLA_SCALE * LOG2_E
    ones_row = (lax.broadcasted_iota(jnp.int32, (V_ROWS - MLA_V, tm), 0) == 0).astype(BF16)
    for h in range(MLA_HEADS):
        qn = _rms(qq[:, h * hp:h * hp + MLA_NOPE], gqn_ref[...])
        qr = rope(qq[:, h * hp + MLA_NOPE:(h + 1) * hp], gqr_ref[...])
        q_ref[h, 0:LANE, :] = (qn * qscale).T.astype(BF16)
        q_ref[h, LANE:2 * LANE, :] = (qr * qscale).T.astype(BF16)
        kn = _rms(kv[:, h * hp:h * hp + MLA_NOPE], gkn_ref[...])
        k_ref[h, :, 0:LANE] = kn.astype(BF16)
        k_ref[h, :, LANE:2 * LANE] = kr.astype(BF16)
        v_ref[h, 0:MLA_V, :] = kv[:, h * hp + MLA_NOPE:(h + 1) * hp].T.astype(BF16)
        v_ref[h, MLA_V:V_ROWS, :] = ones_row


def _rope_table(seq, n_batch, ctx_rows):
    rows = seq // GRID_W
    row = np.repeat(np.arange(rows, dtype=np.float32), GRID_W)
    col = np.tile(np.arange(GRID_W, dtype=np.float32), rows)
    half = MLA_ROPE // 2
    inv = (ROPE_THETA ** (-np.arange(0, half, 2, dtype=np.float32) / half)).astype(np.float32)
    ar = (row[:, None] * inv).astype(np.float64)
    ac = (col[:, None] * inv).astype(np.float64)
    cos = np.concatenate([np.cos(ar), np.cos(ar), np.cos(ac), np.cos(ac)], axis=1)
    sin = np.concatenate([-np.sin(ar), np.sin(ar), -np.sin(ac), np.sin(ac)], axis=1)
    lat = np.tile(np.concatenate([cos, sin], axis=1), (n_batch, 1))
    ctx = np.concatenate([np.ones((ctx_rows, MLA_ROPE)), np.zeros((ctx_rows, MLA_ROPE))], axis=1)
    return jnp.asarray(np.concatenate([lat, ctx], axis=0), F32)


def _pair_gain(g):
    return jnp.concatenate([g, _rope_swap(g)]).reshape(1, 2 * MLA_ROPE)


def _mla_qkv(p, cs, lw, tm=512):
    n_rows = p.shape[0]
    hd, hp = MLA_HEADS, MLA_HEAD_PAD
    w_uq = lw["w_uq"].reshape(-1, hd, MLA_QK)
    wq = jnp.concatenate([w_uq, _rope_swap(w_uq[..., MLA_NOPE:])], axis=-1).reshape(-1, hd * hp).astype(BF16)
    wkv = lw["w_ukv"].astype(BF16)
    ql, kvl = wq.shape[0], wkv.shape[0]
    vec = lambda a: a.reshape(1, -1)
    full = lambda a: pl.BlockSpec(a.shape, lambda i: (0,) * a.ndim)
    args = (wq, wkv, vec(lw["q_a_norm"]), vec(lw["kv_a_norm"]), vec(lw["q_nope_norm"]),
            _pair_gain(lw["q_rope_norm"]), vec(lw["k_nope_norm"]), _pair_gain(lw["k_rope_norm"]))
    return pl.pallas_call(
        _mla_qkv_kernel,
        grid=(n_rows // tm,),
        in_specs=[pl.BlockSpec((tm, ql), lambda i: (i, COL_QA // ql)),
                  pl.BlockSpec((tm, kvl), lambda i: (i, COL_KVA // kvl)),
                  pl.BlockSpec((tm, LANE), lambda i: (i, COL_KR // LANE)),
                  pl.BlockSpec((tm, LANE), lambda i: (i, 0))] + [full(a) for a in args],
        out_specs=[pl.BlockSpec((hd, hp, tm), lambda i: (0, 0, i)),
                   pl.BlockSpec((hd, tm, hp), lambda i: (0, i, 0)),
                   pl.BlockSpec((hd, V_ROWS, tm), lambda i: (0, 0, i))],
        out_shape=[jax.ShapeDtypeStruct((hd, hp, n_rows), BF16),
                   jax.ShapeDtypeStruct((hd, n_rows, hp), BF16),
                   jax.ShapeDtypeStruct((hd, V_ROWS, n_rows), BF16)],
        compiler_params=_cparams(1),
        name="mla_qkv",
    )(p, p, p, cs, *args)


def _key_chunks(k_refs, v_refs, tk):
    chunks, s0 = [], 0
    for kr, vr in zip(k_refs, v_refs):
        n = kr.shape[1]
        step = min(tk, n)
        chunks += [(kr, vr, r0, step, s0 + r0) for r0 in range(0, n, step)]
        s0 += n
    return chunks


def _score_pass(qt, chunks, s_scr):
    m = None
    for kr, _, r0, rn, s0 in chunks:
        s = _dot(kr[0, r0:r0 + rn, :], qt)
        s_scr[s0:s0 + rn, :] = s
        mj = jnp.max(s, axis=0, keepdims=True)
        m = mj if m is None else jnp.maximum(m, mj)
    return m


def _value_pass(chunks, s_scr, m, o_ref):
    acc = None
    for _, vr, r0, rn, s0 in chunks:
        p = jnp.exp2((s_scr[s0:s0 + rn, :] - m).astype(BF16))
        part = _dot(vr[0, :, r0:r0 + rn], p)
        acc = part if acc is None else acc + part
    o_ref[...] = (acc[:MLA_V] / acc[MLA_V:MLA_V + 1]).astype(o_ref.dtype)


def _attn_ctx_kernel(q_ref, kc_ref, vc_ref, o_ref, s_scr, *, tk):
    chunks = _key_chunks([kc_ref], [vc_ref], tk)
    _value_pass(chunks, s_scr, _score_pass(q_ref[0], chunks, s_scr), o_ref)


def _attn_kernel(q_ref, k_ref, v_ref, kc_ref, vc_ref, o_ref, s_a, s_b, m_a, m_b, *, tk):
    i = pl.program_id(2)
    chunks = _key_chunks([kc_ref, k_ref], [vc_ref, v_ref], tk)

    @pl.when((pl.program_id(0) == 0) & (pl.program_id(1) == 0) & (i == 0))
    def _():
        s_b[...] = jnp.zeros_like(s_b)
        m_b[...] = jnp.zeros_like(m_b)

    def step(s_cur, m_cur, s_prev, m_prev):
        _value_pass(chunks, s_prev, m_prev[0:1, :], o_ref)
        m_cur[...] = jnp.broadcast_to(_score_pass(q_ref[0], chunks, s_cur), m_cur.shape)

    @pl.when(i % 2 == 0)
    def _():
        step(s_a, m_a, s_b, m_b)

    @pl.when(i % 2 == 1)
    def _():
        step(s_b, m_b, s_a, m_a)


def _attention(qt, k, vt, seq, ctx_len, n_batch, latent, tq=256, tk=1024):
    hd, hp, n_rows = qt.shape
    lat_rows = n_batch * seq
    cblk = lat_rows // ctx_len
    kc_spec = pl.BlockSpec((1, ctx_len, hp), lambda b, h, i: (h, cblk + b, 0))
    vc_spec = pl.BlockSpec((1, V_ROWS, ctx_len), lambda b, h, i: (h, 0, cblk + b))
    if latent:
        nq = seq // tq
        n_pairs = hd * nq
        heads, steps = 1, n_pairs + 1
        cur = lambda s: jnp.minimum(s, n_pairs - 1)
        prv = lambda s: jnp.maximum(s - 1, 0)
        in_specs = [pl.BlockSpec((1, hp, tq), lambda b, _, s: (cur(s) // nq, 0, b * nq + cur(s) % nq)),
                    pl.BlockSpec((1, seq, hp), lambda b, _, s: (cur(s) // nq, b, 0)),
                    pl.BlockSpec((1, V_ROWS, seq), lambda b, _, s: (prv(s) // nq, 0, b)),
                    pl.BlockSpec((1, ctx_len, hp), lambda b, _, s: (cur(s) // nq, cblk + b, 0)),
                    pl.BlockSpec((1, V_ROWS, ctx_len), lambda b, _, s: (prv(s) // nq, 0, cblk + b))]
        args = (qt, k, vt, k, vt)
        out_spec = pl.BlockSpec((MLA_V, tq), lambda b, _, s: (prv(s) // nq, b * nq + prv(s) % nq))
        out_cols, n_keys = lat_rows, seq + ctx_len
        body = functools.partial(_attn_kernel, tk=tk)
        scratch = [pltpu.VMEM((n_keys, tq), F32)] * 2 + [pltpu.VMEM((8, tq), F32)] * 2
    else:
        heads, steps = hd, 1
        tq = ctx_len
        in_specs = [pl.BlockSpec((1, hp, tq), lambda b, h, i: (h, 0, cblk + b)), kc_spec, vc_spec]
        args = (qt, k, vt)
        out_spec = pl.BlockSpec((MLA_V, tq), lambda b, h, i: (h, b))
        out_cols, n_keys = n_batch * ctx_len, ctx_len
        body = functools.partial(_attn_ctx_kernel, tk=tk)
        scratch = [pltpu.VMEM((n_keys, tq), F32)]
    return pl.pallas_call(
        body,
        grid=(n_batch, heads, steps),
        in_specs=in_specs,
        out_specs=out_spec,
        out_shape=jax.ShapeDtypeStruct((hd * MLA_V, out_cols), BF16),
        scratch_shapes=scratch,
        compiler_params=_cparams(3),
        name="mla_attention" if latent else "mla_attention_ctx",
    )(*args)


def _merge_kernel(*refs, lat_tiles):
    (x_ref, mod_ref, of_ref, ob_ref, g_ref, yb_ref, yc_ref, ga_ref, gb_ref, gc_ref,
     gain_ref, wa_ref, wb_ref, wc_ref, wo_ref) = refs[:15]
    ctx_refs = refs[15:-1]
    o_ref = refs[-1]
    is_ctx = pl.program_id(0) >= lat_tiles
    m = mod_ref[0]
    o = of_ref[...].astype(F32) + ob_ref[...].astype(F32)
    if ctx_refs:
        o = jnp.where(is_ctx, ctx_refs[0][...].astype(F32) + ctx_refs[1][...].astype(F32), o)
    gain = gain_ref[...]
    ya = jnp.concatenate([_rms(o[:, h * HG_D:(h + 1) * HG_D], gain) for h in range(HG_HEADS)], axis=1)
    ya = (ya * _silu(g_ref[...])).astype(BF16)
    nslab, tcols = yb_ref.shape[1], yb_ref.shape[3]
    qn = x_ref.shape[0] // yb_ref.shape[2]
    q0 = (pl.program_id(0) % (tcols // qn)) * qn
    yb = jnp.concatenate([jnp.concatenate([yb_ref[0, k, :, q0 + q, :] for k in range(nslab)], axis=1)
                          for q in range(qn)], axis=0)
    yc = yc_ref[...]
    if ctx_refs:
        yb = jnp.where(is_ctx, ctx_refs[2][...], yb)
        yc = jnp.where(is_ctx, ctx_refs[3][...], yc)
    mix = (_sigmoid(ga_ref[...]) * _dot(ya, wa_ref[...])
           + _sigmoid(gb_ref[...]) * _dot(yb.astype(BF16), wb_ref[...])
           + _sigmoid(gc_ref[...]) * _dot_tn(yc, wc_ref[...]))
    o_ref[...] = x_ref[...] + m[5:6] * _dot(mix.astype(BF16), wo_ref[...])


def _merge(xa, mod, o_f, o_b, p, y_b, y_c, ctx_parts, lw, n_rows, seq, n_batch, tm=512):
    d = xa.shape[1]
    hk = HG_HEADS * HG_D
    tpb = seq // tm
    lat_tiles = n_batch * tpb
    n1 = FFT_N1
    tcols = 8
    assert tm % n1 == 0 and tcols % (tm // n1) == 0 and seq % (tcols * n1) == 0
    per_blk = tcols // (tm // n1)
    y_b = y_b.reshape(n_batch, y_b.shape[1], n1, seq // n1, LANE)
    row = lambda w: pl.BlockSpec((tm, w), lambda i: (i, 0))
    pcol = lambda w, c: pl.BlockSpec((tm, w), lambda i: (i, c // w))
    full = lambda a: pl.BlockSpec(a.shape, lambda i: (0,) * a.ndim)
    ws = (lw["hg_out_norm"].reshape(1, HG_D), lw["w_br_a"].astype(BF16), lw["w_br_b"].astype(BF16),
          lw["w_br_c"].astype(BF16), lw["w_out"].astype(BF16))
    yb_spec = pl.BlockSpec((1, y_b.shape[1], n1, tcols, LANE),
                           lambda i: (jnp.minimum(i // tpb, n_batch - 1), 0, 0, (i % tpb) // per_blk, 0))
    lat = lambda i: jnp.minimum(i, lat_tiles - 1)
    late = lambda i: jnp.maximum(i - lat_tiles, 0)
    extra_specs, extra = [], []
    if ctx_parts is not None:
        extra_specs = [pl.BlockSpec((tm, hk), lambda i: (late(i), 0)),
                       pl.BlockSpec((tm, hk), lambda i: (late(i), 0)),
                       pl.BlockSpec((tm, HY_W), lambda i: (late(i), 0)),
                       pl.BlockSpec((MLA_HEADS * MLA_V, tm), lambda i: (0, late(i)))]
        extra = list(ctx_parts)
    return pl.pallas_call(
        functools.partial(_merge_kernel, lat_tiles=lat_tiles),
        grid=(n_rows // tm,),
        in_specs=[row(d), pl.BlockSpec((1, N_MOD, d), _group_map(tpb, n_batch)),
                  pl.BlockSpec((tm, hk), lambda i: (lat(i), 0)), pl.BlockSpec((tm, hk), lambda i: (lat(i), 0)),
                  pcol(hk, COL_G), yb_spec,
                  pl.BlockSpec((MLA_HEADS * MLA_V, tm), lambda i: (0, lat(i))),
                  pcol(d, COL_GA), pcol(d, COL_GB), pcol(d, COL_GC)] + [full(a) for a in ws] + extra_specs,
        out_specs=row(d),
        out_shape=jax.ShapeDtypeStruct((n_rows, d), F32),
        compiler_params=_cparams(1),
        name="merge",
    )(xa, mod, o_f, o_b, p, y_b, y_c, p, p, p, *ws, *extra)


def kernel(x, c, ctx, c_ctx, ada_w, ada_b, ffn1_norm, ffn1_w13, ffn1_w2, mix_norm, w_in, hg_lb_logits, hg_out_norm, hy_conv_w, hy_conv_b, hy_w1, hy_b1, hy_w2, hy_b2, hy_w3, hy_b3, hy_w4, hy_freq, hy_skip, q_a_norm, w_uq, kv_a_norm, w_ukv, q_nope_norm, q_rope_norm, k_nope_norm, k_rope_norm, w_br_a, w_br_b, w_br_c, w_out, ffn2_norm, ffn2_w13, ffn2_w2):
    stacked = dict(
        ada_w=ada_w, ada_b=ada_b, ffn1_norm=ffn1_norm, ffn1_w13=ffn1_w13, ffn1_w2=ffn1_w2, mix_norm=mix_norm,
        w_in=w_in, hg_out_norm=hg_out_norm, hy_conv_w=hy_conv_w, hy_conv_b=hy_conv_b, hy_w1=hy_w1, hy_b1=hy_b1,
        hy_w2=hy_w2, hy_b2=hy_b2, hy_w3=hy_w3, hy_b3=hy_b3, hy_w4=hy_w4, hy_freq=hy_freq, hy_skip=hy_skip,
        q_a_norm=q_a_norm, w_uq=w_uq, kv_a_norm=kv_a_norm, w_ukv=w_ukv, q_nope_norm=q_nope_norm,
        q_rope_norm=q_rope_norm, k_nope_norm=k_nope_norm, k_rope_norm=k_rope_norm, w_br_a=w_br_a,
        w_br_b=w_br_b, w_br_c=w_br_c, w_out=w_out, ffn2_norm=ffn2_norm, ffn2_w13=ffn2_w13, ffn2_w2=ffn2_w2)
    n_batch, seq, d = x.shape
    ctx_len = ctx.shape[1]
    depth = ada_w.shape[0]
    lat_rows, ctx_rows = n_batch * seq, n_batch * ctx_len
    all_rows = lat_rows + ctx_rows
    assert seq % 512 == 0 and ctx_rows % 512 == 0 and seq % ctx_len == 0 and seq % GRID_W == 0
    assert ctx_len % HG_CHUNK == 0 and n_batch < 8

    xa, xc = x.reshape(lat_rows, d), ctx.reshape(ctx_rows, d)
    cs = jnp.concatenate([c, c_ctx.reshape(1, d), jnp.zeros((7 - n_batch, d), F32)], axis=0)
    rope_cs = _rope_table(seq, n_batch, ctx_rows)
    zero_state = jnp.zeros((2, n_batch, HG_HEADS, HG_D, HG_D), F32)
    hk = HG_HEADS * HG_D

    for l in range(depth):
        lw = {name: val[l] for name, val in stacked.items()}
        need_ctx = l < depth - 1
        mod = _modulation(cs, lw["ada_w"], lw["ada_b"])
        xa = _half_ffn(xa, mod, lw["ffn1_norm"], lw["ffn1_w13"], lw["ffn1_w2"], 0, all_rows, seq, n_batch, xc=xc)
        xc = None
        p = _in_projection(xa, mod, lw["mix_norm"], _pack_w_in(lw["w_in"].astype(BF16)), seq, n_batch)

        ocf, ocb, s_c = _hgrn2_scan(p, hg_lb_logits, zero_state, l, lat_rows, ctx_len, n_batch)
        olf, olb, _ = _hgrn2_scan(p, hg_lb_logits, s_c, l, 0, seq, n_batch)
        y_b = _hyena(p, lw, 0, seq, n_batch)
        qt, k, vt = _mla_qkv(p, rope_cs, lw)
        y_c = _attention(qt, k, vt, seq, ctx_len, n_batch, latent=True)

        ctx_parts = None
        if need_ctx:
            ctx_parts = (ocf.reshape(ctx_rows, hk), ocb.reshape(ctx_rows, hk),
                         _hyena(p, lw, lat_rows, ctx_len, n_batch),
                         _attention(qt, k, vt, seq, ctx_len, n_batch, latent=False))
        mix_rows = all_rows if need_ctx else lat_rows
        xa = _merge(xa, mod, olf.reshape(lat_rows, hk), olb.reshape(lat_rows, hk), p, y_b, y_c, ctx_parts, lw,
                    mix_rows, seq, n_batch)
        xa = _half_ffn(xa, mod, lw["ffn2_norm"], lw["ffn2_w13"], lw["ffn2_w2"], 6, mix_rows, seq, n_batch)
    return xa[:lat_rows].reshape(n_batch, seq, d)
```

```python
import functools
import math

import numpy as np
import jax
import jax.numpy as jnp
from jax import lax
from jax.experimental import pallas as pl
from jax.experimental.pallas import tpu as pltpu

F32 = jnp.float32
BF16 = jnp.bfloat16

RMS_EPS = 1e-6
N_MOD = 9
GRID_W = 64
ROPE_THETA = 10000.0
HG_HEADS = 4
HG_D = 128
HG_CHUNK = 128
HY_W = 512
HY_TARGET = 1e-2
HY_MIN_DECAY = math.log(HY_TARGET) / 1.5
HY_MAX_DECAY = math.log(HY_TARGET) / 0.3
HY_SHIFT = 0.05
MLA_HEADS = 4
MLA_NOPE = 128
MLA_ROPE = 64
MLA_V = 128
MLA_QK = MLA_NOPE + MLA_ROPE
MLA_SCALE = MLA_QK ** -0.5
MLA_HEAD_PAD = 256
V_ROWS = MLA_V + 16
LOG2_E = math.log2(math.e)
FFT_N1 = 128
LANE = 128

VMEM_LIMIT = 52 * 1024 * 1024

COL_Q, COL_ZF, COL_ZB, COL_IV, COL_G = 0, 512, 1024, 1536, 2048
COL_HY = 2560
COL_QA, COL_KVA, COL_KR = 4096, 4352, 4480
COL_GA, COL_GB, COL_GC = 0, 1024, 2048


def _cparams(n_axes):
    return pltpu.CompilerParams(dimension_semantics=("arbitrary",) * n_axes,
                                vmem_limit_bytes=VMEM_LIMIT)


def _dot(a, b):
    return jnp.dot(a, b, preferred_element_type=F32)


def _dot_nt(a, b):
    return lax.dot_general(a, b, (((1,), (1,)), ((), ())), preferred_element_type=F32)


def _dot_tn(a, b):
    return lax.dot_general(a, b, (((0,), (0,)), ((), ())), preferred_element_type=F32)


def _sigmoid(x):
    return 1.0 / (1.0 + jnp.exp(-x))


def _silu(x):
    return x * _sigmoid(x)


def _rms(x, gain):
    return x * lax.rsqrt(jnp.mean(x * x, axis=-1, keepdims=True) + RMS_EPS) * gain


def _mod_kernel(c_ref, w_ref, b_ref, o_ref):
    o_ref[...] = _dot(_silu(c_ref[...]).astype(BF16), w_ref[...]) + b_ref[...]


def _modulation(cs, ada_w, ada_b):
    g, d = cs.shape
    n = ada_w.shape[1]
    tn = n // 4
    out = pl.pallas_call(
        _mod_kernel,
        grid=(n // tn,),
        in_specs=[pl.BlockSpec((g, d), lambda j: (0, 0)),
                  pl.BlockSpec((d, tn), lambda j: (0, j)),
                  pl.BlockSpec((1, tn), lambda j: (0, j))],
        out_specs=pl.BlockSpec((g, tn), lambda j: (0, j)),
        out_shape=jax.ShapeDtypeStruct((g, n), F32),
        compiler_params=_cparams(1),
        name="modulation",
    )(cs, ada_w.astype(BF16), ada_b.reshape(1, n))
    return out.reshape(g, N_MOD, d)


def _ffn_kernel(*refs, idx, ff, ck, lat_tiles):
    x_ref, mod_ref, g_ref, w13_ref, w2_ref = refs[:5]
    o_ref = refs[-1]
    x = x_ref[...]
    if len(refs) == 7:
        x = jnp.where(pl.program_id(0) >= lat_tiles, refs[5][...], x)
    m = mod_ref[0]
    h = (_rms(x, g_ref[...]) * (1.0 + m[idx + 1:idx + 2]) + m[idx:idx + 1]).astype(BF16)
    acc = jnp.zeros(x.shape, F32)
    for c0 in range(0, ff, ck):
        c1 = min(c0 + ck, ff)
        a = _dot(h, w13_ref[:, c0:c1])
        b = _dot(h, w13_ref[:, ff + c0:ff + c1])
        acc = acc + _dot((_silu(a) * b).astype(BF16), w2_ref[c0:c1, :])
    o_ref[...] = x + (0.5 * m[idx + 2:idx + 3]) * acc


def _group_map(tiles_per_batch, n_batch):
    return lambda i: (jnp.minimum(i // tiles_per_batch, n_batch), 0, 0)


def _half_ffn(xa, mod, gain, w13, w2, idx, n_rows, seq, n_batch, xc=None, tm=512, ck=512):
    d = xa.shape[1]
    ff = w2.shape[0]
    lat_tiles = n_batch * seq // tm
    const = dict(pipeline_mode=pl.Buffered(1))
    x_specs, xs = [pl.BlockSpec((tm, d), lambda i: (jnp.minimum(i, xa.shape[0] // tm - 1), 0))], [xa]
    tail_specs, tail = [], []
    if xc is not None:
        tail_specs, tail = [pl.BlockSpec((tm, d), lambda i: (jnp.maximum(i - lat_tiles, 0), 0))], [xc]
    return pl.pallas_call(
        functools.partial(_ffn_kernel, idx=idx, ff=ff, ck=ck, lat_tiles=lat_tiles),
        grid=(n_rows // tm,),
        in_specs=x_specs + [pl.BlockSpec((1, N_MOD, d), _group_map(seq // tm, n_batch)),
                            pl.BlockSpec((1, d), lambda i: (0, 0)),
                            pl.BlockSpec((d, 2 * ff), lambda i: (0, 0), **const),
                            pl.BlockSpec((ff, d), lambda i: (0, 0), **const)] + tail_specs,
        out_specs=pl.BlockSpec((tm, d), lambda i: (i, 0)),
        out_shape=jax.ShapeDtypeStruct((n_rows, d), F32),
        compiler_params=_cparams(1),
        name="half_ffn",
    )(*xs, mod, gain.reshape(1, d), w13.astype(BF16), w2.astype(BF16), *tail)


def _inproj_kernel(x_ref, mod_ref, g_ref, w_ref, o_ref):
    m = mod_ref[0]
    h = (_rms(x_ref[...], g_ref[...]) * (1.0 + m[4:5]) + m[3:4]).astype(BF16)
    o_ref[...] = _dot(h, w_ref[...]).astype(o_ref.dtype)


def _pack_w_in(w_in):
    d = w_in.shape[0]
    hk = HG_HEADS * HG_D
    sizes = (hk, hk, hk, hk, hk, 3 * HY_W, 256, 128, MLA_ROPE, d, d, d)
    offs = np.cumsum((0,) + sizes)
    q, zf, zb, iv, g, hy, qa, kva, kr, ga, gb, gc = (w_in[:, offs[i]:offs[i + 1]] for i in range(12))
    return (jnp.concatenate([ga, gb, gc], axis=1),
            jnp.concatenate([q, zf, zb, iv, g, hy, qa, kva, kr, _rope_swap(kr)], axis=1))


def _rope_swap(a):
    q = MLA_ROPE // 4
    return jnp.concatenate([a[..., q:2 * q], a[..., :q], a[..., 3 * q:], a[..., 2 * q:3 * q]], axis=-1)


def _in_projection(xa, mod, gain, w_packed, seq, n_batch, out_dtype, tm=512, tn=1536):
    n_rows, d = xa.shape
    n = w_packed.shape[1]
    assert n % tn == 0
    return pl.pallas_call(
        _inproj_kernel,
        grid=(n // tn, n_rows // tm),
        in_specs=[pl.BlockSpec((tm, d), lambda j, i: (i, 0)),
                  pl.BlockSpec((1, N_MOD, d), lambda j, i: (jnp.minimum(i // (seq // tm), n_batch), 0, 0)),
                  pl.BlockSpec((1, d), lambda j, i: (0, 0)),
                  pl.BlockSpec((d, tn), lambda j, i: (0, j))],
        out_specs=pl.BlockSpec((tm, tn), lambda j, i: (i, j)),
        out_shape=jax.ShapeDtypeStruct((n_rows, n), out_dtype),
        compiler_params=_cparams(2),
        name="in_projection",
    )(xa, mod, gain.reshape(1, d), w_packed)


def _hgrn2_tables(c, rev):
    t = np.arange(c)[:, None]
    u = np.arange(c)[None, :]
    mats = [(u <= t), (u > t)]
    masks = [(t == u)]
    h = c // 2
    while h >= 1:
        mid = (t // (2 * h)) * (2 * h) + h
        mats.append(np.where(t >= mid, (u >= mid) & (u <= t), (u >= t + 1) & (u <= mid - 1)))
        mid_s = (u // (2 * h)) * (2 * h) + h
        masks.append((t // (2 * h) == u // (2 * h)) & (u < mid_s) & (t >= mid))
        h //= 2
    mats = np.stack([m.astype(np.float32) for m in mats])
    masks = np.stack([m.astype(np.float32) for m in masks])
    if rev:
        mats = mats[:, ::-1, ::-1]
        masks = masks[:, ::-1, ::-1]
    mats = np.ascontiguousarray(mats).reshape(-1, c)
    return (jnp.asarray(np.concatenate([mats, mats], axis=1), BF16),
            jnp.asarray(np.ascontiguousarray(masks), F32))


HG_PAIR = 4


def _hgrn2_chunk(q, z, v, lg, dst_ref, msk_ref, st_ref, o_ref, ci, *, layer, rev, n_levels):
    c = q.shape[0]
    hk = HG_HEADS * HG_D
    e = jnp.exp(lg - jnp.max(lg, axis=0, keepdims=True))
    sm = e / jnp.sum(e, axis=0, keepdims=True)
    lb = jnp.zeros((1, hk), F32)
    for i in range(1, layer + 1):
        lb = lb + sm[i:i + 1]

    f = lb + (1.0 - lb) * _sigmoid(z)
    kk = 1.0 - f
    g = jnp.log(f)
    g_hi = g.astype(BF16)
    g_lo = (g - g_hi.astype(F32)).astype(BF16)
    dg = _dot(dst_ref[...], jnp.concatenate([g_hi, g_lo], axis=0))
    qs = _silu(q) * HG_D ** -0.5
    last = 0 if rev else c - 1

    for h in range(HG_HEADS):
        hs = slice(h * HG_D, (h + 1) * HG_D)
        qh, kh, vh = qs[:, hs].astype(BF16), kk[:, hs].astype(BF16), v[:, hs].astype(BF16)
        b_in = dg[0:c, hs]
        ex = lambda blk: jnp.exp(dg[blk * c:(blk + 1) * c, hs].astype(BF16))
        a = _dot_nt(qh, kh) * msk_ref[0]
        for l in range(1, n_levels + 1):
            el = ex(1 + l)
            a = a + _dot_nt(qh * el, kh * el) * msk_ref[l]
        st = st_ref[ci, h]
        o = _dot_nt(qh * ex(0), st.astype(BF16)) + _dot(a.astype(BF16), vh)
        o_ref[:, hs] = o.astype(o_ref.dtype)
        st_ref[ci, h] = jnp.exp(b_in[last:last + 1, :]) * st + _dot_tn(vh, kh * ex(1))


def _hgrn2_kernel(*refs, layer, n_levels, pair):
    n_in = 6 * pair
    ins, (lg_ref, dstf_ref, mskf_ref, dstb_ref, mskb_ref, s0_ref) = refs[:n_in], refs[n_in:n_in + 6]
    of_ref, ob_ref, sf_ref, st_ref = refs[n_in + 6:]
    j = pl.program_id(1)

    @pl.when(j == 0)
    def _():
        st_ref[...] = s0_ref[...].reshape(st_ref.shape)

    for e in range(pair):
        qf, zf, vf, qb, zb, vb = (r[...] for r in ins[6 * e:6 * e + 6])
        _hgrn2_chunk(qf, zf, vf, lg_ref[0], dstf_ref, mskf_ref, st_ref, of_ref.at[e], e,
                     layer=layer, rev=False, n_levels=n_levels)
        _hgrn2_chunk(qb, zb, vb, lg_ref[1], dstb_ref, mskb_ref, st_ref, ob_ref.at[e], pair + e,
                     layer=layer, rev=True, n_levels=n_levels)

    @pl.when(j == pl.num_programs(1) - 1)
    def _():
        sf_ref[...] = st_ref[...].reshape(sf_ref.shape)


def _hgrn2_scan(p, logits, s0, layer, row0, seq, n_batch):
    c = HG_CHUNK
    hk = HG_HEADS * HG_D
    nc = seq // c
    base = row0 // c
    n_levels = int(math.log2(c))
    tabs = _hgrn2_tables(c, False) + _hgrn2_tables(c, True)
    pair = math.gcd(HG_PAIR, n_batch)

    def prow(w, e, rev):
        return pl.BlockSpec((c, hk), lambda bp, j: (base + (bp * pair + e) * nc + (nc - 1 - j if rev else j),
                                                   w // hk))

    in_specs = []
    for e in range(pair):
        in_specs += [prow(COL_Q, e, False), prow(COL_ZF, e, False), prow(COL_IV, e, False),
                     prow(COL_Q, e, True), prow(COL_ZB, e, True), prow(COL_IV, e, True)]
    full = lambda a: pl.BlockSpec(a.shape, lambda bp, j: (0,) * a.ndim)
    st_spec = pl.BlockSpec((2, pair, HG_HEADS, HG_D, HG_D), lambda bp, j: (0, bp, 0, 0, 0))
    return pl.pallas_call(
        functools.partial(_hgrn2_kernel, layer=layer, n_levels=n_levels, pair=pair),
        grid=(n_batch // pair, nc),
        in_specs=in_specs + [full(logits)] + [full(t) for t in tabs] + [st_spec],
        out_specs=[pl.BlockSpec((pair, c, hk), lambda bp, j: (bp, j, 0)),
                   pl.BlockSpec((pair, c, hk), lambda bp, j: (bp, nc - 1 - j, 0)),
                   st_spec],
        out_shape=[jax.ShapeDtypeStruct((n_batch, seq, hk), BF16),
                   jax.ShapeDtypeStruct((n_batch, seq, hk), BF16),
                   jax.ShapeDtypeStruct((2, n_batch, HG_HEADS, HG_D, HG_D), F32)],
        scratch_shapes=[pltpu.VMEM((2 * pair, HG_HEADS, HG_D, HG_D), F32)],
        compiler_params=_cparams(2),
        name="hgrn2_scan",
    )(*([p] * (6 * pair)), logits, *tabs, s0)


def _shortconv_kernel(u_ref, w_ref, b_ref, o_ref):
    u = u_ref[...]
    n = u.shape[0]
    row = lax.broadcasted_iota(jnp.int32, u.shape, 0)
    prev = jnp.where(row == 0, 0.0, pltpu.roll(u, 1, 0))
    nxt = jnp.where(row == n - 1, 0.0, pltpu.roll(u, n - 1, 0))
    w = w_ref[...]
    o_ref[0, 0, 0] = prev * w[0:1] + u * w[1:2] + nxt * w[2:3] + b_ref[...]


def _shortconv_t1major_kernel(u_ref, w_ref, b_ref, o_ref, u_scr, *, n1):
    half = u_ref.shape[0] // n1
    w = w_ref[...]
    b = b_ref[...]
    pitch = n1 + 8

    def copy(g, carry):
        u_scr[pl.ds(pl.multiple_of(g * pitch, 8), n1), :] = u_ref[pl.ds(pl.multiple_of(g * n1, 8), n1), :]
        return carry

    lax.fori_loop(0, half, copy, 0, unroll=8)
    row = lax.broadcasted_iota(jnp.int32, (half, LANE), 0)
    col = lambda t1: u_scr[pl.ds(t1, half, stride=pitch), :]
    before = jnp.where(row == 0, 0.0, pltpu.roll(col(n1 - 1), 1, 0))
    after = jnp.where(row == half - 1, 0.0, pltpu.roll(col(0), half - 1, 0))

    def body(t1, carry):
        prev, cur = carry
        nxt = jnp.where(t1 == n1 - 1, after, col(jnp.minimum(t1 + 1, n1 - 1)))
        o_ref[0, 0, 0, pl.ds(pl.multiple_of(t1 * half, 8), half), :] = prev * w[0:1] + cur * w[1:2] + nxt * w[2:3] + b
        return cur, nxt

    lax.fori_loop(0, n1, body, (before, col(0)), unroll=16)


def _short_conv(p, w, b, row0, seq, n_batch, n1):
    nb = 3 * HY_W // LANE
    per = HY_W // LANE
    body, scratch = _shortconv_kernel, []
    if n1 > 1:
        body = functools.partial(_shortconv_t1major_kernel, n1=n1)
        scratch = [pltpu.VMEM((seq // n1 * (n1 + 8), LANE), F32)]
    return pl.pallas_call(
        body,
        grid=(n_batch, nb),
        in_specs=[pl.BlockSpec((seq, LANE), lambda bi, j: (row0 // seq + bi, COL_HY // LANE + j)),
                  pl.BlockSpec((3, LANE), lambda bi, j: (0, j)),
                  pl.BlockSpec((1, LANE), lambda bi, j: (0, j))],
        out_specs=pl.BlockSpec((1, 1, 1, seq, LANE), lambda bi, j: (j // per, bi, j % per, 0, 0)),
        out_shape=jax.ShapeDtypeStruct((3, n_batch, per, seq, LANE), F32),
        scratch_shapes=scratch,
        compiler_params=_cparams(2),
        name="hyena_short_conv",
    )(p, w, b.reshape(1, -1))


def _hy_filter_kernel(emb_ref, w1_ref, b1_ref, w2_ref, b2_ref, w3_ref, b3_ref, w4_ref, fr_ref,
                      dl_ref, o_ref, nrm_ref, *, seq, group):
    i = pl.program_id(0)
    hp = lax.Precision.HIGHEST
    fr = fr_ref[...]
    hid = jnp.sin(fr * (jnp.dot(emb_ref[...], w1_ref[...], precision=hp) + b1_ref[...]))
    hid = jnp.sin(fr * (jnp.dot(hid, w2_ref[...], precision=hp) + b2_ref[...]))
    hid = jnp.sin(fr * (jnp.dot(hid, w3_ref[...], precision=hp) + b3_ref[...]))
    h = jnp.dot(hid, w4_ref[...], precision=hp)
    tl, n = h.shape
    r = lax.broadcasted_iota(jnp.int32, h.shape, 0) + i * tl
    col = lax.broadcasted_iota(jnp.int32, h.shape, 1)
    per = seq // group
    pos = (r >> (per.bit_length() - 1)) + group * (r & (per - 1))
    t = pos.astype(F32) * (1.0 / (seq - 1))
    h = h * (jnp.exp(-t * dl_ref[...]) + HY_SHIFT)
    h = jnp.where((pos == 0) & (col >= n // 2), 0.0, h)
    for k in range(n // LANE):
        o_ref[k] = h[:, k * LANE:(k + 1) * LANE]

    @pl.when(i == 0)
    def _():
        nrm_ref[...] = jnp.zeros_like(nrm_ref)

    nrm_ref[...] += jnp.sum(jnp.abs(h), axis=0, keepdims=True)


def _hyena_filter_taps(seq, w1, b1, w2, b2, w3, b3, w4, freq, group):
    fh = w1.shape[1]
    n_emb = w1.shape[0]
    bands_n = (n_emb - 1) // 2
    tt = np.linspace(0.0, 1.0, seq, dtype=np.float32)[:, None].astype(np.float64)
    ww = (2.0 * math.pi / seq) * np.arange(seq, dtype=np.float64)[:, None]
    bands = np.linspace(1e-4, bands_n - 1, bands_n, dtype=np.float32)[None, :].astype(np.float64)
    emb = np.concatenate([tt, np.cos(bands * ww), -np.sin(bands * ww)], axis=-1)
    emb = np.pad(emb, ((0, 0), (0, LANE - n_emb))).astype(np.float32)
    assert seq % group == 0 and group & (group - 1) == 0 and (seq // group) & (seq // group - 1) == 0
    r = np.arange(seq)
    pos = r // (seq // group) + group * (r % (seq // group))
    emb = emb[pos]
    w1p = jnp.pad(w1, ((0, LANE - n_emb), (0, 0)))
    deltas = np.abs(np.linspace(HY_MIN_DECAY, HY_MAX_DECAY, HY_W, dtype=np.float32))
    deltas = np.tile(deltas, 4)[None, :]
    n = w4.shape[1]
    tl = min(seq, 512)
    full = lambda a: pl.BlockSpec(a.shape, lambda i: (0,) * a.ndim)
    args = (w1p, b1.reshape(1, fh), w2, b2.reshape(1, fh), w3, b3.reshape(1, fh), w4, freq.reshape(1, fh),
            jnp.asarray(deltas))
    return pl.pallas_call(
        functools.partial(_hy_filter_kernel, seq=seq, group=group),
        grid=(seq // tl,),
        in_specs=[pl.BlockSpec((tl, LANE), lambda i: (i, 0))] + [full(a) for a in args],
        out_specs=[pl.BlockSpec((n // LANE, tl, LANE), lambda i: (0, i, 0)),
                   pl.BlockSpec((1, n), lambda i: (0, 0))],
        out_shape=[jax.ShapeDtypeStruct((n // LANE, seq, LANE), F32), jax.ShapeDtypeStruct((1, n), F32)],
        compiler_params=_cparams(1),
        name="hyena_filter_taps",
    )(jnp.asarray(emb), *args)


def _fft_split(seq):
    n = 2 * seq
    n1 = FFT_N1 if n > 1024 else 1
    return n, n1, n // n1


def _cis(idx, n):
    ph = 2.0 * np.pi * (idx % n) / n
    return np.cos(ph), -np.sin(ph)


def _dft_tables_short(seq):
    n = 2 * seq
    cr, ci = _cis(np.arange(n)[:, None] * np.arange(seq)[None, :], n)
    w_fwd = np.concatenate([cr, ci], axis=0)
    w_inv = np.concatenate([cr.T, ci.T], axis=1) / n
    return dict(w_fwd=jnp.asarray(w_fwd, BF16), w_inv=jnp.asarray(w_inv, BF16))


def _dft_tables_long(seq):
    n, n1, n2 = _fft_split(seq)
    t1 = np.arange(n1)[:, None, None]
    f2 = np.arange(n2)[None, :, None]
    t2 = np.arange(n2 // 2)[None, None, :]
    cr, ci = _cis(f2 * (t1 + n1 * t2), n)
    w1 = np.concatenate([cr, ci], axis=1)
    w4 = np.concatenate([np.swapaxes(cr, 1, 2), np.swapaxes(ci, 1, 2)], axis=2) / n
    j = (n1 - t1) % n1 + n1 * t2
    br, bi = _cis(f2 * (n - j), n)
    w1f = np.concatenate([np.concatenate([cr, br], axis=2), np.concatenate([ci, bi], axis=2)], axis=1)
    gr, gi = _cis(np.arange(n1)[:, None] * np.arange(n1)[None, :], n1)
    wd = np.concatenate([np.concatenate([gr, -gi], axis=1), np.concatenate([gi, gr], axis=1)], axis=0)
    return dict(w1=jnp.asarray(np.swapaxes(w1, 1, 2), BF16), w4=jnp.asarray(w4, BF16),
                w1f=jnp.asarray(np.swapaxes(w1f, 1, 2), BF16),
                wd=jnp.asarray(wd, BF16), wdi=jnp.asarray(wd.T, BF16))


def _dft_rows_kernel(w_ref, x_ref, o_ref):
    o_ref[0] = _dot(w_ref[...], x_ref[0].astype(BF16)).astype(o_ref.dtype)


def _dft_rows(w, x, tn):
    nb, k, cols = x.shape
    m = w.shape[0]
    return pl.pallas_call(
        _dft_rows_kernel,
        grid=(nb, cols // tn),
        in_specs=[pl.BlockSpec((m, k), lambda b, j: (0, 0)),
                  pl.BlockSpec((1, k, tn), lambda b, j: (b, 0, j))],
        out_specs=pl.BlockSpec((1, m, tn), lambda b, j: (b, 0, j)),
        out_shape=jax.ShapeDtypeStruct((nb, m, cols), BF16),
        compiler_params=_cparams(2),
        name="hyena_dft_rows",
    )(w, x)


def _idft_gate_kernel(w_ref, b_ref, xg_ref, z_ref, sk_ref, o_ref):
    y = _dot(w_ref[...], b_ref[0])
    z = z_ref[0]
    o_ref[0] = xg_ref[0] * (y + z * sk_ref[...])


def _idft_gate(w, bc, xg, z, skip_t, tn):
    nb, k, cols = bc.shape
    m = w.shape[0]
    return pl.pallas_call(
        _idft_gate_kernel,
        grid=(nb, cols // tn),
        in_specs=[pl.BlockSpec((m, k), lambda b, j: (0, 0)),
                  pl.BlockSpec((1, k, tn), lambda b, j: (b, 0, j)),
                  pl.BlockSpec((1, m, tn), lambda b, j: (b, 0, j)),
                  pl.BlockSpec((1, m, tn), lambda b, j: (b, 0, j)),
                  pl.BlockSpec((1, tn), lambda b, j: (0, j))],
        out_specs=pl.BlockSpec((1, m, tn), lambda b, j: (b, 0, j)),
        out_shape=jax.ShapeDtypeStruct((nb, m, cols), F32),
        compiler_params=_cparams(2),
        name="hyena_idft_gate",
    )(w, bc, xg, z, skip_t)


def _spec_combine_kernel(af_ref, ab_ref, nf_ref, nb_ref, o_ref):
    inv = 1.0 / (nf_ref[...] + nb_ref[...])
    n = o_ref.shape[1]
    o_ref[0] = (af_ref[0, :n].astype(F32) + ab_ref[0, :n].astype(F32)) * inv
    o_ref[1] = (af_ref[0, n:].astype(F32) - ab_ref[0, n:].astype(F32)) * inv


def _spec_mul_kernel(a_ref, k_ref, o_ref):
    n = k_ref.shape[1]
    xr, xi = a_ref[0, :n].astype(F32), a_ref[0, n:].astype(F32)
    kr, ki = k_ref[0], k_ref[1]
    o_ref[0, :n] = (xr * kr - xi * ki).astype(o_ref.dtype)
    o_ref[0, n:] = (xr * ki + xi * kr).astype(o_ref.dtype)


def _hyena_short(seq, taps, nrm, uc, skip):
    n = 2 * seq
    uc = jnp.swapaxes(uc, 2, 3).reshape(3, uc.shape[1], seq, HY_W)
    tabs = _dft_tables_short(seq)
    cf = taps.shape[1]
    c = HY_W
    a = _dft_rows(tabs["w_fwd"], taps.reshape(1, seq, cf), tn=cf)
    nblk = cf // 2 // c
    spec = pl.pallas_call(
        _spec_combine_kernel,
        grid=(nblk,),
        in_specs=[pl.BlockSpec((1, 2 * n, c), lambda j: (0, 0, j)),
                  pl.BlockSpec((1, 2 * n, c), lambda j: (0, 0, j + nblk)),
                  pl.BlockSpec((1, c), lambda j: (0, j)),
                  pl.BlockSpec((1, c), lambda j: (0, j + nblk))],
        out_specs=pl.BlockSpec((2, n, c), lambda j: (0, 0, j)),
        out_shape=jax.ShapeDtypeStruct((2, n, cf // 2), F32),
        compiler_params=_cparams(1),
        name="hyena_spec_combine",
    )(a, a, nrm, nrm)
    nb = uc.shape[1]
    z = uc[2]
    for order in range(2):
        a = _dft_rows(tabs["w_fwd"], z, tn=c)
        bc = pl.pallas_call(
            _spec_mul_kernel,
            grid=(nb,),
            in_specs=[pl.BlockSpec((1, 2 * n, c), lambda b: (b, 0, 0)),
                      pl.BlockSpec((2, n, c), lambda b: (0, 0, order))],
            out_specs=pl.BlockSpec((1, 2 * n, c), lambda b: (b, 0, 0)),
            out_shape=jax.ShapeDtypeStruct((nb, 2 * n, c), BF16),
            compiler_params=_cparams(1),
            name="hyena_spec_mul",
        )(a, spec)
        z = _idft_gate(tabs["w_inv"], bc, uc[order], z, skip[order].reshape(1, c), tn=c)
    return z


HY_SLABS = 2


def _slab_pitch(n2):
    return n2 + 8


def _pack_c(re, im):
    hi = lax.bitcast_convert_type(re.astype(BF16).astype(F32), jnp.uint32)
    lo = lax.bitcast_convert_type(im.astype(BF16).astype(F32), jnp.uint32)
    return hi | (lo >> 16)


def _unpack_c(w):
    re = lax.bitcast_convert_type(w & jnp.uint32(0xFFFF0000), F32)
    im = lax.bitcast_convert_type(w << 16, F32)
    return re.astype(BF16), im.astype(BF16)


def _store_slab(a_scr, row, words):
    for s in range(HY_SLABS):
        a_scr[s, pl.ds(row, words.shape[0]), :] = words[:, s * LANE:(s + 1) * LANE]


def _stage1(a_scr, w_ref, xs_of, i, tb, n2, pitch):
    for j in range(tb):
        a = _dot_tn(w_ref[j], xs_of(j))
        _store_slab(a_scr, pl.multiple_of((i * tb + j) * pitch, 8), _pack_c(a[:n2], a[n2:]))


def _stage2(a_scr, wd_ref, f2, n1, pitch):
    w = jnp.concatenate([a_scr[s, pl.ds(f2, n1, stride=pitch), :] for s in range(HY_SLABS)], axis=1)
    re, im = _unpack_c(w)
    return _dot(wd_ref[...], jnp.concatenate([re, im], axis=0))


def _time_col(ref, j, half):
    lead = (0,) * (len(ref.shape) - 3)
    return jnp.concatenate([ref[lead + (s, slice(j * half, (j + 1) * half), slice(None))]
                            for s in range(HY_SLABS)], axis=1)


def _hy_spectrum_kernel(hf_ref, hb_ref, hb0_ref, nf_ref, nb_ref, w1f_ref, wd_ref, o_ref, a_scr,
                        *, n1, n2, tb, fb):
    i = pl.program_id(1)
    pitch = _slab_pitch(n2)
    half = n2 // 2
    nt = n1 // tb

    @pl.when(i < nt)
    def _():
        def xs_of(j):
            fut = _time_col(hb0_ref, 0, half) if j == 0 else _time_col(hb_ref, tb - j, half)
            return jnp.concatenate([_time_col(hf_ref, j, half), fut], axis=0).astype(BF16)
        _stage1(a_scr, w1f_ref, xs_of, i, tb, n2, pitch)

    @pl.when(i >= nt)
    def _():
        inv = 1.0 / (nf_ref[...] + nb_ref[...])
        for jj in range(fb):
            x = _stage2(a_scr, wd_ref, (i - nt) * fb + jj, n1, pitch)
            o_ref[0, 0, jj] = (x[:n1] * inv).astype(o_ref.dtype)
            o_ref[0, 1, jj] = (x[n1:] * inv).astype(o_ref.dtype)


def _hy_fftconv_kernel(z_ref, xg_ref, k_ref, sk_ref, w1_ref, wd_ref, wdi_ref, w4_ref, o_ref, a_scr,
                       *, n1, n2, tb, fb):
    i = pl.program_id(2)
    pitch = _slab_pitch(n2)
    half = n2 // 2
    nt, nf = n1 // tb, n2 // fb

    @pl.when(i < nt)
    def _():
        _stage1(a_scr, w1_ref, lambda j: _time_col(z_ref, j, half).astype(BF16), i, tb, n2, pitch)

    @pl.when((i >= nt) & (i < nt + nf))
    def _():
        for jj in range(fb):
            f2 = (i - nt) * fb + jj
            x = _stage2(a_scr, wd_ref, f2, n1, pitch)
            xr, xi = x[:n1], x[n1:]
            kr, ki = k_ref[0, 0, jj].astype(F32), k_ref[0, 1, jj].astype(F32)
            y = jnp.concatenate([xr * kr - xi * ki, xr * ki + xi * kr], axis=0).astype(BF16)
            bv = _dot(wdi_ref[...], y)
            words = _pack_c(bv[:n1], bv[n1:])
            for s in range(HY_SLABS):
                a_scr[s, pl.ds(f2, n1, stride=pitch), :] = words[:, s * LANE:(s + 1) * LANE]

    @pl.when(i >= nt + nf)
    def _():
        sk = sk_ref[...]
        for j in range(tb):
            row = pl.multiple_of(((i - nt - nf) * tb + j) * pitch, 8)
            w = jnp.concatenate([a_scr[s, pl.ds(row, n2), :] for s in range(HY_SLABS)], axis=1)
            re, im = _unpack_c(w)
            y = _dot(w4_ref[j], jnp.concatenate([re, im], axis=0))
            out = _time_col(xg_ref, j, half) * (y + _time_col(z_ref, j, half) * sk)
            for s in range(HY_SLABS):
                o_ref[0, s, j * half:(j + 1) * half, :] = out[:, s * LANE:(s + 1) * LANE]


def _hyena_long(seq, taps, nrm, uc, skip, tb=32, fb=32):
    n, n1, n2 = _fft_split(seq)
    tabs = _dft_tables_long(seq)
    pitch = _slab_pitch(n2)
    half = n2 // 2
    cb = HY_SLABS * LANE
    tb, fb = min(tb, n1), min(fb, n2)
    nt, nf = n1 // tb, n2 // fb
    cf = taps.shape[0] * LANE
    ngrp = cf // 2 // cb
    scratch = [pltpu.VMEM((HY_SLABS, n1 * pitch, LANE), jnp.uint32)]
    tcol = lambda i: jnp.minimum(i, nt - 1)
    spec = pl.pallas_call(
        functools.partial(_hy_spectrum_kernel, n1=n1, n2=n2, tb=tb, fb=fb),
        grid=(ngrp, nt + nf),
        in_specs=[pl.BlockSpec((HY_SLABS, tb * half, LANE), lambda g, i: (g, tcol(i), 0)),
                  pl.BlockSpec((HY_SLABS, tb * half, LANE), lambda g, i: (ngrp + g, nt - 1 - tcol(i), 0)),
                  pl.BlockSpec((HY_SLABS, tb * half, LANE), lambda g, i: (ngrp + g, (nt - tcol(i)) % nt, 0)),
                  pl.BlockSpec((1, cb), lambda g, i: (0, g)),
                  pl.BlockSpec((1, cb), lambda g, i: (0, ngrp + g)),
                  pl.BlockSpec((tb, n2, 2 * n2), lambda g, i: (tcol(i), 0, 0)),
                  pl.BlockSpec(tabs["wd"].shape, lambda g, i: (0, 0))],
        out_specs=pl.BlockSpec((1, 2, fb, n1, cb), lambda g, i: (g, 0, jnp.maximum(i - nt, 0), 0, 0)),
        out_shape=jax.ShapeDtypeStruct((ngrp, 2, n2, n1, cb), BF16),
        scratch_shapes=scratch,
        compiler_params=_cparams(2),
        name="hyena_filter_spectrum",
    )(taps, taps, taps, nrm, nrm, tabs["w1f"], tabs["wd"])
    nb, nblk = uc.shape[1], uc.shape[2]
    ngc = nblk // HY_SLABS
    z, zsel = uc, 2

    def tblk(b, g, i):
        return (b, g, jnp.where(i < nt, i, jnp.maximum(i - nt - nf, 0)), 0)

    def tblk_late(b, g, i):
        return (b, g, jnp.maximum(i - nt - nf, 0), 0)

    for order in range(2):
        stacked = lambda sel, f: (lambda b, g, i: (sel,) + f(b, g, i))
        z = pl.pallas_call(
            functools.partial(_hy_fftconv_kernel, n1=n1, n2=n2, tb=tb, fb=fb),
            grid=(nb, ngc, 2 * nt + nf),
            in_specs=[pl.BlockSpec((1, 1, HY_SLABS, tb * half, LANE), stacked(zsel, tblk)),
                      pl.BlockSpec((1, 1, HY_SLABS, tb * half, LANE), stacked(order, tblk_late)),
                      pl.BlockSpec((1, 2, fb, n1, cb),
                                   lambda b, g, i: (order * ngc + g, 0, jnp.clip(i - nt, 0, nf - 1), 0, 0)),
                      pl.BlockSpec((1, cb), lambda b, g, i: (0, g)),
                      pl.BlockSpec((tb, half, 2 * n2), lambda b, g, i: (tcol(i), 0, 0)),
                      pl.BlockSpec(tabs["wd"].shape, lambda b, g, i: (0, 0)),
                      pl.BlockSpec(tabs["wdi"].shape, lambda b, g, i: (0, 0)),
                      pl.BlockSpec((tb, half, 2 * n2), lambda b, g, i: (jnp.maximum(i - nt - nf, 0), 0, 0))],
            out_specs=pl.BlockSpec((1, HY_SLABS, tb * half, LANE), tblk_late),
            out_shape=jax.ShapeDtypeStruct((nb, nblk, seq, LANE), F32),
            scratch_shapes=scratch,
            compiler_params=_cparams(3),
            name="hyena_fft_conv",
        )(z, uc, spec, skip[order].reshape(1, HY_W), tabs["w1"], tabs["wd"], tabs["wdi"], tabs["w4"])
        z, zsel = z.reshape((1,) + z.shape), 0
    return z[0]


def _hyena(p, lw, row0, seq, n_batch):
    long = _fft_split(seq)[1] > 1
    n1 = FFT_N1 if long else 1
    taps, nrm = _hyena_filter_taps(seq, lw["hy_w1"], lw["hy_b1"], lw["hy_w2"], lw["hy_b2"], lw["hy_w3"],
                                   lw["hy_b3"], lw["hy_w4"], lw["hy_freq"], n1)
    uc = _short_conv(p, lw["hy_conv_w"], lw["hy_conv_b"], row0, seq, n_batch, n1)
    if long:
        return _hyena_long(seq, taps, nrm, uc, lw["hy_skip"])
    taps = jnp.swapaxes(taps, 0, 1).reshape(seq, -1)
    return _hyena_short(seq, taps, nrm, uc, lw["hy_skip"]).reshape(n_batch * seq, HY_W)


def _mla_qkv_kernel(qa_ref, kva_ref, kr_ref, cs_ref, wq_ref, wkv_ref, gqa_ref, gkva_ref, gqn_ref, gqr_ref,
                    gkn_ref, gkr_ref, q_ref, k_ref, v_ref):
    cs = cs_ref[...]
    lane = lax.broadcasted_iota(jnp.int32, cs.shape, 1)
    low = lane < MLA_ROPE

    def rope(pair, gain2):
        ms = jnp.sum(jnp.where(low, pair * pair, 0.0), axis=-1, keepdims=True) * (1.0 / MLA_ROPE)
        t = pair * lax.rsqrt(ms + RMS_EPS) * gain2 * cs
        return jnp.where(low, t + pltpu.roll(t, MLA_ROPE, 1), 0.0)

    qq = _dot(_rms(qa_ref[...], gqa_ref[...]).astype(BF16), wq_ref[...])
    kv = _dot(_rms(kva_ref[...], gkva_ref[...]).astype(BF16), wkv_ref[...])
    kr = rope(kr_ref[...], gkr_ref[...])
    hp = MLA_HEAD_PAD
    tm = kr.shape[0]
    qscale = MLA_SCALE * LOG2_E
    ones_row = (lax.broadcasted_iota(jnp.int32, (V_ROWS - MLA_V, tm), 0) == 0).astype(BF16)
    for h in range(MLA_HEADS):
        qn = _rms(qq[:, h * hp:h * hp + MLA_NOPE], gqn_ref[...])
        qr = rope(qq[:, h * hp + MLA_NOPE:(h + 1) * hp], gqr_ref[...])
        q_ref[h, 0:LANE, :] = (qn * qscale).T.astype(BF16)
        q_ref[h, LANE:2 * LANE, :] = (qr * qscale).T.astype(BF16)
        kn = _rms(kv[:, h * hp:h * hp + MLA_NOPE], gkn_ref[...])
        k_ref[h, :, 0:LANE] = kn.astype(BF16)
        k_ref[h, :, LANE:2 * LANE] = kr.astype(BF16)
        v_ref[h, 0:MLA_V, :] = kv[:, h * hp + MLA_NOPE:(h + 1) * hp].T.astype(BF16)
        v_ref[h, MLA_V:V_ROWS, :] = ones_row


def _rope_table(seq, n_batch, ctx_rows):
    rows = seq // GRID_W
    row = np.repeat(np.arange(rows, dtype=np.float32), GRID_W)
    col = np.tile(np.arange(GRID_W, dtype=np.float32), rows)
    half = MLA_ROPE // 2
    inv = (ROPE_THETA ** (-np.arange(0, half, 2, dtype=np.float32) / half)).astype(np.float32)
    ar = (row[:, None] * inv).astype(np.float64)
    ac = (col[:, None] * inv).astype(np.float64)
    cos = np.concatenate([np.cos(ar), np.cos(ar), np.cos(ac), np.cos(ac)], axis=1)
    sin = np.concatenate([-np.sin(ar), np.sin(ar), -np.sin(ac), np.sin(ac)], axis=1)
    lat = np.tile(np.concatenate([cos, sin], axis=1), (n_batch, 1))
    ctx = np.concatenate([np.ones((ctx_rows, MLA_ROPE)), np.zeros((ctx_rows, MLA_ROPE))], axis=1)
    return jnp.asarray(np.concatenate([lat, ctx], axis=0), F32)


def _pair_gain(g):
    return jnp.concatenate([g, _rope_swap(g)]).reshape(1, 2 * MLA_ROPE)


def _mla_qkv(p, cs, lw, tm=512):
    n_rows = p.shape[0]
    hd, hp = MLA_HEADS, MLA_HEAD_PAD
    w_uq = lw["w_uq"].reshape(-1, hd, MLA_QK)
    wq = jnp.concatenate([w_uq, _rope_swap(w_uq[..., MLA_NOPE:])], axis=-1).reshape(-1, hd * hp).astype(BF16)
    wkv = lw["w_ukv"].astype(BF16)
    ql, kvl = wq.shape[0], wkv.shape[0]
    vec = lambda a: a.reshape(1, -1)
    full = lambda a: pl.BlockSpec(a.shape, lambda i: (0,) * a.ndim)
    args = (wq, wkv, vec(lw["q_a_norm"]), vec(lw["kv_a_norm"]), vec(lw["q_nope_norm"]),
            _pair_gain(lw["q_rope_norm"]), vec(lw["k_nope_norm"]), _pair_gain(lw["k_rope_norm"]))
    return pl.pallas_call(
        _mla_qkv_kernel,
        grid=(n_rows // tm,),
        in_specs=[pl.BlockSpec((tm, ql), lambda i: (i, COL_QA // ql)),
                  pl.BlockSpec((tm, kvl), lambda i: (i, COL_KVA // kvl)),
                  pl.BlockSpec((tm, LANE), lambda i: (i, COL_KR // LANE)),
                  pl.BlockSpec((tm, LANE), lambda i: (i, 0))] + [full(a) for a in args],
        out_specs=[pl.BlockSpec((hd, hp, tm), lambda i: (0, 0, i)),
                   pl.BlockSpec((hd, tm, hp), lambda i: (0, i, 0)),
                   pl.BlockSpec((hd, V_ROWS, tm), lambda i: (0, 0, i))],
        out_shape=[jax.ShapeDtypeStruct((hd, hp, n_rows), BF16),
                   jax.ShapeDtypeStruct((hd, n_rows, hp), BF16),
                   jax.ShapeDtypeStruct((hd, V_ROWS, n_rows), BF16)],
        compiler_params=_cparams(1),
        name="mla_qkv",
    )(p, p, p, cs, *args)


def _key_chunks(k_refs, v_refs, tk):
    chunks, s0 = [], 0
    for kr, vr in zip(k_refs, v_refs):
        n = kr.shape[1]
        step = min(tk, n)
        chunks += [(kr, vr, r0, step, s0 + r0) for r0 in range(0, n, step)]
        s0 += n
    return chunks


def _score_pass(qt, chunks, s_scr):
    m = None
    for kr, _, r0, rn, s0 in chunks:
        s = _dot(kr[0, r0:r0 + rn, :], qt)
        s_scr[s0:s0 + rn, :] = s
        mj = jnp.max(s, axis=0, keepdims=True)
        m = mj if m is None else jnp.maximum(m, mj)
    return m


def _value_pass(chunks, s_scr, m, o_ref):
    acc = None
    for _, vr, r0, rn, s0 in chunks:
        p = jnp.exp2((s_scr[s0:s0 + rn, :] - m).astype(BF16))
        part = _dot(vr[0, :, r0:r0 + rn], p)
        acc = part if acc is None else acc + part
    o_ref[...] = (acc[:MLA_V] / acc[MLA_V:MLA_V + 1]).astype(o_ref.dtype)


def _attn_ctx_kernel(q_ref, kc_ref, vc_ref, o_ref, s_scr, *, tk):
    chunks = _key_chunks([kc_ref], [vc_ref], tk)
    _value_pass(chunks, s_scr, _score_pass(q_ref[0], chunks, s_scr), o_ref)


def _attn_kernel(q_ref, k_ref, v_ref, kc_ref, vc_ref, o_ref, s_a, s_b, m_a, m_b, *, tk):
    i = pl.program_id(2)
    chunks = _key_chunks([kc_ref, k_ref], [vc_ref, v_ref], tk)

    @pl.when((pl.program_id(0) == 0) & (pl.program_id(1) == 0) & (i == 0))
    def _():
        s_b[...] = jnp.zeros_like(s_b)
        m_b[...] = jnp.zeros_like(m_b)

    def step(s_cur, m_cur, s_prev, m_prev):
        _value_pass(chunks, s_prev, m_prev[0:1, :], o_ref)
        m_cur[...] = jnp.broadcast_to(_score_pass(q_ref[0], chunks, s_cur), m_cur.shape)

    @pl.when(i % 2 == 0)
    def _():
        step(s_a, m_a, s_b, m_b)

    @pl.when(i % 2 == 1)
    def _():
        step(s_b, m_b, s_a, m_a)


def _attention(qt, k, vt, seq, ctx_len, n_batch, latent, tq=256, tk=1024):
    hd, hp, n_rows = qt.shape
    lat_rows = n_batch * seq
    cblk = lat_rows // ctx_len
    kc_spec = pl.BlockSpec((1, ctx_len, hp), lambda b, h, i: (h, cblk + b, 0))
    vc_spec = pl.BlockSpec((1, V_ROWS, ctx_len), lambda b, h, i: (h, 0, cblk + b))
    if latent:
        nq = seq // tq
        n_pairs = hd * nq
        heads, steps = 1, n_pairs + 1
        cur = lambda s: jnp.minimum(s, n_pairs - 1)
        prv = lambda s: jnp.maximum(s - 1, 0)
        in_specs = [pl.BlockSpec((1, hp, tq), lambda b, _, s: (cur(s) // nq, 0, b * nq + cur(s) % nq)),
                    pl.BlockSpec((1, seq, hp), lambda b, _, s: (cur(s) // nq, b, 0)),
                    pl.BlockSpec((1, V_ROWS, seq), lambda b, _, s: (prv(s) // nq, 0, b)),
                    pl.BlockSpec((1, ctx_len, hp), lambda b, _, s: (cur(s) // nq, cblk + b, 0)),
                    pl.BlockSpec((1, V_ROWS, ctx_len), lambda b, _, s: (prv(s) // nq, 0, cblk + b))]
        args = (qt, k, vt, k, vt)
        out_spec = pl.BlockSpec((MLA_V, tq), lambda b, _, s: (prv(s) // nq, b * nq + prv(s) % nq))
        out_cols, n_keys = lat_rows, seq + ctx_len
        body = functools.partial(_attn_kernel, tk=tk)
        scratch = [pltpu.VMEM((n_keys, tq), F32)] * 2 + [pltpu.VMEM((8, tq), F32)] * 2
    else:
        heads, steps = hd, 1
        tq = ctx_len
        in_specs = [pl.BlockSpec((1, hp, tq), lambda b, h, i: (h, 0, cblk + b)), kc_spec, vc_spec]
        args = (qt, k, vt)
        out_spec = pl.BlockSpec((MLA_V, tq), lambda b, h, i: (h, b))
        out_cols, n_keys = n_batch * ctx_len, ctx_len
        body = functools.partial(_attn_ctx_kernel, tk=tk)
        scratch = [pltpu.VMEM((n_keys, tq), F32)]
    return pl.pallas_call(
        body,
        grid=(n_batch, heads, steps),
        in_specs=in_specs,
        out_specs=out_spec,
        out_shape=jax.ShapeDtypeStruct((hd * MLA_V, out_cols), BF16),
        scratch_shapes=scratch,
        compiler_params=_cparams(3),
        name="mla_attention" if latent else "mla_attention_ctx",
    )(*args)


def _merge_kernel(*refs, lat_tiles):
    (x_ref, mod_ref, of_ref, ob_ref, g_ref, yb_ref, yc_ref, ga_ref, gb_ref, gc_ref,
     gain_ref, wa_ref, wb_ref, wc_ref, wo_ref) = refs[:15]
    ctx_refs = refs[15:-1]
    o_ref = refs[-1]
    is_ctx = pl.program_id(0) >= lat_tiles
    m = mod_ref[0]
    o = of_ref[...].astype(F32) + ob_ref[...].astype(F32)
    if ctx_refs:
        o = jnp.where(is_ctx, ctx_refs[0][...].astype(F32) + ctx_refs[1][...].astype(F32), o)
    gain = gain_ref[...]
    ya = jnp.concatenate([_rms(o[:, h * HG_D:(h + 1) * HG_D], gain) for h in range(HG_HEADS)], axis=1)
    ya = (ya * _silu(g_ref[...])).astype(BF16)
    nslab, tcols = yb_ref.shape[1], yb_ref.shape[3]
    qn = x_ref.shape[0] // yb_ref.shape[2]
    q0 = (pl.program_id(0) % (tcols // qn)) * qn
    yb = jnp.concatenate([jnp.concatenate([yb_ref[0, k, :, q0 + q, :] for k in range(nslab)], axis=1)
                          for q in range(qn)], axis=0)
    yc = yc_ref[...]
    if ctx_refs:
        yb = jnp.where(is_ctx, ctx_refs[2][...], yb)
        yc = jnp.where(is_ctx, ctx_refs[3][...], yc)
    mix = (_sigmoid(ga_ref[...]) * _dot(ya, wa_ref[...])
           + _sigmoid(gb_ref[...]) * _dot(yb.astype(BF16), wb_ref[...])
           + _sigmoid(gc_ref[...]) * _dot_tn(yc, wc_ref[...]))
    o_ref[...] = x_ref[...] + m[5:6] * _dot(mix.astype(BF16), wo_ref[...])


def _merge(xa, mod, o_f, o_b, p, pg, y_b, y_c, ctx_parts, lw, n_rows, seq, n_batch, tm=512):
    d = xa.shape[1]
    hk = HG_HEADS * HG_D
    tpb = seq // tm
    lat_tiles = n_batch * tpb
    n1 = FFT_N1
    tcols = 8
    assert tm % n1 == 0 and tcols % (tm // n1) == 0 and seq % (tcols * n1) == 0
    per_blk = tcols // (tm // n1)
    y_b = y_b.reshape(n_batch, y_b.shape[1], n1, seq // n1, LANE)
    row = lambda w: pl.BlockSpec((tm, w), lambda i: (i, 0))
    pcol = lambda w, c: pl.BlockSpec((tm, w), lambda i: (i, c // w))
    full = lambda a: pl.BlockSpec(a.shape, lambda i: (0,) * a.ndim)
    ws = (lw["hg_out_norm"].reshape(1, HG_D), lw["w_br_a"].astype(BF16), lw["w_br_b"].astype(BF16),
          lw["w_br_c"].astype(BF16), lw["w_out"].astype(BF16))
    yb_spec = pl.BlockSpec((1, y_b.shape[1], n1, tcols, LANE),
                           lambda i: (jnp.minimum(i // tpb, n_batch - 1), 0, 0, (i % tpb) // per_blk, 0))
    lat = lambda i: jnp.minimum(i, lat_tiles - 1)
    late = lambda i: jnp.maximum(i - lat_tiles, 0)
    extra_specs, extra = [], []
    if ctx_parts is not None:
        extra_specs = [pl.BlockSpec((tm, hk), lambda i: (late(i), 0)),
                       pl.BlockSpec((tm, hk), lambda i: (late(i), 0)),
                       pl.BlockSpec((tm, HY_W), lambda i: (late(i), 0)),
                       pl.BlockSpec((MLA_HEADS * MLA_V, tm), lambda i: (0, late(i)))]
        extra = list(ctx_parts)
    return pl.pallas_call(
        functools.partial(_merge_kernel, lat_tiles=lat_tiles),
        grid=(n_rows // tm,),
        in_specs=[row(d), pl.BlockSpec((1, N_MOD, d), _group_map(tpb, n_batch)),
                  pl.BlockSpec((tm, hk), lambda i: (lat(i), 0)), pl.BlockSpec((tm, hk), lambda i: (lat(i), 0)),
                  pcol(hk, COL_G), yb_spec,
                  pl.BlockSpec((MLA_HEADS * MLA_V, tm), lambda i: (0, lat(i))),
                  pcol(d, COL_GA), pcol(d, COL_GB), pcol(d, COL_GC)] + [full(a) for a in ws] + extra_specs,
        out_specs=row(d),
        out_shape=jax.ShapeDtypeStruct((n_rows, d), F32),
        compiler_params=_cparams(1),
        name="merge",
    )(xa, mod, o_f, o_b, p, y_b, y_c, pg, pg, pg, *ws, *extra)


def kernel(x, c, ctx, c_ctx, ada_w, ada_b, ffn1_norm, ffn1_w13, ffn1_w2, mix_norm, w_in, hg_lb_logits, hg_out_norm, hy_conv_w, hy_conv_b, hy_w1, hy_b1, hy_w2, hy_b2, hy_w3, hy_b3, hy_w4, hy_freq, hy_skip, q_a_norm, w_uq, kv_a_norm, w_ukv, q_nope_norm, q_rope_norm, k_nope_norm, k_rope_norm, w_br_a, w_br_b, w_br_c, w_out, ffn2_norm, ffn2_w13, ffn2_w2):
    stacked = dict(
        ada_w=ada_w, ada_b=ada_b, ffn1_norm=ffn1_norm, ffn1_w13=ffn1_w13, ffn1_w2=ffn1_w2, mix_norm=mix_norm,
        w_in=w_in, hg_out_norm=hg_out_norm, hy_conv_w=hy_conv_w, hy_conv_b=hy_conv_b, hy_w1=hy_w1, hy_b1=hy_b1,
        hy_w2=hy_w2, hy_b2=hy_b2, hy_w3=hy_w3, hy_b3=hy_b3, hy_w4=hy_w4, hy_freq=hy_freq, hy_skip=hy_skip,
        q_a_norm=q_a_norm, w_uq=w_uq, kv_a_norm=kv_a_norm, w_ukv=w_ukv, q_nope_norm=q_nope_norm,
        q_rope_norm=q_rope_norm, k_nope_norm=k_nope_norm, k_rope_norm=k_rope_norm, w_br_a=w_br_a,
        w_br_b=w_br_b, w_br_c=w_br_c, w_out=w_out, ffn2_norm=ffn2_norm, ffn2_w13=ffn2_w13, ffn2_w2=ffn2_w2)
    n_batch, seq, d = x.shape
    ctx_len = ctx.shape[1]
    depth = ada_w.shape[0]
    lat_rows, ctx_rows = n_batch * seq, n_batch * ctx_len
    all_rows = lat_rows + ctx_rows
    assert seq % 512 == 0 and ctx_rows % 512 == 0 and seq % ctx_len == 0 and seq % GRID_W == 0
    assert ctx_len % HG_CHUNK == 0 and n_batch < 8

    xa, xc = x.reshape(lat_rows, d), ctx.reshape(ctx_rows, d)
    cs = jnp.concatenate([c, c_ctx.reshape(1, d), jnp.zeros((7 - n_batch, d), F32)], axis=0)
    rope_cs = _rope_table(seq, n_batch, ctx_rows)
    zero_state = jnp.zeros((2, n_batch, HG_HEADS, HG_D, HG_D), F32)
    hk = HG_HEADS * HG_D

    for l in range(depth):
        lw = {name: val[l] for name, val in stacked.items()}
        need_ctx = l < depth - 1
        mod = _modulation(cs, lw["ada_w"], lw["ada_b"])
        xa = _half_ffn(xa, mod, lw["ffn1_norm"], lw["ffn1_w13"], lw["ffn1_w2"], 0, all_rows, seq, n_batch, xc=xc)
        xc = None
        w_gates, w_rest = _pack_w_in(lw["w_in"].astype(BF16))
        p = _in_projection(xa, mod, lw["mix_norm"], w_rest, seq, n_batch, F32)
        pg = _in_projection(xa, mod, lw["mix_norm"], w_gates, seq, n_batch, BF16)

        ocf, ocb, s_c = _hgrn2_scan(p, hg_lb_logits, zero_state, l, lat_rows, ctx_len, n_batch)
        olf, olb, _ = _hgrn2_scan(p, hg_lb_logits, s_c, l, 0, seq, n_batch)
        y_b = _hyena(p, lw, 0, seq, n_batch)
        qt, k, vt = _mla_qkv(p, rope_cs, lw)
        y_c = _attention(qt, k, vt, seq, ctx_len, n_batch, latent=True)

        ctx_parts = None
        if need_ctx:
            ctx_parts = (ocf.reshape(ctx_rows, hk), ocb.reshape(ctx_rows, hk),
                         _hyena(p, lw, lat_rows, ctx_len, n_batch),
                         _attention(qt, k, vt, seq, ctx_len, n_batch, latent=False))
        mix_rows = all_rows if need_ctx else lat_rows
        xa = _merge(xa, mod, olf.reshape(lat_rows, hk), olb.reshape(lat_rows, hk), p, pg, y_b, y_c, ctx_parts,
                    lw, mix_rows, seq, n_batch)
        xa = _half_ffn(xa, mod, lw["ffn2_norm"], lw["ffn2_w13"], lw["ffn2_w2"], 6, mix_rows, seq, n_batch)
    return xa[:lat_rows].reshape(n_batch, seq, d)
```

```python
import functools
import math

import numpy as np
import jax
import jax.numpy as jnp
from jax import lax
from jax.experimental import pallas as pl
from jax.experimental.pallas import tpu as pltpu

F32 = jnp.float32
BF16 = jnp.bfloat16

RMS_EPS = 1e-6
N_MOD = 9
GRID_W = 64
ROPE_THETA = 10000.0
HG_HEADS = 4
HG_D = 128
HG_CHUNK = 128
HY_W = 512
HY_TARGET = 1e-2
HY_MIN_DECAY = math.log(HY_TARGET) / 1.5
HY_MAX_DECAY = math.log(HY_TARGET) / 0.3
HY_SHIFT = 0.05
MLA_HEADS = 4
MLA_NOPE = 128
MLA_ROPE = 64
MLA_V = 128
MLA_QK = MLA_NOPE + MLA_ROPE
MLA_SCALE = MLA_QK ** -0.5
MLA_HEAD_PAD = 256
V_ROWS = MLA_V + 16
LOG2_E = math.log2(math.e)
FFT_N1 = 128
LANE = 128

VMEM_LIMIT = 52 * 1024 * 1024

COL_Q, COL_ZF, COL_ZB, COL_IV, COL_G = 0, 512, 1024, 1536, 2048
COL_HY = 2560
COL_QA, COL_KVA, COL_KR = 4096, 4352, 4480
COL_GA, COL_GB, COL_GC = 0, 1024, 2048


def _cparams(n_axes):
    return pltpu.CompilerParams(dimension_semantics=("arbitrary",) * n_axes,
                                vmem_limit_bytes=VMEM_LIMIT)


def _dot(a, b):
    return jnp.dot(a, b, preferred_element_type=F32)


def _dot_nt(a, b):
    return lax.dot_general(a, b, (((1,), (1,)), ((), ())), preferred_element_type=F32)


def _dot_tn(a, b):
    return lax.dot_general(a, b, (((0,), (0,)), ((), ())), preferred_element_type=F32)


def _sigmoid(x):
    return 1.0 / (1.0 + jnp.exp(-x))


def _silu(x):
    return x * _sigmoid(x)


def _rms(x, gain):
    return x * lax.rsqrt(jnp.mean(x * x, axis=-1, keepdims=True) + RMS_EPS) * gain


def _mod_kernel(c_ref, w_ref, b_ref, o_ref):
    o_ref[...] = _dot(_silu(c_ref[...]).astype(BF16), w_ref[...]) + b_ref[...]


def _modulation(cs, ada_w, ada_b):
    g, d = cs.shape
    n = ada_w.shape[1]
    tn = n // 4
    out = pl.pallas_call(
        _mod_kernel,
        grid=(n // tn,),
        in_specs=[pl.BlockSpec((g, d), lambda j: (0, 0)),
                  pl.BlockSpec((d, tn), lambda j: (0, j)),
                  pl.BlockSpec((1, tn), lambda j: (0, j))],
        out_specs=pl.BlockSpec((g, tn), lambda j: (0, j)),
        out_shape=jax.ShapeDtypeStruct((g, n), F32),
        compiler_params=_cparams(1),
        name="modulation",
    )(cs, ada_w.astype(BF16), ada_b.reshape(1, n))
    return out.reshape(g, N_MOD, d)


def _ffn_kernel(*refs, idx, ff, ck, lat_tiles):
    x_ref, mod_ref, g_ref, w13_ref, w2_ref = refs[:5]
    o_ref = refs[-1]
    x = x_ref[...]
    if len(refs) == 7:
        x = jnp.where(pl.program_id(0) >= lat_tiles, refs[5][...], x)
    m = mod_ref[0]
    h = (_rms(x, g_ref[...]) * (1.0 + m[idx + 1:idx + 2]) + m[idx:idx + 1]).astype(BF16)
    acc = jnp.zeros(x.shape, F32)
    for c0 in range(0, ff, ck):
        c1 = min(c0 + ck, ff)
        a = _dot(h, w13_ref[:, c0:c1])
        b = _dot(h, w13_ref[:, ff + c0:ff + c1])
        acc = acc + _dot((_silu(a) * b).astype(BF16), w2_ref[c0:c1, :])
    o_ref[...] = x + (0.5 * m[idx + 2:idx + 3]) * acc


def _group_map(tiles_per_batch, n_batch):
    return lambda i: (jnp.minimum(i // tiles_per_batch, n_batch), 0, 0)


def _half_ffn(xa, mod, gain, w13, w2, idx, n_rows, seq, n_batch, xc=None, tm=512, ck=512):
    d = xa.shape[1]
    ff = w2.shape[0]
    lat_tiles = n_batch * seq // tm
    const = dict(pipeline_mode=pl.Buffered(1))
    x_specs, xs = [pl.BlockSpec((tm, d), lambda i: (jnp.minimum(i, xa.shape[0] // tm - 1), 0))], [xa]
    tail_specs, tail = [], []
    if xc is not None:
        tail_specs, tail = [pl.BlockSpec((tm, d), lambda i: (jnp.maximum(i - lat_tiles, 0), 0))], [xc]
    return pl.pallas_call(
        functools.partial(_ffn_kernel, idx=idx, ff=ff, ck=ck, lat_tiles=lat_tiles),
        grid=(n_rows // tm,),
        in_specs=x_specs + [pl.BlockSpec((1, N_MOD, d), _group_map(seq // tm, n_batch)),
                            pl.BlockSpec((1, d), lambda i: (0, 0)),
                            pl.BlockSpec((d, 2 * ff), lambda i: (0, 0), **const),
                            pl.BlockSpec((ff, d), lambda i: (0, 0), **const)] + tail_specs,
        out_specs=pl.BlockSpec((tm, d), lambda i: (i, 0)),
        out_shape=jax.ShapeDtypeStruct((n_rows, d), F32),
        compiler_params=_cparams(1),
        name="half_ffn",
    )(*xs, mod, gain.reshape(1, d), w13.astype(BF16), w2.astype(BF16), *tail)


def _inproj_kernel(x_ref, mod_ref, g_ref, w_ref, o_ref):
    m = mod_ref[0]
    h = (_rms(x_ref[...], g_ref[...]) * (1.0 + m[4:5]) + m[3:4]).astype(BF16)
    o_ref[...] = _dot(h, w_ref[...]).astype(o_ref.dtype)


def _pack_w_in(w_in):
    d = w_in.shape[0]
    hk = HG_HEADS * HG_D
    sizes = (hk, hk, hk, hk, hk, 3 * HY_W, 256, 128, MLA_ROPE, d, d, d)
    offs = np.cumsum((0,) + sizes)
    q, zf, zb, iv, g, hy, qa, kva, kr, ga, gb, gc = (w_in[:, offs[i]:offs[i + 1]] for i in range(12))
    return (jnp.concatenate([ga, gb, gc], axis=1),
            jnp.concatenate([q, zf, zb, iv, g, hy, qa, kva, kr, _rope_swap(kr)], axis=1))


def _rope_swap(a):
    q = MLA_ROPE // 4
    return jnp.concatenate([a[..., q:2 * q], a[..., :q], a[..., 3 * q:], a[..., 2 * q:3 * q]], axis=-1)


def _in_projection(xa, mod, gain, w_packed, seq, n_batch, out_dtype, tm=512, tn=1536):
    n_rows, d = xa.shape
    n = w_packed.shape[1]
    assert n % tn == 0
    return pl.pallas_call(
        _inproj_kernel,
        grid=(n // tn, n_rows // tm),
        in_specs=[pl.BlockSpec((tm, d), lambda j, i: (i, 0)),
                  pl.BlockSpec((1, N_MOD, d), lambda j, i: (jnp.minimum(i // (seq // tm), n_batch), 0, 0)),
                  pl.BlockSpec((1, d), lambda j, i: (0, 0)),
                  pl.BlockSpec((d, tn), lambda j, i: (0, j))],
        out_specs=pl.BlockSpec((tm, tn), lambda j, i: (i, j)),
        out_shape=jax.ShapeDtypeStruct((n_rows, n), out_dtype),
        compiler_params=_cparams(2),
        name="in_projection",
    )(xa, mod, gain.reshape(1, d), w_packed)


def _hgrn2_tables(c, rev):
    t = np.arange(c)[:, None]
    u = np.arange(c)[None, :]
    mats = [(u <= t), (u > t)]
    masks = [(t == u)]
    h = c // 2
    while h >= 1:
        mid = (t // (2 * h)) * (2 * h) + h
        mats.append(np.where(t >= mid, (u >= mid) & (u <= t), (u >= t + 1) & (u <= mid - 1)))
        mid_s = (u // (2 * h)) * (2 * h) + h
        masks.append((t // (2 * h) == u // (2 * h)) & (u < mid_s) & (t >= mid))
        h //= 2
    mats = np.stack([m.astype(np.float32) for m in mats])
    masks = np.stack([m.astype(np.float32) for m in masks])
    if rev:
        mats = mats[:, ::-1, ::-1]
        masks = masks[:, ::-1, ::-1]
    mats = np.ascontiguousarray(mats).reshape(-1, c)
    return (jnp.asarray(np.concatenate([mats, mats], axis=1), BF16),
            jnp.asarray(np.ascontiguousarray(masks), F32))


HG_PAIR = 4


def _hgrn2_chunk(q, z, v, lg, dst_ref, msk_ref, st_ref, o_ref, ci, *, layer, rev, n_levels):
    c = q.shape[0]
    hk = HG_HEADS * HG_D
    e = jnp.exp(lg - jnp.max(lg, axis=0, keepdims=True))
    sm = e / jnp.sum(e, axis=0, keepdims=True)
    lb = jnp.zeros((1, hk), F32)
    for i in range(1, layer + 1):
        lb = lb + sm[i:i + 1]

    f = lb + (1.0 - lb) * _sigmoid(z)
    kk = 1.0 - f
    g = jnp.log(f)
    g_hi = g.astype(BF16)
    g_lo = (g - g_hi.astype(F32)).astype(BF16)
    dg = _dot(dst_ref[...], jnp.concatenate([g_hi, g_lo], axis=0))
    qs = _silu(q) * HG_D ** -0.5
    last = 0 if rev else c - 1

    for h in range(HG_HEADS):
        hs = slice(h * HG_D, (h + 1) * HG_D)
        qh, kh, vh = qs[:, hs].astype(BF16), kk[:, hs].astype(BF16), v[:, hs].astype(BF16)
        b_in = dg[0:c, hs]
        ex = lambda blk: jnp.exp(dg[blk * c:(blk + 1) * c, hs].astype(BF16))
        a = _dot_nt(qh, kh) * msk_ref[0]
        for l in range(1, n_levels + 1):
            el = ex(1 + l)
            a = a + _dot_nt(qh * el, kh * el) * msk_ref[l]
        st = st_ref[ci, h]
        o = _dot_nt(qh * ex(0), st.astype(BF16)) + _dot(a.astype(BF16), vh)
        o_ref[:, hs] = o.astype(o_ref.dtype)
        st_ref[ci, h] = jnp.exp(b_in[last:last + 1, :]) * st + _dot_tn(vh, kh * ex(1))


def _hgrn2_kernel(*refs, layer, n_levels, pair):
    n_in = 6 * pair
    ins, (lg_ref, dstf_ref, mskf_ref, dstb_ref, mskb_ref, s0_ref) = refs[:n_in], refs[n_in:n_in + 6]
    of_ref, ob_ref, sf_ref, st_ref = refs[n_in + 6:]
    j = pl.program_id(1)

    @pl.when(j == 0)
    def _():
        st_ref[...] = s0_ref[...].reshape(st_ref.shape)

    for e in range(pair):
        qf, zf, vf, qb, zb, vb = (r[...] for r in ins[6 * e:6 * e + 6])
        _hgrn2_chunk(qf, zf, vf, lg_ref[0], dstf_ref, mskf_ref, st_ref, of_ref.at[e], e,
                     layer=layer, rev=False, n_levels=n_levels)
        _hgrn2_chunk(qb, zb, vb, lg_ref[1], dstb_ref, mskb_ref, st_ref, ob_ref.at[e], pair + e,
                     layer=layer, rev=True, n_levels=n_levels)

    @pl.when(j == pl.num_programs(1) - 1)
    def _():
        sf_ref[...] = st_ref[...].reshape(sf_ref.shape)


def _hgrn2_scan(p, logits, s0, layer, row0, seq, n_batch):
    c = HG_CHUNK
    hk = HG_HEADS * HG_D
    nc = seq // c
    base = row0 // c
    n_levels = int(math.log2(c))
    tabs = _hgrn2_tables(c, False) + _hgrn2_tables(c, True)
    pair = math.gcd(HG_PAIR, n_batch)

    def prow(w, e, rev):
        return pl.BlockSpec((c, hk), lambda bp, j: (base + (bp * pair + e) * nc + (nc - 1 - j if rev else j),
                                                   w // hk))

    in_specs = []
    for e in range(pair):
        in_specs += [prow(COL_Q, e, False), prow(COL_ZF, e, False), prow(COL_IV, e, False),
                     prow(COL_Q, e, True), prow(COL_ZB, e, True), prow(COL_IV, e, True)]
    full = lambda a: pl.BlockSpec(a.shape, lambda bp, j: (0,) * a.ndim)
    st_spec = pl.BlockSpec((2, pair, HG_HEADS, HG_D, HG_D), lambda bp, j: (0, bp, 0, 0, 0))
    return pl.pallas_call(
        functools.partial(_hgrn2_kernel, layer=layer, n_levels=n_levels, pair=pair),
        grid=(n_batch // pair, nc),
        in_specs=in_specs + [full(logits)] + [full(t) for t in tabs] + [st_spec],
        out_specs=[pl.BlockSpec((pair, c, hk), lambda bp, j: (bp, j, 0)),
                   pl.BlockSpec((pair, c, hk), lambda bp, j: (bp, nc - 1 - j, 0)),
                   st_spec],
        out_shape=[jax.ShapeDtypeStruct((n_batch, seq, hk), BF16),
                   jax.ShapeDtypeStruct((n_batch, seq, hk), BF16),
                   jax.ShapeDtypeStruct((2, n_batch, HG_HEADS, HG_D, HG_D), F32)],
        scratch_shapes=[pltpu.VMEM((2 * pair, HG_HEADS, HG_D, HG_D), F32)],
        compiler_params=_cparams(2),
        name="hgrn2_scan",
    )(*([p] * (6 * pair)), logits, *tabs, s0)


def _shortconv_kernel(u_ref, w_ref, b_ref, o_ref):
    u = u_ref[...]
    n = u.shape[0]
    row = lax.broadcasted_iota(jnp.int32, u.shape, 0)
    prev = jnp.where(row == 0, 0.0, pltpu.roll(u, 1, 0))
    nxt = jnp.where(row == n - 1, 0.0, pltpu.roll(u, n - 1, 0))
    w = w_ref[...]
    o_ref[0, 0, 0] = prev * w[0:1] + u * w[1:2] + nxt * w[2:3] + b_ref[...]


def _shortconv_t1major_kernel(u_ref, w_ref, b_ref, o_ref, u_scr, *, n1):
    half = u_ref.shape[0] // n1
    w = w_ref[...]
    b = b_ref[...]
    pitch = n1 + 8

    def copy(g, carry):
        u_scr[pl.ds(pl.multiple_of(g * pitch, 8), n1), :] = u_ref[pl.ds(pl.multiple_of(g * n1, 8), n1), :]
        return carry

    lax.fori_loop(0, half, copy, 0, unroll=8)
    row = lax.broadcasted_iota(jnp.int32, (half, LANE), 0)
    col = lambda t1: u_scr[pl.ds(t1, half, stride=pitch), :]
    before = jnp.where(row == 0, 0.0, pltpu.roll(col(n1 - 1), 1, 0))
    after = jnp.where(row == half - 1, 0.0, pltpu.roll(col(0), half - 1, 0))

    def body(t1, carry):
        prev, cur = carry
        nxt = jnp.where(t1 == n1 - 1, after, col(jnp.minimum(t1 + 1, n1 - 1)))
        o_ref[0, 0, 0, pl.ds(pl.multiple_of(t1 * half, 8), half), :] = prev * w[0:1] + cur * w[1:2] + nxt * w[2:3] + b
        return cur, nxt

    lax.fori_loop(0, n1, body, (before, col(0)), unroll=16)


def _short_conv(p, w, b, row0, seq, n_batch, n1):
    nb = 3 * HY_W // LANE
    per = HY_W // LANE
    body, scratch = _shortconv_kernel, []
    if n1 > 1:
        body = functools.partial(_shortconv_t1major_kernel, n1=n1)
        scratch = [pltpu.VMEM((seq // n1 * (n1 + 8), LANE), F32)]
    return pl.pallas_call(
        body,
        grid=(n_batch, nb),
        in_specs=[pl.BlockSpec((seq, LANE), lambda bi, j: (row0 // seq + bi, COL_HY // LANE + j)),
                  pl.BlockSpec((3, LANE), lambda bi, j: (0, j)),
                  pl.BlockSpec((1, LANE), lambda bi, j: (0, j))],
        out_specs=pl.BlockSpec((1, 1, 1, seq, LANE), lambda bi, j: (j // per, bi, j % per, 0, 0)),
        out_shape=jax.ShapeDtypeStruct((3, n_batch, per, seq, LANE), F32),
        scratch_shapes=scratch,
        compiler_params=_cparams(2),
        name="hyena_short_conv",
    )(p, w, b.reshape(1, -1))


def _hy_filter_kernel(emb_ref, w1_ref, b1_ref, w2_ref, b2_ref, w3_ref, b3_ref, w4_ref, fr_ref,
                      dl_ref, o_ref, nrm_ref, *, seq, group):
    i = pl.program_id(0)
    hp = lax.Precision.HIGHEST
    fr = fr_ref[...]
    hid = jnp.sin(fr * (jnp.dot(emb_ref[...], w1_ref[...], precision=hp) + b1_ref[...]))
    hid = jnp.sin(fr * (jnp.dot(hid, w2_ref[...], precision=hp) + b2_ref[...]))
    hid = jnp.sin(fr * (jnp.dot(hid, w3_ref[...], precision=hp) + b3_ref[...]))
    h = jnp.dot(hid, w4_ref[...], precision=hp)
    tl, n = h.shape
    r = lax.broadcasted_iota(jnp.int32, h.shape, 0) + i * tl
    col = lax.broadcasted_iota(jnp.int32, h.shape, 1)
    per = seq // group
    pos = (r >> (per.bit_length() - 1)) + group * (r & (per - 1))
    t = pos.astype(F32) * (1.0 / (seq - 1))
    h = h * (jnp.exp(-t * dl_ref[...]) + HY_SHIFT)
    h = jnp.where((pos == 0) & (col >= n // 2), 0.0, h)
    for k in range(n // LANE):
        o_ref[k] = h[:, k * LANE:(k + 1) * LANE]

    @pl.when(i == 0)
    def _():
        nrm_ref[...] = jnp.zeros_like(nrm_ref)

    nrm_ref[...] += jnp.sum(jnp.abs(h), axis=0, keepdims=True)


def _hyena_filter_taps(seq, w1, b1, w2, b2, w3, b3, w4, freq, group):
    fh = w1.shape[1]
    n_emb = w1.shape[0]
    bands_n = (n_emb - 1) // 2
    tt = np.linspace(0.0, 1.0, seq, dtype=np.float32)[:, None].astype(np.float64)
    ww = (2.0 * math.pi / seq) * np.arange(seq, dtype=np.float64)[:, None]
    bands = np.linspace(1e-4, bands_n - 1, bands_n, dtype=np.float32)[None, :].astype(np.float64)
    emb = np.concatenate([tt, np.cos(bands * ww), -np.sin(bands * ww)], axis=-1)
    emb = np.pad(emb, ((0, 0), (0, LANE - n_emb))).astype(np.float32)
    assert seq % group == 0 and group & (group - 1) == 0 and (seq // group) & (seq // group - 1) == 0
    r = np.arange(seq)
    pos = r // (seq // group) + group * (r % (seq // group))
    emb = emb[pos]
    w1p = jnp.pad(w1, ((0, LANE - n_emb), (0, 0)))
    deltas = np.abs(np.linspace(HY_MIN_DECAY, HY_MAX_DECAY, HY_W, dtype=np.float32))
    deltas = np.tile(deltas, 4)[None, :]
    n = w4.shape[1]
    tl = min(seq, 512)
    full = lambda a: pl.BlockSpec(a.shape, lambda i: (0,) * a.ndim)
    args = (w1p, b1.reshape(1, fh), w2, b2.reshape(1, fh), w3, b3.reshape(1, fh), w4, freq.reshape(1, fh),
            jnp.asarray(deltas))
    return pl.pallas_call(
        functools.partial(_hy_filter_kernel, seq=seq, group=group),
        grid=(seq // tl,),
        in_specs=[pl.BlockSpec((tl, LANE), lambda i: (i, 0))] + [full(a) for a in args],
        out_specs=[pl.BlockSpec((n // LANE, tl, LANE), lambda i: (0, i, 0)),
                   pl.BlockSpec((1, n), lambda i: (0, 0))],
        out_shape=[jax.ShapeDtypeStruct((n // LANE, seq, LANE), F32), jax.ShapeDtypeStruct((1, n), F32)],
        compiler_params=_cparams(1),
        name="hyena_filter_taps",
    )(jnp.asarray(emb), *args)


def _fft_split(seq):
    n = 2 * seq
    n1 = FFT_N1 if n > 1024 else 1
    return n, n1, n // n1


def _cis(idx, n):
    ph = 2.0 * np.pi * (idx % n) / n
    return np.cos(ph), -np.sin(ph)


def _dft_tables_short(seq):
    n = 2 * seq
    cr, ci = _cis(np.arange(n)[:, None] * np.arange(seq)[None, :], n)
    w_fwd = np.concatenate([cr, ci], axis=0)
    w_inv = np.concatenate([cr.T, ci.T], axis=1) / n
    return dict(w_fwd=jnp.asarray(w_fwd, BF16), w_inv=jnp.asarray(w_inv, BF16))


def _dft_tables_long(seq):
    n, n1, n2 = _fft_split(seq)
    t1 = np.arange(n1)[:, None, None]
    f2 = np.arange(n2)[None, :, None]
    t2 = np.arange(n2 // 2)[None, None, :]
    cr, ci = _cis(f2 * (t1 + n1 * t2), n)
    w1 = np.concatenate([cr, ci], axis=1)
    w4 = np.concatenate([np.swapaxes(cr, 1, 2), np.swapaxes(ci, 1, 2)], axis=2) / n
    j = (n1 - t1) % n1 + n1 * t2
    br, bi = _cis(f2 * (n - j), n)
    w1f = np.concatenate([np.concatenate([cr, br], axis=2), np.concatenate([ci, bi], axis=2)], axis=1)
    gr, gi = _cis(np.arange(n1)[:, None] * np.arange(n1)[None, :], n1)
    wd = np.concatenate([np.concatenate([gr, -gi], axis=1), np.concatenate([gi, gr], axis=1)], axis=0)
    return dict(w1=jnp.asarray(np.swapaxes(w1, 1, 2), BF16), w4=jnp.asarray(w4, BF16),
                w1f=jnp.asarray(np.swapaxes(w1f, 1, 2), BF16),
                wd=jnp.asarray(wd, BF16), wdi=jnp.asarray(wd.T, BF16))


def _dft_rows_kernel(w_ref, x_ref, o_ref):
    o_ref[0] = _dot(w_ref[...], x_ref[0].astype(BF16)).astype(o_ref.dtype)


def _dft_rows(w, x, tn):
    nb, k, cols = x.shape
    m = w.shape[0]
    return pl.pallas_call(
        _dft_rows_kernel,
        grid=(nb, cols // tn),
        in_specs=[pl.BlockSpec((m, k), lambda b, j: (0, 0)),
                  pl.BlockSpec((1, k, tn), lambda b, j: (b, 0, j))],
        out_specs=pl.BlockSpec((1, m, tn), lambda b, j: (b, 0, j)),
        out_shape=jax.ShapeDtypeStruct((nb, m, cols), BF16),
        compiler_params=_cparams(2),
        name="hyena_dft_rows",
    )(w, x)


def _idft_gate_kernel(w_ref, b_ref, xg_ref, z_ref, sk_ref, o_ref):
    y = _dot(w_ref[...], b_ref[0])
    z = z_ref[0]
    o_ref[0] = xg_ref[0] * (y + z * sk_ref[...])


def _idft_gate(w, bc, xg, z, skip_t, tn):
    nb, k, cols = bc.shape
    m = w.shape[0]
    return pl.pallas_call(
        _idft_gate_kernel,
        grid=(nb, cols // tn),
        in_specs=[pl.BlockSpec((m, k), lambda b, j: (0, 0)),
                  pl.BlockSpec((1, k, tn), lambda b, j: (b, 0, j)),
                  pl.BlockSpec((1, m, tn), lambda b, j: (b, 0, j)),
                  pl.BlockSpec((1, m, tn), lambda b, j: (b, 0, j)),
                  pl.BlockSpec((1, tn), lambda b, j: (0, j))],
        out_specs=pl.BlockSpec((1, m, tn), lambda b, j: (b, 0, j)),
        out_shape=jax.ShapeDtypeStruct((nb, m, cols), F32),
        compiler_params=_cparams(2),
        name="hyena_idft_gate",
    )(w, bc, xg, z, skip_t)


def _spec_combine_kernel(af_ref, ab_ref, nf_ref, nb_ref, o_ref):
    inv = 1.0 / (nf_ref[...] + nb_ref[...])
    n = o_ref.shape[1]
    o_ref[0] = (af_ref[0, :n].astype(F32) + ab_ref[0, :n].astype(F32)) * inv
    o_ref[1] = (af_ref[0, n:].astype(F32) - ab_ref[0, n:].astype(F32)) * inv


def _spec_mul_kernel(a_ref, k_ref, o_ref):
    n = k_ref.shape[1]
    xr, xi = a_ref[0, :n].astype(F32), a_ref[0, n:].astype(F32)
    kr, ki = k_ref[0], k_ref[1]
    o_ref[0, :n] = (xr * kr - xi * ki).astype(o_ref.dtype)
    o_ref[0, n:] = (xr * ki + xi * kr).astype(o_ref.dtype)


def _hyena_short(seq, taps, nrm, uc, skip):
    n = 2 * seq
    uc = jnp.swapaxes(uc, 2, 3).reshape(3, uc.shape[1], seq, HY_W)
    tabs = _dft_tables_short(seq)
    cf = taps.shape[1]
    c = HY_W
    a = _dft_rows(tabs["w_fwd"], taps.reshape(1, seq, cf), tn=cf)
    nblk = cf // 2 // c
    spec = pl.pallas_call(
        _spec_combine_kernel,
        grid=(nblk,),
        in_specs=[pl.BlockSpec((1, 2 * n, c), lambda j: (0, 0, j)),
                  pl.BlockSpec((1, 2 * n, c), lambda j: (0, 0, j + nblk)),
                  pl.BlockSpec((1, c), lambda j: (0, j)),
                  pl.BlockSpec((1, c), lambda j: (0, j + nblk))],
        out_specs=pl.BlockSpec((2, n, c), lambda j: (0, 0, j)),
        out_shape=jax.ShapeDtypeStruct((2, n, cf // 2), F32),
        compiler_params=_cparams(1),
        name="hyena_spec_combine",
    )(a, a, nrm, nrm)
    nb = uc.shape[1]
    z = uc[2]
    for order in range(2):
        a = _dft_rows(tabs["w_fwd"], z, tn=c)
        bc = pl.pallas_call(
            _spec_mul_kernel,
            grid=(nb,),
            in_specs=[pl.BlockSpec((1, 2 * n, c), lambda b: (b, 0, 0)),
                      pl.BlockSpec((2, n, c), lambda b: (0, 0, order))],
            out_specs=pl.BlockSpec((1, 2 * n, c), lambda b: (b, 0, 0)),
            out_shape=jax.ShapeDtypeStruct((nb, 2 * n, c), BF16),
            compiler_params=_cparams(1),
            name="hyena_spec_mul",
        )(a, spec)
        z = _idft_gate(tabs["w_inv"], bc, uc[order], z, skip[order].reshape(1, c), tn=c)
    return z


HY_SLABS = 2


def _slab_pitch(n2):
    return n2 + 8


def _pack_c(re, im):
    hi = lax.bitcast_convert_type(re.astype(BF16).astype(F32), jnp.uint32)
    lo = lax.bitcast_convert_type(im.astype(BF16).astype(F32), jnp.uint32)
    return hi | (lo >> 16)


def _unpack_c(w):
    re = lax.bitcast_convert_type(w & jnp.uint32(0xFFFF0000), F32)
    im = lax.bitcast_convert_type(w << 16, F32)
    return re.astype(BF16), im.astype(BF16)


def _store_slab(a_scr, row, words):
    for s in range(HY_SLABS):
        a_scr[s, pl.ds(row, words.shape[0]), :] = words[:, s * LANE:(s + 1) * LANE]


def _stage1(a_scr, w_ref, xs_of, i, tb, n2, pitch):
    for j in range(tb):
        a = _dot_tn(w_ref[j], xs_of(j))
        _store_slab(a_scr, pl.multiple_of((i * tb + j) * pitch, 8), _pack_c(a[:n2], a[n2:]))


def _stage2(a_scr, wd_ref, f2, n1, pitch):
    w = jnp.concatenate([a_scr[s, pl.ds(f2, n1, stride=pitch), :] for s in range(HY_SLABS)], axis=1)
    re, im = _unpack_c(w)
    return _dot(wd_ref[...], jnp.concatenate([re, im], axis=0))


def _time_col(ref, j, half):
    lead = (0,) * (len(ref.shape) - 3)
    return jnp.concatenate([ref[lead + (s, slice(j * half, (j + 1) * half), slice(None))]
                            for s in range(HY_SLABS)], axis=1)


def _hy_spectrum_kernel(hf_ref, hb_ref, hb0_ref, nf_ref, nb_ref, w1f_ref, wd_ref, o_ref, a_scr,
                        *, n1, n2, tb, fb):
    i = pl.program_id(1)
    pitch = _slab_pitch(n2)
    half = n2 // 2
    nt = n1 // tb

    @pl.when(i < nt)
    def _():
        def xs_of(j):
            fut = _time_col(hb0_ref, 0, half) if j == 0 else _time_col(hb_ref, tb - j, half)
            return jnp.concatenate([_time_col(hf_ref, j, half), fut], axis=0).astype(BF16)
        _stage1(a_scr, w1f_ref, xs_of, i, tb, n2, pitch)

    @pl.when(i >= nt)
    def _():
        inv = 1.0 / (nf_ref[...] + nb_ref[...])
        for jj in range(fb):
            x = _stage2(a_scr, wd_ref, (i - nt) * fb + jj, n1, pitch)
            o_ref[0, 0, jj] = (x[:n1] * inv).astype(o_ref.dtype)
            o_ref[0, 1, jj] = (x[n1:] * inv).astype(o_ref.dtype)


def _hy_fftconv_kernel(z_ref, xg_ref, k_ref, sk_ref, w1_ref, wd_ref, wdi_ref, w4_ref, o_ref, a_scr,
                       *, n1, n2, tb, fb):
    i = pl.program_id(2)
    pitch = _slab_pitch(n2)
    half = n2 // 2
    nt, nf = n1 // tb, n2 // fb

    @pl.when(i < nt)
    def _():
        _stage1(a_scr, w1_ref, lambda j: _time_col(z_ref, j, half).astype(BF16), i, tb, n2, pitch)

    @pl.when((i >= nt) & (i < nt + nf))
    def _():
        for jj in range(fb):
            f2 = (i - nt) * fb + jj
            x = _stage2(a_scr, wd_ref, f2, n1, pitch)
            xr, xi = x[:n1], x[n1:]
            kr, ki = k_ref[0, 0, jj].astype(F32), k_ref[0, 1, jj].astype(F32)
            y = jnp.concatenate([xr * kr - xi * ki, xr * ki + xi * kr], axis=0).astype(BF16)
            bv = _dot(wdi_ref[...], y)
            words = _pack_c(bv[:n1], bv[n1:])
            for s in range(HY_SLABS):
                a_scr[s, pl.ds(f2, n1, stride=pitch), :] = words[:, s * LANE:(s + 1) * LANE]

    @pl.when(i >= nt + nf)
    def _():
        sk = sk_ref[...]
        for j in range(tb):
            row = pl.multiple_of(((i - nt - nf) * tb + j) * pitch, 8)
            w = jnp.concatenate([a_scr[s, pl.ds(row, n2), :] for s in range(HY_SLABS)], axis=1)
            re, im = _unpack_c(w)
            y = _dot(w4_ref[j], jnp.concatenate([re, im], axis=0))
            out = _time_col(xg_ref, j, half) * (y + _time_col(z_ref, j, half) * sk)
            for s in range(HY_SLABS):
                o_ref[0, s, j * half:(j + 1) * half, :] = out[:, s * LANE:(s + 1) * LANE]


def _hyena_long(seq, taps, nrm, uc, skip, tb=32, fb=32):
    n, n1, n2 = _fft_split(seq)
    tabs = _dft_tables_long(seq)
    pitch = _slab_pitch(n2)
    half = n2 // 2
    cb = HY_SLABS * LANE
    tb, fb = min(tb, n1), min(fb, n2)
    nt, nf = n1 // tb, n2 // fb
    cf = taps.shape[0] * LANE
    ngrp = cf // 2 // cb
    scratch = [pltpu.VMEM((HY_SLABS, n1 * pitch, LANE), jnp.uint32)]
    tcol = lambda i: jnp.minimum(i, nt - 1)
    spec = pl.pallas_call(
        functools.partial(_hy_spectrum_kernel, n1=n1, n2=n2, tb=tb, fb=fb),
        grid=(ngrp, nt + nf),
        in_specs=[pl.BlockSpec((HY_SLABS, tb * half, LANE), lambda g, i: (g, tcol(i), 0)),
                  pl.BlockSpec((HY_SLABS, tb * half, LANE), lambda g, i: (ngrp + g, nt - 1 - tcol(i), 0)),
                  pl.BlockSpec((HY_SLABS, tb * half, LANE), lambda g, i: (ngrp + g, (nt - tcol(i)) % nt, 0)),
                  pl.BlockSpec((1, cb), lambda g, i: (0, g)),
                  pl.BlockSpec((1, cb), lambda g, i: (0, ngrp + g)),
                  pl.BlockSpec((tb, n2, 2 * n2), lambda g, i: (tcol(i), 0, 0)),
                  pl.BlockSpec(tabs["wd"].shape, lambda g, i: (0, 0))],
        out_specs=pl.BlockSpec((1, 2, fb, n1, cb), lambda g, i: (g, 0, jnp.maximum(i - nt, 0), 0, 0)),
        out_shape=jax.ShapeDtypeStruct((ngrp, 2, n2, n1, cb), BF16),
        scratch_shapes=scratch,
        compiler_params=_cparams(2),
        name="hyena_filter_spectrum",
    )(taps, taps, taps, nrm, nrm, tabs["w1f"], tabs["wd"])
    nb, nblk = uc.shape[1], uc.shape[2]
    ngc = nblk // HY_SLABS
    z, zsel = uc, 2

    def tblk(b, g, i):
        return (b, g, jnp.where(i < nt, i, jnp.maximum(i - nt - nf, 0)), 0)

    def tblk_late(b, g, i):
        return (b, g, jnp.maximum(i - nt - nf, 0), 0)

    for order in range(2):
        stacked = lambda sel, f: (lambda b, g, i: (sel,) + f(b, g, i))
        z = pl.pallas_call(
            functools.partial(_hy_fftconv_kernel, n1=n1, n2=n2, tb=tb, fb=fb),
            grid=(nb, ngc, 2 * nt + nf),
            in_specs=[pl.BlockSpec((1, 1, HY_SLABS, tb * half, LANE), stacked(zsel, tblk)),
                      pl.BlockSpec((1, 1, HY_SLABS, tb * half, LANE), stacked(order, tblk_late)),
                      pl.BlockSpec((1, 2, fb, n1, cb),
                                   lambda b, g, i: (order * ngc + g, 0, jnp.clip(i - nt, 0, nf - 1), 0, 0)),
                      pl.BlockSpec((1, cb), lambda b, g, i: (0, g)),
                      pl.BlockSpec((tb, half, 2 * n2), lambda b, g, i: (tcol(i), 0, 0)),
                      pl.BlockSpec(tabs["wd"].shape, lambda b, g, i: (0, 0)),
                      pl.BlockSpec(tabs["wdi"].shape, lambda b, g, i: (0, 0)),
                      pl.BlockSpec((tb, half, 2 * n2), lambda b, g, i: (jnp.maximum(i - nt - nf, 0), 0, 0))],
            out_specs=pl.BlockSpec((1, HY_SLABS, tb * half, LANE), tblk_late),
            out_shape=jax.ShapeDtypeStruct((nb, nblk, seq, LANE), F32),
            scratch_shapes=scratch,
            compiler_params=_cparams(3),
            name="hyena_fft_conv",
        )(z, uc, spec, skip[order].reshape(1, HY_W), tabs["w1"], tabs["wd"], tabs["wdi"], tabs["w4"])
        z, zsel = z.reshape((1,) + z.shape), 0
    return z[0]


def _hyena(p, lw, row0, seq, n_batch):
    long = _fft_split(seq)[1] > 1
    n1 = FFT_N1 if long else 1
    taps, nrm = _hyena_filter_taps(seq, lw["hy_w1"], lw["hy_b1"], lw["hy_w2"], lw["hy_b2"], lw["hy_w3"],
                                   lw["hy_b3"], lw["hy_w4"], lw["hy_freq"], n1)
    uc = _short_conv(p, lw["hy_conv_w"], lw["hy_conv_b"], row0, seq, n_batch, n1)
    if long:
        return _hyena_long(seq, taps, nrm, uc, lw["hy_skip"])
    taps = jnp.swapaxes(taps, 0, 1).reshape(seq, -1)
    return _hyena_short(seq, taps, nrm, uc, lw["hy_skip"]).reshape(n_batch * seq, HY_W)


def _mla_qkv_kernel(qa_ref, kva_ref, kr_ref, cs_ref, wq_ref, wkv_ref, gqa_ref, gkva_ref, gqn_ref, gqr_ref,
                    gkn_ref, gkr_ref, q_ref, k_ref, v_ref):
    cs = cs_ref[...]
    lane = lax.broadcasted_iota(jnp.int32, cs.shape, 1)
    low = lane < MLA_ROPE

    def rope(pair, gain2):
        ms = jnp.sum(jnp.where(low, pair * pair, 0.0), axis=-1, keepdims=True) * (1.0 / MLA_ROPE)
        t = pair * lax.rsqrt(ms + RMS_EPS) * gain2 * cs
        return jnp.where(low, t + pltpu.roll(t, MLA_ROPE, 1), 0.0)

    qq = _dot(_rms(qa_ref[...], gqa_ref[...]).astype(BF16), wq_ref[...])
    kv = _dot(_rms(kva_ref[...], gkva_ref[...]).astype(BF16), wkv_ref[...])
    kr = rope(kr_ref[...], gkr_ref[...])
    hp = MLA_HEAD_PAD
    tm = kr.shape[0]
    qscale = MLA_SCALE * LOG2_E
    ones_row = (lax.broadcasted_iota(jnp.int32, (V_ROWS - MLA_V, tm), 0) == 0).astype(BF16)
    for h in range(MLA_HEADS):
        qn = _rms(qq[:, h * hp:h * hp + MLA_NOPE], gqn_ref[...])
        qr = rope(qq[:, h * hp + MLA_NOPE:(h + 1) * hp], gqr_ref[...])
        q_ref[h, 0:LANE, :] = (qn * qscale).T.astype(BF16)
        q_ref[h, LANE:2 * LANE, :] = (qr * qscale).T.astype(BF16)
        kn = _rms(kv[:, h * hp:h * hp + MLA_NOPE], gkn_ref[...])
        k_ref[h, :, 0:LANE] = kn.astype(BF16)
        k_ref[h, :, LANE:2 * LANE] = kr.astype(BF16)
        v_ref[h, 0:MLA_V, :] = kv[:, h * hp + MLA_NOPE:(h + 1) * hp].T.astype(BF16)
        v_ref[h, MLA_V:V_ROWS, :] = ones_row


def _rope_table(seq, n_batch, ctx_rows):
    rows = seq // GRID_W
    row = np.repeat(np.arange(rows, dtype=np.float32), GRID_W)
    col = np.tile(np.arange(GRID_W, dtype=np.float32), rows)
    half = MLA_ROPE // 2
    inv = (ROPE_THETA ** (-np.arange(0, half, 2, dtype=np.float32) / half)).astype(np.float32)
    ar = (row[:, None] * inv).astype(np.float64)
    ac = (col[:, None] * inv).astype(np.float64)
    cos = np.concatenate([np.cos(ar), np.cos(ar), np.cos(ac), np.cos(ac)], axis=1)
    sin = np.concatenate([-np.sin(ar), np.sin(ar), -np.sin(ac), np.sin(ac)], axis=1)
    lat = np.tile(np.concatenate([cos, sin], axis=1), (n_batch, 1))
    ctx = np.concatenate([np.ones((ctx_rows, MLA_ROPE)), np.zeros((ctx_rows, MLA_ROPE))], axis=1)
    return jnp.asarray(np.concatenate([lat, ctx], axis=0), F32)


def _pair_gain(g):
    return jnp.concatenate([g, _rope_swap(g)]).reshape(1, 2 * MLA_ROPE)


def _mla_qkv(p, cs, lw, tm=512):
    n_rows = p.shape[0]
    hd, hp = MLA_HEADS, MLA_HEAD_PAD
    w_uq = lw["w_uq"].reshape(-1, hd, MLA_QK)
    wq = jnp.concatenate([w_uq, _rope_swap(w_uq[..., MLA_NOPE:])], axis=-1).reshape(-1, hd * hp).astype(BF16)
    wkv = lw["w_ukv"].astype(BF16)
    ql, kvl = wq.shape[0], wkv.shape[0]
    vec = lambda a: a.reshape(1, -1)
    full = lambda a: pl.BlockSpec(a.shape, lambda i: (0,) * a.ndim)
    args = (wq, wkv, vec(lw["q_a_norm"]), vec(lw["kv_a_norm"]), vec(lw["q_nope_norm"]),
            _pair_gain(lw["q_rope_norm"]), vec(lw["k_nope_norm"]), _pair_gain(lw["k_rope_norm"]))
    return pl.pallas_call(
        _mla_qkv_kernel,
        grid=(n_rows // tm,),
        in_specs=[pl.BlockSpec((tm, ql), lambda i: (i, COL_QA // ql)),
                  pl.BlockSpec((tm, kvl), lambda i: (i, COL_KVA // kvl)),
                  pl.BlockSpec((tm, LANE), lambda i: (i, COL_KR // LANE)),
                  pl.BlockSpec((tm, LANE), lambda i: (i, 0))] + [full(a) for a in args],
        out_specs=[pl.BlockSpec((hd, hp, tm), lambda i: (0, 0, i)),
                   pl.BlockSpec((hd, tm, hp), lambda i: (0, i, 0)),
                   pl.BlockSpec((hd, V_ROWS, tm), lambda i: (0, 0, i))],
        out_shape=[jax.ShapeDtypeStruct((hd, hp, n_rows), BF16),
                   jax.ShapeDtypeStruct((hd, n_rows, hp), BF16),
                   jax.ShapeDtypeStruct((hd, V_ROWS, n_rows), BF16)],
        compiler_params=_cparams(1),
        name="mla_qkv",
    )(p, p, p, cs, *args)


def _key_chunks(k_refs, v_refs, tk):
    chunks, s0 = [], 0
    for kr, vr in zip(k_refs, v_refs):
        n = kr.shape[1]
        step = min(tk, n)
        chunks += [(kr, vr, r0, step, s0 + r0) for r0 in range(0, n, step)]
        s0 += n
    return chunks


def _score_pass(qt, chunks, s_scr):
    m = None
    for kr, _, r0, rn, s0 in chunks:
        s = _dot(kr[0, r0:r0 + rn, :], qt)
        s_scr[s0:s0 + rn, :] = s
        mj = jnp.max(s, axis=0, keepdims=True)
        m = mj if m is None else jnp.maximum(m, mj)
    return m


def _value_pass(chunks, s_scr, m, o_ref):
    acc = None
    for _, vr, r0, rn, s0 in chunks:
        p = jnp.exp2((s_scr[s0:s0 + rn, :] - m).astype(BF16))
        part = _dot(vr[0, :, r0:r0 + rn], p)
        acc = part if acc is None else acc + part
    o_ref[...] = (acc[:MLA_V] / acc[MLA_V:MLA_V + 1]).astype(o_ref.dtype)


def _attn_ctx_kernel(q_ref, kc_ref, vc_ref, o_ref, s_scr, *, tk):
    chunks = _key_chunks([kc_ref], [vc_ref], tk)
    _value_pass(chunks, s_scr, _score_pass(q_ref[0], chunks, s_scr), o_ref)


def _attn_kernel(q_ref, k_ref, v_ref, kc_ref, vc_ref, o_ref, s_a, s_b, m_a, m_b, *, tk):
    i = pl.program_id(2)
    chunks = _key_chunks([kc_ref, k_ref], [vc_ref, v_ref], tk)

    @pl.when((pl.program_id(0) == 0) & (pl.program_id(1) == 0) & (i == 0))
    def _():
        s_b[...] = jnp.zeros_like(s_b)
        m_b[...] = jnp.zeros_like(m_b)

    def step(s_cur, m_cur, s_prev, m_prev):
        _value_pass(chunks, s_prev, m_prev[0:1, :], o_ref)
        m_cur[...] = jnp.broadcast_to(_score_pass(q_ref[0], chunks, s_cur), m_cur.shape)

    @pl.when(i % 2 == 0)
    def _():
        step(s_a, m_a, s_b, m_b)

    @pl.when(i % 2 == 1)
    def _():
        step(s_b, m_b, s_a, m_a)


def _attention(qt, k, vt, seq, ctx_len, n_batch, latent, tq=256, tk=1024):
    hd, hp, n_rows = qt.shape
    lat_rows = n_batch * seq
    cblk = lat_rows // ctx_len
    kc_spec = pl.BlockSpec((1, ctx_len, hp), lambda b, h, i: (h, cblk + b, 0))
    vc_spec = pl.BlockSpec((1, V_ROWS, ctx_len), lambda b, h, i: (h, 0, cblk + b))
    if latent:
        nq = seq // tq
        n_pairs = hd * nq
        heads, steps = 1, n_pairs + 1
        cur = lambda s: jnp.minimum(s, n_pairs - 1)
        prv = lambda s: jnp.maximum(s - 1, 0)
        in_specs = [pl.BlockSpec((1, hp, tq), lambda b, _, s: (cur(s) // nq, 0, b * nq + cur(s) % nq)),
                    pl.BlockSpec((1, seq, hp), lambda b, _, s: (cur(s) // nq, b, 0)),
                    pl.BlockSpec((1, V_ROWS, seq), lambda b, _, s: (prv(s) // nq, 0, b)),
                    pl.BlockSpec((1, ctx_len, hp), lambda b, _, s: (cur(s) // nq, cblk + b, 0)),
                    pl.BlockSpec((1, V_ROWS, ctx_len), lambda b, _, s: (prv(s) // nq, 0, cblk + b))]
        args = (qt, k, vt, k, vt)
        out_spec = pl.BlockSpec((MLA_V, tq), lambda b, _, s: (prv(s) // nq, b * nq + prv(s) % nq))
        out_cols, n_keys = lat_rows, seq + ctx_len
        body = functools.partial(_attn_kernel, tk=tk)
        scratch = [pltpu.VMEM((n_keys, tq), F32)] * 2 + [pltpu.VMEM((8, tq), F32)] * 2
    else:
        heads, steps = hd, 1
        tq = ctx_len
        in_specs = [pl.BlockSpec((1, hp, tq), lambda b, h, i: (h, 0, cblk + b)), kc_spec, vc_spec]
        args = (qt, k, vt)
        out_spec = pl.BlockSpec((MLA_V, tq), lambda b, h, i: (h, b))
        out_cols, n_keys = n_batch * ctx_len, ctx_len
        body = functools.partial(_attn_ctx_kernel, tk=tk)
        scratch = [pltpu.VMEM((n_keys, tq), F32)]
    return pl.pallas_call(
        body,
        grid=(n_batch, heads, steps),
        in_specs=in_specs,
        out_specs=out_spec,
        out_shape=jax.ShapeDtypeStruct((hd * MLA_V, out_cols), BF16),
        scratch_shapes=scratch,
        compiler_params=_cparams(3),
        name="mla_attention" if latent else "mla_attention_ctx",
    )(*args)


def _merge_kernel(*refs, lat_tiles):
    (x_ref, mod_ref, of_ref, ob_ref, g_ref, yb_ref, yc_ref, ga_ref, gb_ref, gc_ref,
     gain_ref, wa_ref, wb_ref, wc_ref, wo_ref) = refs[:15]
    ctx_refs = refs[15:-1]
    o_ref = refs[-1]
    is_ctx = pl.program_id(0) >= lat_tiles
    m = mod_ref[0]
    o = of_ref[...].astype(F32) + ob_ref[...].astype(F32)
    if ctx_refs:
        o = jnp.where(is_ctx, ctx_refs[0][...].astype(F32) + ctx_refs[1][...].astype(F32), o)
    gain = gain_ref[...]
    ya = jnp.concatenate([_rms(o[:, h * HG_D:(h + 1) * HG_D], gain) for h in range(HG_HEADS)], axis=1)
    ya = (ya * _silu(g_ref[...])).astype(BF16)
    nslab, tcols = yb_ref.shape[1], yb_ref.shape[3]
    qn = x_ref.shape[0] // yb_ref.shape[2]
    q0 = (pl.program_id(0) % (tcols // qn)) * qn
    yb = jnp.concatenate([jnp.concatenate([yb_ref[0, k, :, q0 + q, :] for k in range(nslab)], axis=1)
                          for q in range(qn)], axis=0)
    yc = yc_ref[...]
    if ctx_refs:
        yb = jnp.where(is_ctx, ctx_refs[2][...], yb)
        yc = jnp.where(is_ctx, ctx_refs[3][...], yc)
    mix = (_sigmoid(ga_ref[...]) * _dot(ya, wa_ref[...])
           + _sigmoid(gb_ref[...]) * _dot(yb.astype(BF16), wb_ref[...])
           + _sigmoid(gc_ref[...]) * _dot_tn(yc, wc_ref[...]))
    o_ref[...] = x_ref[...] + m[5:6] * _dot(mix.astype(BF16), wo_ref[...])


def _merge(xa, mod, o_f, o_b, p, pg, y_b, y_c, ctx_parts, lw, n_rows, seq, n_batch, tm=512):
    d = xa.shape[1]
    hk = HG_HEADS * HG_D
    tpb = seq // tm
    lat_tiles = n_batch * tpb
    n1 = FFT_N1
    tcols = 8
    assert tm % n1 == 0 and tcols % (tm // n1) == 0 and seq % (tcols * n1) == 0
    per_blk = tcols // (tm // n1)
    y_b = y_b.reshape(n_batch, y_b.shape[1], n1, seq // n1, LANE)
    row = lambda w: pl.BlockSpec((tm, w), lambda i: (i, 0))
    pcol = lambda w, c: pl.BlockSpec((tm, w), lambda i: (i, c // w))
    full = lambda a: pl.BlockSpec(a.shape, lambda i: (0,) * a.ndim)
    ws = (lw["hg_out_norm"].reshape(1, HG_D), lw["w_br_a"].astype(BF16), lw["w_br_b"].astype(BF16),
          lw["w_br_c"].astype(BF16), lw["w_out"].astype(BF16))
    yb_spec = pl.BlockSpec((1, y_b.shape[1], n1, tcols, LANE),
                           lambda i: (jnp.minimum(i // tpb, n_batch - 1), 0, 0, (i % tpb) // per_blk, 0))
    lat = lambda i: jnp.minimum(i, lat_tiles - 1)
    late = lambda i: jnp.maximum(i - lat_tiles, 0)
    extra_specs, extra = [], []
    if ctx_parts is not None:
        extra_specs = [pl.BlockSpec((tm, hk), lambda i: (late(i), 0)),
                       pl.BlockSpec((tm, hk), lambda i: (late(i), 0)),
                       pl.BlockSpec((tm, HY_W), lambda i: (late(i), 0)),
                       pl.BlockSpec((MLA_HEADS * MLA_V, tm), lambda i: (0, late(i)))]
        extra = list(ctx_parts)
    return pl.pallas_call(
        functools.partial(_merge_kernel, lat_tiles=lat_tiles),
        grid=(n_rows // tm,),
        in_specs=[row(d), pl.BlockSpec((1, N_MOD, d), _group_map(tpb, n_batch)),
                  pl.BlockSpec((tm, hk), lambda i: (lat(i), 0)), pl.BlockSpec((tm, hk), lambda i: (lat(i), 0)),
                  pcol(hk, COL_G), yb_spec,
                  pl.BlockSpec((MLA_HEADS * MLA_V, tm), lambda i: (0, lat(i))),
                  pcol(d, COL_GA), pcol(d, COL_GB), pcol(d, COL_GC)] + [full(a) for a in ws] + extra_specs,
        out_specs=row(d),
        out_shape=jax.ShapeDtypeStruct((n_rows, d), F32),
        compiler_params=_cparams(1),
        name="merge",
    )(xa, mod, o_f, o_b, p, y_b, y_c, pg, pg, pg, *ws, *extra)


def kernel(x, c, ctx, c_ctx, ada_w, ada_b, ffn1_norm, ffn1_w13, ffn1_w2, mix_norm, w_in, hg_lb_logits, hg_out_norm, hy_conv_w, hy_conv_b, hy_w1, hy_b1, hy_w2, hy_b2, hy_w3, hy_b3, hy_w4, hy_freq, hy_skip, q_a_norm, w_uq, kv_a_norm, w_ukv, q_nope_norm, q_rope_norm, k_nope_norm, k_rope_norm, w_br_a, w_br_b, w_br_c, w_out, ffn2_norm, ffn2_w13, ffn2_w2):
    stacked = dict(
        ada_w=ada_w, ada_b=ada_b, ffn1_norm=ffn1_norm, ffn1_w13=ffn1_w13, ffn1_w2=ffn1_w2, mix_norm=mix_norm,
        w_in=w_in, hg_out_norm=hg_out_norm, hy_conv_w=hy_conv_w, hy_conv_b=hy_conv_b, hy_w1=hy_w1, hy_b1=hy_b1,
        hy_w2=hy_w2, hy_b2=hy_b2, hy_w3=hy_w3, hy_b3=hy_b3, hy_w4=hy_w4, hy_freq=hy_freq, hy_skip=hy_skip,
        q_a_norm=q_a_norm, w_uq=w_uq, kv_a_norm=kv_a_norm, w_ukv=w_ukv, q_nope_norm=q_nope_norm,
        q_rope_norm=q_rope_norm, k_nope_norm=k_nope_norm, k_rope_norm=k_rope_norm, w_br_a=w_br_a,
        w_br_b=w_br_b, w_br_c=w_br_c, w_out=w_out, ffn2_norm=ffn2_norm, ffn2_w13=ffn2_w13, ffn2_w2=ffn2_w2)
    n_batch, seq, d = x.shape
    ctx_len = ctx.shape[1]
    depth = ada_w.shape[0]
    lat_rows, ctx_rows = n_batch * seq, n_batch * ctx_len
    all_rows = lat_rows + ctx_rows
    assert seq % 512 == 0 and ctx_rows % 512 == 0 and seq % ctx_len == 0 and seq % GRID_W == 0
    assert ctx_len % HG_CHUNK == 0 and n_batch < 8

    xa, xc = x.reshape(lat_rows, d), ctx.reshape(ctx_rows, d)
    cs = jnp.concatenate([c, c_ctx.reshape(1, d), jnp.zeros((7 - n_batch, d), F32)], axis=0)
    rope_cs = _rope_table(seq, n_batch, ctx_rows)
    zero_state = jnp.zeros((2, n_batch, HG_HEADS, HG_D, HG_D), F32)
    hk = HG_HEADS * HG_D

    for l in range(depth):
        lw = {name: val[l] for name, val in stacked.items()}
        need_ctx = l < depth - 1
        mod = _modulation(cs, lw["ada_w"], lw["ada_b"])
        xa = _half_ffn(xa, mod, lw["ffn1_norm"], lw["ffn1_w13"], lw["ffn1_w2"], 0, all_rows, seq, n_batch, xc=xc)
        xc = None
        w_gates, w_rest = _pack_w_in(lw["w_in"].astype(BF16))
        p = _in_projection(xa, mod, lw["mix_norm"], w_rest, seq, n_batch, F32, tn=w_rest.shape[1] // 2)
        pg = _in_projection(xa, mod, lw["mix_norm"], w_gates, seq, n_batch, BF16, tn=w_gates.shape[1])

        ocf, ocb, s_c = _hgrn2_scan(p, hg_lb_logits, zero_state, l, lat_rows, ctx_len, n_batch)
        olf, olb, _ = _hgrn2_scan(p, hg_lb_logits, s_c, l, 0, seq, n_batch)
        y_b = _hyena(p, lw, 0, seq, n_batch)
        qt, k, vt = _mla_qkv(p, rope_cs, lw)
        y_c = _attention(qt, k, vt, seq, ctx_len, n_batch, latent=True)

        ctx_parts = None
        if need_ctx:
            ctx_parts = (ocf.reshape(ctx_rows, hk), ocb.reshape(ctx_rows, hk),
                         _hyena(p, lw, lat_rows, ctx_len, n_batch),
                         _attention(qt, k, vt, seq, ctx_len, n_batch, latent=False))
        mix_rows = all_rows if need_ctx else lat_rows
        xa = _merge(xa, mod, olf.reshape(lat_rows, hk), olb.reshape(lat_rows, hk), p, pg, y_b, y_c, ctx_parts,
                    lw, mix_rows, seq, n_batch)
        xa = _half_ffn(xa, mod, lw["ffn2_norm"], lw["ffn2_w13"], lw["ffn2_w2"], 6, mix_rows, seq, n_batch)
    return xa[:lat_rows].reshape(n_batch, seq, d)
```

```python
import functools
import math

import numpy as np
import jax
import jax.numpy as jnp
from jax import lax
from jax.experimental import pallas as pl
from jax.experimental.pallas import tpu as pltpu

F32 = jnp.float32
BF16 = jnp.bfloat16

RMS_EPS = 1e-6
N_MOD = 9
GRID_W = 64
ROPE_THETA = 10000.0
HG_HEADS = 4
HG_D = 128
HG_CHUNK = 128
HY_W = 512
HY_TARGET = 1e-2
HY_MIN_DECAY = math.log(HY_TARGET) / 1.5
HY_MAX_DECAY = math.log(HY_TARGET) / 0.3
HY_SHIFT = 0.05
MLA_HEADS = 4
MLA_NOPE = 128
MLA_ROPE = 64
MLA_V = 128
MLA_QK = MLA_NOPE + MLA_ROPE
MLA_SCALE = MLA_QK ** -0.5
MLA_HEAD_PAD = 256
V_ROWS = MLA_V + 16
LOG2_E = math.log2(math.e)
FFT_N1 = 128
LANE = 128

VMEM_LIMIT = 52 * 1024 * 1024

COL_Q, COL_ZF, COL_ZB, COL_IV, COL_G = 0, 512, 1024, 1536, 2048
COL_HY = 2560
COL_QA, COL_KVA, COL_KR = 4096, 4352, 4480
COL_GA, COL_GB, COL_GC = 0, 1024, 2048


def _cparams(n_axes):
    return pltpu.CompilerParams(dimension_semantics=("arbitrary",) * n_axes,
                                vmem_limit_bytes=VMEM_LIMIT)


def _dot(a, b):
    return jnp.dot(a, b, preferred_element_type=F32)


def _dot_nt(a, b):
    return lax.dot_general(a, b, (((1,), (1,)), ((), ())), preferred_element_type=F32)


def _dot_tn(a, b):
    return lax.dot_general(a, b, (((0,), (0,)), ((), ())), preferred_element_type=F32)


def _sigmoid(x):
    return 1.0 / (1.0 + jnp.exp(-x))


def _silu(x):
    return x * _sigmoid(x)


def _rms(x, gain):
    return x * lax.rsqrt(jnp.mean(x * x, axis=-1, keepdims=True) + RMS_EPS) * gain


def _mod_kernel(c_ref, w_ref, b_ref, o_ref):
    o_ref[...] = _dot(_silu(c_ref[...]).astype(BF16), w_ref[...]) + b_ref[...]


def _modulation(cs, ada_w, ada_b):
    g, d = cs.shape
    n = ada_w.shape[1]
    tn = n // 4
    out = pl.pallas_call(
        _mod_kernel,
        grid=(n // tn,),
        in_specs=[pl.BlockSpec((g, d), lambda j: (0, 0)),
                  pl.BlockSpec((d, tn), lambda j: (0, j)),
                  pl.BlockSpec((1, tn), lambda j: (0, j))],
        out_specs=pl.BlockSpec((g, tn), lambda j: (0, j)),
        out_shape=jax.ShapeDtypeStruct((g, n), F32),
        compiler_params=_cparams(1),
        name="modulation",
    )(cs, ada_w.astype(BF16), ada_b.reshape(1, n))
    return out.reshape(g, N_MOD, d)


def _ffn_kernel(*refs, idx, ff, ck, lat_tiles):
    x_ref, mod_ref, g_ref, w13_ref, w2_ref = refs[:5]
    o_ref = refs[-1]
    x = x_ref[...]
    if len(refs) == 7:
        x = jnp.where(pl.program_id(0) >= lat_tiles, refs[5][...], x)
    m = mod_ref[0]
    h = (_rms(x, g_ref[...]) * (1.0 + m[idx + 1:idx + 2]) + m[idx:idx + 1]).astype(BF16)
    acc = jnp.zeros(x.shape, F32)
    for c0 in range(0, ff, ck):
        c1 = min(c0 + ck, ff)
        a = _dot(h, w13_ref[:, c0:c1])
        b = _dot(h, w13_ref[:, ff + c0:ff + c1])
        acc = acc + _dot((_silu(a) * b).astype(BF16), w2_ref[c0:c1, :])
    o_ref[...] = x + (0.5 * m[idx + 2:idx + 3]) * acc


def _group_map(tiles_per_batch, n_batch):
    return lambda i: (jnp.minimum(i // tiles_per_batch, n_batch), 0, 0)


def _half_ffn(xa, mod, gain, w13, w2, idx, n_rows, seq, n_batch, xc=None, tm=512, ck=512):
    d = xa.shape[1]
    ff = w2.shape[0]
    lat_tiles = n_batch * seq // tm
    const = dict(pipeline_mode=pl.Buffered(1))
    x_specs, xs = [pl.BlockSpec((tm, d), lambda i: (jnp.minimum(i, xa.shape[0] // tm - 1), 0))], [xa]
    tail_specs, tail = [], []
    if xc is not None:
        tail_specs, tail = [pl.BlockSpec((tm, d), lambda i: (jnp.maximum(i - lat_tiles, 0), 0))], [xc]
    return pl.pallas_call(
        functools.partial(_ffn_kernel, idx=idx, ff=ff, ck=ck, lat_tiles=lat_tiles),
        grid=(n_rows // tm,),
        in_specs=x_specs + [pl.BlockSpec((1, N_MOD, d), _group_map(seq // tm, n_batch)),
                            pl.BlockSpec((1, d), lambda i: (0, 0)),
                            pl.BlockSpec((d, 2 * ff), lambda i: (0, 0), **const),
                            pl.BlockSpec((ff, d), lambda i: (0, 0), **const)] + tail_specs,
        out_specs=pl.BlockSpec((tm, d), lambda i: (i, 0)),
        out_shape=jax.ShapeDtypeStruct((n_rows, d), F32),
        compiler_params=_cparams(1),
        name="half_ffn",
    )(*xs, mod, gain.reshape(1, d), w13.astype(BF16), w2.astype(BF16), *tail)


def _inproj_kernel(x_ref, mod_ref, g_ref, w_ref, o_ref):
    m = mod_ref[0]
    h = (_rms(x_ref[...], g_ref[...]) * (1.0 + m[4:5]) + m[3:4]).astype(BF16)
    o_ref[...] = _dot(h, w_ref[...]).astype(o_ref.dtype)


def _pack_w_in(w_in):
    d = w_in.shape[0]
    hk = HG_HEADS * HG_D
    sizes = (hk, hk, hk, hk, hk, 3 * HY_W, 256, 128, MLA_ROPE, d, d, d)
    offs = np.cumsum((0,) + sizes)
    q, zf, zb, iv, g, hy, qa, kva, kr, ga, gb, gc = (w_in[:, offs[i]:offs[i + 1]] for i in range(12))
    return (jnp.concatenate([ga, gb, gc], axis=1),
            jnp.concatenate([q, zf, zb, iv, g, hy, qa, kva, kr, _rope_swap(kr)], axis=1))


def _rope_swap(a):
    q = MLA_ROPE // 4
    return jnp.concatenate([a[..., q:2 * q], a[..., :q], a[..., 3 * q:], a[..., 2 * q:3 * q]], axis=-1)


def _in_projection(xa, mod, gain, w_packed, seq, n_batch, out_dtype, tm=512, tn=1536):
    n_rows, d = xa.shape
    n = w_packed.shape[1]
    assert n % tn == 0
    return pl.pallas_call(
        _inproj_kernel,
        grid=(n // tn, n_rows // tm),
        in_specs=[pl.BlockSpec((tm, d), lambda j, i: (i, 0)),
                  pl.BlockSpec((1, N_MOD, d), lambda j, i: (jnp.minimum(i // (seq // tm), n_batch), 0, 0)),
                  pl.BlockSpec((1, d), lambda j, i: (0, 0)),
                  pl.BlockSpec((d, tn), lambda j, i: (0, j))],
        out_specs=pl.BlockSpec((tm, tn), lambda j, i: (i, j)),
        out_shape=jax.ShapeDtypeStruct((n_rows, n), out_dtype),
        compiler_params=_cparams(2),
        name="in_projection",
    )(xa, mod, gain.reshape(1, d), w_packed)


def _hgrn2_tables(c, rev):
    t = np.arange(c)[:, None]
    u = np.arange(c)[None, :]
    mats = [(u <= t), (u > t)]
    masks = [(t == u)]
    h = c // 2
    while h >= 1:
        mid = (t // (2 * h)) * (2 * h) + h
        mats.append(np.where(t >= mid, (u >= mid) & (u <= t), (u >= t + 1) & (u <= mid - 1)))
        mid_s = (u // (2 * h)) * (2 * h) + h
        masks.append((t // (2 * h) == u // (2 * h)) & (u < mid_s) & (t >= mid))
        h //= 2
    mats = np.stack([m.astype(np.float32) for m in mats])
    masks = np.stack([m.astype(np.float32) for m in masks])
    if rev:
        mats = mats[:, ::-1, ::-1]
        masks = masks[:, ::-1, ::-1]
    mats = np.ascontiguousarray(mats).reshape(-1, c)
    return (jnp.asarray(np.concatenate([mats, mats], axis=1), BF16),
            jnp.asarray(np.ascontiguousarray(masks), F32))


HG_PAIR = 4


def _hgrn2_chunk(q, z, v, lg, dst_ref, msk_ref, st_ref, o_ref, ci, *, layer, rev, n_levels):
    c = q.shape[0]
    hk = HG_HEADS * HG_D
    e = jnp.exp(lg - jnp.max(lg, axis=0, keepdims=True))
    sm = e / jnp.sum(e, axis=0, keepdims=True)
    lb = jnp.zeros((1, hk), F32)
    for i in range(1, layer + 1):
        lb = lb + sm[i:i + 1]

    f = lb + (1.0 - lb) * _sigmoid(z)
    kk = 1.0 - f
    g = jnp.log(f)
    g_hi = g.astype(BF16)
    g_lo = (g - g_hi.astype(F32)).astype(BF16)
    dg = _dot(dst_ref[...], jnp.concatenate([g_hi, g_lo], axis=0))
    qs = _silu(q) * HG_D ** -0.5
    last = 0 if rev else c - 1

    for h in range(HG_HEADS):
        hs = slice(h * HG_D, (h + 1) * HG_D)
        qh, kh, vh = qs[:, hs].astype(BF16), kk[:, hs].astype(BF16), v[:, hs].astype(BF16)
        b_in = dg[0:c, hs]
        ex = lambda blk: jnp.exp(dg[blk * c:(blk + 1) * c, hs].astype(BF16))
        a = _dot_nt(qh, kh) * msk_ref[0]
        for l in range(1, n_levels + 1):
            el = ex(1 + l)
            a = a + _dot_nt(qh * el, kh * el) * msk_ref[l]
        st = st_ref[ci, h]
        o = _dot_nt(qh * ex(0), st.astype(BF16)) + _dot(a.astype(BF16), vh)
        o_ref[:, hs] = o.astype(o_ref.dtype)
        st_ref[ci, h] = jnp.exp(b_in[last:last + 1, :]) * st + _dot_tn(vh, kh * ex(1))


def _hgrn2_kernel(*refs, layer, n_levels, pair):
    n_in = 6 * pair
    ins, (lg_ref, dstf_ref, mskf_ref, dstb_ref, mskb_ref, s0_ref) = refs[:n_in], refs[n_in:n_in + 6]
    of_ref, ob_ref, sf_ref, st_ref = refs[n_in + 6:]
    j = pl.program_id(1)

    @pl.when(j == 0)
    def _():
        st_ref[...] = s0_ref[...].reshape(st_ref.shape)

    for e in range(pair):
        qf, zf, vf, qb, zb, vb = (r[...] for r in ins[6 * e:6 * e + 6])
        _hgrn2_chunk(qf, zf, vf, lg_ref[0], dstf_ref, mskf_ref, st_ref, of_ref.at[e], e,
                     layer=layer, rev=False, n_levels=n_levels)
        _hgrn2_chunk(qb, zb, vb, lg_ref[1], dstb_ref, mskb_ref, st_ref, ob_ref.at[e], pair + e,
                     layer=layer, rev=True, n_levels=n_levels)

    @pl.when(j == pl.num_programs(1) - 1)
    def _():
        sf_ref[...] = st_ref[...].reshape(sf_ref.shape)


def _hgrn2_scan(p, logits, s0, layer, row0, seq, n_batch):
    c = HG_CHUNK
    hk = HG_HEADS * HG_D
    nc = seq // c
    base = row0 // c
    n_levels = int(math.log2(c))
    tabs = _hgrn2_tables(c, False) + _hgrn2_tables(c, True)
    pair = math.gcd(HG_PAIR, n_batch)

    def prow(w, e, rev):
        return pl.BlockSpec((c, hk), lambda bp, j: (base + (bp * pair + e) * nc + (nc - 1 - j if rev else j),
                                                   w // hk))

    in_specs = []
    for e in range(pair):
        in_specs += [prow(COL_Q, e, False), prow(COL_ZF, e, False), prow(COL_IV, e, False),
                     prow(COL_Q, e, True), prow(COL_ZB, e, True), prow(COL_IV, e, True)]
    full = lambda a: pl.BlockSpec(a.shape, lambda bp, j: (0,) * a.ndim)
    st_spec = pl.BlockSpec((2, pair, HG_HEADS, HG_D, HG_D), lambda bp, j: (0, bp, 0, 0, 0))
    return pl.pallas_call(
        functools.partial(_hgrn2_kernel, layer=layer, n_levels=n_levels, pair=pair),
        grid=(n_batch // pair, nc),
        in_specs=in_specs + [full(logits)] + [full(t) for t in tabs] + [st_spec],
        out_specs=[pl.BlockSpec((pair, c, hk), lambda bp, j: (bp, j, 0)),
                   pl.BlockSpec((pair, c, hk), lambda bp, j: (bp, nc - 1 - j, 0)),
                   st_spec],
        out_shape=[jax.ShapeDtypeStruct((n_batch, seq, hk), BF16),
                   jax.ShapeDtypeStruct((n_batch, seq, hk), BF16),
                   jax.ShapeDtypeStruct((2, n_batch, HG_HEADS, HG_D, HG_D), F32)],
        scratch_shapes=[pltpu.VMEM((2 * pair, HG_HEADS, HG_D, HG_D), F32)],
        compiler_params=_cparams(2),
        name="hgrn2_scan",
    )(*([p] * (6 * pair)), logits, *tabs, s0)


def _shortconv_kernel(u_ref, w_ref, b_ref, o_ref):
    u = u_ref[...]
    n = u.shape[0]
    row = lax.broadcasted_iota(jnp.int32, u.shape, 0)
    prev = jnp.where(row == 0, 0.0, pltpu.roll(u, 1, 0))
    nxt = jnp.where(row == n - 1, 0.0, pltpu.roll(u, n - 1, 0))
    w = w_ref[...]
    o_ref[0, 0, 0] = prev * w[0:1] + u * w[1:2] + nxt * w[2:3] + b_ref[...]


def _shortconv_t1major_kernel(u_ref, w_ref, b_ref, o_ref, u_scr, *, n1):
    half = u_ref.shape[0] // n1
    w = w_ref[...]
    b = b_ref[...]
    pitch = n1 + 8

    def copy(g, carry):
        u_scr[pl.ds(pl.multiple_of(g * pitch, 8), n1), :] = u_ref[pl.ds(pl.multiple_of(g * n1, 8), n1), :]
        return carry

    lax.fori_loop(0, half, copy, 0, unroll=8)
    row = lax.broadcasted_iota(jnp.int32, (half, LANE), 0)
    col = lambda t1: u_scr[pl.ds(t1, half, stride=pitch), :]
    before = jnp.where(row == 0, 0.0, pltpu.roll(col(n1 - 1), 1, 0))
    after = jnp.where(row == half - 1, 0.0, pltpu.roll(col(0), half - 1, 0))

    def body(t1, carry):
        prev, cur = carry
        nxt = jnp.where(t1 == n1 - 1, after, col(jnp.minimum(t1 + 1, n1 - 1)))
        o_ref[0, 0, 0, pl.ds(pl.multiple_of(t1 * half, 8), half), :] = prev * w[0:1] + cur * w[1:2] + nxt * w[2:3] + b
        return cur, nxt

    lax.fori_loop(0, n1, body, (before, col(0)), unroll=16)


def _short_conv(p, w, b, row0, seq, n_batch, n1):
    nb = 3 * HY_W // LANE
    per = HY_W // LANE
    body, scratch = _shortconv_kernel, []
    if n1 > 1:
        body = functools.partial(_shortconv_t1major_kernel, n1=n1)
        scratch = [pltpu.VMEM((seq // n1 * (n1 + 8), LANE), F32)]
    return pl.pallas_call(
        body,
        grid=(n_batch, nb),
        in_specs=[pl.BlockSpec((seq, LANE), lambda bi, j: (row0 // seq + bi, COL_HY // LANE + j)),
                  pl.BlockSpec((3, LANE), lambda bi, j: (0, j)),
                  pl.BlockSpec((1, LANE), lambda bi, j: (0, j))],
        out_specs=pl.BlockSpec((1, 1, 1, seq, LANE), lambda bi, j: (j // per, bi, j % per, 0, 0)),
        out_shape=jax.ShapeDtypeStruct((3, n_batch, per, seq, LANE), F32),
        scratch_shapes=scratch,
        compiler_params=_cparams(2),
        name="hyena_short_conv",
    )(p, w, b.reshape(1, -1))


def _hy_filter_kernel(emb_ref, w1_ref, b1_ref, w2_ref, b2_ref, w3_ref, b3_ref, w4_ref, fr_ref,
                      dl_ref, o_ref, nrm_ref, *, seq, group):
    i = pl.program_id(0)
    hp = lax.Precision.HIGHEST
    fr = fr_ref[...]
    hid = jnp.sin(fr * (jnp.dot(emb_ref[...], w1_ref[...], precision=hp) + b1_ref[...]))
    hid = jnp.sin(fr * (jnp.dot(hid, w2_ref[...], precision=hp) + b2_ref[...]))
    hid = jnp.sin(fr * (jnp.dot(hid, w3_ref[...], precision=hp) + b3_ref[...]))
    h = jnp.dot(hid, w4_ref[...], precision=hp)
    tl, n = h.shape
    r = lax.broadcasted_iota(jnp.int32, h.shape, 0) + i * tl
    col = lax.broadcasted_iota(jnp.int32, h.shape, 1)
    per = seq // group
    pos = (r >> (per.bit_length() - 1)) + group * (r & (per - 1))
    t = pos.astype(F32) * (1.0 / (seq - 1))
    h = h * (jnp.exp(-t * dl_ref[...]) + HY_SHIFT)
    h = jnp.where((pos == 0) & (col >= n // 2), 0.0, h)
    for k in range(n // LANE):
        o_ref[k] = h[:, k * LANE:(k + 1) * LANE]

    @pl.when(i == 0)
    def _():
        nrm_ref[...] = jnp.zeros_like(nrm_ref)

    nrm_ref[...] += jnp.sum(jnp.abs(h), axis=0, keepdims=True)


def _hyena_filter_taps(seq, w1, b1, w2, b2, w3, b3, w4, freq, group):
    fh = w1.shape[1]
    n_emb = w1.shape[0]
    bands_n = (n_emb - 1) // 2
    tt = np.linspace(0.0, 1.0, seq, dtype=np.float32)[:, None].astype(np.float64)
    ww = (2.0 * math.pi / seq) * np.arange(seq, dtype=np.float64)[:, None]
    bands = np.linspace(1e-4, bands_n - 1, bands_n, dtype=np.float32)[None, :].astype(np.float64)
    emb = np.concatenate([tt, np.cos(bands * ww), -np.sin(bands * ww)], axis=-1)
    emb = np.pad(emb, ((0, 0), (0, LANE - n_emb))).astype(np.float32)
    assert seq % group == 0 and group & (group - 1) == 0 and (seq // group) & (seq // group - 1) == 0
    r = np.arange(seq)
    pos = r // (seq // group) + group * (r % (seq // group))
    emb = emb[pos]
    w1p = jnp.pad(w1, ((0, LANE - n_emb), (0, 0)))
    deltas = np.abs(np.linspace(HY_MIN_DECAY, HY_MAX_DECAY, HY_W, dtype=np.float32))
    deltas = np.tile(deltas, 4)[None, :]
    n = w4.shape[1]
    tl = min(seq, 512)
    full = lambda a: pl.BlockSpec(a.shape, lambda i: (0,) * a.ndim)
    args = (w1p, b1.reshape(1, fh), w2, b2.reshape(1, fh), w3, b3.reshape(1, fh), w4, freq.reshape(1, fh),
            jnp.asarray(deltas))
    return pl.pallas_call(
        functools.partial(_hy_filter_kernel, seq=seq, group=group),
        grid=(seq // tl,),
        in_specs=[pl.BlockSpec((tl, LANE), lambda i: (i, 0))] + [full(a) for a in args],
        out_specs=[pl.BlockSpec((n // LANE, tl, LANE), lambda i: (0, i, 0)),
                   pl.BlockSpec((1, n), lambda i: (0, 0))],
        out_shape=[jax.ShapeDtypeStruct((n // LANE, seq, LANE), F32), jax.ShapeDtypeStruct((1, n), F32)],
        compiler_params=_cparams(1),
        name="hyena_filter_taps",
    )(jnp.asarray(emb), *args)


def _fft_split(seq):
    n = 2 * seq
    n1 = FFT_N1 if n > 1024 else 1
    return n, n1, n // n1


def _cis(idx, n):
    ph = 2.0 * np.pi * (idx % n) / n
    return np.cos(ph), -np.sin(ph)


def _dft_tables_short(seq):
    n = 2 * seq
    cr, ci = _cis(np.arange(n)[:, None] * np.arange(seq)[None, :], n)
    w_fwd = np.concatenate([cr, ci], axis=0)
    w_inv = np.concatenate([cr.T, ci.T], axis=1) / n
    return dict(w_fwd=jnp.asarray(w_fwd, BF16), w_inv=jnp.asarray(w_inv, BF16))


def _dft_tables_long(seq):
    n, n1, n2 = _fft_split(seq)
    t1 = np.arange(n1)[:, None, None]
    f2 = np.arange(n2)[None, :, None]
    t2 = np.arange(n2 // 2)[None, None, :]
    cr, ci = _cis(f2 * (t1 + n1 * t2), n)
    w1 = np.concatenate([cr, ci], axis=1)
    w4 = np.concatenate([np.swapaxes(cr, 1, 2), np.swapaxes(ci, 1, 2)], axis=2) / n
    j = (n1 - t1) % n1 + n1 * t2
    br, bi = _cis(f2 * (n - j), n)
    w1f = np.concatenate([np.concatenate([cr, br], axis=2), np.concatenate([ci, bi], axis=2)], axis=1)
    gr, gi = _cis(np.arange(n1)[:, None] * np.arange(n1)[None, :], n1)
    wd = np.concatenate([np.concatenate([gr, -gi], axis=1), np.concatenate([gi, gr], axis=1)], axis=0)
    return dict(w1=jnp.asarray(np.swapaxes(w1, 1, 2), BF16), w4=jnp.asarray(w4, BF16),
                w1f=jnp.asarray(np.swapaxes(w1f, 1, 2), BF16),
                wd=jnp.asarray(wd, BF16), wdi=jnp.asarray(wd.T, BF16))


def _dft_rows_kernel(w_ref, x_ref, o_ref):
    o_ref[0] = _dot(w_ref[...], x_ref[0].astype(BF16)).astype(o_ref.dtype)


def _dft_rows(w, x, tn):
    nb, k, cols = x.shape
    m = w.shape[0]
    return pl.pallas_call(
        _dft_rows_kernel,
        grid=(nb, cols // tn),
        in_specs=[pl.BlockSpec((m, k), lambda b, j: (0, 0)),
                  pl.BlockSpec((1, k, tn), lambda b, j: (b, 0, j))],
        out_specs=pl.BlockSpec((1, m, tn), lambda b, j: (b, 0, j)),
        out_shape=jax.ShapeDtypeStruct((nb, m, cols), BF16),
        compiler_params=_cparams(2),
        name="hyena_dft_rows",
    )(w, x)


def _idft_gate_kernel(w_ref, b_ref, xg_ref, z_ref, sk_ref, o_ref):
    y = _dot(w_ref[...], b_ref[0])
    z = z_ref[0]
    o_ref[0] = xg_ref[0] * (y + z * sk_ref[...])


def _idft_gate(w, bc, xg, z, skip_t, tn):
    nb, k, cols = bc.shape
    m = w.shape[0]
    return pl.pallas_call(
        _idft_gate_kernel,
        grid=(nb, cols // tn),
        in_specs=[pl.BlockSpec((m, k), lambda b, j: (0, 0)),
                  pl.BlockSpec((1, k, tn), lambda b, j: (b, 0, j)),
                  pl.BlockSpec((1, m, tn), lambda b, j: (b, 0, j)),
                  pl.BlockSpec((1, m, tn), lambda b, j: (b, 0, j)),
                  pl.BlockSpec((1, tn), lambda b, j: (0, j))],
        out_specs=pl.BlockSpec((1, m, tn), lambda b, j: (b, 0, j)),
        out_shape=jax.ShapeDtypeStruct((nb, m, cols), F32),
        compiler_params=_cparams(2),
        name="hyena_idft_gate",
    )(w, bc, xg, z, skip_t)


def _spec_combine_kernel(af_ref, ab_ref, nf_ref, nb_ref, o_ref):
    inv = 1.0 / (nf_ref[...] + nb_ref[...])
    n = o_ref.shape[1]
    o_ref[0] = (af_ref[0, :n].astype(F32) + ab_ref[0, :n].astype(F32)) * inv
    o_ref[1] = (af_ref[0, n:].astype(F32) - ab_ref[0, n:].astype(F32)) * inv


def _spec_mul_kernel(a_ref, k_ref, o_ref):
    n = k_ref.shape[1]
    xr, xi = a_ref[0, :n].astype(F32), a_ref[0, n:].astype(F32)
    kr, ki = k_ref[0], k_ref[1]
    o_ref[0, :n] = (xr * kr - xi * ki).astype(o_ref.dtype)
    o_ref[0, n:] = (xr * ki + xi * kr).astype(o_ref.dtype)


def _hyena_short(seq, taps, nrm, uc, skip):
    n = 2 * seq
    uc = jnp.swapaxes(uc, 2, 3).reshape(3, uc.shape[1], seq, HY_W)
    tabs = _dft_tables_short(seq)
    cf = taps.shape[1]
    c = HY_W
    a = _dft_rows(tabs["w_fwd"], taps.reshape(1, seq, cf), tn=cf)
    nblk = cf // 2 // c
    spec = pl.pallas_call(
        _spec_combine_kernel,
        grid=(nblk,),
        in_specs=[pl.BlockSpec((1, 2 * n, c), lambda j: (0, 0, j)),
                  pl.BlockSpec((1, 2 * n, c), lambda j: (0, 0, j + nblk)),
                  pl.BlockSpec((1, c), lambda j: (0, j)),
                  pl.BlockSpec((1, c), lambda j: (0, j + nblk))],
        out_specs=pl.BlockSpec((2, n, c), lambda j: (0, 0, j)),
        out_shape=jax.ShapeDtypeStruct((2, n, cf // 2), F32),
        compiler_params=_cparams(1),
        name="hyena_spec_combine",
    )(a, a, nrm, nrm)
    nb = uc.shape[1]
    z = uc[2]
    for order in range(2):
        a = _dft_rows(tabs["w_fwd"], z, tn=c)
        bc = pl.pallas_call(
            _spec_mul_kernel,
            grid=(nb,),
            in_specs=[pl.BlockSpec((1, 2 * n, c), lambda b: (b, 0, 0)),
                      pl.BlockSpec((2, n, c), lambda b: (0, 0, order))],
            out_specs=pl.BlockSpec((1, 2 * n, c), lambda b: (b, 0, 0)),
            out_shape=jax.ShapeDtypeStruct((nb, 2 * n, c), BF16),
            compiler_params=_cparams(1),
            name="hyena_spec_mul",
        )(a, spec)
        z = _idft_gate(tabs["w_inv"], bc, uc[order], z, skip[order].reshape(1, c), tn=c)
    return z


HY_SLABS = 2


def _slab_pitch(n2):
    return n2 + 4


def _pack_c(re, im):
    hi = lax.bitcast_convert_type(re.astype(BF16).astype(F32), jnp.uint32)
    lo = lax.bitcast_convert_type(im.astype(BF16).astype(F32), jnp.uint32)
    return hi | (lo >> 16)


def _unpack_c(w):
    re = lax.bitcast_convert_type(w & jnp.uint32(0xFFFF0000), F32)
    im = lax.bitcast_convert_type(w << 16, F32)
    return re.astype(BF16), im.astype(BF16)


def _store_slab(a_scr, row, words):
    for s in range(HY_SLABS):
        a_scr[s, pl.ds(row, words.shape[0]), :] = words[:, s * LANE:(s + 1) * LANE]


def _stage1(a_scr, w_ref, xs_of, i, tb, n2, pitch):
    for j in range(tb):
        a = _dot_tn(w_ref[j], xs_of(j))
        _store_slab(a_scr, pl.multiple_of(i * tb * pitch, 8) + j * pitch, _pack_c(a[:n2], a[n2:]))


def _stage2(a_scr, wd_ref, f2, n1, pitch):
    w = jnp.concatenate([a_scr[s, pl.ds(f2, n1, stride=pitch), :] for s in range(HY_SLABS)], axis=1)
    re, im = _unpack_c(w)
    return _dot(wd_ref[...], jnp.concatenate([re, im], axis=0))


def _time_col(ref, j, half):
    lead = (0,) * (len(ref.shape) - 3)
    return jnp.concatenate([ref[lead + (s, slice(j * half, (j + 1) * half), slice(None))]
                            for s in range(HY_SLABS)], axis=1)


def _hy_spectrum_kernel(hf_ref, hb_ref, hb0_ref, nf_ref, nb_ref, w1f_ref, wd_ref, o_ref, a_scr,
                        *, n1, n2, tb, fb):
    i = pl.program_id(1)
    pitch = _slab_pitch(n2)
    half = n2 // 2
    nt = n1 // tb

    @pl.when(i < nt)
    def _():
        def xs_of(j):
            fut = _time_col(hb0_ref, 0, half) if j == 0 else _time_col(hb_ref, tb - j, half)
            return jnp.concatenate([_time_col(hf_ref, j, half), fut], axis=0).astype(BF16)
        _stage1(a_scr, w1f_ref, xs_of, i, tb, n2, pitch)

    @pl.when(i >= nt)
    def _():
        inv = 1.0 / (nf_ref[...] + nb_ref[...])
        for jj in range(fb):
            x = _stage2(a_scr, wd_ref, (i - nt) * fb + jj, n1, pitch)
            o_ref[0, 0, jj] = (x[:n1] * inv).astype(o_ref.dtype)
            o_ref[0, 1, jj] = (x[n1:] * inv).astype(o_ref.dtype)


def _hy_fftconv_kernel(z_ref, xg_ref, k_ref, sk_ref, w1_ref, wd_ref, wdi_ref, w4_ref, o_ref, a_scr,
                       *, n1, n2, tb, fb):
    i = pl.program_id(2)
    pitch = _slab_pitch(n2)
    half = n2 // 2
    nt, nf = n1 // tb, n2 // fb

    @pl.when(i < nt)
    def _():
        _stage1(a_scr, w1_ref, lambda j: _time_col(z_ref, j, half).astype(BF16), i, tb, n2, pitch)

    @pl.when((i >= nt) & (i < nt + nf))
    def _():
        for jj in range(fb):
            f2 = (i - nt) * fb + jj
            x = _stage2(a_scr, wd_ref, f2, n1, pitch)
            xr, xi = x[:n1], x[n1:]
            kr, ki = k_ref[0, 0, jj].astype(F32), k_ref[0, 1, jj].astype(F32)
            y = jnp.concatenate([xr * kr - xi * ki, xr * ki + xi * kr], axis=0).astype(BF16)
            bv = _dot(wdi_ref[...], y)
            words = _pack_c(bv[:n1], bv[n1:])
            for s in range(HY_SLABS):
                a_scr[s, pl.ds(f2, n1, stride=pitch), :] = words[:, s * LANE:(s + 1) * LANE]

    @pl.when(i >= nt + nf)
    def _():
        sk = sk_ref[...]
        for j in range(tb):
            row = pl.multiple_of((i - nt - nf) * tb * pitch, 8) + j * pitch
            w = jnp.concatenate([a_scr[s, pl.ds(row, n2), :] for s in range(HY_SLABS)], axis=1)
            re, im = _unpack_c(w)
            y = _dot(w4_ref[j], jnp.concatenate([re, im], axis=0))
            out = _time_col(xg_ref, j, half) * (y + _time_col(z_ref, j, half) * sk)
            for s in range(HY_SLABS):
                o_ref[0, s, j * half:(j + 1) * half, :] = out[:, s * LANE:(s + 1) * LANE]


def _hyena_long(seq, taps, nrm, uc, skip, tb=32, fb=32):
    n, n1, n2 = _fft_split(seq)
    tabs = _dft_tables_long(seq)
    pitch = _slab_pitch(n2)
    half = n2 // 2
    cb = HY_SLABS * LANE
    tb, fb = min(tb, n1), min(fb, n2)
    nt, nf = n1 // tb, n2 // fb
    cf = taps.shape[0] * LANE
    ngrp = cf // 2 // cb
    scratch = [pltpu.VMEM((HY_SLABS, n1 * pitch, LANE), jnp.uint32)]
    tcol = lambda i: jnp.minimum(i, nt - 1)
    spec = pl.pallas_call(
        functools.partial(_hy_spectrum_kernel, n1=n1, n2=n2, tb=tb, fb=fb),
        grid=(ngrp, nt + nf),
        in_specs=[pl.BlockSpec((HY_SLABS, tb * half, LANE), lambda g, i: (g, tcol(i), 0)),
                  pl.BlockSpec((HY_SLABS, tb * half, LANE), lambda g, i: (ngrp + g, nt - 1 - tcol(i), 0)),
                  pl.BlockSpec((HY_SLABS, tb * half, LANE), lambda g, i: (ngrp + g, (nt - tcol(i)) % nt, 0)),
                  pl.BlockSpec((1, cb), lambda g, i: (0, g)),
                  pl.BlockSpec((1, cb), lambda g, i: (0, ngrp + g)),
                  pl.BlockSpec((tb, n2, 2 * n2), lambda g, i: (tcol(i), 0, 0)),
                  pl.BlockSpec(tabs["wd"].shape, lambda g, i: (0, 0))],
        out_specs=pl.BlockSpec((1, 2, fb, n1, cb), lambda g, i: (g, 0, jnp.maximum(i - nt, 0), 0, 0)),
        out_shape=jax.ShapeDtypeStruct((ngrp, 2, n2, n1, cb), BF16),
        scratch_shapes=scratch,
        compiler_params=_cparams(2),
        name="hyena_filter_spectrum",
    )(taps, taps, taps, nrm, nrm, tabs["w1f"], tabs["wd"])
    nb, nblk = uc.shape[1], uc.shape[2]
    ngc = nblk // HY_SLABS
    z, zsel = uc, 2

    def tblk(b, g, i):
        return (b, g, jnp.where(i < nt, i, jnp.maximum(i - nt - nf, 0)), 0)

    def tblk_late(b, g, i):
        return (b, g, jnp.maximum(i - nt - nf, 0), 0)

    for order in range(2):
        stacked = lambda sel, f: (lambda b, g, i: (sel,) + f(b, g, i))
        z = pl.pallas_call(
            functools.partial(_hy_fftconv_kernel, n1=n1, n2=n2, tb=tb, fb=fb),
            grid=(nb, ngc, 2 * nt + nf),
            in_specs=[pl.BlockSpec((1, 1, HY_SLABS, tb * half, LANE), stacked(zsel, tblk)),
                      pl.BlockSpec((1, 1, HY_SLABS, tb * half, LANE), stacked(order, tblk_late)),
                      pl.BlockSpec((1, 2, fb, n1, cb),
                                   lambda b, g, i: (order * ngc + g, 0, jnp.clip(i - nt, 0, nf - 1), 0, 0)),
                      pl.BlockSpec((1, cb), lambda b, g, i: (0, g)),
                      pl.BlockSpec((tb, half, 2 * n2), lambda b, g, i: (tcol(i), 0, 0)),
                      pl.BlockSpec(tabs["wd"].shape, lambda b, g, i: (0, 0)),
                      pl.BlockSpec(tabs["wdi"].shape, lambda b, g, i: (0, 0)),
                      pl.BlockSpec((tb, half, 2 * n2), lambda b, g, i: (jnp.maximum(i - nt - nf, 0), 0, 0))],
            out_specs=pl.BlockSpec((1, HY_SLABS, tb * half, LANE), tblk_late),
            out_shape=jax.ShapeDtypeStruct((nb, nblk, seq, LANE), F32),
            scratch_shapes=scratch,
            compiler_params=_cparams(3),
            name="hyena_fft_conv",
        )(z, uc, spec, skip[order].reshape(1, HY_W), tabs["w1"], tabs["wd"], tabs["wdi"], tabs["w4"])
        z, zsel = z.reshape((1,) + z.shape), 0
    return z[0]


def _hyena(p, lw, row0, seq, n_batch):
    long = _fft_split(seq)[1] > 1
    n1 = FFT_N1 if long else 1
    taps, nrm = _hyena_filter_taps(seq, lw["hy_w1"], lw["hy_b1"], lw["hy_w2"], lw["hy_b2"], lw["hy_w3"],
                                   lw["hy_b3"], lw["hy_w4"], lw["hy_freq"], n1)
    uc = _short_conv(p, lw["hy_conv_w"], lw["hy_conv_b"], row0, seq, n_batch, n1)
    if long:
        return _hyena_long(seq, taps, nrm, uc, lw["hy_skip"])
    taps = jnp.swapaxes(taps, 0, 1).reshape(seq, -1)
    return _hyena_short(seq, taps, nrm, uc, lw["hy_skip"]).reshape(n_batch * seq, HY_W)


def _mla_qkv_kernel(qa_ref, kva_ref, kr_ref, cs_ref, wq_ref, wkv_ref, gqa_ref, gkva_ref, gqn_ref, gqr_ref,
                    gkn_ref, gkr_ref, q_ref, k_ref, v_ref):
    cs = cs_ref[...]
    lane = lax.broadcasted_iota(jnp.int32, cs.shape, 1)
    low = lane < MLA_ROPE

    def rope(pair, gain2):
        ms = jnp.sum(jnp.where(low, pair * pair, 0.0), axis=-1, keepdims=True) * (1.0 / MLA_ROPE)
        t = pair * lax.rsqrt(ms + RMS_EPS) * gain2 * cs
        return jnp.where(low, t + pltpu.roll(t, MLA_ROPE, 1), 0.0)

    qq = _dot(_rms(qa_ref[...], gqa_ref[...]).astype(BF16), wq_ref[...])
    kv = _dot(_rms(kva_ref[...], gkva_ref[...]).astype(BF16), wkv_ref[...])
    kr = rope(kr_ref[...], gkr_ref[...])
    hp = MLA_HEAD_PAD
    tm = kr.shape[0]
    qscale = MLA_SCALE * LOG2_E
    ones_row = (lax.broadcasted_iota(jnp.int32, (V_ROWS - MLA_V, tm), 0) == 0).astype(BF16)
    for h in range(MLA_HEADS):
        qn = _rms(qq[:, h * hp:h * hp + MLA_NOPE], gqn_ref[...])
        qr = rope(qq[:, h * hp + MLA_NOPE:(h + 1) * hp], gqr_ref[...])
        q_ref[h, 0:LANE, :] = (qn * qscale).T.astype(BF16)
        q_ref[h, LANE:2 * LANE, :] = (qr * qscale).T.astype(BF16)
        kn = _rms(kv[:, h * hp:h * hp + MLA_NOPE], gkn_ref[...])
        k_ref[h, :, 0:LANE] = kn.astype(BF16)
        k_ref[h, :, LANE:2 * LANE] = kr.astype(BF16)
        v_ref[h, 0:MLA_V, :] = kv[:, h * hp + MLA_NOPE:(h + 1) * hp].T.astype(BF16)
        v_ref[h, MLA_V:V_ROWS, :] = ones_row


def _rope_table(seq, n_batch, ctx_rows):
    rows = seq // GRID_W
    row = np.repeat(np.arange(rows, dtype=np.float32), GRID_W)
    col = np.tile(np.arange(GRID_W, dtype=np.float32), rows)
    half = MLA_ROPE // 2
    inv = (ROPE_THETA ** (-np.arange(0, half, 2, dtype=np.float32) / half)).astype(np.float32)
    ar = (row[:, None] * inv).astype(np.float64)
    ac = (col[:, None] * inv).astype(np.float64)
    cos = np.concatenate([np.cos(ar), np.cos(ar), np.cos(ac), np.cos(ac)], axis=1)
    sin = np.concatenate([-np.sin(ar), np.sin(ar), -np.sin(ac), np.sin(ac)], axis=1)
    lat = np.tile(np.concatenate([cos, sin], axis=1), (n_batch, 1))
    ctx = np.concatenate([np.ones((ctx_rows, MLA_ROPE)), np.zeros((ctx_rows, MLA_ROPE))], axis=1)
    return jnp.asarray(np.concatenate([lat, ctx], axis=0), F32)


def _pair_gain(g):
    return jnp.concatenate([g, _rope_swap(g)]).reshape(1, 2 * MLA_ROPE)


def _mla_qkv(p, cs, lw, tm=512):
    n_rows = p.shape[0]
    hd, hp = MLA_HEADS, MLA_HEAD_PAD
    w_uq = lw["w_uq"].reshape(-1, hd, MLA_QK)
    wq = jnp.concatenate([w_uq, _rope_swap(w_uq[..., MLA_NOPE:])], axis=-1).reshape(-1, hd * hp).astype(BF16)
    wkv = lw["w_ukv"].astype(BF16)
    ql, kvl = wq.shape[0], wkv.shape[0]
    vec = lambda a: a.reshape(1, -1)
    full = lambda a: pl.BlockSpec(a.shape, lambda i: (0,) * a.ndim)
    args = (wq, wkv, vec(lw["q_a_norm"]), vec(lw["kv_a_norm"]), vec(lw["q_nope_norm"]),
            _pair_gain(lw["q_rope_norm"]), vec(lw["k_nope_norm"]), _pair_gain(lw["k_rope_norm"]))
    return pl.pallas_call(
        _mla_qkv_kernel,
        grid=(n_rows // tm,),
        in_specs=[pl.BlockSpec((tm, ql), lambda i: (i, COL_QA // ql)),
                  pl.BlockSpec((tm, kvl), lambda i: (i, COL_KVA // kvl)),
                  pl.BlockSpec((tm, LANE), lambda i: (i, COL_KR // LANE)),
                  pl.BlockSpec((tm, LANE), lambda i: (i, 0))] + [full(a) for a in args],
        out_specs=[pl.BlockSpec((hd, hp, tm), lambda i: (0, 0, i)),
                   pl.BlockSpec((hd, tm, hp), lambda i: (0, i, 0)),
                   pl.BlockSpec((hd, V_ROWS, tm), lambda i: (0, 0, i))],
        out_shape=[jax.ShapeDtypeStruct((hd, hp, n_rows), BF16),
                   jax.ShapeDtypeStruct((hd, n_rows, hp), BF16),
                   jax.ShapeDtypeStruct((hd, V_ROWS, n_rows), BF16)],
        compiler_params=_cparams(1),
        name="mla_qkv",
    )(p, p, p, cs, *args)


def _key_chunks(k_refs, v_refs, tk):
    chunks, s0 = [], 0
    for kr, vr in zip(k_refs, v_refs):
        n = kr.shape[1]
        step = min(tk, n)
        chunks += [(kr, vr, r0, step, s0 + r0) for r0 in range(0, n, step)]
        s0 += n
    return chunks


def _score_pass(qt, chunks, s_scr):
    m = None
    for kr, _, r0, rn, s0 in chunks:
        s = _dot(kr[0, r0:r0 + rn, :], qt)
        s_scr[s0:s0 + rn, :] = s
        mj = jnp.max(s, axis=0, keepdims=True)
        m = mj if m is None else jnp.maximum(m, mj)
    return m


def _value_pass(chunks, s_scr, m, o_ref):
    acc = None
    for _, vr, r0, rn, s0 in chunks:
        p = jnp.exp2((s_scr[s0:s0 + rn, :] - m).astype(BF16))
        part = _dot(vr[0, :, r0:r0 + rn], p)
        acc = part if acc is None else acc + part
    o_ref[...] = (acc[:MLA_V] / acc[MLA_V:MLA_V + 1]).astype(o_ref.dtype)


def _attn_ctx_kernel(q_ref, kc_ref, vc_ref, o_ref, s_scr, *, tk):
    chunks = _key_chunks([kc_ref], [vc_ref], tk)
    _value_pass(chunks, s_scr, _score_pass(q_ref[0], chunks, s_scr), o_ref)


def _attn_kernel(q_ref, k_ref, v_ref, kc_ref, vc_ref, o_ref, s_a, s_b, m_a, m_b, *, tk):
    i = pl.program_id(2)
    chunks = _key_chunks([kc_ref, k_ref], [vc_ref, v_ref], tk)

    @pl.when((pl.program_id(0) == 0) & (pl.program_id(1) == 0) & (i == 0))
    def _():
        s_b[...] = jnp.zeros_like(s_b)
        m_b[...] = jnp.zeros_like(m_b)

    def step(s_cur, m_cur, s_prev, m_prev):
        _value_pass(chunks, s_prev, m_prev[0:1, :], o_ref)
        m_cur[...] = jnp.broadcast_to(_score_pass(q_ref[0], chunks, s_cur), m_cur.shape)

    @pl.when(i % 2 == 0)
    def _():
        step(s_a, m_a, s_b, m_b)

    @pl.when(i % 2 == 1)
    def _():
        step(s_b, m_b, s_a, m_a)


def _attention(qt, k, vt, seq, ctx_len, n_batch, latent, tq=256, tk=1024):
    hd, hp, n_rows = qt.shape
    lat_rows = n_batch * seq
    cblk = lat_rows // ctx_len
    kc_spec = pl.BlockSpec((1, ctx_len, hp), lambda b, h, i: (h, cblk + b, 0))
    vc_spec = pl.BlockSpec((1, V_ROWS, ctx_len), lambda b, h, i: (h, 0, cblk + b))
    if latent:
        nq = seq // tq
        n_pairs = hd * nq
        heads, steps = 1, n_pairs + 1
        cur = lambda s: jnp.minimum(s, n_pairs - 1)
        prv = lambda s: jnp.maximum(s - 1, 0)
        in_specs = [pl.BlockSpec((1, hp, tq), lambda b, _, s: (cur(s) // nq, 0, b * nq + cur(s) % nq)),
                    pl.BlockSpec((1, seq, hp), lambda b, _, s: (cur(s) // nq, b, 0)),
                    pl.BlockSpec((1, V_ROWS, seq), lambda b, _, s: (prv(s) // nq, 0, b)),
                    pl.BlockSpec((1, ctx_len, hp), lambda b, _, s: (cur(s) // nq, cblk + b, 0)),
                    pl.BlockSpec((1, V_ROWS, ctx_len), lambda b, _, s: (prv(s) // nq, 0, cblk + b))]
        args = (qt, k, vt, k, vt)
        out_spec = pl.BlockSpec((MLA_V, tq), lambda b, _, s: (prv(s) // nq, b * nq + prv(s) % nq))
        out_cols, n_keys = lat_rows, seq + ctx_len
        body = functools.partial(_attn_kernel, tk=tk)
        scratch = [pltpu.VMEM((n_keys, tq), F32)] * 2 + [pltpu.VMEM((8, tq), F32)] * 2
    else:
        heads, steps = hd, 1
        tq = ctx_len
        in_specs = [pl.BlockSpec((1, hp, tq), lambda b, h, i: (h, 0, cblk + b)), kc_spec, vc_spec]
        args = (qt, k, vt)
        out_spec = pl.BlockSpec((MLA_V, tq), lambda b, h, i: (h, b))
        out_cols, n_keys = n_batch * ctx_len, ctx_len
        body = functools.partial(_attn_ctx_kernel, tk=tk)
        scratch = [pltpu.VMEM((n_keys, tq), F32)]
    return pl.pallas_call(
        body,
        grid=(n_batch, heads, steps),
        in_specs=in_specs,
        out_specs=out_spec,
        out_shape=jax.ShapeDtypeStruct((hd * MLA_V, out_cols), BF16),
        scratch_shapes=scratch,
        compiler_params=_cparams(3),
        name="mla_attention" if latent else "mla_attention_ctx",
    )(*args)


def _merge_kernel(*refs, lat_tiles):
    (x_ref, mod_ref, of_ref, ob_ref, g_ref, yb_ref, yc_ref, ga_ref, gb_ref, gc_ref,
     gain_ref, wa_ref, wb_ref, wc_ref, wo_ref) = refs[:15]
    ctx_refs = refs[15:-1]
    o_ref = refs[-1]
    is_ctx = pl.program_id(0) >= lat_tiles
    m = mod_ref[0]
    o = of_ref[...].astype(F32) + ob_ref[...].astype(F32)
    if ctx_refs:
        o = jnp.where(is_ctx, ctx_refs[0][...].astype(F32) + ctx_refs[1][...].astype(F32), o)
    gain = gain_ref[...]
    ya = jnp.concatenate([_rms(o[:, h * HG_D:(h + 1) * HG_D], gain) for h in range(HG_HEADS)], axis=1)
    ya = (ya * _silu(g_ref[...])).astype(BF16)
    nslab, tcols = yb_ref.shape[1], yb_ref.shape[3]
    qn = x_ref.shape[0] // yb_ref.shape[2]
    q0 = (pl.program_id(0) % (tcols // qn)) * qn
    yb = jnp.concatenate([jnp.concatenate([yb_ref[0, k, :, q0 + q, :] for k in range(nslab)], axis=1)
                          for q in range(qn)], axis=0)
    yc = yc_ref[...]
    if ctx_refs:
        yb = jnp.where(is_ctx, ctx_refs[2][...], yb)
        yc = jnp.where(is_ctx, ctx_refs[3][...], yc)
    mix = (_sigmoid(ga_ref[...]) * _dot(ya, wa_ref[...])
           + _sigmoid(gb_ref[...]) * _dot(yb.astype(BF16), wb_ref[...])
           + _sigmoid(gc_ref[...]) * _dot_tn(yc, wc_ref[...]))
    o_ref[...] = x_ref[...] + m[5:6] * _dot(mix.astype(BF16), wo_ref[...])


def _merge(xa, mod, o_f, o_b, p, pg, y_b, y_c, ctx_parts, lw, n_rows, seq, n_batch, tm=512):
    d = xa.shape[1]
    hk = HG_HEADS * HG_D
    tpb = seq // tm
    lat_tiles = n_batch * tpb
    n1 = FFT_N1
    tcols = 8
    assert tm % n1 == 0 and tcols % (tm // n1) == 0 and seq % (tcols * n1) == 0
    per_blk = tcols // (tm // n1)
    y_b = y_b.reshape(n_batch, y_b.shape[1], n1, seq // n1, LANE)
    row = lambda w: pl.BlockSpec((tm, w), lambda i: (i, 0))
    pcol = lambda w, c: pl.BlockSpec((tm, w), lambda i: (i, c // w))
    full = lambda a: pl.BlockSpec(a.shape, lambda i: (0,) * a.ndim)
    ws = (lw["hg_out_norm"].reshape(1, HG_D), lw["w_br_a"].astype(BF16), lw["w_br_b"].astype(BF16),
          lw["w_br_c"].astype(BF16), lw["w_out"].astype(BF16))
    yb_spec = pl.BlockSpec((1, y_b.shape[1], n1, tcols, LANE),
                           lambda i: (jnp.minimum(i // tpb, n_batch - 1), 0, 0, (i % tpb) // per_blk, 0))
    lat = lambda i: jnp.minimum(i, lat_tiles - 1)
    late = lambda i: jnp.maximum(i - lat_tiles, 0)
    extra_specs, extra = [], []
    if ctx_parts is not None:
        extra_specs = [pl.BlockSpec((tm, hk), lambda i: (late(i), 0)),
                       pl.BlockSpec((tm, hk), lambda i: (late(i), 0)),
                       pl.BlockSpec((tm, HY_W), lambda i: (late(i), 0)),
                       pl.BlockSpec((MLA_HEADS * MLA_V, tm), lambda i: (0, late(i)))]
        extra = list(ctx_parts)
    return pl.pallas_call(
        functools.partial(_merge_kernel, lat_tiles=lat_tiles),
        grid=(n_rows // tm,),
        in_specs=[row(d), pl.BlockSpec((1, N_MOD, d), _group_map(tpb, n_batch)),
                  pl.BlockSpec((tm, hk), lambda i: (lat(i), 0)), pl.BlockSpec((tm, hk), lambda i: (lat(i), 0)),
                  pcol(hk, COL_G), yb_spec,
                  pl.BlockSpec((MLA_HEADS * MLA_V, tm), lambda i: (0, lat(i))),
                  pcol(d, COL_GA), pcol(d, COL_GB), pcol(d, COL_GC)] + [full(a) for a in ws] + extra_specs,
        out_specs=row(d),
        out_shape=jax.ShapeDtypeStruct((n_rows, d), F32),
        compiler_params=_cparams(1),
        name="merge",
    )(xa, mod, o_f, o_b, p, y_b, y_c, pg, pg, pg, *ws, *extra)


def kernel(x, c, ctx, c_ctx, ada_w, ada_b, ffn1_norm, ffn1_w13, ffn1_w2, mix_norm, w_in, hg_lb_logits, hg_out_norm, hy_conv_w, hy_conv_b, hy_w1, hy_b1, hy_w2, hy_b2, hy_w3, hy_b3, hy_w4, hy_freq, hy_skip, q_a_norm, w_uq, kv_a_norm, w_ukv, q_nope_norm, q_rope_norm, k_nope_norm, k_rope_norm, w_br_a, w_br_b, w_br_c, w_out, ffn2_norm, ffn2_w13, ffn2_w2):
    stacked = dict(
        ada_w=ada_w, ada_b=ada_b, ffn1_norm=ffn1_norm, ffn1_w13=ffn1_w13, ffn1_w2=ffn1_w2, mix_norm=mix_norm,
        w_in=w_in, hg_out_norm=hg_out_norm, hy_conv_w=hy_conv_w, hy_conv_b=hy_conv_b, hy_w1=hy_w1, hy_b1=hy_b1,
        hy_w2=hy_w2, hy_b2=hy_b2, hy_w3=hy_w3, hy_b3=hy_b3, hy_w4=hy_w4, hy_freq=hy_freq, hy_skip=hy_skip,
        q_a_norm=q_a_norm, w_uq=w_uq, kv_a_norm=kv_a_norm, w_ukv=w_ukv, q_nope_norm=q_nope_norm,
        q_rope_norm=q_rope_norm, k_nope_norm=k_nope_norm, k_rope_norm=k_rope_norm, w_br_a=w_br_a,
        w_br_b=w_br_b, w_br_c=w_br_c, w_out=w_out, ffn2_norm=ffn2_norm, ffn2_w13=ffn2_w13, ffn2_w2=ffn2_w2)
    n_batch, seq, d = x.shape
    ctx_len = ctx.shape[1]
    depth = ada_w.shape[0]
    lat_rows, ctx_rows = n_batch * seq, n_batch * ctx_len
    all_rows = lat_rows + ctx_rows
    assert seq % 512 == 0 and ctx_rows % 512 == 0 and seq % ctx_len == 0 and seq % GRID_W == 0
    assert ctx_len % HG_CHUNK == 0 and n_batch < 8

    xa, xc = x.reshape(lat_rows, d), ctx.reshape(ctx_rows, d)
    cs = jnp.concatenate([c, c_ctx.reshape(1, d), jnp.zeros((7 - n_batch, d), F32)], axis=0)
    rope_cs = _rope_table(seq, n_batch, ctx_rows)
    zero_state = jnp.zeros((2, n_batch, HG_HEADS, HG_D, HG_D), F32)
    hk = HG_HEADS * HG_D

    for l in range(depth):
        lw = {name: val[l] for name, val in stacked.items()}
        need_ctx = l < depth - 1
        mod = _modulation(cs, lw["ada_w"], lw["ada_b"])
        xa = _half_ffn(xa, mod, lw["ffn1_norm"], lw["ffn1_w13"], lw["ffn1_w2"], 0, all_rows, seq, n_batch, xc=xc)
        xc = None
        w_gates, w_rest = _pack_w_in(lw["w_in"].astype(BF16))
        p = _in_projection(xa, mod, lw["mix_norm"], w_rest, seq, n_batch, F32, tn=w_rest.shape[1] // 2)
        pg = _in_projection(xa, mod, lw["mix_norm"], w_gates, seq, n_batch, BF16, tn=w_gates.shape[1])

        ocf, ocb, s_c = _hgrn2_scan(p, hg_lb_logits, zero_state, l, lat_rows, ctx_len, n_batch)
        olf, olb, _ = _hgrn2_scan(p, hg_lb_logits, s_c, l, 0, seq, n_batch)
        y_b = _hyena(p, lw, 0, seq, n_batch)
        qt, k, vt = _mla_qkv(p, rope_cs, lw)
        y_c = _attention(qt, k, vt, seq, ctx_len, n_batch, latent=True)

        ctx_parts = None
        if need_ctx:
            ctx_parts = (ocf.reshape(ctx_rows, hk), ocb.reshape(ctx_rows, hk),
                         _hyena(p, lw, lat_rows, ctx_len, n_batch),
                         _attention(qt, k, vt, seq, ctx_len, n_batch, latent=False))
        mix_rows = all_rows if need_ctx else lat_rows
        xa = _merge(xa, mod, olf.reshape(lat_rows, hk), olb.reshape(lat_rows, hk), p, pg, y_b, y_c, ctx_parts,
                    lw, mix_rows, seq, n_batch)
        xa = _half_ffn(xa, mod, lw["ffn2_norm"], lw["ffn2_w13"], lw["ffn2_w2"], 6, mix_rows, seq, n_batch)
    return xa[:lat_rows].reshape(n_batch, seq, d)
```
